```python
import jax
import jax.numpy as jnp
from jax import lax
import numpy as np

D_MODEL = 1024
BATCH = 16
SEQ = 2048
DEPTH = 2

CHUNK = 64
PLE_DIM = 256
HEAD_DIM = 64
D_RWKV = D_MODEL // 2
H_RWKV = D_RWKV // HEAD_DIM
D_ATT = D_MODEL - D_RWKV
H_ATT = D_ATT // HEAD_DIM
LORA_DECAY = 64
LORA_ICLR = 64
LORA_GATE = 128
LORA_VRES = 32
N_RWKV_IN = 3 * D_RWKV + LORA_DECAY + LORA_ICLR + LORA_GATE
N_ATT_IN = 3 * D_ATT
N_IN = N_RWKV_IN + N_ATT_IN
RWKV_SPLITS = (D_RWKV, 2 * D_RWKV, 3 * D_RWKV, 3 * D_RWKV + LORA_DECAY, 3 * D_RWKV + LORA_DECAY + LORA_ICLR)
LEFT_CHUNKS = 8
BAND = (LEFT_CHUNKS + 1) * CHUNK
MAX_REL = 256
N_REL = CHUNK + MAX_REL
N_GROUPS = 4
EXPERTS_PER_GROUP = 8
TOP_K = 2
D_EXPERT = 256
RMS_EPS = 1e-6
GN_EPS = 64e-5
NEG_INF = -1e30

kernel_name = 'hybrid_rwkv7_chunkattn_hmoe_ple'


def rms_norm(x, w):
    xf = x.astype(jnp.float32)
    y = xf * lax.rsqrt(jnp.mean(xf * xf, axis=-1, keepdims=True) + RMS_EPS)
    return (y * w.astype(jnp.float32)).astype(x.dtype)


def token_shift_mix(z, mu):
    z_prev = jnp.pad(z, ((0, 0), (1, 0), (0, 0)))[:, :-1]
    return z + (z_prev - z) * mu


def rwkv7_recurrence(r, decay, k, v, a, b):
    bsz, _, nh, n = r.shape

    def step(state, inp):
        r_t, w_t, k_t, v_t, a_t, b_t = inp
        sa = jnp.einsum('bhij,bhj->bhi', state, a_t)
        state = (state * w_t[:, :, None, :] + sa[..., None] * b_t[:, :, None, :]
                 + v_t[..., None] * k_t[:, :, None, :])
        y_t = jnp.einsum('bhij,bhj->bhi', state, r_t)
        return state, y_t

    s0 = jnp.zeros((bsz, nh, n, n), jnp.float32)
    xs = tuple(jnp.moveaxis(t, 1, 0) for t in (r, decay, k, v, a, b))
    _, ys = lax.scan(step, s0, xs)
    return jnp.moveaxis(ys, 0, 1)


def rwkv7_time_mix(z, v_first, v_res, w0, w2, a0, a2, g2, k_k, k_a, r_k, ln_w, ln_b):
    out_dtype = z.dtype
    z = z.astype(jnp.float32)
    bsz, seq, _ = z.shape
    r, k, v, w_lo, a_lo, g_lo = jnp.split(z, RWKV_SPLITS, axis=-1)
    if v_res is None:
        v_first = v
    else:
        v0, v1, v2 = v_res
        v = v + (v_first - v) * jax.nn.sigmoid(v0 + (v @ v1) @ v2)
    w_log = -jax.nn.softplus(-(w0 + jnp.tanh(w_lo) @ w2)) - 0.5
    decay = jnp.exp(-jnp.exp(w_log))
    a = jax.nn.sigmoid(a0 + a_lo @ a2)
    g = jax.nn.sigmoid(g_lo) @ g2

    def heads(t):
        return t.reshape(bsz, seq, H_RWKV, HEAD_DIM)

    kk = heads(k * k_k)
    kk = kk * lax.rsqrt(jnp.maximum(jnp.sum(kk * kk, axis=-1, keepdims=True), 1e-24))
    k = k * (1.0 + (a - 1.0) * k_a)
    rh, kh, vh, ah = heads(r), heads(k), heads(v), heads(a)
    y = rwkv7_recurrence(rh, heads(decay), kh, vh, -kk, kk * ah)
    mean = jnp.mean(y, axis=-1, keepdims=True)
    var = jnp.mean(jnp.square(y - mean), axis=-1, keepdims=True)
    y = ((y - mean) * lax.rsqrt(var + GN_EPS)).reshape(bsz, seq, D_RWKV) * ln_w + ln_b
    bonus = jnp.sum(rh * kh * r_k.reshape(H_RWKV, HEAD_DIM), axis=-1, keepdims=True) * vh
    y = y + bonus.reshape(bsz, seq, D_RWKV)
    return (y * g).astype(out_dtype), v_first


def chunk_band_attention(q, k, v, rel_bias):
    bsz, seq, nh, dh = q.shape
    nc = seq // CHUNK
    pad = ((0, 0), (LEFT_CHUNKS * CHUNK, 0), (0, 0), (0, 0))
    k_pad = jnp.pad(k, pad)
    v_pad = jnp.pad(v, pad)
    qi = jnp.arange(CHUNK)[:, None]
    kj = jnp.arange(BAND)
    rel = qi + LEFT_CHUNKS * CHUNK - kj[None, :]
    rel_idx = jnp.clip(rel, -(CHUNK - 1), MAX_REL) + (CHUNK - 1)
    bias = rel_bias[:, rel_idx].astype(jnp.float32)
    q_chunks = jnp.moveaxis(q.reshape(bsz, nc, CHUNK, nh, dh), 1, 0)
    scale = dh ** -0.5

    def one_chunk(args):
        c, qc = args
        kb = lax.dynamic_slice_in_dim(k_pad, c * CHUNK, BAND, axis=1)
        vb = lax.dynamic_slice_in_dim(v_pad, c * CHUNK, BAND, axis=1)
        s = jnp.einsum('bqhd,bkhd->bhqk', qc, kb).astype(jnp.float32) * scale + bias
        valid = kj >= (LEFT_CHUNKS - c) * CHUNK
        s = jnp.where(valid, s, NEG_INF)
        pr = jax.nn.softmax(s, axis=-1).astype(vb.dtype)
        return jnp.einsum('bhqk,bkhd->bqhd', pr, vb)

    out = lax.map(one_chunk, (jnp.arange(nc), q_chunks))
    return jnp.moveaxis(out, 0, 1).reshape(bsz, seq, nh * dh)


def hierarchical_moe(x, wg, bg, we, be, w1, w3, w2):
    bsz, seq, d = x.shape
    t = x.reshape(-1, d)
    group_prob = jax.nn.softmax((t @ wg + bg).astype(jnp.float32), axis=-1)
    g_sel = jnp.argmax(group_prob, axis=-1)
    p_g = jnp.take_along_axis(group_prob, g_sel[:, None], axis=-1)
    exp_logits = (t @ we + be).astype(jnp.float32).reshape(-1, N_GROUPS, EXPERTS_PER_GROUP)
    sel_logits = jnp.take_along_axis(exp_logits, g_sel[:, None, None], axis=1)[:, 0]
    exp_prob = jax.nn.softmax(sel_logits, axis=-1)
    top_v, top_i = lax.top_k(exp_prob, TOP_K)
    top_v = top_v / jnp.sum(top_v, axis=-1, keepdims=True)
    w_group = jnp.sum(jax.nn.one_hot(top_i, EXPERTS_PER_GROUP, dtype=jnp.float32) * top_v[..., None], axis=1)
    combine = (jax.nn.one_hot(g_sel, N_GROUPS, dtype=jnp.float32)[:, :, None]
               * (p_g * w_group)[:, None, :]).astype(x.dtype)
    out = jnp.zeros_like(t)
    for gi in range(N_GROUPS):
        h_gate = jnp.einsum('td,edf->tef', t, w1[gi])
        h_up = jnp.einsum('td,edf->tef', t, w3[gi])
        hid = jax.nn.silu(h_gate) * h_up * combine[:, gi, :, None]
        out = out + jnp.einsum('tef,efd->td', hid, w2[gi])
    return out.reshape(bsz, seq, d)


def setup_inputs(seed: int = 0) -> dict:
    key = jax.random.key(seed)
    ks = list(jax.random.split(key, 40))
    cnt = [0]
    f32 = jnp.float32

    def nk():
        cnt[0] += 1
        return ks[cnt[0] - 1]

    def nrm(shape, scale):
        return jax.random.normal(nk(), shape, f32) * scale

    def unif(shape, lo, hi):
        return jax.random.uniform(nk(), shape, f32, lo, hi)

    L = DEPTH
    Lv = DEPTH - 1
    return {
        'x': nrm((BATCH, SEQ, D_MODEL), 1.0),
        'p': nrm((DEPTH, BATCH, SEQ, PLE_DIM), 1.0),
        'norm_mix_w': 1.0 + nrm((L, D_MODEL), 0.02),
        'w_in': nrm((L, D_MODEL, N_IN), D_MODEL ** -0.5),
        'rwkv_mu': unif((L, N_RWKV_IN), 0.0, 1.0),
        'rwkv_w0': unif((L, D_RWKV), -6.0, -1.0),
        'rwkv_w2': nrm((L, LORA_DECAY, D_RWKV), 0.5 * LORA_DECAY ** -0.5),
        'rwkv_a0': nrm((L, D_RWKV), 0.1),
        'rwkv_a2': nrm((L, LORA_ICLR, D_RWKV), 0.5 * LORA_ICLR ** -0.5),
        'rwkv_g2': nrm((L, LORA_GATE, D_RWKV), LORA_GATE ** -0.5),
        'rwkv_k_k': 0.85 + nrm((L, D_RWKV), 0.02),
        'rwkv_k_a': 1.0 + nrm((L, D_RWKV), 0.02),
        'rwkv_r_k': nrm((L, D_RWKV), 0.1),
        'rwkv_ln_w': 1.0 + nrm((L, D_RWKV), 0.02),
        'rwkv_ln_b': nrm((L, D_RWKV), 0.01),
        'rwkv_v0': 1.0 + nrm((Lv, D_RWKV), 0.1),
        'rwkv_v1': nrm((Lv, D_RWKV, LORA_VRES), D_RWKV ** -0.5),
        'rwkv_v2': nrm((Lv, LORA_VRES, D_RWKV), 0.5 * LORA_VRES ** -0.5),
        'att_rel_bias': nrm((L, H_ATT, N_REL), 0.1),
        'w_out': nrm((L, D_MODEL, D_MODEL), D_MODEL ** -0.5),
        'norm_ffn_w': 1.0 + nrm((L, D_MODEL), 0.02),
        'router_group_w': nrm((L, D_MODEL, N_GROUPS), D_MODEL ** -0.5),
        'router_group_b': nrm((L, N_GROUPS), 0.01),
        'router_expert_w': nrm((L, D_MODEL, N_GROUPS * EXPERTS_PER_GROUP), D_MODEL ** -0.5),
        'router_expert_b': nrm((L, N_GROUPS * EXPERTS_PER_GROUP), 0.01),
        'expert_w1': nrm((L, N_GROUPS, EXPERTS_PER_GROUP, D_MODEL, D_EXPERT), D_MODEL ** -0.5),
        'expert_w3': nrm((L, N_GROUPS, EXPERTS_PER_GROUP, D_MODEL, D_EXPERT), D_MODEL ** -0.5),
        'expert_w2': nrm((L, N_GROUPS, EXPERTS_PER_GROUP, D_EXPERT, D_MODEL), D_EXPERT ** -0.5),
        'norm_ple_w': 1.0 + nrm((L, D_MODEL), 0.02),
        'ple_gate_w': nrm((L, D_MODEL, D_MODEL), D_MODEL ** -0.5),
        'ple_gate_b': nrm((L, D_MODEL), 0.01),
        'ple_proj_w': nrm((L, PLE_DIM, D_MODEL), PLE_DIM ** -0.5),
        'final_norm_w': 1.0 + nrm((D_MODEL,), 0.02),
    }


def reference(x, p, norm_mix_w, w_in, rwkv_mu, rwkv_w0, rwkv_w2, rwkv_a0, rwkv_a2, rwkv_g2,
              rwkv_k_k, rwkv_k_a, rwkv_r_k, rwkv_ln_w, rwkv_ln_b, rwkv_v0, rwkv_v1, rwkv_v2,
              att_rel_bias, w_out, norm_ffn_w, router_group_w, router_group_b,
              router_expert_w, router_expert_b, expert_w1, expert_w3, expert_w2,
              norm_ple_w, ple_gate_w, ple_gate_b, ple_proj_w, final_norm_w):
    h = x
    bsz, seq, _ = x.shape
    v_first = None
    for i in range(DEPTH):
        hn = rms_norm(h, norm_mix_w[i])
        z = hn @ w_in[i]
        z_rwkv = token_shift_mix(z[..., :N_RWKV_IN], rwkv_mu[i])
        z_att = z[..., N_RWKV_IN:]
        v_res = None if i == 0 else (rwkv_v0[i - 1], rwkv_v1[i - 1], rwkv_v2[i - 1])
        y_rwkv, v_first = rwkv7_time_mix(z_rwkv, v_first, v_res, rwkv_w0[i], rwkv_w2[i],
                                         rwkv_a0[i], rwkv_a2[i], rwkv_g2[i], rwkv_k_k[i],
                                         rwkv_k_a[i], rwkv_r_k[i], rwkv_ln_w[i], rwkv_ln_b[i])
        q, k, v = jnp.split(z_att, 3, axis=-1)
        q = q.reshape(bsz, seq, H_ATT, HEAD_DIM)
        k = k.reshape(bsz, seq, H_ATT, HEAD_DIM)
        v = v.reshape(bsz, seq, H_ATT, HEAD_DIM)
        y_att = chunk_band_attention(q, k, v, att_rel_bias[i])
        h = h + jnp.concatenate([y_rwkv, y_att.astype(y_rwkv.dtype)], axis=-1) @ w_out[i]
        h = h + hierarchical_moe(rms_norm(h, norm_ffn_w[i]), router_group_w[i], router_group_b[i],
                                 router_expert_w[i], router_expert_b[i],
                                 expert_w1[i], expert_w3[i], expert_w2[i])
        gate = jax.nn.sigmoid(rms_norm(h, norm_ple_w[i]) @ ple_gate_w[i] + ple_gate_b[i])
        h = h + gate * (p[i] @ ple_proj_w[i])
    return rms_norm(h, final_norm_w)
```

```python
import functools

import jax
import jax.numpy as jnp
from jax import lax
from jax.experimental import pallas as pl
from jax.experimental.pallas import tpu as pltpu

F32 = jnp.float32
BF16 = jnp.bfloat16
I32 = jnp.int32

CHUNK = 64
HEAD_DIM = 64
LEFT_CHUNKS = 8
MAX_REL = 256
N_GROUPS = 4
EXPERTS_PER_GROUP = 8
N_EXPERTS = N_GROUPS * EXPERTS_PER_GROUP
RMS_EPS = 1e-6
GN_EPS = 64e-5
NEG_INF = -1e30

V7X_LANES = 128
V7X_SUBLANES = 8
V7X_VMEM_LIMIT_BYTES = 48 * 1024 * 1024

TM_PROJ = 512
TB_RWKV = 256
QB_ATTN = 256
TM_ROUTE = 256
TM_DISPATCH = 512
TM_EXPERT = 256
TM_COMBINE = 512


def _params(*sem):
    return pltpu.CompilerParams(dimension_semantics=sem, vmem_limit_bytes=V7X_VMEM_LIMIT_BYTES)


def _rms(x, w):
    return x * lax.rsqrt(jnp.mean(x * x, axis=-1, keepdims=True) + RMS_EPS) * w


def _mm(a, b):
    return jnp.dot(a.astype(BF16), b.astype(BF16), preferred_element_type=F32)


def _mm_nt(a, b):
    return lax.dot_general(a.astype(BF16), b.astype(BF16), (((1,), (1,)), ((), ())),
                           preferred_element_type=F32)


def _mm_tn(a, b):
    return lax.dot_general(a.astype(BF16), b.astype(BF16), (((0,), (0,)), ((), ())),
                           preferred_element_type=F32)


def _to_token_tiles(ref, x):
    m, d = x.shape
    for s in range(d // V7X_LANES):
        ref[pl.ds(s, m, stride=V7X_SUBLANES), :] = x[:, s * V7X_LANES:(s + 1) * V7X_LANES]


def _from_token_tiles(ref, m):
    return jnp.concatenate([ref[pl.ds(s, m, stride=V7X_SUBLANES), :] for s in range(V7X_SUBLANES)],
                           axis=-1)


def _token_tile(ref, row):
    return ref.at[pl.ds(pl.multiple_of(row * V7X_SUBLANES, V7X_SUBLANES), V7X_SUBLANES), :]


def _split3(x):
    hi = x.astype(BF16)
    r1 = x - hi.astype(F32)
    mid = r1.astype(BF16)
    lo = (r1 - mid.astype(F32)).astype(BF16)
    return hi, mid, lo


def _mm_exact_lhs(a_bf16, x):
    hi, mid, lo = _split3(x)
    return (jnp.dot(a_bf16, hi, preferred_element_type=F32)
            + jnp.dot(a_bf16, mid, preferred_element_type=F32)
            + jnp.dot(a_bf16, lo, preferred_element_type=F32))


def _mm_exact_rhs(x, b_bf16):
    hi, mid, lo = _split3(x)
    return (jnp.dot(hi, b_bf16, preferred_element_type=F32)
            + jnp.dot(mid, b_bf16, preferred_element_type=F32)
            + jnp.dot(lo, b_bf16, preferred_element_type=F32))


def _norm_proj_kernel(h_ref, nw_ref, wr_ref, wa_ref, zr_ref, qkv_ref):
    hn = _rms(h_ref[...], nw_ref[...]).astype(BF16)
    zr_ref[...] = jnp.dot(hn, wr_ref[...], preferred_element_type=F32)
    qkv_ref[...] = jnp.dot(hn, wa_ref[...], preferred_element_type=F32).astype(BF16)


def _norm_proj(h, nw, wr, wa):
    t, d = h.shape
    tm = min(TM_PROJ, t)
    n_r, n_a = wr.shape[1], wa.shape[1]
    return pl.pallas_call(
        _norm_proj_kernel,
        out_shape=(jax.ShapeDtypeStruct((t, n_r), F32), jax.ShapeDtypeStruct((t, n_a), BF16)),
        grid=(t // tm,),
        in_specs=[pl.BlockSpec((tm, d), lambda i: (i, 0)),
                  pl.BlockSpec((1, d), lambda i: (0, 0)),
                  pl.BlockSpec((d, n_r), lambda i: (0, 0)),
                  pl.BlockSpec((d, n_a), lambda i: (0, 0))],
        out_specs=(pl.BlockSpec((tm, n_r), lambda i: (i, 0)),
                   pl.BlockSpec((tm, n_a), lambda i: (i, 0))),
        compiler_params=_params("parallel"),
        name="norm_proj",
    )(h, nw, wr, wa)


_V_W0, _V_A0, _V_KK, _V_KA, _V_RK, _V_LNW, _V_LNB, _V_V0 = range(8)


def _rwkv_kernel(*refs, has_vres, n_heads, d_r):
    if has_vres:
        (z_ref, vf_ref, mu_ref, vec_ref, wl_ref, v1_ref, v2_ref, y_ref,
         s_ref, carry_ref, r_s, k_s, v_s, kk_s, a_s, lc_s, lw_s, bon_s, g_s) = refs
        vf_out_ref = None
    else:
        (z_ref, mu_ref, vec_ref, wl_ref, y_ref, vf_out_ref,
         s_ref, carry_ref, r_s, k_s, v_s, kk_s, a_s, lc_s, lw_s, bon_s, g_s) = refs
    tb = z_ref.shape[0]
    n_chunks = tb // CHUNK
    j = pl.program_id(1)

    @pl.when(j == 0)
    def _():
        s_ref[...] = jnp.zeros_like(s_ref)
        carry_ref[...] = jnp.zeros_like(carry_ref)

    z = z_ref[...]
    row = lax.broadcasted_iota(I32, z.shape, 0)
    z_prev = jnp.where(row == 0, carry_ref[0:1, :], pltpu.roll(z, 1, axis=0))
    carry_ref[0:1, :] = z[tb - 1:tb, :]
    zs = z + (z_prev - z) * mu_ref[...]

    vec = vec_ref[...]

    def vrow(i):
        return vec[i:i + 1, :]

    r = zs[:, 0:d_r]
    k = zs[:, d_r:2 * d_r]
    v = zs[:, 2 * d_r:3 * d_r]
    lo = zs[:, 3 * d_r:]
    n_lo = lo.shape[1]
    lane = lax.broadcasted_iota(I32, lo.shape, 1)
    lo_act = jnp.where(lane < n_lo // 4, jnp.tanh(lo),
                       jnp.where(lane < n_lo // 2, lo, jax.nn.sigmoid(lo)))
    lo_out = _mm(lo_act, wl_ref[...])

    if has_vres:
        vv = _mm(_mm(v, v1_ref[...]), v2_ref[...])
        v = v + (vf_ref[...] - v) * jax.nn.sigmoid(vrow(_V_V0) + vv)
    else:
        vf_out_ref[...] = v

    w_log = -jax.nn.softplus(-(vrow(_V_W0) + lo_out[:, 0:d_r])) - 0.5
    lw = -jnp.exp(w_log)
    a = jax.nn.sigmoid(vrow(_V_A0) + lo_out[:, d_r:2 * d_r])
    g = lo_out[:, 2 * d_r:3 * d_r]

    hi_ = lax.broadcasted_iota(I32, (d_r, d_r), 0) // HEAD_DIM
    hj_ = lax.broadcasted_iota(I32, (d_r, d_r), 1) // HEAD_DIM
    head_ones = jnp.where(hi_ == hj_, 1.0, 0.0).astype(BF16)

    kk = k * vrow(_V_KK)
    kk = kk * lax.rsqrt(jnp.maximum(_mm_exact_rhs(kk * kk, head_ones), 1e-24))
    k2 = k * (1.0 + (a - 1.0) * vrow(_V_KA))
    bonus = _mm_exact_rhs(r * k2 * vrow(_V_RK), head_ones) * v

    ti = lax.broadcasted_iota(I32, (tb, tb), 0)
    tj = lax.broadcasted_iota(I32, (tb, tb), 1)
    tril = jnp.where((ti // CHUNK == tj // CHUNK) & (tj <= ti), 1.0, 0.0).astype(BF16)
    lc = _mm_exact_lhs(tril, lw)

    r_s[...] = r
    k_s[...] = k2
    v_s[...] = v
    kk_s[...] = kk
    a_s[...] = a
    lc_s[...] = lc
    lw_s[...] = lw
    bon_s[...] = bonus
    g_s[...] = g

    ci = lax.broadcasted_iota(I32, (CHUNK, CHUNK), 0)
    cj = lax.broadcasted_iota(I32, (CHUNK, CHUNK), 1)
    strict = cj < ci
    lower = cj <= ci
    eye = ci == cj
    eye_f = jnp.where(eye, 1.0, 0.0)
    ln_w = vrow(_V_LNW)
    ln_b = vrow(_V_LNB)

    def chunk_body(c, carry):
        r0 = pl.multiple_of(c * CHUNK, CHUNK)
        rs = pl.ds(r0, CHUNK)
        lc_c = lc_s[rs, :]
        lw_c = lw_s[rs, :]
        l_end = lc_s[pl.ds(r0 + CHUNK - 1, 1), :]
        p_in = jnp.exp(lc_c)
        p_prev = jnp.exp(lc_c - lw_c)
        p_inv = jnp.exp(-lc_c)
        p_end = jnp.exp(l_end - lc_c)
        p_last = jnp.exp(l_end)
        kk_c = kk_s[rs, :]
        b_c = kk_c * a_s[rs, :]
        k_c = k_s[rs, :]
        at = (-kk_c * p_prev).astype(BF16)
        bt = (b_c * p_inv).astype(BF16)
        bh = (b_c * p_end).astype(BF16)
        kt = (k_c * p_inv).astype(BF16)
        kh = (k_c * p_end).astype(BF16)
        rt = (r_s[rs, :] * p_in).astype(BF16)
        vc = v_s[rs, :].astype(BF16)
        y_heads = []
        for h in range(n_heads):
            sl = slice(h * HEAD_DIM, (h + 1) * HEAD_DIM)
            at_h, bt_h, bh_h, kt_h, kh_h, rt_h, v_h = (x[:, sl] for x in (at, bt, bh, kt, kh, rt, vc))
            ar_h = jnp.concatenate([at_h, rt_h], axis=0)
            m_b = _mm_nt(ar_h, bt_h)
            m_k = _mm_nt(ar_h, kt_h)
            n_ab = jnp.where(strict, m_b[:CHUNK], 0.0)
            a_ak = jnp.where(strict, m_k[:CHUNK], 0.0)
            a_rb = jnp.where(lower, m_b[CHUNK:], 0.0)
            a_rk = jnp.where(lower, m_k[CHUNK:], 0.0)
            x_inv = eye_f + n_ab
            pw = _mm(n_ab, n_ab)
            n_sq = CHUNK.bit_length() - 2
            for it in range(n_sq):
                if it < n_sq - 1:
                    st = _mm(jnp.concatenate([x_inv, pw], axis=0), pw)
                    x_inv = x_inv + st[:CHUNK]
                    pw = st[CHUNK:]
                else:
                    x_inv = x_inv + _mm(x_inv, pw)
            w_h = _mm(x_inv, at_h)
            u0 = _mm(x_inv, _mm(a_ak, v_h))
            y0 = _mm(a_rk, v_h) + _mm(a_rb, u0)
            r_p = rt_h.astype(F32) + _mm(a_rb, w_h)
            g_h = jnp.where(eye, p_last[:, sl], 0.0) + _mm_tn(w_h, bh_h)
            d_h = _mm_tn(u0, bh_h) + _mm_tn(v_h, kh_h)
            s_h = s_ref[h]
            y_h = y0 + _mm_nt(r_p, s_h)
            s_ref[h] = _mm(s_h, g_h) + d_h
            mean = jnp.mean(y_h, axis=-1, keepdims=True)
            yc = y_h - mean
            var = jnp.mean(yc * yc, axis=-1, keepdims=True)
            y_heads.append(yc * lax.rsqrt(var + GN_EPS))
        y_n = jnp.concatenate(y_heads, axis=-1)
        out = (y_n * ln_w + ln_b + bon_s[rs, :]) * g_s[rs, :]
        y_ref[rs, :] = out.astype(y_ref.dtype)
        return carry

    lax.fori_loop(0, n_chunks, chunk_body, 0)


def _rwkv(z, v_first, mu, vec, wl, v1, v2, *, batch, seq, n_heads):
    t, n_z = z.shape
    d_r = n_heads * HEAD_DIM
    tb = min(TB_RWKV, seq)
    nb = seq // tb
    has_vres = v_first is not None
    tok = lambda b, j: (b * nb + j, 0)
    const = lambda b, j: (0, 0)
    in_specs = [pl.BlockSpec((tb, n_z), tok)]
    args = [z]
    if has_vres:
        in_specs.append(pl.BlockSpec((tb, d_r), tok))
        args.append(v_first)
    in_specs += [pl.BlockSpec(mu.shape, const), pl.BlockSpec(vec.shape, const),
                 pl.BlockSpec(wl.shape, const)]
    args += [mu, vec, wl]
    if has_vres:
        in_specs += [pl.BlockSpec(v1.shape, const), pl.BlockSpec(v2.shape, const)]
        args += [v1, v2]
        out_shape = jax.ShapeDtypeStruct((t, d_r), BF16)
        out_specs = pl.BlockSpec((tb, d_r), tok)
    else:
        out_shape = (jax.ShapeDtypeStruct((t, d_r), BF16), jax.ShapeDtypeStruct((t, d_r), F32))
        out_specs = (pl.BlockSpec((tb, d_r), tok), pl.BlockSpec((tb, d_r), tok))
    scratch = [pltpu.VMEM((n_heads, HEAD_DIM, HEAD_DIM), F32),
               pltpu.VMEM((V7X_SUBLANES, n_z), F32)]
    scratch += [pltpu.VMEM((tb, d_r), F32) for _ in range(9)]
    return pl.pallas_call(
        functools.partial(_rwkv_kernel, has_vres=has_vres, n_heads=n_heads, d_r=d_r),
        out_shape=out_shape,
        grid=(batch, nb),
        in_specs=in_specs,
        out_specs=out_specs,
        scratch_shapes=scratch,
        compiler_params=_params("arbitrary", "arbitrary"),
        name="rwkv_vres" if has_vres else "rwkv",
    )(*args)


def _attn_kernel(*refs, n_heads, n_parts):
    q_ref = refs[0]
    k_refs = refs[1:1 + n_parts]
    v_refs = refs[1 + n_parts:1 + 2 * n_parts]
    tab_ref = refs[1 + 2 * n_parts]
    o_ref = refs[2 + 2 * n_parts]
    qb = q_ref.shape[0]
    j = pl.program_id(1)
    scale = HEAD_DIM ** -0.5
    q = q_ref[...] * jnp.asarray(scale, q_ref.dtype)
    ks = [r[...] for r in k_refs]
    vs = [r[...] for r in v_refs]
    outs = []
    for h in range(n_heads):
        sl = slice(h * HEAD_DIM, (h + 1) * HEAD_DIM)
        q_h = q[:, sl]
        s_parts = []
        for p in range(n_parts):
            s = _mm_nt(q_h, ks[p][:, sl]) + tab_ref[h, :, p * qb:(p + 1) * qb]
            back = n_parts - 1 - p
            if back > 0:
                s = jnp.where(j >= back, s, NEG_INF)
            s_parts.append(s)
        m = s_parts[0].max(axis=-1, keepdims=True)
        for s in s_parts[1:]:
            m = jnp.maximum(m, s.max(axis=-1, keepdims=True))
        l = jnp.zeros_like(m)
        acc = jnp.zeros((qb, HEAD_DIM), F32)
        for p in range(n_parts):
            e = jnp.exp(s_parts[p] - m)
            l = l + e.sum(axis=-1, keepdims=True)
            acc = acc + _mm(e, vs[p][:, sl])
        outs.append(acc / l)
    o_ref[...] = jnp.concatenate(outs, axis=-1).astype(o_ref.dtype)


def _attn(qkv, table, *, batch, seq, n_heads):
    t = qkv.shape[0]
    d_a = n_heads * HEAD_DIM
    qb = min(QB_ATTN, seq)
    left = LEFT_CHUNKS * CHUNK
    assert left % qb == 0 and seq % qb == 0
    n_parts = left // qb + 1
    nb = seq // qb
    in_specs = [pl.BlockSpec((qb, d_a), lambda b, j: (b * nb + j, 0))]
    for p in range(n_parts):
        back = n_parts - 1 - p
        in_specs.append(pl.BlockSpec((qb, d_a), lambda b, j, back=back: (b * nb + jnp.maximum(j - back, 0), 1)))
    for p in range(n_parts):
        back = n_parts - 1 - p
        in_specs.append(pl.BlockSpec((qb, d_a), lambda b, j, back=back: (b * nb + jnp.maximum(j - back, 0), 2)))
    in_specs.append(pl.BlockSpec(table.shape, lambda b, j: (0, 0, 0)))
    return pl.pallas_call(
        functools.partial(_attn_kernel, n_heads=n_heads, n_parts=n_parts),
        out_shape=jax.ShapeDtypeStruct((t, d_a), BF16),
        grid=(batch, nb),
        in_specs=in_specs,
        out_specs=pl.BlockSpec((qb, d_a), lambda b, j: (b * nb + j, 0)),
        compiler_params=_params("parallel", "arbitrary"),
        name="attn",
    )(*([qkv] * (1 + 2 * n_parts)), table)


def _attn_table(rel_bias, qb):
    left = LEFT_CHUNKS * CHUNK
    qi = jnp.arange(qb)[:, None]
    ki = jnp.arange(left + qb)[None, :]
    rel = left + qi - ki
    idx = jnp.clip(rel, -(CHUNK - 1), MAX_REL) + (CHUNK - 1)
    cq = qi // CHUNK
    ck = ki // CHUNK
    valid = (ck >= cq) & (ck <= cq + LEFT_CHUNKS)
    return jnp.where(valid[None], rel_bias[:, idx].astype(F32), NEG_INF)


_R_E1, _R_E2, _R_C1, _R_C2, _R_RANK1, _R_RANK2 = range(6)


def _outproj_route_kernel(yr_ref, ya_ref, h_ref, wor_ref, woa_ref, nw_ref, wrt_ref, brt_ref,
                          h1_ref, hn_ref, route_ref, cnt_ref, carry_ref):
    i = pl.program_id(0)

    @pl.when(i == 0)
    def _():
        carry_ref[...] = jnp.zeros_like(carry_ref)

    h1 = (h_ref[...] + jnp.dot(yr_ref[...], wor_ref[...], preferred_element_type=F32)
          + jnp.dot(ya_ref[...], woa_ref[...], preferred_element_type=F32))
    h1_ref[...] = h1
    hn = _rms(h1, nw_ref[...])
    _to_token_tiles(hn_ref, hn)
    logits = jnp.dot(hn, wrt_ref[...], preferred_element_type=F32,
                     precision=lax.Precision.HIGHEST) + brt_ref[...]
    tm, nl = logits.shape
    lane = lax.broadcasted_iota(I32, (tm, nl), 1)
    lane_f = lane.astype(F32)
    ninf = -jnp.inf
    big = float(nl)
    is_g = lane < N_GROUPS
    gl = jnp.where(is_g, logits, ninf)
    g_max = gl.max(axis=-1, keepdims=True)
    g_sel = jnp.where(gl == g_max, lane_f, big).min(axis=-1, keepdims=True)
    p_g = 1.0 / jnp.where(is_g, jnp.exp(logits - g_max), 0.0).sum(axis=-1, keepdims=True)
    e_lo = N_GROUPS + EXPERTS_PER_GROUP * g_sel
    in_grp = (lane_f >= e_lo) & (lane_f < e_lo + EXPERTS_PER_GROUP)
    el = jnp.where(in_grp, logits, ninf)
    m1 = el.max(axis=-1, keepdims=True)
    i1 = jnp.where(el == m1, lane_f, big).min(axis=-1, keepdims=True)
    el2 = jnp.where(lane_f == i1, ninf, el)
    m2 = el2.max(axis=-1, keepdims=True)
    i2 = jnp.where(el2 == m2, lane_f, big).min(axis=-1, keepdims=True)
    t2 = jnp.exp(m2 - m1)
    c1 = p_g / (1.0 + t2)
    c2 = p_g * t2 / (1.0 + t2)
    e1 = i1 - N_GROUPS
    e2 = i2 - N_GROUPS
    oh1 = lane_f == e1
    oh2 = lane_f == e2
    ohs = jnp.where(oh1 | oh2, 1.0, 0.0)
    ri = lax.broadcasted_iota(I32, (tm, tm), 0)
    rj = lax.broadcasted_iota(I32, (tm, tm), 1)
    before = jnp.where(rj < ri, 1.0, 0.0).astype(BF16)
    cnt = jnp.dot(before, ohs.astype(BF16), preferred_element_type=F32) + carry_ref[0:1, :]
    rank1 = jnp.where(oh1, cnt, 0.0).sum(axis=-1, keepdims=True)
    rank2 = jnp.where(oh2, cnt, 0.0).sum(axis=-1, keepdims=True)
    new_carry = carry_ref[0:1, :] + ohs.sum(axis=0, keepdims=True)
    carry_ref[0:1, :] = new_carry
    cnt_ref[...] = jnp.broadcast_to(new_carry, cnt_ref.shape)
    route = jnp.zeros((tm, nl), F32)
    for idx, val in ((_R_E1, e1), (_R_E2, e2), (_R_C1, c1), (_R_C2, c2),
                     (_R_RANK1, rank1), (_R_RANK2, rank2)):
        route = jnp.where(lane == idx, val, route)
    route_ref[...] = route


def _outproj_route(yr, ya, h, wor, woa, nw, wrt, brt):
    t, d = h.shape
    tm = min(TM_ROUTE, t)
    d_r, d_a = yr.shape[1], ya.shape[1]
    nl = wrt.shape[1]
    tok = lambda i: (i, 0)
    const = lambda i: (0, 0)
    return pl.pallas_call(
        _outproj_route_kernel,
        out_shape=(jax.ShapeDtypeStruct((t, d), F32),
                   jax.ShapeDtypeStruct((t * V7X_SUBLANES, V7X_LANES), F32),
                   jax.ShapeDtypeStruct((t, nl), F32), jax.ShapeDtypeStruct((V7X_SUBLANES, nl), F32)),
        grid=(t // tm,),
        in_specs=[pl.BlockSpec((tm, d_r), tok), pl.BlockSpec((tm, d_a), tok), pl.BlockSpec((tm, d), tok),
                  pl.BlockSpec((d_r, d), const), pl.BlockSpec((d_a, d), const),
                  pl.BlockSpec((1, d), const), pl.BlockSpec((d, nl), const), pl.BlockSpec((1, nl), const)],
        out_specs=(pl.BlockSpec((tm, d), tok), pl.BlockSpec((tm * V7X_SUBLANES, V7X_LANES), tok),
                   pl.BlockSpec((tm, nl), tok), pl.BlockSpec((V7X_SUBLANES, nl), const)),
        scratch_shapes=[pltpu.VMEM((V7X_SUBLANES, nl), F32)],
        compiler_params=_params("arbitrary"),
        name="outproj_route",
    )(yr, ya, h, wor, woa, nw, wrt, brt)


def _load_indices(pos_hbm, i, idx_smem, sem):
    cp = pltpu.make_async_copy(pos_hbm.at[i], idx_smem, sem)
    cp.start()
    cp.wait()


def _dispatch_kernel(pos_hbm, free_hbm, x_hbm, xs_hbm, idx_smem, free_smem, zero_vmem,
                     idx_sem, row_sem, zero_sem, *, tm, n_free):
    i = pl.program_id(0)
    _load_indices(pos_hbm, i, idx_smem, idx_sem)
    base = i * tm

    def row_copy(src_row, dst_row):
        return pltpu.make_async_copy(_token_tile(x_hbm, src_row), _token_tile(xs_hbm, dst_row), row_sem)

    def issue(tt, carry):
        for s in range(2):
            n = 2 * tt + s
            row_copy(base + tt, idx_smem[n // V7X_LANES, n % V7X_LANES]).start()
        return carry

    lax.fori_loop(0, tm, issue, 0)

    _load_indices(free_hbm, i, free_smem, idx_sem)
    zero_vmem[...] = jnp.zeros_like(zero_vmem)

    def zero_copy(dst_row):
        return pltpu.make_async_copy(zero_vmem, _token_tile(xs_hbm, dst_row), zero_sem)

    def issue_zero(n, carry):
        zero_copy(free_smem[n // V7X_LANES, n % V7X_LANES]).start()
        return carry

    lax.fori_loop(0, n_free, issue_zero, 0)

    def drain(n, carry):
        row_copy(0, 0).wait()
        return carry

    lax.fori_loop(0, 2 * tm, drain, 0)

    def drain_zero(n, carry):
        zero_copy(0).wait()
        return carry

    lax.fori_loop(0, n_free, drain_zero, 0)


def _dispatch(x, pos_tiles, free_tiles, n_rows, tm):
    t = x.shape[0] // V7X_SUBLANES
    nb = t // tm
    n_free = free_tiles.shape[1] * free_tiles.shape[2]
    return pl.pallas_call(
        functools.partial(_dispatch_kernel, tm=tm, n_free=n_free),
        out_shape=jax.ShapeDtypeStruct((n_rows * V7X_SUBLANES, V7X_LANES), x.dtype),
        grid=(nb,),
        in_specs=[pl.BlockSpec(memory_space=pl.ANY), pl.BlockSpec(memory_space=pl.ANY),
                  pl.BlockSpec(memory_space=pl.ANY)],
        out_specs=pl.BlockSpec(memory_space=pl.ANY),
        scratch_shapes=[pltpu.SMEM(pos_tiles.shape[1:], I32), pltpu.SMEM(free_tiles.shape[1:], I32),
                        pltpu.VMEM((V7X_SUBLANES, V7X_LANES), x.dtype),
                        pltpu.SemaphoreType.DMA, pltpu.SemaphoreType.DMA, pltpu.SemaphoreType.DMA],
        compiler_params=_params("arbitrary"),
        name="dispatch",
    )(pos_tiles, free_tiles, x)


def _experts_kernel(te_ref, tv_ref, x_ref, w1_ref, w3_ref, w2_ref, y_ref, *, tm):
    i = pl.program_id(0)

    @pl.when(tv_ref[i] > 0)
    def _():
        x = _from_token_tiles(x_ref, tm).astype(BF16)
        h_gate = jnp.dot(x, w1_ref[...], preferred_element_type=F32)
        h_up = jnp.dot(x, w3_ref[...], preferred_element_type=F32)
        hid = (h_gate * jax.nn.sigmoid(h_gate) * h_up).astype(BF16)
        _to_token_tiles(y_ref, jnp.dot(hid, w2_ref[...], preferred_element_type=F32))

    @pl.when(tv_ref[i] == 0)
    def _():
        y_ref[...] = jnp.zeros_like(y_ref)


def _experts(xs, w1, w3, w2, tile_expert, tile_valid):
    n_rows = xs.shape[0] // V7X_SUBLANES
    tm = TM_EXPERT
    nt = n_rows // tm
    d, f = w1.shape[1:]
    tile_spec = pl.BlockSpec((tm * V7X_SUBLANES, V7X_LANES), lambda i, te, tv: (i, 0))
    grid_spec = pltpu.PrefetchScalarGridSpec(
        num_scalar_prefetch=2,
        grid=(nt,),
        in_specs=[tile_spec,
                  pl.BlockSpec((None, d, f), lambda i, te, tv: (te[i], 0, 0)),
                  pl.BlockSpec((None, d, f), lambda i, te, tv: (te[i], 0, 0)),
                  pl.BlockSpec((None, f, d), lambda i, te, tv: (te[i], 0, 0))],
        out_specs=tile_spec,
    )
    return pl.pallas_call(
        functools.partial(_experts_kernel, tm=tm),
        out_shape=jax.ShapeDtypeStruct(xs.shape, F32),
        grid_spec=grid_spec,
        compiler_params=_params("arbitrary"),
        name="experts",
    )(tile_expert, tile_valid, xs, w1, w3, w2)


def _combine_ple_kernel(pos_hbm, ys_hbm, h_ref, route_ref, p_ref, nw_ref, wg_ref, bg_ref, wp_ref,
                        fw_ref, o_ref, idx_smem, ybuf, idx_sem, row_sem, *, tm, final):
    i = pl.program_id(0)
    _load_indices(pos_hbm, i, idx_smem, idx_sem)

    def row_copy(src_row, slot, dst_row):
        return pltpu.make_async_copy(_token_tile(ys_hbm, src_row), _token_tile(ybuf.at[slot], dst_row),
                                     row_sem)

    def issue(tt, carry):
        for s in range(2):
            n = 2 * tt + s
            row_copy(idx_smem[n // V7X_LANES, n % V7X_LANES], s, tt).start()
        return carry

    lax.fori_loop(0, tm, issue, 0)

    def drain(n, carry):
        row_copy(0, 0, 0).wait()
        return carry

    lax.fori_loop(0, 2 * tm, drain, 0)

    route = route_ref[...]
    c1 = route[:, _R_C1:_R_C1 + 1]
    c2 = route[:, _R_C2:_R_C2 + 1]
    h2 = (h_ref[...] + c1 * _from_token_tiles(ybuf.at[0], tm)
          + c2 * _from_token_tiles(ybuf.at[1], tm))
    hn = _rms(h2, nw_ref[...]).astype(BF16)
    gate = jax.nn.sigmoid(jnp.dot(hn, wg_ref[...], preferred_element_type=F32) + bg_ref[...])
    h3 = h2 + gate * jnp.dot(p_ref[...].astype(BF16), wp_ref[...], preferred_element_type=F32)
    if final:
        h3 = _rms(h3, fw_ref[...])
    o_ref[...] = h3


def _combine_ple(pos_tiles, ys, h, route, p, nw, wg, bg, wp, fw, *, tm, final):
    t, d = h.shape
    nl = route.shape[1]
    dp = p.shape[1]
    tok = lambda i: (i, 0)
    const = lambda i: (0, 0)
    return pl.pallas_call(
        functools.partial(_combine_ple_kernel, tm=tm, final=final),
        out_shape=jax.ShapeDtypeStruct((t, d), F32),
        grid=(t // tm,),
        in_specs=[pl.BlockSpec(memory_space=pl.ANY), pl.BlockSpec(memory_space=pl.ANY),
                  pl.BlockSpec((tm, d), tok), pl.BlockSpec((tm, nl), tok), pl.BlockSpec((tm, dp), tok),
                  pl.BlockSpec((1, d), const), pl.BlockSpec((d, d), const), pl.BlockSpec((1, d), const),
                  pl.BlockSpec((dp, d), const), pl.BlockSpec((1, d), const)],
        out_specs=pl.BlockSpec((tm, d), tok),
        scratch_shapes=[pltpu.SMEM(pos_tiles.shape[1:], I32), pltpu.VMEM((2, tm * V7X_SUBLANES, V7X_LANES), F32),
                        pltpu.SemaphoreType.DMA, pltpu.SemaphoreType.DMA],
        compiler_params=_params("arbitrary"),
        name="combine_ple_final" if final else "combine_ple",
    )(pos_tiles, ys, h, route, p, nw, wg, bg, wp, fw)


def _pos_tiles(pos1, pos2, tm):
    t = pos1.shape[0]
    flat = jnp.stack([pos1, pos2], axis=-1).reshape(t // tm, 2 * tm)
    return flat.reshape(t // tm, (2 * tm) // V7X_LANES, V7X_LANES)


def kernel(x, p, norm_mix_w, w_in, rwkv_mu, rwkv_w0, rwkv_w2, rwkv_a0, rwkv_a2, rwkv_g2, rwkv_k_k, rwkv_k_a, rwkv_r_k, rwkv_ln_w, rwkv_ln_b, rwkv_v0, rwkv_v1, rwkv_v2, att_rel_bias, w_out, norm_ffn_w, router_group_w, router_group_b, router_expert_w, router_expert_b, expert_w1, expert_w3, expert_w2, norm_ple_w, ple_gate_w, ple_gate_b, ple_proj_w, final_norm_w):
    batch, seq, d = x.shape
    depth = w_in.shape[0]
    t = batch * seq
    d_r = rwkv_w0.shape[1]
    n_heads_r = d_r // HEAD_DIM
    n_rwkv_in = rwkv_mu.shape[1]
    d_a = (w_in.shape[2] - n_rwkv_in) // 3
    n_heads_a = d_a // HEAD_DIM
    n_dec, n_iclr, n_gate = rwkv_w2.shape[1], rwkv_a2.shape[1], rwkv_g2.shape[1]
    assert n_dec == n_iclr and n_gate == n_dec + n_iclr
    n_lo = n_dec + n_iclr + n_gate
    f_exp = expert_w1.shape[-1]
    assert d == V7X_SUBLANES * V7X_LANES
    n_rows = 2 * t + N_EXPERTS * TM_EXPERT
    n_tiles = n_rows // TM_EXPERT
    qb = min(QB_ATTN, seq)

    h = x.reshape(t, d)
    v_first = None
    for i in range(depth):
        wr = w_in[i, :, :n_rwkv_in].astype(BF16)
        wa = w_in[i, :, n_rwkv_in:].astype(BF16)
        wl = jnp.zeros((n_lo, 3 * d_r), F32)
        wl = wl.at[:n_dec, :d_r].set(rwkv_w2[i])
        wl = wl.at[n_dec:n_dec + n_iclr, d_r:2 * d_r].set(rwkv_a2[i])
        wl = wl.at[n_dec + n_iclr:, 2 * d_r:].set(rwkv_g2[i]).astype(BF16)
        v0 = rwkv_v0[i - 1] if i > 0 else jnp.zeros((d_r,), F32)
        vec = jnp.stack([rwkv_w0[i], rwkv_a0[i], rwkv_k_k[i], rwkv_k_a[i], rwkv_r_k[i],
                         rwkv_ln_w[i], rwkv_ln_b[i], v0])
        if i > 0:
            n_vr = rwkv_v1.shape[2]
            v1 = jnp.zeros((d_r, V7X_LANES), F32).at[:, :n_vr].set(rwkv_v1[i - 1]).astype(BF16)
            v2 = jnp.zeros((V7X_LANES, d_r), F32).at[:n_vr, :].set(rwkv_v2[i - 1]).astype(BF16)
        else:
            v1 = v2 = None
        table = _attn_table(att_rel_bias[i], qb)
        wor = w_out[i, :d_r].astype(BF16)
        woa = w_out[i, d_r:].astype(BF16)
        n_rt = N_GROUPS + N_EXPERTS
        wrt = jnp.zeros((d, V7X_LANES), F32)
        wrt = wrt.at[:, :N_GROUPS].set(router_group_w[i]).at[:, N_GROUPS:n_rt].set(router_expert_w[i])
        brt = jnp.zeros((1, V7X_LANES), F32)
        brt = brt.at[0, :N_GROUPS].set(router_group_b[i]).at[0, N_GROUPS:n_rt].set(router_expert_b[i])
        w1 = expert_w1[i].reshape(N_EXPERTS, d, f_exp).astype(BF16)
        w3 = expert_w3[i].reshape(N_EXPERTS, d, f_exp).astype(BF16)
        w2 = expert_w2[i].reshape(N_EXPERTS, f_exp, d).astype(BF16)

        z_r, qkv = _norm_proj(h, norm_mix_w[i][None], wr, wa)
        if i == 0:
            y_r, v_first = _rwkv(z_r, None, rwkv_mu[i][None], vec, wl, None, None,
                                 batch=batch, seq=seq, n_heads=n_heads_r)
        else:
            y_r = _rwkv(z_r, v_first, rwkv_mu[i][None], vec, wl, v1, v2,
                        batch=batch, seq=seq, n_heads=n_heads_r)
        y_a = _attn(qkv, table, batch=batch, seq=seq, n_heads=n_heads_a)

        h1, hn, route, cnt = _outproj_route(y_r, y_a, h, wor, woa, norm_ffn_w[i][None], wrt, brt)
        e1 = route[:, _R_E1].astype(I32)
        e2 = route[:, _R_E2].astype(I32)
        counts = cnt[0, :N_EXPERTS].astype(I32)
        padded = ((counts + TM_EXPERT - 1) // TM_EXPERT) * TM_EXPERT
        p_end = jnp.cumsum(padded)
        p_start = p_end - padded
        pos1 = p_start[e1] + route[:, _R_RANK1].astype(I32)
        pos2 = p_start[e2] + route[:, _R_RANK2].astype(I32)
        tile_start = jnp.arange(n_tiles, dtype=I32) * TM_EXPERT
        tile_expert = jnp.minimum(jnp.searchsorted(p_end, tile_start, side="right"),
                                  N_EXPERTS - 1).astype(I32)
        tile_valid = (tile_start < p_end[-1]).astype(I32)

        n_free = n_rows - 2 * t
        f_end = jnp.cumsum(padded - counts)
        kf = jnp.arange(n_free, dtype=I32)
        ef = jnp.minimum(jnp.searchsorted(f_end, kf, side="right"), N_EXPERTS - 1).astype(I32)
        in_pad = kf < f_end[-1]
        pad_row = p_start[ef] + counts[ef] + (kf - (f_end[ef] - (padded - counts)[ef]))
        free_rows = jnp.where(in_pad, pad_row, p_end[-1] + (kf - f_end[-1])).astype(I32)

        tm_d = min(TM_DISPATCH, t)
        nb_d = t // tm_d
        assert n_free % (nb_d * V7X_LANES) == 0
        free_tiles = free_rows.reshape(nb_d, n_free // (nb_d * V7X_LANES), V7X_LANES)
        xs = _dispatch(hn, _pos_tiles(pos1, pos2, tm_d), free_tiles, n_rows, tm_d)
        ys = _experts(xs, w1, w3, w2, tile_expert, tile_valid)

        tm_c = min(TM_COMBINE, t)
        h = _combine_ple(_pos_tiles(pos1, pos2, tm_c), ys, h1, route, p[i].reshape(t, -1),
                         norm_ple_w[i][None], ple_gate_w[i].astype(BF16), ple_gate_b[i][None],
                         ple_proj_w[i].astype(BF16), final_norm_w[None],
                         tm=tm_c, final=(i == depth - 1))
    return h.reshape(batch, seq, d)
```

```python
import functools

import jax
import jax.numpy as jnp
from jax import lax
from jax.experimental import pallas as pl
from jax.experimental.pallas import tpu as pltpu

F32 = jnp.float32
BF16 = jnp.bfloat16
I32 = jnp.int32

CHUNK = 64
HEAD_DIM = 64
LEFT_CHUNKS = 8
MAX_REL = 256
N_GROUPS = 4
EXPERTS_PER_GROUP = 8
N_EXPERTS = N_GROUPS * EXPERTS_PER_GROUP
RMS_EPS = 1e-6
GN_EPS = 64e-5
NEG_INF = -1e30

V7X_LANES = 128
V7X_SUBLANES = 8
V7X_VMEM_LIMIT_BYTES = 48 * 1024 * 1024

TM_PROJ = 512
TB_RWKV = 256
QB_ATTN = 256
TM_ROUTE = 256
TM_DISPATCH = 512
TM_EXPERT = 256
TM_COMBINE = 512


def _params(*sem):
    return pltpu.CompilerParams(dimension_semantics=sem, vmem_limit_bytes=V7X_VMEM_LIMIT_BYTES)


def _rms(x, w):
    return x * lax.rsqrt(jnp.mean(x * x, axis=-1, keepdims=True) + RMS_EPS) * w


def _mm(a, b):
    return jnp.dot(a.astype(BF16), b.astype(BF16), preferred_element_type=F32)


def _mm_nt(a, b):
    return lax.dot_general(a.astype(BF16), b.astype(BF16), (((1,), (1,)), ((), ())),
                           preferred_element_type=F32)


def _mm_tn(a, b):
    return lax.dot_general(a.astype(BF16), b.astype(BF16), (((0,), (0,)), ((), ())),
                           preferred_element_type=F32)


def _to_token_tiles(ref, x):
    m, d = x.shape
    for s in range(d // V7X_LANES):
        ref[pl.ds(s, m, stride=V7X_SUBLANES), :] = x[:, s * V7X_LANES:(s + 1) * V7X_LANES]


def _from_token_tiles(ref, m):
    return jnp.concatenate([ref[pl.ds(s, m, stride=V7X_SUBLANES), :] for s in range(V7X_SUBLANES)],
                           axis=-1)


def _token_tile(ref, row):
    return ref.at[pl.ds(pl.multiple_of(row * V7X_SUBLANES, V7X_SUBLANES), V7X_SUBLANES), :]


def _split3(x):
    hi = x.astype(BF16)
    r1 = x - hi.astype(F32)
    mid = r1.astype(BF16)
    lo = (r1 - mid.astype(F32)).astype(BF16)
    return hi, mid, lo


def _mm_exact_lhs(a_bf16, x):
    hi, mid, lo = _split3(x)
    return (jnp.dot(a_bf16, hi, preferred_element_type=F32)
            + jnp.dot(a_bf16, mid, preferred_element_type=F32)
            + jnp.dot(a_bf16, lo, preferred_element_type=F32))


def _mm_exact_rhs(x, b_bf16):
    hi, mid, lo = _split3(x)
    return (jnp.dot(hi, b_bf16, preferred_element_type=F32)
            + jnp.dot(mid, b_bf16, preferred_element_type=F32)
            + jnp.dot(lo, b_bf16, preferred_element_type=F32))


def _norm_proj_kernel(h_ref, nw_ref, wr_ref, wa_ref, zr_ref, qkv_ref):
    hn = _rms(h_ref[...], nw_ref[...]).astype(BF16)
    zr_ref[...] = jnp.dot(hn, wr_ref[...], preferred_element_type=F32)
    qkv_ref[...] = jnp.dot(hn, wa_ref[...], preferred_element_type=F32).astype(BF16)


def _norm_proj(h, nw, wr, wa):
    t, d = h.shape
    tm = min(TM_PROJ, t)
    n_r, n_a = wr.shape[1], wa.shape[1]
    return pl.pallas_call(
        _norm_proj_kernel,
        out_shape=(jax.ShapeDtypeStruct((t, n_r), F32), jax.ShapeDtypeStruct((t, n_a), BF16)),
        grid=(t // tm,),
        in_specs=[pl.BlockSpec((tm, d), lambda i: (i, 0)),
                  pl.BlockSpec((1, d), lambda i: (0, 0)),
                  pl.BlockSpec((d, n_r), lambda i: (0, 0)),
                  pl.BlockSpec((d, n_a), lambda i: (0, 0))],
        out_specs=(pl.BlockSpec((tm, n_r), lambda i: (i, 0)),
                   pl.BlockSpec((tm, n_a), lambda i: (i, 0))),
        compiler_params=_params("parallel"),
        name="norm_proj",
    )(h, nw, wr, wa)


_V_W0, _V_A0, _V_KK, _V_KA, _V_RK, _V_LNW, _V_LNB, _V_V0 = range(8)


def _rwkv_kernel(*refs, has_vres, n_heads, d_r):
    if has_vres:
        (z_ref, vf_ref, mu_ref, vec_ref, wl_ref, v1_ref, v2_ref, y_ref,
         s_ref, carry_ref, r_s, k_s, v_s, kk_s, a_s, lc_s, lw_s, bon_s, g_s) = refs
        vf_out_ref = None
    else:
        (z_ref, mu_ref, vec_ref, wl_ref, y_ref, vf_out_ref,
         s_ref, carry_ref, r_s, k_s, v_s, kk_s, a_s, lc_s, lw_s, bon_s, g_s) = refs
    tb = z_ref.shape[0]
    n_chunks = tb // CHUNK
    j = pl.program_id(1)

    @pl.when(j == 0)
    def _():
        s_ref[...] = jnp.zeros_like(s_ref)
        carry_ref[...] = jnp.zeros_like(carry_ref)

    z = z_ref[...]
    row = lax.broadcasted_iota(I32, z.shape, 0)
    z_prev = jnp.where(row == 0, carry_ref[0:1, :], pltpu.roll(z, 1, axis=0))
    carry_ref[0:1, :] = z[tb - 1:tb, :]
    zs = z + (z_prev - z) * mu_ref[...]

    vec = vec_ref[...]

    def vrow(i):
        return vec[i:i + 1, :]

    r = zs[:, 0:d_r]
    k = zs[:, d_r:2 * d_r]
    v = zs[:, 2 * d_r:3 * d_r]
    lo = zs[:, 3 * d_r:]
    n_lo = lo.shape[1]
    lane = lax.broadcasted_iota(I32, lo.shape, 1)
    lo_act = jnp.where(lane < n_lo // 4, jnp.tanh(lo),
                       jnp.where(lane < n_lo // 2, lo, jax.nn.sigmoid(lo)))
    lo_out = _mm(lo_act, wl_ref[...])

    if has_vres:
        vv = _mm(_mm(v, v1_ref[...]), v2_ref[...])
        v = v + (vf_ref[...] - v) * jax.nn.sigmoid(vrow(_V_V0) + vv)
    else:
        vf_out_ref[...] = v

    w_log = -jax.nn.softplus(-(vrow(_V_W0) + lo_out[:, 0:d_r])) - 0.5
    lw = -jnp.exp(w_log)
    a = jax.nn.sigmoid(vrow(_V_A0) + lo_out[:, d_r:2 * d_r])
    g = lo_out[:, 2 * d_r:3 * d_r]

    hi_ = lax.broadcasted_iota(I32, (d_r, d_r), 0) // HEAD_DIM
    hj_ = lax.broadcasted_iota(I32, (d_r, d_r), 1) // HEAD_DIM
    head_ones = jnp.where(hi_ == hj_, 1.0, 0.0).astype(BF16)

    kk = k * vrow(_V_KK)
    kk = kk * lax.rsqrt(jnp.maximum(_mm_exact_rhs(kk * kk, head_ones), 1e-24))
    k2 = k * (1.0 + (a - 1.0) * vrow(_V_KA))
    bonus = _mm_exact_rhs(r * k2 * vrow(_V_RK), head_ones) * v

    ti = lax.broadcasted_iota(I32, (tb, tb), 0)
    tj = lax.broadcasted_iota(I32, (tb, tb), 1)
    tril = jnp.where((ti // CHUNK == tj // CHUNK) & (tj <= ti), 1.0, 0.0).astype(BF16)
    lc = _mm_exact_lhs(tril, lw)

    r_s[...] = r
    k_s[...] = k2
    v_s[...] = v
    kk_s[...] = kk
    a_s[...] = a
    lc_s[...] = lc
    lw_s[...] = lw
    bon_s[...] = bonus
    g_s[...] = g

    ci = lax.broadcasted_iota(I32, (CHUNK, CHUNK), 0)
    cj = lax.broadcasted_iota(I32, (CHUNK, CHUNK), 1)
    strict = cj < ci
    lower = cj <= ci
    eye = ci == cj
    eye_f = jnp.where(eye, 1.0, 0.0)
    ln_w = vrow(_V_LNW)
    ln_b = vrow(_V_LNB)

    def chunk_body(c, carry):
        r0 = pl.multiple_of(c * CHUNK, CHUNK)
        rs = pl.ds(r0, CHUNK)
        lc_c = lc_s[rs, :]
        lw_c = lw_s[rs, :]
        l_end = lc_s[pl.ds(r0 + CHUNK - 1, 1), :]
        p_in = jnp.exp(lc_c)
        p_prev = jnp.exp(lc_c - lw_c)
        p_inv = jnp.exp(-lc_c)
        p_end = jnp.exp(l_end - lc_c)
        p_last = jnp.exp(l_end)
        kk_c = kk_s[rs, :]
        b_c = kk_c * a_s[rs, :]
        k_c = k_s[rs, :]
        at = (-kk_c * p_prev).astype(BF16)
        bt = (b_c * p_inv).astype(BF16)
        bh = (b_c * p_end).astype(BF16)
        kt = (k_c * p_inv).astype(BF16)
        kh = (k_c * p_end).astype(BF16)
        rt = (r_s[rs, :] * p_in).astype(BF16)
        vc = v_s[rs, :].astype(BF16)
        heads = range(n_heads)
        hs = [slice(h * HEAD_DIM, (h + 1) * HEAD_DIM) for h in heads]
        at_h, bt_h, bh_h, kt_h, kh_h, rt_h, v_h = ([x[:, sl] for sl in hs]
                                                   for x in (at, bt, bh, kt, kh, rt, vc))
        ar_h = [jnp.concatenate([at_h[h], rt_h[h]], axis=0) for h in heads]
        m_b = [_mm_nt(ar_h[h], bt_h[h]) for h in heads]
        m_k = [_mm_nt(ar_h[h], kt_h[h]) for h in heads]
        n_ab = [jnp.where(strict, m_b[h][:CHUNK], 0.0) for h in heads]
        a_ak = [jnp.where(strict, m_k[h][:CHUNK], 0.0) for h in heads]
        a_rb = [jnp.where(lower, m_b[h][CHUNK:], 0.0) for h in heads]
        a_rk = [jnp.where(lower, m_k[h][CHUNK:], 0.0) for h in heads]
        x_inv = [eye_f + n_ab[h] for h in heads]
        pw = [_mm(n_ab[h], n_ab[h]) for h in heads]
        akv = [_mm(a_ak[h], v_h[h]) for h in heads]
        n_sq = CHUNK.bit_length() - 2
        for it in range(n_sq):
            if it < n_sq - 1:
                st = [_mm(jnp.concatenate([x_inv[h], pw[h]], axis=0), pw[h]) for h in heads]
                x_inv = [x_inv[h] + st[h][:CHUNK] for h in heads]
                pw = [st[h][CHUNK:] for h in heads]
            else:
                st = [_mm(x_inv[h], pw[h]) for h in heads]
                x_inv = [x_inv[h] + st[h] for h in heads]
        w_h = [_mm(x_inv[h], at_h[h]) for h in heads]
        u0 = [_mm(x_inv[h], akv[h]) for h in heads]
        y0 = [_mm(a_rk[h], v_h[h]) + _mm(a_rb[h], u0[h]) for h in heads]
        r_p = [rt_h[h].astype(F32) + _mm(a_rb[h], w_h[h]) for h in heads]
        g_h = [jnp.where(eye, p_last[:, hs[h]], 0.0) + _mm_tn(w_h[h], bh_h[h]) for h in heads]
        d_h = [_mm_tn(u0[h], bh_h[h]) + _mm_tn(v_h[h], kh_h[h]) for h in heads]
        s_h = [s_ref[h] for h in heads]
        y_h = [y0[h] + _mm_nt(r_p[h], s_h[h]) for h in heads]
        s_new = [_mm(s_h[h], g_h[h]) + d_h[h] for h in heads]
        for h in heads:
            s_ref[h] = s_new[h]
        y_heads = []
        for h in heads:
            mean = jnp.mean(y_h[h], axis=-1, keepdims=True)
            yc = y_h[h] - mean
            var = jnp.mean(yc * yc, axis=-1, keepdims=True)
            y_heads.append(yc * lax.rsqrt(var + GN_EPS))
        y_n = jnp.concatenate(y_heads, axis=-1)
        out = (y_n * ln_w + ln_b + bon_s[rs, :]) * g_s[rs, :]
        y_ref[rs, :] = out.astype(y_ref.dtype)
        return carry

    lax.fori_loop(0, n_chunks, chunk_body, 0)


def _rwkv(z, v_first, mu, vec, wl, v1, v2, *, batch, seq, n_heads):
    t, n_z = z.shape
    d_r = n_heads * HEAD_DIM
    tb = min(TB_RWKV, seq)
    nb = seq // tb
    has_vres = v_first is not None
    tok = lambda b, j: (b * nb + j, 0)
    const = lambda b, j: (0, 0)
    in_specs = [pl.BlockSpec((tb, n_z), tok)]
    args = [z]
    if has_vres:
        in_specs.append(pl.BlockSpec((tb, d_r), tok))
        args.append(v_first)
    in_specs += [pl.BlockSpec(mu.shape, const), pl.BlockSpec(vec.shape, const),
                 pl.BlockSpec(wl.shape, const)]
    args += [mu, vec, wl]
    if has_vres:
        in_specs += [pl.BlockSpec(v1.shape, const), pl.BlockSpec(v2.shape, const)]
        args += [v1, v2]
        out_shape = jax.ShapeDtypeStruct((t, d_r), BF16)
        out_specs = pl.BlockSpec((tb, d_r), tok)
    else:
        out_shape = (jax.ShapeDtypeStruct((t, d_r), BF16), jax.ShapeDtypeStruct((t, d_r), F32))
        out_specs = (pl.BlockSpec((tb, d_r), tok), pl.BlockSpec((tb, d_r), tok))
    scratch = [pltpu.VMEM((n_heads, HEAD_DIM, HEAD_DIM), F32),
               pltpu.VMEM((V7X_SUBLANES, n_z), F32)]
    scratch += [pltpu.VMEM((tb, d_r), F32) for _ in range(9)]
    return pl.pallas_call(
        functools.partial(_rwkv_kernel, has_vres=has_vres, n_heads=n_heads, d_r=d_r),
        out_shape=out_shape,
        grid=(batch, nb),
        in_specs=in_specs,
        out_specs=out_specs,
        scratch_shapes=scratch,
        compiler_params=_params("arbitrary", "arbitrary"),
        name="rwkv_vres" if has_vres else "rwkv",
    )(*args)


def _attn_kernel(*refs, n_heads, n_parts):
    q_ref = refs[0]
    k_refs = refs[1:1 + n_parts]
    v_refs = refs[1 + n_parts:1 + 2 * n_parts]
    tab_ref = refs[1 + 2 * n_parts]
    o_ref = refs[2 + 2 * n_parts]
    qb = q_ref.shape[0]
    j = pl.program_id(1)
    scale = HEAD_DIM ** -0.5
    q = q_ref[...] * jnp.asarray(scale, q_ref.dtype)
    ks = [r[...] for r in k_refs]
    vs = [r[...] for r in v_refs]
    outs = []
    for h in range(n_heads):
        sl = slice(h * HEAD_DIM, (h + 1) * HEAD_DIM)
        q_h = q[:, sl]
        s_parts = []
        for p in range(n_parts):
            s = _mm_nt(q_h, ks[p][:, sl]) + tab_ref[h, :, p * qb:(p + 1) * qb]
            back = n_parts - 1 - p
            if back > 0:
                s = jnp.where(j >= back, s, NEG_INF)
            s_parts.append(s)
        m = s_parts[0].max(axis=-1, keepdims=True)
        for s in s_parts[1:]:
            m = jnp.maximum(m, s.max(axis=-1, keepdims=True))
        l = jnp.zeros_like(m)
        acc = jnp.zeros((qb, HEAD_DIM), F32)
        for p in range(n_parts):
            e = jnp.exp(s_parts[p] - m)
            l = l + e.sum(axis=-1, keepdims=True)
            acc = acc + _mm(e, vs[p][:, sl])
        outs.append(acc / l)
    o_ref[...] = jnp.concatenate(outs, axis=-1).astype(o_ref.dtype)


def _attn(qkv, table, *, batch, seq, n_heads):
    t = qkv.shape[0]
    d_a = n_heads * HEAD_DIM
    qb = min(QB_ATTN, seq)
    left = LEFT_CHUNKS * CHUNK
    assert left % qb == 0 and seq % qb == 0
    n_parts = left // qb + 1
    nb = seq // qb
    in_specs = [pl.BlockSpec((qb, d_a), lambda b, j: (b * nb + j, 0))]
    for p in range(n_parts):
        back = n_parts - 1 - p
        in_specs.append(pl.BlockSpec((qb, d_a), lambda b, j, back=back: (b * nb + jnp.maximum(j - back, 0), 1)))
    for p in range(n_parts):
        back = n_parts - 1 - p
        in_specs.append(pl.BlockSpec((qb, d_a), lambda b, j, back=back: (b * nb + jnp.maximum(j - back, 0), 2)))
    in_specs.append(pl.BlockSpec(table.shape, lambda b, j: (0, 0, 0)))
    return pl.pallas_call(
        functools.partial(_attn_kernel, n_heads=n_heads, n_parts=n_parts),
        out_shape=jax.ShapeDtypeStruct((t, d_a), BF16),
        grid=(batch, nb),
        in_specs=in_specs,
        out_specs=pl.BlockSpec((qb, d_a), lambda b, j: (b * nb + j, 0)),
        compiler_params=_params("parallel", "arbitrary"),
        name="attn",
    )(*([qkv] * (1 + 2 * n_parts)), table)


def _attn_table(rel_bias, qb):
    left = LEFT_CHUNKS * CHUNK
    n_keys = left + qb
    period = qb + n_keys - 1
    n_heads = rel_bias.shape[0]
    rel = left + (qb - 1) - jnp.arange(period)
    g = rel_bias[:, jnp.clip(rel, -(CHUNK - 1), MAX_REL) + (CHUNK - 1)].astype(F32)
    flat = jnp.tile(g, (1, qb + 1))[:, :qb * (period + 1)]
    bias = flat.reshape(n_heads, qb, period + 1)[:, ::-1, :n_keys]
    cq = jnp.arange(qb)[:, None] // CHUNK
    ck = jnp.arange(n_keys)[None, :] // CHUNK
    valid = (ck >= cq) & (ck <= cq + LEFT_CHUNKS)
    return jnp.where(valid[None], bias, NEG_INF)


_R_E1, _R_E2, _R_C1, _R_C2, _R_RANK1, _R_RANK2 = range(6)


def _outproj_route_kernel(yr_ref, ya_ref, h_ref, wor_ref, woa_ref, nw_ref, wrt_ref, brt_ref,
                          h1_ref, hn_ref, route_ref, cnt_ref, carry_ref):
    i = pl.program_id(0)

    @pl.when(i == 0)
    def _():
        carry_ref[...] = jnp.zeros_like(carry_ref)

    h1 = (h_ref[...] + jnp.dot(yr_ref[...], wor_ref[...], preferred_element_type=F32)
          + jnp.dot(ya_ref[...], woa_ref[...], preferred_element_type=F32))
    h1_ref[...] = h1
    hn = _rms(h1, nw_ref[...])
    _to_token_tiles(hn_ref, hn)
    logits = jnp.dot(hn, wrt_ref[...], preferred_element_type=F32,
                     precision=lax.Precision.HIGHEST) + brt_ref[...]
    tm, nl = logits.shape
    lane = lax.broadcasted_iota(I32, (tm, nl), 1)
    lane_f = lane.astype(F32)
    ninf = -jnp.inf
    big = float(nl)
    is_g = lane < N_GROUPS
    gl = jnp.where(is_g, logits, ninf)
    g_max = gl.max(axis=-1, keepdims=True)
    g_sel = jnp.where(gl == g_max, lane_f, big).min(axis=-1, keepdims=True)
    p_g = 1.0 / jnp.where(is_g, jnp.exp(logits - g_max), 0.0).sum(axis=-1, keepdims=True)
    e_lo = N_GROUPS + EXPERTS_PER_GROUP * g_sel
    in_grp = (lane_f >= e_lo) & (lane_f < e_lo + EXPERTS_PER_GROUP)
    el = jnp.where(in_grp, logits, ninf)
    m1 = el.max(axis=-1, keepdims=True)
    i1 = jnp.where(el == m1, lane_f, big).min(axis=-1, keepdims=True)
    el2 = jnp.where(lane_f == i1, ninf, el)
    m2 = el2.max(axis=-1, keepdims=True)
    i2 = jnp.where(el2 == m2, lane_f, big).min(axis=-1, keepdims=True)
    t2 = jnp.exp(m2 - m1)
    c1 = p_g / (1.0 + t2)
    c2 = p_g * t2 / (1.0 + t2)
    e1 = i1 - N_GROUPS
    e2 = i2 - N_GROUPS
    oh1 = lane_f == e1
    oh2 = lane_f == e2
    ohs = jnp.where(oh1 | oh2, 1.0, 0.0)
    ri = lax.broadcasted_iota(I32, (tm, tm), 0)
    rj = lax.broadcasted_iota(I32, (tm, tm), 1)
    before = jnp.where(rj < ri, 1.0, 0.0).astype(BF16)
    cnt = jnp.dot(before, ohs.astype(BF16), preferred_element_type=F32) + carry_ref[0:1, :]
    rank1 = jnp.where(oh1, cnt, 0.0).sum(axis=-1, keepdims=True)
    rank2 = jnp.where(oh2, cnt, 0.0).sum(axis=-1, keepdims=True)
    new_carry = carry_ref[0:1, :] + ohs.sum(axis=0, keepdims=True)
    carry_ref[0:1, :] = new_carry
    cnt_ref[...] = jnp.broadcast_to(new_carry, cnt_ref.shape)
    route = jnp.zeros((tm, nl), F32)
    for idx, val in ((_R_E1, e1), (_R_E2, e2), (_R_C1, c1), (_R_C2, c2),
                     (_R_RANK1, rank1), (_R_RANK2, rank2)):
        route = jnp.where(lane == idx, val, route)
    route_ref[...] = route


def _outproj_route(yr, ya, h, wor, woa, nw, wrt, brt):
    t, d = h.shape
    tm = min(TM_ROUTE, t)
    d_r, d_a = yr.shape[1], ya.shape[1]
    nl = wrt.shape[1]
    tok = lambda i: (i, 0)
    const = lambda i: (0, 0)
    return pl.pallas_call(
        _outproj_route_kernel,
        out_shape=(jax.ShapeDtypeStruct((t, d), F32),
                   jax.ShapeDtypeStruct((t * V7X_SUBLANES, V7X_LANES), F32),
                   jax.ShapeDtypeStruct((t, nl), F32), jax.ShapeDtypeStruct((V7X_SUBLANES, nl), F32)),
        grid=(t // tm,),
        in_specs=[pl.BlockSpec((tm, d_r), tok), pl.BlockSpec((tm, d_a), tok), pl.BlockSpec((tm, d), tok),
                  pl.BlockSpec((d_r, d), const), pl.BlockSpec((d_a, d), const),
                  pl.BlockSpec((1, d), const), pl.BlockSpec((d, nl), const), pl.BlockSpec((1, nl), const)],
        out_specs=(pl.BlockSpec((tm, d), tok), pl.BlockSpec((tm * V7X_SUBLANES, V7X_LANES), tok),
                   pl.BlockSpec((tm, nl), tok), pl.BlockSpec((V7X_SUBLANES, nl), const)),
        scratch_shapes=[pltpu.VMEM((V7X_SUBLANES, nl), F32)],
        compiler_params=_params("arbitrary"),
        name="outproj_route",
    )(yr, ya, h, wor, woa, nw, wrt, brt)


def _load_indices(idx_hbm, i, idx_smem, sem):
    n = idx_smem.shape[0]
    cp = pltpu.make_async_copy(idx_hbm.at[pl.ds(pl.multiple_of(i * n, n), n)], idx_smem, sem)
    cp.start()
    cp.wait()


def _dispatch_kernel(idx_hbm, x_ref, xs_hbm, idx_smem, zero_vmem, idx_sem, row_sem, zero_sem,
                     *, tm, n_free):
    i = pl.program_id(0)
    _load_indices(idx_hbm, i, idx_smem, idx_sem)

    def issue(tt, carry):
        for s in range(2):
            pltpu.make_async_copy(_token_tile(x_ref, tt), _token_tile(xs_hbm, idx_smem[s * tm + tt]),
                                  row_sem).start()
        return carry

    lax.fori_loop(0, tm, issue, 0, unroll=8)

    zero_vmem[...] = jnp.zeros_like(zero_vmem)

    def zero_copy(dst_row):
        return pltpu.make_async_copy(zero_vmem, _token_tile(xs_hbm, dst_row), zero_sem)

    def issue_zero(n, carry):
        zero_copy(idx_smem[2 * tm + n]).start()
        return carry

    lax.fori_loop(0, n_free, issue_zero, 0, unroll=8)

    for s in range(2):
        pltpu.make_async_copy(x_ref, xs_hbm.at[pl.ds(0, x_ref.shape[0]), :], row_sem).wait()

    def drain_zero(n, carry):
        zero_copy(0).wait()
        return carry

    lax.fori_loop(0, n_free, drain_zero, 0)


def _dispatch(x, idx, n_rows, tm, n_free):
    t = x.shape[0] // V7X_SUBLANES
    nb = t // tm
    rec = idx.shape[0] // nb
    return pl.pallas_call(
        functools.partial(_dispatch_kernel, tm=tm, n_free=n_free),
        out_shape=jax.ShapeDtypeStruct((n_rows * V7X_SUBLANES, V7X_LANES), x.dtype),
        grid=(nb,),
        in_specs=[pl.BlockSpec(memory_space=pl.ANY),
                  pl.BlockSpec((tm * V7X_SUBLANES, V7X_LANES), lambda i: (i, 0))],
        out_specs=pl.BlockSpec(memory_space=pl.ANY),
        scratch_shapes=[pltpu.SMEM((rec,), I32), pltpu.VMEM((V7X_SUBLANES, V7X_LANES), x.dtype),
                        pltpu.SemaphoreType.DMA, pltpu.SemaphoreType.DMA, pltpu.SemaphoreType.DMA],
        compiler_params=_params("arbitrary"),
        name="dispatch",
    )(idx, x)


def _experts_kernel(te_ref, tv_ref, x_ref, w1_ref, w3_ref, w2_ref, y_ref, *, tm):
    i = pl.program_id(0)

    @pl.when(tv_ref[i] > 0)
    def _():
        x = _from_token_tiles(x_ref, tm).astype(BF16)
        h_gate = jnp.dot(x, w1_ref[...], preferred_element_type=F32)
        h_up = jnp.dot(x, w3_ref[...], preferred_element_type=F32)
        hid = (h_gate * jax.nn.sigmoid(h_gate) * h_up).astype(BF16)
        _to_token_tiles(y_ref, jnp.dot(hid, w2_ref[...], preferred_element_type=F32))

    @pl.when(tv_ref[i] == 0)
    def _():
        y_ref[...] = jnp.zeros_like(y_ref)


def _experts(xs, w1, w3, w2, tile_expert, tile_valid):
    n_rows = xs.shape[0] // V7X_SUBLANES
    tm = TM_EXPERT
    nt = n_rows // tm
    d, f = w1.shape[1:]
    tile_spec = pl.BlockSpec((tm * V7X_SUBLANES, V7X_LANES), lambda i, te, tv: (i, 0))
    grid_spec = pltpu.PrefetchScalarGridSpec(
        num_scalar_prefetch=2,
        grid=(nt,),
        in_specs=[tile_spec,
                  pl.BlockSpec((None, d, f), lambda i, te, tv: (te[i], 0, 0)),
                  pl.BlockSpec((None, d, f), lambda i, te, tv: (te[i], 0, 0)),
                  pl.BlockSpec((None, f, d), lambda i, te, tv: (te[i], 0, 0))],
        out_specs=tile_spec,
    )
    return pl.pallas_call(
        functools.partial(_experts_kernel, tm=tm),
        out_shape=jax.ShapeDtypeStruct(xs.shape, F32),
        grid_spec=grid_spec,
        compiler_params=_params("arbitrary"),
        name="experts",
    )(tile_expert, tile_valid, xs, w1, w3, w2)


def _combine_ple_kernel(pos_hbm, ys_hbm, h_ref, route_ref, p_ref, nw_ref, wg_ref, bg_ref, wp_ref,
                        fw_ref, o_ref, idx_smem, ybuf, idx_sem, row_sem, *, tm, final):
    i = pl.program_id(0)
    _load_indices(pos_hbm, i, idx_smem, idx_sem)

    def issue(tt, carry):
        for s in range(2):
            pltpu.make_async_copy(_token_tile(ys_hbm, idx_smem[s * tm + tt]),
                                  _token_tile(ybuf.at[s], tt), row_sem).start()
        return carry

    lax.fori_loop(0, tm, issue, 0, unroll=8)

    for s in range(2):
        pltpu.make_async_copy(ys_hbm.at[pl.ds(0, ybuf.shape[1]), :], ybuf.at[s], row_sem).wait()

    route = route_ref[...]
    c1 = route[:, _R_C1:_R_C1 + 1]
    c2 = route[:, _R_C2:_R_C2 + 1]
    h2 = (h_ref[...] + c1 * _from_token_tiles(ybuf.at[0], tm)
          + c2 * _from_token_tiles(ybuf.at[1], tm))
    hn = _rms(h2, nw_ref[...]).astype(BF16)
    gate = jax.nn.sigmoid(jnp.dot(hn, wg_ref[...], preferred_element_type=F32) + bg_ref[...])
    h3 = h2 + gate * jnp.dot(p_ref[...].astype(BF16), wp_ref[...], preferred_element_type=F32)
    if final:
        h3 = _rms(h3, fw_ref[...])
    o_ref[...] = h3


def _combine_ple(pos_tiles, ys, h, route, p, nw, wg, bg, wp, fw, *, tm, final):
    t, d = h.shape
    nl = route.shape[1]
    dp = p.shape[1]
    tok = lambda i: (i, 0)
    const = lambda i: (0, 0)
    return pl.pallas_call(
        functools.partial(_combine_ple_kernel, tm=tm, final=final),
        out_shape=jax.ShapeDtypeStruct((t, d), F32),
        grid=(t // tm,),
        in_specs=[pl.BlockSpec(memory_space=pl.ANY), pl.BlockSpec(memory_space=pl.ANY),
                  pl.BlockSpec((tm, d), tok), pl.BlockSpec((tm, nl), tok), pl.BlockSpec((tm, dp), tok),
                  pl.BlockSpec((1, d), const), pl.BlockSpec((d, d), const), pl.BlockSpec((1, d), const),
                  pl.BlockSpec((dp, d), const), pl.BlockSpec((1, d), const)],
        out_specs=pl.BlockSpec((tm, d), tok),
        scratch_shapes=[pltpu.SMEM((pos_tiles.shape[0] // (t // tm),), I32),
                        pltpu.VMEM((2, tm * V7X_SUBLANES, V7X_LANES), F32),
                        pltpu.SemaphoreType.DMA, pltpu.SemaphoreType.DMA],
        compiler_params=_params("arbitrary"),
        name="combine_ple_final" if final else "combine_ple",
    )(pos_tiles, ys, h, route, p, nw, wg, bg, wp, fw)


_SMEM_RECORD_WORDS = 1024


def _index_records(pos1, pos2, tm, extra=None):
    nb = pos1.shape[0] // tm
    parts = [pos1.reshape(nb, tm), pos2.reshape(nb, tm)]
    if extra is not None:
        parts.append(extra.reshape(nb, -1))
    rec = jnp.concatenate(parts, axis=1)
    pad = -rec.shape[1] % _SMEM_RECORD_WORDS
    return jnp.pad(rec, ((0, 0), (0, pad))).reshape(-1)


def _lookup(table, idx):
    ids = jnp.arange(table.shape[0], dtype=I32)
    return jnp.sum(jnp.where(idx[:, None] == ids[None, :], table[None, :], 0), axis=1)


def _bucket(ends, x):
    return jnp.minimum(jnp.sum((x[:, None] >= ends[None, :]).astype(I32), axis=1), ends.shape[0] - 1)


def kernel(x, p, norm_mix_w, w_in, rwkv_mu, rwkv_w0, rwkv_w2, rwkv_a0, rwkv_a2, rwkv_g2, rwkv_k_k, rwkv_k_a, rwkv_r_k, rwkv_ln_w, rwkv_ln_b, rwkv_v0, rwkv_v1, rwkv_v2, att_rel_bias, w_out, norm_ffn_w, router_group_w, router_group_b, router_expert_w, router_expert_b, expert_w1, expert_w3, expert_w2, norm_ple_w, ple_gate_w, ple_gate_b, ple_proj_w, final_norm_w):
    batch, seq, d = x.shape
    depth = w_in.shape[0]
    t = batch * seq
    d_r = rwkv_w0.shape[1]
    n_heads_r = d_r // HEAD_DIM
    n_rwkv_in = rwkv_mu.shape[1]
    d_a = (w_in.shape[2] - n_rwkv_in) // 3
    n_heads_a = d_a // HEAD_DIM
    n_dec, n_iclr, n_gate = rwkv_w2.shape[1], rwkv_a2.shape[1], rwkv_g2.shape[1]
    assert n_dec == n_iclr and n_gate == n_dec + n_iclr
    n_lo = n_dec + n_iclr + n_gate
    f_exp = expert_w1.shape[-1]
    assert d == V7X_SUBLANES * V7X_LANES
    n_rows = 2 * t + N_EXPERTS * TM_EXPERT
    n_tiles = n_rows // TM_EXPERT
    qb = min(QB_ATTN, seq)

    h = x.reshape(t, d)
    v_first = None
    for i in range(depth):
        wr = w_in[i, :, :n_rwkv_in].astype(BF16)
        wa = w_in[i, :, n_rwkv_in:].astype(BF16)
        wl = jnp.zeros((n_lo, 3 * d_r), F32)
        wl = wl.at[:n_dec, :d_r].set(rwkv_w2[i])
        wl = wl.at[n_dec:n_dec + n_iclr, d_r:2 * d_r].set(rwkv_a2[i])
        wl = wl.at[n_dec + n_iclr:, 2 * d_r:].set(rwkv_g2[i]).astype(BF16)
        v0 = rwkv_v0[i - 1] if i > 0 else jnp.zeros((d_r,), F32)
        vec = jnp.stack([rwkv_w0[i], rwkv_a0[i], rwkv_k_k[i], rwkv_k_a[i], rwkv_r_k[i],
                         rwkv_ln_w[i], rwkv_ln_b[i], v0])
        if i > 0:
            n_vr = rwkv_v1.shape[2]
            v1 = jnp.zeros((d_r, V7X_LANES), F32).at[:, :n_vr].set(rwkv_v1[i - 1]).astype(BF16)
            v2 = jnp.zeros((V7X_LANES, d_r), F32).at[:n_vr, :].set(rwkv_v2[i - 1]).astype(BF16)
        else:
            v1 = v2 = None
        table = _attn_table(att_rel_bias[i], qb)
        wor = w_out[i, :d_r].astype(BF16)
        woa = w_out[i, d_r:].astype(BF16)
        n_rt = N_GROUPS + N_EXPERTS
        wrt = jnp.zeros((d, V7X_LANES), F32)
        wrt = wrt.at[:, :N_GROUPS].set(router_group_w[i]).at[:, N_GROUPS:n_rt].set(router_expert_w[i])
        brt = jnp.zeros((1, V7X_LANES), F32)
        brt = brt.at[0, :N_GROUPS].set(router_group_b[i]).at[0, N_GROUPS:n_rt].set(router_expert_b[i])
        w1 = expert_w1[i].reshape(N_EXPERTS, d, f_exp).astype(BF16)
        w3 = expert_w3[i].reshape(N_EXPERTS, d, f_exp).astype(BF16)
        w2 = expert_w2[i].reshape(N_EXPERTS, f_exp, d).astype(BF16)

        z_r, qkv = _norm_proj(h, norm_mix_w[i][None], wr, wa)
        if i == 0:
            y_r, v_first = _rwkv(z_r, None, rwkv_mu[i][None], vec, wl, None, None,
                                 batch=batch, seq=seq, n_heads=n_heads_r)
        else:
            y_r = _rwkv(z_r, v_first, rwkv_mu[i][None], vec, wl, v1, v2,
                        batch=batch, seq=seq, n_heads=n_heads_r)
        y_a = _attn(qkv, table, batch=batch, seq=seq, n_heads=n_heads_a)

        h1, hn, route, cnt = _outproj_route(y_r, y_a, h, wor, woa, norm_ffn_w[i][None], wrt, brt)
        ri = route[:, :V7X_SUBLANES].astype(I32)
        counts = cnt[0, :N_EXPERTS].astype(I32)
        padded = ((counts + TM_EXPERT - 1) // TM_EXPERT) * TM_EXPERT
        p_end = jnp.cumsum(padded)
        p_start = p_end - padded
        pos1 = _lookup(p_start, ri[:, _R_E1]) + ri[:, _R_RANK1]
        pos2 = _lookup(p_start, ri[:, _R_E2]) + ri[:, _R_RANK2]
        tile_start = jnp.arange(n_tiles, dtype=I32) * TM_EXPERT
        tile_expert = _bucket(p_end, tile_start)
        tile_valid = (tile_start < p_end[-1]).astype(I32)

        n_free = n_rows - 2 * t
        n_pad = padded - counts
        f_end = jnp.cumsum(n_pad)
        kf = jnp.arange(n_free, dtype=I32)
        ef = _bucket(f_end, kf)
        pad_row = _lookup(p_start + counts - (f_end - n_pad), ef) + kf
        free_rows = jnp.where(kf < f_end[-1], pad_row, p_end[-1] + (kf - f_end[-1])).astype(I32)

        tm_d = min(TM_DISPATCH, t)
        nb_d = t // tm_d
        assert n_free % nb_d == 0
        xs = _dispatch(hn, _index_records(pos1, pos2, tm_d, free_rows), n_rows, tm_d, n_free // nb_d)
        ys = _experts(xs, w1, w3, w2, tile_expert, tile_valid)

        tm_c = min(TM_COMBINE, t)
        h = _combine_ple(_index_records(pos1, pos2, tm_c), ys, h1, route, p[i].reshape(t, -1),
                         norm_ple_w[i][None], ple_gate_w[i].astype(BF16), ple_gate_b[i][None],
                         ple_proj_w[i].astype(BF16), final_norm_w[None],
                         tm=tm_c, final=(i == depth - 1))
    return h.reshape(batch, seq, d)
```

```python
import functools

import jax
import jax.numpy as jnp
from jax import lax
from jax.experimental import pallas as pl
from jax.experimental.pallas import tpu as pltpu

F32 = jnp.float32
BF16 = jnp.bfloat16
I32 = jnp.int32

CHUNK = 64
HEAD_DIM = 64
LEFT_CHUNKS = 8
MAX_REL = 256
N_GROUPS = 4
EXPERTS_PER_GROUP = 8
N_EXPERTS = N_GROUPS * EXPERTS_PER_GROUP
RMS_EPS = 1e-6
GN_EPS = 64e-5
NEG_INF = -1e30

V7X_LANES = 128
V7X_SUBLANES = 8
V7X_VMEM_LIMIT_BYTES = 48 * 1024 * 1024

TM_PROJ = 512
TB_RWKV = 256
RWKV_CHUNK_GROUP = 2
QB_ATTN = 256
TM_ROUTE = 512
TM_DISPATCH = 512
TM_EXPERT = 256
TM_COMBINE = 512


def _params(*sem):
    return pltpu.CompilerParams(dimension_semantics=sem, vmem_limit_bytes=V7X_VMEM_LIMIT_BYTES)


def _rms(x, w):
    return x * lax.rsqrt(jnp.mean(x * x, axis=-1, keepdims=True) + RMS_EPS) * w


def _mm(a, b):
    return jnp.dot(a.astype(BF16), b.astype(BF16), preferred_element_type=F32)


def _mm_nt(a, b):
    return lax.dot_general(a.astype(BF16), b.astype(BF16), (((1,), (1,)), ((), ())),
                           preferred_element_type=F32)


def _mm_tn(a, b):
    return lax.dot_general(a.astype(BF16), b.astype(BF16), (((0,), (0,)), ((), ())),
                           preferred_element_type=F32)


def _to_token_tiles(ref, x):
    m, d = x.shape
    for s in range(d // V7X_LANES):
        ref[pl.ds(s, m, stride=V7X_SUBLANES), :] = x[:, s * V7X_LANES:(s + 1) * V7X_LANES]


def _from_token_tiles(ref, m):
    return jnp.concatenate([ref[pl.ds(s, m, stride=V7X_SUBLANES), :] for s in range(V7X_SUBLANES)],
                           axis=-1)


def _token_tile(ref, row):
    return ref.at[pl.ds(pl.multiple_of(row * V7X_SUBLANES, V7X_SUBLANES), V7X_SUBLANES), :]


def _split3(x):
    hi = x.astype(BF16)
    r1 = x - hi.astype(F32)
    mid = r1.astype(BF16)
    lo = (r1 - mid.astype(F32)).astype(BF16)
    return hi, mid, lo


def _mm_exact_lhs(a_bf16, x):
    hi, mid, lo = _split3(x)
    return (jnp.dot(a_bf16, hi, preferred_element_type=F32)
            + jnp.dot(a_bf16, mid, preferred_element_type=F32)
            + jnp.dot(a_bf16, lo, preferred_element_type=F32))


def _mm_exact_rhs(x, b_bf16):
    hi, mid, lo = _split3(x)
    return (jnp.dot(hi, b_bf16, preferred_element_type=F32)
            + jnp.dot(mid, b_bf16, preferred_element_type=F32)
            + jnp.dot(lo, b_bf16, preferred_element_type=F32))


def _norm_proj_kernel(h_ref, nw_ref, wr_ref, wa_ref, zr_ref, qkv_ref):
    hn = _rms(h_ref[...], nw_ref[...]).astype(BF16)
    zr_ref[...] = jnp.dot(hn, wr_ref[...], preferred_element_type=F32)
    qkv_ref[...] = jnp.dot(hn, wa_ref[...], preferred_element_type=F32).astype(BF16)


def _norm_proj(h, nw, wr, wa):
    t, d = h.shape
    tm = min(TM_PROJ, t)
    n_r, n_a = wr.shape[1], wa.shape[1]
    return pl.pallas_call(
        _norm_proj_kernel,
        out_shape=(jax.ShapeDtypeStruct((t, n_r), F32), jax.ShapeDtypeStruct((t, n_a), BF16)),
        grid=(t // tm,),
        in_specs=[pl.BlockSpec((tm, d), lambda i: (i, 0)),
                  pl.BlockSpec((1, d), lambda i: (0, 0)),
                  pl.BlockSpec((d, n_r), lambda i: (0, 0)),
                  pl.BlockSpec((d, n_a), lambda i: (0, 0))],
        out_specs=(pl.BlockSpec((tm, n_r), lambda i: (i, 0)),
                   pl.BlockSpec((tm, n_a), lambda i: (i, 0))),
        compiler_params=_params("parallel"),
        name="norm_proj",
    )(h, nw, wr, wa)


_V_W0, _V_A0, _V_KK, _V_KA, _V_RK, _V_LNW, _V_LNB, _V_V0 = range(8)


def _rwkv_kernel(*refs, has_vres, n_heads, d_r, group):
    if has_vres:
        (z_ref, vf_ref, mu_ref, vec_ref, wl_ref, tril_ref, ones_ref, v1_ref, v2_ref, y_ref,
         s_ref, carry_ref, r_s, k_s, v_s, kk_s, a_s, lc_s, lw_s, bon_s, g_s) = refs
        vf_out_ref = None
    else:
        (z_ref, mu_ref, vec_ref, wl_ref, tril_ref, ones_ref, y_ref, vf_out_ref,
         s_ref, carry_ref, r_s, k_s, v_s, kk_s, a_s, lc_s, lw_s, bon_s, g_s) = refs
    tb = z_ref.shape[0]
    n_chunks = tb // CHUNK
    j = pl.program_id(1)

    @pl.when(j == 0)
    def _():
        s_ref[...] = jnp.zeros_like(s_ref)
        carry_ref[...] = jnp.zeros_like(carry_ref)

    z = z_ref[...]
    row = lax.broadcasted_iota(I32, z.shape, 0)
    z_prev = jnp.where(row == 0, carry_ref[0:1, :], pltpu.roll(z, 1, axis=0))
    carry_ref[0:1, :] = z[tb - 1:tb, :]
    zs = z + (z_prev - z) * mu_ref[...]

    vec = vec_ref[...]

    def vrow(i):
        return vec[i:i + 1, :]

    r = zs[:, 0:d_r]
    k = zs[:, d_r:2 * d_r]
    v = zs[:, 2 * d_r:3 * d_r]
    lo = zs[:, 3 * d_r:]
    n_lo = lo.shape[1]
    lane = lax.broadcasted_iota(I32, lo.shape, 1)
    lo_act = jnp.where(lane < n_lo // 4, jnp.tanh(lo),
                       jnp.where(lane < n_lo // 2, lo, jax.nn.sigmoid(lo)))
    lo_out = _mm(lo_act, wl_ref[...])

    if has_vres:
        vv = _mm(_mm(v, v1_ref[...]), v2_ref[...])
        v = v + (vf_ref[...] - v) * jax.nn.sigmoid(vrow(_V_V0) + vv)
    else:
        vf_out_ref[...] = v

    w_log = -jax.nn.softplus(-(vrow(_V_W0) + lo_out[:, 0:d_r])) - 0.5
    lw = -jnp.exp(w_log)
    a = jax.nn.sigmoid(vrow(_V_A0) + lo_out[:, d_r:2 * d_r])
    g = lo_out[:, 2 * d_r:3 * d_r]

    head_ones = ones_ref[...]
    kk = k * vrow(_V_KK)
    kk = kk * lax.rsqrt(jnp.maximum(_mm_exact_rhs(kk * kk, head_ones), 1e-24))
    k2 = k * (1.0 + (a - 1.0) * vrow(_V_KA))
    bonus = _mm_exact_rhs(r * k2 * vrow(_V_RK), head_ones) * v

    lc = _mm_exact_lhs(tril_ref[...], lw)

    r_s[...] = r
    k_s[...] = k2
    v_s[...] = v
    kk_s[...] = kk
    a_s[...] = a
    lc_s[...] = lc
    lw_s[...] = lw
    bon_s[...] = bonus
    g_s[...] = g

    ci = lax.broadcasted_iota(I32, (CHUNK, CHUNK), 0)
    cj = lax.broadcasted_iota(I32, (CHUNK, CHUNK), 1)
    strict = cj < ci
    lower = cj <= ci
    eye = ci == cj
    eye_f = jnp.where(eye, 1.0, 0.0)
    ln_w = vrow(_V_LNW)
    ln_b = vrow(_V_LNB)

    hs = [slice(h * HEAD_DIM, (h + 1) * HEAD_DIM) for h in range(n_heads)]

    def chunk_operands(r0):
        rs = pl.ds(r0, CHUNK)
        lc_c = lc_s[rs, :]
        lw_c = lw_s[rs, :]
        l_end = lc_s[pl.ds(r0 + CHUNK - 1, 1), :]
        p_in = jnp.exp(lc_c)
        p_prev = jnp.exp(lc_c - lw_c)
        p_inv = jnp.exp(-lc_c)
        p_end = jnp.exp(l_end - lc_c)
        p_last = jnp.exp(l_end)
        kk_c = kk_s[rs, :]
        b_c = kk_c * a_s[rs, :]
        k_c = k_s[rs, :]
        at = (-kk_c * p_prev).astype(BF16)
        bt = (b_c * p_inv).astype(BF16)
        bh = (b_c * p_end).astype(BF16)
        kt = (k_c * p_inv).astype(BF16)
        kh = (k_c * p_end).astype(BF16)
        rt = (r_s[rs, :] * p_in).astype(BF16)
        vc = v_s[rs, :].astype(BF16)
        per_head = [[x[:, sl] for sl in hs] for x in (at, bt, bh, kt, kh, rt, vc)]
        per_head.append([p_last[:, sl] for sl in hs])
        return per_head

    def group_body(gi, carry):
        g0 = pl.multiple_of(gi * (group * CHUNK), group * CHUNK)
        ops = [chunk_operands(g0 + c * CHUNK) for c in range(group)]
        at_h, bt_h, bh_h, kt_h, kh_h, rt_h, v_h, pl_h = ([x for c in range(group) for x in ops[c][q]]
                                                         for q in range(8))
        heads = range(group * n_heads)
        ar_h = [jnp.concatenate([at_h[h], rt_h[h]], axis=0) for h in heads]
        m_b = [_mm_nt(ar_h[h], bt_h[h]) for h in heads]
        m_k = [_mm_nt(ar_h[h], kt_h[h]) for h in heads]
        n_ab = [jnp.where(strict, m_b[h][:CHUNK], 0.0) for h in heads]
        a_ak = [jnp.where(strict, m_k[h][:CHUNK], 0.0) for h in heads]
        a_rb = [jnp.where(lower, m_b[h][CHUNK:], 0.0) for h in heads]
        a_rk = [jnp.where(lower, m_k[h][CHUNK:], 0.0) for h in heads]
        x_inv = [eye_f + n_ab[h] for h in heads]
        pw = [_mm(n_ab[h], n_ab[h]) for h in heads]
        akv = [_mm(a_ak[h], v_h[h]) for h in heads]
        n_sq = CHUNK.bit_length() - 2
        for it in range(n_sq):
            if it < n_sq - 1:
                st = [_mm(jnp.concatenate([x_inv[h], pw[h]], axis=0), pw[h]) for h in heads]
                x_inv = [x_inv[h] + st[h][:CHUNK] for h in heads]
                pw = [st[h][CHUNK:] for h in heads]
            else:
                st = [_mm(x_inv[h], pw[h]) for h in heads]
                x_inv = [x_inv[h] + st[h] for h in heads]
        w_h = [_mm(x_inv[h], at_h[h]) for h in heads]
        u0 = [_mm(x_inv[h], akv[h]) for h in heads]
        y0 = [_mm(a_rk[h], v_h[h]) + _mm(a_rb[h], u0[h]) for h in heads]
        r_p = [rt_h[h].astype(F32) + _mm(a_rb[h], w_h[h]) for h in heads]
        g_h = [jnp.where(eye, pl_h[h], 0.0) + _mm_tn(w_h[h], bh_h[h]) for h in heads]
        d_h = [_mm_tn(u0[h], bh_h[h]) + _mm_tn(v_h[h], kh_h[h]) for h in heads]
        s_h = [s_ref[h] for h in range(n_heads)]
        for c in range(group):
            rs = pl.ds(g0 + c * CHUNK, CHUNK)
            idx = [c * n_heads + h for h in range(n_heads)]
            y_h = [y0[i] + _mm_nt(r_p[i], s_h[h]) for h, i in enumerate(idx)]
            s_h = [_mm(s_h[h], g_h[i]) + d_h[i] for h, i in enumerate(idx)]
            y_heads = []
            for h in range(n_heads):
                mean = jnp.mean(y_h[h], axis=-1, keepdims=True)
                yc = y_h[h] - mean
                var = jnp.mean(yc * yc, axis=-1, keepdims=True)
                y_heads.append(yc * lax.rsqrt(var + GN_EPS))
            y_n = jnp.concatenate(y_heads, axis=-1)
            out = (y_n * ln_w + ln_b + bon_s[rs, :]) * g_s[rs, :]
            y_ref[rs, :] = out.astype(y_ref.dtype)
        for h in range(n_heads):
            s_ref[h] = s_h[h]
        return carry

    lax.fori_loop(0, n_chunks // group, group_body, 0)


def _rwkv(z, v_first, mu, vec, wl, v1, v2, *, batch, seq, n_heads):
    t, n_z = z.shape
    d_r = n_heads * HEAD_DIM
    tb = min(TB_RWKV, seq)
    nb = seq // tb
    has_vres = v_first is not None
    tok = lambda b, j: (b * nb + j, 0)
    const = lambda b, j: (0, 0)
    in_specs = [pl.BlockSpec((tb, n_z), tok)]
    args = [z]
    if has_vres:
        in_specs.append(pl.BlockSpec((tb, d_r), tok))
        args.append(v_first)
    ti = jnp.arange(tb)
    tril = ((ti[:, None] // CHUNK == ti[None, :] // CHUNK) & (ti[None, :] <= ti[:, None])).astype(BF16)
    hi = jnp.arange(d_r) // HEAD_DIM
    head_ones = (hi[:, None] == hi[None, :]).astype(BF16)
    in_specs += [pl.BlockSpec(mu.shape, const), pl.BlockSpec(vec.shape, const),
                 pl.BlockSpec(wl.shape, const), pl.BlockSpec(tril.shape, const),
                 pl.BlockSpec(head_ones.shape, const)]
    args += [mu, vec, wl, tril, head_ones]
    n_chunks = tb // CHUNK
    group = RWKV_CHUNK_GROUP if n_chunks % RWKV_CHUNK_GROUP == 0 else 1
    if has_vres:
        in_specs += [pl.BlockSpec(v1.shape, const), pl.BlockSpec(v2.shape, const)]
        args += [v1, v2]
        out_shape = jax.ShapeDtypeStruct((t, d_r), BF16)
        out_specs = pl.BlockSpec((tb, d_r), tok)
    else:
        out_shape = (jax.ShapeDtypeStruct((t, d_r), BF16), jax.ShapeDtypeStruct((t, d_r), F32))
        out_specs = (pl.BlockSpec((tb, d_r), tok), pl.BlockSpec((tb, d_r), tok))
    scratch = [pltpu.VMEM((n_heads, HEAD_DIM, HEAD_DIM), F32),
               pltpu.VMEM((V7X_SUBLANES, n_z), F32)]
    scratch += [pltpu.VMEM((tb, d_r), F32) for _ in range(9)]
    return pl.pallas_call(
        functools.partial(_rwkv_kernel, has_vres=has_vres, n_heads=n_heads, d_r=d_r, group=group),
        out_shape=out_shape,
        grid=(batch, nb),
        in_specs=in_specs,
        out_specs=out_specs,
        scratch_shapes=scratch,
        compiler_params=_params("arbitrary", "arbitrary"),
        name="rwkv_vres" if has_vres else "rwkv",
    )(*args)


def _attn_kernel(*refs, n_heads, n_parts):
    q_ref = refs[0]
    k_refs = refs[1:1 + n_parts]
    v_refs = refs[1 + n_parts:1 + 2 * n_parts]
    tab_ref = refs[1 + 2 * n_parts]
    o_ref = refs[2 + 2 * n_parts]
    qb = q_ref.shape[0]
    j = pl.program_id(1)
    scale = HEAD_DIM ** -0.5
    q = q_ref[...] * jnp.asarray(scale, q_ref.dtype)
    ks = [r[...] for r in k_refs]
    vs = [r[...] for r in v_refs]
    outs = []
    for h in range(n_heads):
        sl = slice(h * HEAD_DIM, (h + 1) * HEAD_DIM)
        q_h = q[:, sl]
        s_parts = []
        for p in range(n_parts):
            s = _mm_nt(q_h, ks[p][:, sl]) + tab_ref[h, :, p * qb:(p + 1) * qb]
            back = n_parts - 1 - p
            if back > 0:
                s = jnp.where(j >= back, s, NEG_INF)
            s_parts.append(s)
        m = s_parts[0].max(axis=-1, keepdims=True)
        for s in s_parts[1:]:
            m = jnp.maximum(m, s.max(axis=-1, keepdims=True))
        l = jnp.zeros_like(m)
        acc = jnp.zeros((qb, HEAD_DIM), F32)
        for p in range(n_parts):
            e = jnp.exp(s_parts[p] - m)
            l = l + e.sum(axis=-1, keepdims=True)
            acc = acc + _mm(e, vs[p][:, sl])
        outs.append(acc / l)
    o_ref[...] = jnp.concatenate(outs, axis=-1).astype(o_ref.dtype)


def _attn(qkv, table, *, batch, seq, n_heads):
    t = qkv.shape[0]
    d_a = n_heads * HEAD_DIM
    qb = min(QB_ATTN, seq)
    left = LEFT_CHUNKS * CHUNK
    assert left % qb == 0 and seq % qb == 0
    n_parts = left // qb + 1
    nb = seq // qb
    in_specs = [pl.BlockSpec((qb, d_a), lambda b, j: (b * nb + j, 0))]
    for p in range(n_parts):
        back = n_parts - 1 - p
        in_specs.append(pl.BlockSpec((qb, d_a), lambda b, j, back=back: (b * nb + jnp.maximum(j - back, 0), 1)))
    for p in range(n_parts):
        back = n_parts - 1 - p
        in_specs.append(pl.BlockSpec((qb, d_a), lambda b, j, back=back: (b * nb + jnp.maximum(j - back, 0), 2)))
    in_specs.append(pl.BlockSpec(table.shape, lambda b, j: (0, 0, 0)))
    return pl.pallas_call(
        functools.partial(_attn_kernel, n_heads=n_heads, n_parts=n_parts),
        out_shape=jax.ShapeDtypeStruct((t, d_a), BF16),
        grid=(batch, nb),
        in_specs=in_specs,
        out_specs=pl.BlockSpec((qb, d_a), lambda b, j: (b * nb + j, 0)),
        compiler_params=_params("parallel", "arbitrary"),
        name="attn",
    )(*([qkv] * (1 + 2 * n_parts)), table)


def _attn_table(rel_bias, qb):
    left = LEFT_CHUNKS * CHUNK
    n_keys = left + qb
    period = qb + n_keys - 1
    n_heads = rel_bias.shape[0]
    rel = left + (qb - 1) - jnp.arange(period)
    g = rel_bias[:, jnp.clip(rel, -(CHUNK - 1), MAX_REL) + (CHUNK - 1)].astype(F32)
    flat = jnp.tile(g, (1, qb + 1))[:, :qb * (period + 1)]
    bias = flat.reshape(n_heads, qb, period + 1)[:, ::-1, :n_keys]
    cq = jnp.arange(qb)[:, None] // CHUNK
    ck = jnp.arange(n_keys)[None, :] // CHUNK
    valid = (ck >= cq) & (ck <= cq + LEFT_CHUNKS)
    return jnp.where(valid[None], bias, NEG_INF)


_R_E1, _R_E2, _R_C1, _R_C2, _R_RANK1, _R_RANK2 = range(6)


def _outproj_route_kernel(yr_ref, ya_ref, h_ref, wor_ref, woa_ref, nw_ref, wrt_ref, brt_ref,
                          h1_ref, hn_ref, route_ref, cnt_ref, carry_ref):
    i = pl.program_id(0)

    @pl.when(i == 0)
    def _():
        carry_ref[...] = jnp.zeros_like(carry_ref)

    h1 = (h_ref[...] + jnp.dot(yr_ref[...], wor_ref[...], preferred_element_type=F32)
          + jnp.dot(ya_ref[...], woa_ref[...], preferred_element_type=F32))
    h1_ref[...] = h1
    hn = _rms(h1, nw_ref[...])
    _to_token_tiles(hn_ref, hn)
    nl = brt_ref.shape[1]
    hn_hi = hn.astype(BF16)
    hn_lo = (hn - hn_hi.astype(F32)).astype(BF16)
    part = jnp.dot(hn_hi, wrt_ref[...], preferred_element_type=F32)
    logits = (part[:, :nl] + part[:, nl:]
              + jnp.dot(hn_lo, wrt_ref[:, :nl], preferred_element_type=F32) + brt_ref[...])
    tm = logits.shape[0]
    lane = lax.broadcasted_iota(I32, (tm, nl), 1)
    lane_f = lane.astype(F32)
    ninf = -jnp.inf
    big = float(nl)
    is_g = lane < N_GROUPS
    gl = jnp.where(is_g, logits, ninf)
    g_max = gl.max(axis=-1, keepdims=True)
    g_sel = jnp.where(gl == g_max, lane_f, big).min(axis=-1, keepdims=True)
    p_g = 1.0 / jnp.where(is_g, jnp.exp(logits - g_max), 0.0).sum(axis=-1, keepdims=True)
    e_lo = N_GROUPS + EXPERTS_PER_GROUP * g_sel
    in_grp = (lane_f >= e_lo) & (lane_f < e_lo + EXPERTS_PER_GROUP)
    el = jnp.where(in_grp, logits, ninf)
    m1 = el.max(axis=-1, keepdims=True)
    i1 = jnp.where(el == m1, lane_f, big).min(axis=-1, keepdims=True)
    el2 = jnp.where(lane_f == i1, ninf, el)
    m2 = el2.max(axis=-1, keepdims=True)
    i2 = jnp.where(el2 == m2, lane_f, big).min(axis=-1, keepdims=True)
    t2 = jnp.exp(m2 - m1)
    c1 = p_g / (1.0 + t2)
    c2 = p_g * t2 / (1.0 + t2)
    e1 = i1 - N_GROUPS
    e2 = i2 - N_GROUPS
    oh1 = lane_f == e1
    oh2 = lane_f == e2
    ohs = jnp.where(oh1 | oh2, 1.0, 0.0)
    ri = lax.broadcasted_iota(I32, (tm, tm), 0)
    rj = lax.broadcasted_iota(I32, (tm, tm), 1)
    before = jnp.where(rj < ri, 1.0, 0.0).astype(BF16)
    cnt = jnp.dot(before, ohs.astype(BF16), preferred_element_type=F32) + carry_ref[0:1, :]
    rank1 = jnp.where(oh1, cnt, 0.0).sum(axis=-1, keepdims=True)
    rank2 = jnp.where(oh2, cnt, 0.0).sum(axis=-1, keepdims=True)
    new_carry = carry_ref[0:1, :] + ohs.sum(axis=0, keepdims=True)
    carry_ref[0:1, :] = new_carry
    cnt_ref[...] = jnp.broadcast_to(new_carry, cnt_ref.shape)
    route = jnp.zeros((tm, nl), F32)
    for idx, val in ((_R_E1, e1), (_R_E2, e2), (_R_C1, c1), (_R_C2, c2),
                     (_R_RANK1, rank1), (_R_RANK2, rank2)):
        route = jnp.where(lane == idx, val, route)
    route_ref[...] = route


def _outproj_route(yr, ya, h, wor, woa, nw, wrt, brt):
    t, d = h.shape
    tm = min(TM_ROUTE, t)
    d_r, d_a = yr.shape[1], ya.shape[1]
    nl = brt.shape[1]
    tok = lambda i: (i, 0)
    const = lambda i: (0, 0)
    return pl.pallas_call(
        _outproj_route_kernel,
        out_shape=(jax.ShapeDtypeStruct((t, d), F32),
                   jax.ShapeDtypeStruct((t * V7X_SUBLANES, V7X_LANES), F32),
                   jax.ShapeDtypeStruct((t, nl), F32), jax.ShapeDtypeStruct((V7X_SUBLANES, nl), F32)),
        grid=(t // tm,),
        in_specs=[pl.BlockSpec((tm, d_r), tok), pl.BlockSpec((tm, d_a), tok), pl.BlockSpec((tm, d), tok),
                  pl.BlockSpec((d_r, d), const), pl.BlockSpec((d_a, d), const),
                  pl.BlockSpec((1, d), const), pl.BlockSpec(wrt.shape, const), pl.BlockSpec((1, nl), const)],
        out_specs=(pl.BlockSpec((tm, d), tok), pl.BlockSpec((tm * V7X_SUBLANES, V7X_LANES), tok),
                   pl.BlockSpec((tm, nl), tok), pl.BlockSpec((V7X_SUBLANES, nl), const)),
        scratch_shapes=[pltpu.VMEM((V7X_SUBLANES, nl), F32)],
        compiler_params=_params("arbitrary"),
        name="outproj_route",
    )(yr, ya, h, wor, woa, nw, wrt, brt)


def _load_indices(idx_hbm, i, idx_smem, sem):
    n = idx_smem.shape[0]
    cp = pltpu.make_async_copy(idx_hbm.at[pl.ds(pl.multiple_of(i * n, n), n)], idx_smem, sem)
    cp.start()
    cp.wait()


def _dispatch_kernel(idx_hbm, x_ref, xs_hbm, idx_smem, zero_vmem, idx_sem, row_sem, zero_sem,
                     *, tm, n_free):
    i = pl.program_id(0)
    _load_indices(idx_hbm, i, idx_smem, idx_sem)

    def issue(tt, carry):
        for s in range(2):
            pltpu.make_async_copy(_token_tile(x_ref, tt), _token_tile(xs_hbm, idx_smem[s * tm + tt]),
                                  row_sem).start()
        return carry

    lax.fori_loop(0, tm, issue, 0, unroll=8)

    zero_vmem[...] = jnp.zeros_like(zero_vmem)

    def zero_copy(dst_row):
        return pltpu.make_async_copy(zero_vmem, _token_tile(xs_hbm, dst_row), zero_sem)

    def issue_zero(n, carry):
        zero_copy(idx_smem[2 * tm + n]).start()
        return carry

    lax.fori_loop(0, n_free, issue_zero, 0, unroll=8)

    for s in range(2):
        pltpu.make_async_copy(x_ref, xs_hbm.at[pl.ds(0, x_ref.shape[0]), :], row_sem).wait()

    def drain_zero(n, carry):
        zero_copy(0).wait()
        return carry

    lax.fori_loop(0, n_free, drain_zero, 0)


def _dispatch(x, idx, n_rows, tm, n_free):
    t = x.shape[0] // V7X_SUBLANES
    nb = t // tm
    rec = idx.shape[0] // nb
    return pl.pallas_call(
        functools.partial(_dispatch_kernel, tm=tm, n_free=n_free),
        out_shape=jax.ShapeDtypeStruct((n_rows * V7X_SUBLANES, V7X_LANES), x.dtype),
        grid=(nb,),
        in_specs=[pl.BlockSpec(memory_space=pl.ANY),
                  pl.BlockSpec((tm * V7X_SUBLANES, V7X_LANES), lambda i: (i, 0))],
        out_specs=pl.BlockSpec(memory_space=pl.ANY),
        scratch_shapes=[pltpu.SMEM((rec,), I32), pltpu.VMEM((V7X_SUBLANES, V7X_LANES), x.dtype),
                        pltpu.SemaphoreType.DMA, pltpu.SemaphoreType.DMA, pltpu.SemaphoreType.DMA],
        compiler_params=_params("arbitrary"),
        name="dispatch",
    )(idx, x)


def _experts_kernel(te_ref, tv_ref, x_ref, w1_ref, w3_ref, w2_ref, y_ref, *, tm):
    i = pl.program_id(0)

    @pl.when(tv_ref[i] > 0)
    def _():
        x = _from_token_tiles(x_ref, tm).astype(BF16)
        h_gate = jnp.dot(x, w1_ref[...], preferred_element_type=F32)
        h_up = jnp.dot(x, w3_ref[...], preferred_element_type=F32)
        hid = (h_gate * jax.nn.sigmoid(h_gate) * h_up).astype(BF16)
        _to_token_tiles(y_ref, jnp.dot(hid, w2_ref[...], preferred_element_type=F32))

    @pl.when(tv_ref[i] == 0)
    def _():
        y_ref[...] = jnp.zeros_like(y_ref)


def _experts(xs, w1, w3, w2, tile_expert, tile_valid):
    n_rows = xs.shape[0] // V7X_SUBLANES
    tm = TM_EXPERT
    nt = n_rows // tm
    d, f = w1.shape[1:]
    tile_spec = pl.BlockSpec((tm * V7X_SUBLANES, V7X_LANES), lambda i, te, tv: (i, 0))
    grid_spec = pltpu.PrefetchScalarGridSpec(
        num_scalar_prefetch=2,
        grid=(nt,),
        in_specs=[tile_spec,
                  pl.BlockSpec((None, d, f), lambda i, te, tv: (te[i], 0, 0)),
                  pl.BlockSpec((None, d, f), lambda i, te, tv: (te[i], 0, 0)),
                  pl.BlockSpec((None, f, d), lambda i, te, tv: (te[i], 0, 0))],
        out_specs=tile_spec,
    )
    return pl.pallas_call(
        functools.partial(_experts_kernel, tm=tm),
        out_shape=jax.ShapeDtypeStruct(xs.shape, F32),
        grid_spec=grid_spec,
        compiler_params=_params("arbitrary"),
        name="experts",
    )(tile_expert, tile_valid, xs, w1, w3, w2)


def _combine_ple_kernel(pos_hbm, ys_hbm, h_ref, route_ref, p_ref, nw_ref, wg_ref, bg_ref, wp_ref,
                        fw_ref, o_ref, idx_smem, ybuf, idx_sem, row_sem, *, tm, final):
    i = pl.program_id(0)
    nb = pl.num_programs(0)
    cur = lax.rem(i, 2)
    nxt = 1 - cur

    def gather(step, buf):
        _load_indices(pos_hbm, step, idx_smem, idx_sem)

        def issue(tt, carry):
            for s in range(2):
                pltpu.make_async_copy(_token_tile(ys_hbm, idx_smem[s * tm + tt]),
                                      _token_tile(ybuf.at[buf, s], tt), row_sem.at[buf]).start()
            return carry

        lax.fori_loop(0, tm, issue, 0, unroll=8)

    def wait_gather(buf):
        for s in range(2):
            pltpu.make_async_copy(ys_hbm.at[pl.ds(0, ybuf.shape[2]), :], ybuf.at[buf, s],
                                  row_sem.at[buf]).wait()

    @pl.when(i == 0)
    def _():
        gather(0, 0)

    @pl.when(i + 1 < nb)
    def _():
        gather(i + 1, nxt)

    wait_gather(cur)

    route = route_ref[...]
    c1 = route[:, _R_C1:_R_C1 + 1]
    c2 = route[:, _R_C2:_R_C2 + 1]
    h2 = (h_ref[...] + c1 * _from_token_tiles(ybuf.at[cur, 0], tm)
          + c2 * _from_token_tiles(ybuf.at[cur, 1], tm))
    hn = _rms(h2, nw_ref[...]).astype(BF16)
    gate = jax.nn.sigmoid(jnp.dot(hn, wg_ref[...], preferred_element_type=F32) + bg_ref[...])
    h3 = h2 + gate * jnp.dot(p_ref[...].astype(BF16), wp_ref[...], preferred_element_type=F32)
    if final:
        h3 = _rms(h3, fw_ref[...])
    o_ref[...] = h3


def _combine_ple(pos_tiles, ys, h, route, p, nw, wg, bg, wp, fw, *, tm, final):
    t, d = h.shape
    nl = route.shape[1]
    dp = p.shape[1]
    tok = lambda i: (i, 0)
    const = lambda i: (0, 0)
    return pl.pallas_call(
        functools.partial(_combine_ple_kernel, tm=tm, final=final),
        out_shape=jax.ShapeDtypeStruct((t, d), F32),
        grid=(t // tm,),
        in_specs=[pl.BlockSpec(memory_space=pl.ANY), pl.BlockSpec(memory_space=pl.ANY),
                  pl.BlockSpec((tm, d), tok), pl.BlockSpec((tm, nl), tok), pl.BlockSpec((tm, dp), tok),
                  pl.BlockSpec((1, d), const), pl.BlockSpec((d, d), const), pl.BlockSpec((1, d), const),
                  pl.BlockSpec((dp, d), const), pl.BlockSpec((1, d), const)],
        out_specs=pl.BlockSpec((tm, d), tok),
        scratch_shapes=[pltpu.SMEM((pos_tiles.shape[0] // (t // tm),), I32),
                        pltpu.VMEM((2, 2, tm * V7X_SUBLANES, V7X_LANES), F32),
                        pltpu.SemaphoreType.DMA, pltpu.SemaphoreType.DMA((2,))],
        compiler_params=_params("arbitrary"),
        name="combine_ple_final" if final else "combine_ple",
    )(pos_tiles, ys, h, route, p, nw, wg, bg, wp, fw)


_SMEM_RECORD_WORDS = 1024


def _index_records(pos1, pos2, tm, extra=None):
    nb = pos1.shape[0] // tm
    parts = [pos1.reshape(nb, tm), pos2.reshape(nb, tm)]
    if extra is not None:
        parts.append(extra.reshape(nb, -1))
    rec = jnp.concatenate(parts, axis=1)
    pad = -rec.shape[1] % _SMEM_RECORD_WORDS
    return jnp.pad(rec, ((0, 0), (0, pad))).reshape(-1)


def _lookup(table, idx):
    ids = jnp.arange(table.shape[0], dtype=I32)
    return jnp.sum(jnp.where(idx[:, None] == ids[None, :], table[None, :], 0), axis=1)


def _bucket(ends, x):
    return jnp.minimum(jnp.sum((x[:, None] >= ends[None, :]).astype(I32), axis=1), ends.shape[0] - 1)


def kernel(x, p, norm_mix_w, w_in, rwkv_mu, rwkv_w0, rwkv_w2, rwkv_a0, rwkv_a2, rwkv_g2, rwkv_k_k, rwkv_k_a, rwkv_r_k, rwkv_ln_w, rwkv_ln_b, rwkv_v0, rwkv_v1, rwkv_v2, att_rel_bias, w_out, norm_ffn_w, router_group_w, router_group_b, router_expert_w, router_expert_b, expert_w1, expert_w3, expert_w2, norm_ple_w, ple_gate_w, ple_gate_b, ple_proj_w, final_norm_w):
    batch, seq, d = x.shape
    depth = w_in.shape[0]
    t = batch * seq
    d_r = rwkv_w0.shape[1]
    n_heads_r = d_r // HEAD_DIM
    n_rwkv_in = rwkv_mu.shape[1]
    d_a = (w_in.shape[2] - n_rwkv_in) // 3
    n_heads_a = d_a // HEAD_DIM
    n_dec, n_iclr, n_gate = rwkv_w2.shape[1], rwkv_a2.shape[1], rwkv_g2.shape[1]
    assert n_dec == n_iclr and n_gate == n_dec + n_iclr
    n_lo = n_dec + n_iclr + n_gate
    f_exp = expert_w1.shape[-1]
    assert d == V7X_SUBLANES * V7X_LANES
    n_rows = 2 * t + N_EXPERTS * TM_EXPERT
    n_tiles = n_rows // TM_EXPERT
    qb = min(QB_ATTN, seq)

    h = x.reshape(t, d)
    v_first = None
    for i in range(depth):
        wr = w_in[i, :, :n_rwkv_in].astype(BF16)
        wa = w_in[i, :, n_rwkv_in:].astype(BF16)
        wl = jnp.zeros((n_lo, 3 * d_r), F32)
        wl = wl.at[:n_dec, :d_r].set(rwkv_w2[i])
        wl = wl.at[n_dec:n_dec + n_iclr, d_r:2 * d_r].set(rwkv_a2[i])
        wl = wl.at[n_dec + n_iclr:, 2 * d_r:].set(rwkv_g2[i]).astype(BF16)
        v0 = rwkv_v0[i - 1] if i > 0 else jnp.zeros((d_r,), F32)
        vec = jnp.stack([rwkv_w0[i], rwkv_a0[i], rwkv_k_k[i], rwkv_k_a[i], rwkv_r_k[i],
                         rwkv_ln_w[i], rwkv_ln_b[i], v0])
        if i > 0:
            n_vr = rwkv_v1.shape[2]
            v1 = jnp.zeros((d_r, V7X_LANES), F32).at[:, :n_vr].set(rwkv_v1[i - 1]).astype(BF16)
            v2 = jnp.zeros((V7X_LANES, d_r), F32).at[:n_vr, :].set(rwkv_v2[i - 1]).astype(BF16)
        else:
            v1 = v2 = None
        table = _attn_table(att_rel_bias[i], qb)
        wor = w_out[i, :d_r].astype(BF16)
        woa = w_out[i, d_r:].astype(BF16)
        n_rt = N_GROUPS + N_EXPERTS
        wrt = jnp.zeros((d, V7X_LANES), F32)
        wrt = wrt.at[:, :N_GROUPS].set(router_group_w[i]).at[:, N_GROUPS:n_rt].set(router_expert_w[i])
        wrt_hi = wrt.astype(BF16)
        wrt = jnp.concatenate([wrt_hi, (wrt - wrt_hi.astype(F32)).astype(BF16)], axis=1)
        brt = jnp.zeros((1, V7X_LANES), F32)
        brt = brt.at[0, :N_GROUPS].set(router_group_b[i]).at[0, N_GROUPS:n_rt].set(router_expert_b[i])
        w1 = expert_w1[i].reshape(N_EXPERTS, d, f_exp).astype(BF16)
        w3 = expert_w3[i].reshape(N_EXPERTS, d, f_exp).astype(BF16)
        w2 = expert_w2[i].reshape(N_EXPERTS, f_exp, d).astype(BF16)

        z_r, qkv = _norm_proj(h, norm_mix_w[i][None], wr, wa)
        if i == 0:
            y_r, v_first = _rwkv(z_r, None, rwkv_mu[i][None], vec, wl, None, None,
                                 batch=batch, seq=seq, n_heads=n_heads_r)
        else:
            y_r = _rwkv(z_r, v_first, rwkv_mu[i][None], vec, wl, v1, v2,
                        batch=batch, seq=seq, n_heads=n_heads_r)
        y_a = _attn(qkv, table, batch=batch, seq=seq, n_heads=n_heads_a)

        h1, hn, route, cnt = _outproj_route(y_r, y_a, h, wor, woa, norm_ffn_w[i][None], wrt, brt)
        ri = route[:, :V7X_SUBLANES].astype(I32)
        counts = cnt[0, :N_EXPERTS].astype(I32)
        padded = ((counts + TM_EXPERT - 1) // TM_EXPERT) * TM_EXPERT
        p_end = jnp.cumsum(padded)
        p_start = p_end - padded
        pos1 = _lookup(p_start, ri[:, _R_E1]) + ri[:, _R_RANK1]
        pos2 = _lookup(p_start, ri[:, _R_E2]) + ri[:, _R_RANK2]
        tile_start = jnp.arange(n_tiles, dtype=I32) * TM_EXPERT
        tile_expert = _bucket(p_end, tile_start)
        tile_valid = (tile_start < p_end[-1]).astype(I32)

        n_free = n_rows - 2 * t
        n_pad = padded - counts
        f_end = jnp.cumsum(n_pad)
        kf = jnp.arange(n_free, dtype=I32)
        ef = _bucket(f_end, kf)
        pad_row = _lookup(p_start + counts - (f_end - n_pad), ef) + kf
        free_rows = jnp.where(kf < f_end[-1], pad_row, p_end[-1] + (kf - f_end[-1])).astype(I32)

        tm_d = min(TM_DISPATCH, t)
        nb_d = t // tm_d
        assert n_free % nb_d == 0
        xs = _dispatch(hn, _index_records(pos1, pos2, tm_d, free_rows), n_rows, tm_d, n_free // nb_d)
        ys = _experts(xs, w1, w3, w2, tile_expert, tile_valid)

        tm_c = min(TM_COMBINE, t)
        h = _combine_ple(_index_records(pos1, pos2, tm_c), ys, h1, route, p[i].reshape(t, -1),
                         norm_ple_w[i][None], ple_gate_w[i].astype(BF16), ple_gate_b[i][None],
                         ple_proj_w[i].astype(BF16), final_norm_w[None],
                         tm=tm_c, final=(i == depth - 1))
    return h.reshape(batch, seq, d)
```

```python
import functools

import jax
import jax.numpy as jnp
from jax import lax
from jax.experimental import pallas as pl
from jax.experimental.pallas import tpu as pltpu

F32 = jnp.float32
BF16 = jnp.bfloat16
I32 = jnp.int32

CHUNK = 64
HEAD_DIM = 64
LEFT_CHUNKS = 8
MAX_REL = 256
N_GROUPS = 4
EXPERTS_PER_GROUP = 8
N_EXPERTS = N_GROUPS * EXPERTS_PER_GROUP
RMS_EPS = 1e-6
GN_EPS = 64e-5
NEG_INF = -1e30

V7X_LANES = 128
V7X_SUBLANES = 8
V7X_VMEM_LIMIT_BYTES = 48 * 1024 * 1024

TM_PROJ = 512
TB_RWKV = 256
RWKV_CHUNK_GROUP = 2
QB_ATTN = 256
TM_ROUTE = 512
TM_EXPERT = 256
TM_COMBINE = 512


def _params(*sem):
    return pltpu.CompilerParams(dimension_semantics=sem, vmem_limit_bytes=V7X_VMEM_LIMIT_BYTES)


def _rms(x, w):
    return x * lax.rsqrt(jnp.mean(x * x, axis=-1, keepdims=True) + RMS_EPS) * w


def _mm(a, b):
    return jnp.dot(a.astype(BF16), b.astype(BF16), preferred_element_type=F32)


def _mm_nt(a, b):
    return lax.dot_general(a.astype(BF16), b.astype(BF16), (((1,), (1,)), ((), ())),
                           preferred_element_type=F32)


def _mm_tn(a, b):
    return lax.dot_general(a.astype(BF16), b.astype(BF16), (((0,), (0,)), ((), ())),
                           preferred_element_type=F32)


def _to_token_tiles(ref, x):
    m, d = x.shape
    for s in range(d // V7X_LANES):
        ref[pl.ds(s, m, stride=V7X_SUBLANES), :] = x[:, s * V7X_LANES:(s + 1) * V7X_LANES]


def _from_token_tiles(ref, m):
    return jnp.concatenate([ref[pl.ds(s, m, stride=V7X_SUBLANES), :] for s in range(V7X_SUBLANES)],
                           axis=-1)


def _token_tile(ref, row):
    return ref.at[pl.ds(pl.multiple_of(row * V7X_SUBLANES, V7X_SUBLANES), V7X_SUBLANES), :]


def _split3(x):
    hi = x.astype(BF16)
    r1 = x - hi.astype(F32)
    mid = r1.astype(BF16)
    lo = (r1 - mid.astype(F32)).astype(BF16)
    return hi, mid, lo


def _mm_exact_lhs(a_bf16, x):
    hi, mid, lo = _split3(x)
    return (jnp.dot(a_bf16, hi, preferred_element_type=F32)
            + jnp.dot(a_bf16, mid, preferred_element_type=F32)
            + jnp.dot(a_bf16, lo, preferred_element_type=F32))


def _mm_exact_rhs(x, b_bf16):
    hi, mid, lo = _split3(x)
    return (jnp.dot(hi, b_bf16, preferred_element_type=F32)
            + jnp.dot(mid, b_bf16, preferred_element_type=F32)
            + jnp.dot(lo, b_bf16, preferred_element_type=F32))


def _norm_proj_kernel(h_ref, nw_ref, wr_ref, wa_ref, zr_ref, qkv_ref):
    hn = _rms(h_ref[...], nw_ref[...]).astype(BF16)
    zr_ref[...] = jnp.dot(hn, wr_ref[...], preferred_element_type=F32)
    qkv_ref[...] = jnp.dot(hn, wa_ref[...], preferred_element_type=F32).astype(BF16)


def _norm_proj(h, nw, wr, wa):
    t, d = h.shape
    tm = min(TM_PROJ, t)
    n_r, n_a = wr.shape[1], wa.shape[1]
    return pl.pallas_call(
        _norm_proj_kernel,
        out_shape=(jax.ShapeDtypeStruct((t, n_r), F32), jax.ShapeDtypeStruct((t, n_a), BF16)),
        grid=(t // tm,),
        in_specs=[pl.BlockSpec((tm, d), lambda i: (i, 0)),
                  pl.BlockSpec((1, d), lambda i: (0, 0)),
                  pl.BlockSpec((d, n_r), lambda i: (0, 0)),
                  pl.BlockSpec((d, n_a), lambda i: (0, 0))],
        out_specs=(pl.BlockSpec((tm, n_r), lambda i: (i, 0)),
                   pl.BlockSpec((tm, n_a), lambda i: (i, 0))),
        compiler_params=_params("parallel"),
        name="norm_proj",
    )(h, nw, wr, wa)


_V_W0, _V_A0, _V_KK, _V_KA, _V_RK, _V_LNW, _V_LNB, _V_V0 = range(8)


def _rwkv_kernel(*refs, has_vres, n_heads, d_r, group):
    if has_vres:
        (z_ref, vf_ref, mu_ref, vec_ref, wl_ref, tril_ref, ones_ref, v1_ref, v2_ref, y_ref,
         s_ref, carry_ref, r_s, k_s, v_s, kk_s, a_s, lc_s, lw_s, bon_s, g_s) = refs
        vf_out_ref = None
    else:
        (z_ref, mu_ref, vec_ref, wl_ref, tril_ref, ones_ref, y_ref, vf_out_ref,
         s_ref, carry_ref, r_s, k_s, v_s, kk_s, a_s, lc_s, lw_s, bon_s, g_s) = refs
    tb = z_ref.shape[0]
    n_chunks = tb // CHUNK
    j = pl.program_id(1)

    @pl.when(j == 0)
    def _():
        s_ref[...] = jnp.zeros_like(s_ref)
        carry_ref[...] = jnp.zeros_like(carry_ref)

    z = z_ref[...]
    row = lax.broadcasted_iota(I32, z.shape, 0)
    z_prev = jnp.where(row == 0, carry_ref[0:1, :], pltpu.roll(z, 1, axis=0))
    carry_ref[0:1, :] = z[tb - 1:tb, :]
    zs = z + (z_prev - z) * mu_ref[...]

    vec = vec_ref[...]

    def vrow(i):
        return vec[i:i + 1, :]

    r = zs[:, 0:d_r]
    k = zs[:, d_r:2 * d_r]
    v = zs[:, 2 * d_r:3 * d_r]
    lo = zs[:, 3 * d_r:]
    n_lo = lo.shape[1]
    lane = lax.broadcasted_iota(I32, lo.shape, 1)
    lo_act = jnp.where(lane < n_lo // 4, jnp.tanh(lo),
                       jnp.where(lane < n_lo // 2, lo, jax.nn.sigmoid(lo)))
    lo_out = _mm(lo_act, wl_ref[...])

    if has_vres:
        vv = _mm(_mm(v, v1_ref[...]), v2_ref[...])
        v = v + (vf_ref[...] - v) * jax.nn.sigmoid(vrow(_V_V0) + vv)
    else:
        vf_out_ref[...] = v

    w_log = -jax.nn.softplus(-(vrow(_V_W0) + lo_out[:, 0:d_r])) - 0.5
    lw = -jnp.exp(w_log)
    a = jax.nn.sigmoid(vrow(_V_A0) + lo_out[:, d_r:2 * d_r])
    g = lo_out[:, 2 * d_r:3 * d_r]

    head_ones = ones_ref[...]
    kk = k * vrow(_V_KK)
    kk = kk * lax.rsqrt(jnp.maximum(_mm_exact_rhs(kk * kk, head_ones), 1e-24))
    k2 = k * (1.0 + (a - 1.0) * vrow(_V_KA))
    bonus = _mm_exact_rhs(r * k2 * vrow(_V_RK), head_ones) * v

    lc = _mm_exact_lhs(tril_ref[...], lw)

    r_s[...] = r
    k_s[...] = k2
    v_s[...] = v
    kk_s[...] = kk
    a_s[...] = a
    lc_s[...] = lc
    lw_s[...] = lw
    bon_s[...] = bonus
    g_s[...] = g

    ci = lax.broadcasted_iota(I32, (CHUNK, CHUNK), 0)
    cj = lax.broadcasted_iota(I32, (CHUNK, CHUNK), 1)
    strict = cj < ci
    lower = cj <= ci
    eye = ci == cj
    eye_f = jnp.where(eye, 1.0, 0.0)
    ln_w = vrow(_V_LNW)
    ln_b = vrow(_V_LNB)

    hs = [slice(h * HEAD_DIM, (h + 1) * HEAD_DIM) for h in range(n_heads)]

    def chunk_operands(r0):
        rs = pl.ds(r0, CHUNK)
        lc_c = lc_s[rs, :]
        lw_c = lw_s[rs, :]
        l_end = lc_s[pl.ds(r0 + CHUNK - 1, 1), :]
        p_in = jnp.exp(lc_c)
        p_prev = jnp.exp(lc_c - lw_c)
        p_inv = jnp.exp(-lc_c)
        p_end = jnp.exp(l_end - lc_c)
        p_last = jnp.exp(l_end)
        kk_c = kk_s[rs, :]
        b_c = kk_c * a_s[rs, :]
        k_c = k_s[rs, :]
        at = (-kk_c * p_prev).astype(BF16)
        bt = (b_c * p_inv).astype(BF16)
        bh = (b_c * p_end).astype(BF16)
        kt = (k_c * p_inv).astype(BF16)
        kh = (k_c * p_end).astype(BF16)
        rt = (r_s[rs, :] * p_in).astype(BF16)
        vc = v_s[rs, :].astype(BF16)
        per_head = [[x[:, sl] for sl in hs] for x in (at, bt, bh, kt, kh, rt, vc)]
        per_head.append([p_last[:, sl] for sl in hs])
        return per_head

    def group_body(gi, carry):
        g0 = pl.multiple_of(gi * (group * CHUNK), group * CHUNK)
        ops = [chunk_operands(g0 + c * CHUNK) for c in range(group)]
        at_h, bt_h, bh_h, kt_h, kh_h, rt_h, v_h, pl_h = ([x for c in range(group) for x in ops[c][q]]
                                                         for q in range(8))
        heads = range(group * n_heads)
        ar_h = [jnp.concatenate([at_h[h], rt_h[h]], axis=0) for h in heads]
        m_b = [_mm_nt(ar_h[h], bt_h[h]) for h in heads]
        m_k = [_mm_nt(ar_h[h], kt_h[h]) for h in heads]
        n_ab = [jnp.where(strict, m_b[h][:CHUNK], 0.0) for h in heads]
        a_ak = [jnp.where(strict, m_k[h][:CHUNK], 0.0) for h in heads]
        a_rb = [jnp.where(lower, m_b[h][CHUNK:], 0.0) for h in heads]
        a_rk = [jnp.where(lower, m_k[h][CHUNK:], 0.0) for h in heads]
        x_inv = [eye_f + n_ab[h] for h in heads]
        pw = [_mm(n_ab[h], n_ab[h]) for h in heads]
        akv = [_mm(a_ak[h], v_h[h]) for h in heads]
        n_sq = CHUNK.bit_length() - 2
        for it in range(n_sq):
            if it < n_sq - 1:
                st = [_mm(jnp.concatenate([x_inv[h], pw[h]], axis=0), pw[h]) for h in heads]
                x_inv = [x_inv[h] + st[h][:CHUNK] for h in heads]
                pw = [st[h][CHUNK:] for h in heads]
            else:
                st = [_mm(x_inv[h], pw[h]) for h in heads]
                x_inv = [x_inv[h] + st[h] for h in heads]
        w_h = [_mm(x_inv[h], at_h[h]) for h in heads]
        u0 = [_mm(x_inv[h], akv[h]) for h in heads]
        y0 = [_mm(a_rk[h], v_h[h]) + _mm(a_rb[h], u0[h]) for h in heads]
        r_p = [rt_h[h].astype(F32) + _mm(a_rb[h], w_h[h]) for h in heads]
        g_h = [jnp.where(eye, pl_h[h], 0.0) + _mm_tn(w_h[h], bh_h[h]) for h in heads]
        d_h = [_mm_tn(u0[h], bh_h[h]) + _mm_tn(v_h[h], kh_h[h]) for h in heads]
        s_h = [s_ref[h] for h in range(n_heads)]
        for c in range(group):
            rs = pl.ds(g0 + c * CHUNK, CHUNK)
            idx = [c * n_heads + h for h in range(n_heads)]
            y_h = [y0[i] + _mm_nt(r_p[i], s_h[h]) for h, i in enumerate(idx)]
            s_h = [_mm(s_h[h], g_h[i]) + d_h[i] for h, i in enumerate(idx)]
            y_heads = []
            for h in range(n_heads):
                mean = jnp.mean(y_h[h], axis=-1, keepdims=True)
                yc = y_h[h] - mean
                var = jnp.mean(yc * yc, axis=-1, keepdims=True)
                y_heads.append(yc * lax.rsqrt(var + GN_EPS))
            y_n = jnp.concatenate(y_heads, axis=-1)
            out = (y_n * ln_w + ln_b + bon_s[rs, :]) * g_s[rs, :]
            y_ref[rs, :] = out.astype(y_ref.dtype)
        for h in range(n_heads):
            s_ref[h] = s_h[h]
        return carry

    lax.fori_loop(0, n_chunks // group, group_body, 0)


def _rwkv(z, v_first, mu, vec, wl, v1, v2, *, batch, seq, n_heads):
    t, n_z = z.shape
    d_r = n_heads * HEAD_DIM
    tb = min(TB_RWKV, seq)
    nb = seq // tb
    has_vres = v_first is not None
    tok = lambda b, j: (b * nb + j, 0)
    const = lambda b, j: (0, 0)
    in_specs = [pl.BlockSpec((tb, n_z), tok)]
    args = [z]
    if has_vres:
        in_specs.append(pl.BlockSpec((tb, d_r), tok))
        args.append(v_first)
    ti = jnp.arange(tb)
    tril = ((ti[:, None] // CHUNK == ti[None, :] // CHUNK) & (ti[None, :] <= ti[:, None])).astype(BF16)
    hi = jnp.arange(d_r) // HEAD_DIM
    head_ones = (hi[:, None] == hi[None, :]).astype(BF16)
    in_specs += [pl.BlockSpec(mu.shape, const), pl.BlockSpec(vec.shape, const),
                 pl.BlockSpec(wl.shape, const), pl.BlockSpec(tril.shape, const),
                 pl.BlockSpec(head_ones.shape, const)]
    args += [mu, vec, wl, tril, head_ones]
    n_chunks = tb // CHUNK
    group = RWKV_CHUNK_GROUP if n_chunks % RWKV_CHUNK_GROUP == 0 else 1
    if has_vres:
        in_specs += [pl.BlockSpec(v1.shape, const), pl.BlockSpec(v2.shape, const)]
        args += [v1, v2]
        out_shape = jax.ShapeDtypeStruct((t, d_r), BF16)
        out_specs = pl.BlockSpec((tb, d_r), tok)
    else:
        out_shape = (jax.ShapeDtypeStruct((t, d_r), BF16), jax.ShapeDtypeStruct((t, d_r), F32))
        out_specs = (pl.BlockSpec((tb, d_r), tok), pl.BlockSpec((tb, d_r), tok))
    scratch = [pltpu.VMEM((n_heads, HEAD_DIM, HEAD_DIM), F32),
               pltpu.VMEM((V7X_SUBLANES, n_z), F32)]
    scratch += [pltpu.VMEM((tb, d_r), F32) for _ in range(9)]
    return pl.pallas_call(
        functools.partial(_rwkv_kernel, has_vres=has_vres, n_heads=n_heads, d_r=d_r, group=group),
        out_shape=out_shape,
        grid=(batch, nb),
        in_specs=in_specs,
        out_specs=out_specs,
        scratch_shapes=scratch,
        compiler_params=_params("arbitrary", "arbitrary"),
        name="rwkv_vres" if has_vres else "rwkv",
    )(*args)


def _attn_kernel(*refs, n_heads, n_parts):
    q_ref = refs[0]
    k_refs = refs[1:1 + n_parts]
    v_refs = refs[1 + n_parts:1 + 2 * n_parts]
    tab_ref = refs[1 + 2 * n_parts]
    o_ref = refs[2 + 2 * n_parts]
    qb = q_ref.shape[0]
    j = pl.program_id(1)
    scale = HEAD_DIM ** -0.5
    q = q_ref[...] * jnp.asarray(scale, q_ref.dtype)
    ks = [r[...] for r in k_refs]
    vs = [r[...] for r in v_refs]
    outs = []
    for h in range(n_heads):
        sl = slice(h * HEAD_DIM, (h + 1) * HEAD_DIM)
        q_h = q[:, sl]
        s_parts = []
        for p in range(n_parts):
            s = _mm_nt(q_h, ks[p][:, sl]) + tab_ref[h, :, p * qb:(p + 1) * qb]
            back = n_parts - 1 - p
            if back > 0:
                s = jnp.where(j >= back, s, NEG_INF)
            s_parts.append(s)
        m = s_parts[0].max(axis=-1, keepdims=True)
        for s in s_parts[1:]:
            m = jnp.maximum(m, s.max(axis=-1, keepdims=True))
        l = jnp.zeros_like(m)
        acc = jnp.zeros((qb, HEAD_DIM), F32)
        for p in range(n_parts):
            e = jnp.exp(s_parts[p] - m)
            l = l + e.sum(axis=-1, keepdims=True)
            acc = acc + _mm(e, vs[p][:, sl])
        outs.append(acc / l)
    o_ref[...] = jnp.concatenate(outs, axis=-1).astype(o_ref.dtype)


def _attn(qkv, table, *, batch, seq, n_heads):
    t = qkv.shape[0]
    d_a = n_heads * HEAD_DIM
    qb = min(QB_ATTN, seq)
    left = LEFT_CHUNKS * CHUNK
    assert left % qb == 0 and seq % qb == 0
    n_parts = left // qb + 1
    nb = seq // qb
    in_specs = [pl.BlockSpec((qb, d_a), lambda b, j: (b * nb + j, 0))]
    for p in range(n_parts):
        back = n_parts - 1 - p
        in_specs.append(pl.BlockSpec((qb, d_a), lambda b, j, back=back: (b * nb + jnp.maximum(j - back, 0), 1)))
    for p in range(n_parts):
        back = n_parts - 1 - p
        in_specs.append(pl.BlockSpec((qb, d_a), lambda b, j, back=back: (b * nb + jnp.maximum(j - back, 0), 2)))
    in_specs.append(pl.BlockSpec(table.shape, lambda b, j: (0, 0, 0)))
    return pl.pallas_call(
        functools.partial(_attn_kernel, n_heads=n_heads, n_parts=n_parts),
        out_shape=jax.ShapeDtypeStruct((t, d_a), BF16),
        grid=(batch, nb),
        in_specs=in_specs,
        out_specs=pl.BlockSpec((qb, d_a), lambda b, j: (b * nb + j, 0)),
        compiler_params=_params("parallel", "arbitrary"),
        name="attn",
    )(*([qkv] * (1 + 2 * n_parts)), table)


def _attn_table(rel_bias, qb):
    left = LEFT_CHUNKS * CHUNK
    n_keys = left + qb
    period = qb + n_keys - 1
    n_heads = rel_bias.shape[0]
    rel = left + (qb - 1) - jnp.arange(period)
    g = rel_bias[:, jnp.clip(rel, -(CHUNK - 1), MAX_REL) + (CHUNK - 1)].astype(F32)
    flat = jnp.tile(g, (1, qb + 1))[:, :qb * (period + 1)]
    bias = flat.reshape(n_heads, qb, period + 1)[:, ::-1, :n_keys]
    cq = jnp.arange(qb)[:, None] // CHUNK
    ck = jnp.arange(n_keys)[None, :] // CHUNK
    valid = (ck >= cq) & (ck <= cq + LEFT_CHUNKS)
    return jnp.where(valid[None], bias, NEG_INF)


_R_E1, _R_E2, _R_C1, _R_C2, _R_RANK1, _R_RANK2 = range(6)


def _outproj_route_kernel(yr_ref, ya_ref, h_ref, wor_ref, woa_ref, nw_ref, wrt_ref, brt_ref,
                          h1_ref, hn_ref, route_ref, cnt_ref, carry_ref):
    i = pl.program_id(0)

    @pl.when(i == 0)
    def _():
        carry_ref[...] = jnp.zeros_like(carry_ref)

    h1 = (h_ref[...] + jnp.dot(yr_ref[...], wor_ref[...], preferred_element_type=F32)
          + jnp.dot(ya_ref[...], woa_ref[...], preferred_element_type=F32))
    h1_ref[...] = h1
    hn = _rms(h1, nw_ref[...])
    _to_token_tiles(hn_ref, hn)
    nl = brt_ref.shape[1]
    hn_hi = hn.astype(BF16)
    hn_lo = (hn - hn_hi.astype(F32)).astype(BF16)
    part = jnp.dot(hn_hi, wrt_ref[...], preferred_element_type=F32)
    logits = (part[:, :nl] + part[:, nl:]
              + jnp.dot(hn_lo, wrt_ref[:, :nl], preferred_element_type=F32) + brt_ref[...])
    tm = logits.shape[0]
    lane = lax.broadcasted_iota(I32, (tm, nl), 1)
    lane_f = lane.astype(F32)
    ninf = -jnp.inf
    big = float(nl)
    is_g = lane < N_GROUPS
    gl = jnp.where(is_g, logits, ninf)
    g_max = gl.max(axis=-1, keepdims=True)
    g_sel = jnp.where(gl == g_max, lane_f, big).min(axis=-1, keepdims=True)
    p_g = 1.0 / jnp.where(is_g, jnp.exp(logits - g_max), 0.0).sum(axis=-1, keepdims=True)
    e_lo = N_GROUPS + EXPERTS_PER_GROUP * g_sel
    in_grp = (lane_f >= e_lo) & (lane_f < e_lo + EXPERTS_PER_GROUP)
    el = jnp.where(in_grp, logits, ninf)
    m1 = el.max(axis=-1, keepdims=True)
    i1 = jnp.where(el == m1, lane_f, big).min(axis=-1, keepdims=True)
    el2 = jnp.where(lane_f == i1, ninf, el)
    m2 = el2.max(axis=-1, keepdims=True)
    i2 = jnp.where(el2 == m2, lane_f, big).min(axis=-1, keepdims=True)
    t2 = jnp.exp(m2 - m1)
    c1 = p_g / (1.0 + t2)
    c2 = p_g * t2 / (1.0 + t2)
    e1 = i1 - N_GROUPS
    e2 = i2 - N_GROUPS
    oh1 = lane_f == e1
    oh2 = lane_f == e2
    ohs = jnp.where(oh1 | oh2, 1.0, 0.0)
    ri = lax.broadcasted_iota(I32, (tm, tm), 0)
    rj = lax.broadcasted_iota(I32, (tm, tm), 1)
    before = jnp.where(rj < ri, 1.0, 0.0).astype(BF16)
    cnt = jnp.dot(before, ohs.astype(BF16), preferred_element_type=F32) + carry_ref[0:1, :]
    rank1 = jnp.where(oh1, cnt, 0.0).sum(axis=-1, keepdims=True)
    rank2 = jnp.where(oh2, cnt, 0.0).sum(axis=-1, keepdims=True)
    new_carry = carry_ref[0:1, :] + ohs.sum(axis=0, keepdims=True)
    carry_ref[0:1, :] = new_carry
    cnt_ref[...] = jnp.broadcast_to(new_carry, cnt_ref.shape)
    route = jnp.zeros((tm, nl), F32)
    for idx, val in ((_R_E1, e1), (_R_E2, e2), (_R_C1, c1), (_R_C2, c2),
                     (_R_RANK1, rank1), (_R_RANK2, rank2)):
        route = jnp.where(lane == idx, val, route)
    route_ref[...] = route


def _outproj_route(yr, ya, h, wor, woa, nw, wrt, brt):
    t, d = h.shape
    tm = min(TM_ROUTE, t)
    d_r, d_a = yr.shape[1], ya.shape[1]
    nl = brt.shape[1]
    tok = lambda i: (i, 0)
    const = lambda i: (0, 0)
    return pl.pallas_call(
        _outproj_route_kernel,
        out_shape=(jax.ShapeDtypeStruct((t, d), F32),
                   jax.ShapeDtypeStruct((t * V7X_SUBLANES, V7X_LANES), F32),
                   jax.ShapeDtypeStruct((t, nl), F32), jax.ShapeDtypeStruct((V7X_SUBLANES, nl), F32)),
        grid=(t // tm,),
        in_specs=[pl.BlockSpec((tm, d_r), tok), pl.BlockSpec((tm, d_a), tok), pl.BlockSpec((tm, d), tok),
                  pl.BlockSpec((d_r, d), const), pl.BlockSpec((d_a, d), const),
                  pl.BlockSpec((1, d), const), pl.BlockSpec(wrt.shape, const), pl.BlockSpec((1, nl), const)],
        out_specs=(pl.BlockSpec((tm, d), tok), pl.BlockSpec((tm * V7X_SUBLANES, V7X_LANES), tok),
                   pl.BlockSpec((tm, nl), tok), pl.BlockSpec((V7X_SUBLANES, nl), const)),
        scratch_shapes=[pltpu.VMEM((V7X_SUBLANES, nl), F32)],
        compiler_params=_params("arbitrary"),
        name="outproj_route",
    )(yr, ya, h, wor, woa, nw, wrt, brt)


def _load_indices(idx_hbm, i, idx_smem, sem):
    n = idx_smem.shape[0]
    cp = pltpu.make_async_copy(idx_hbm.at[pl.ds(pl.multiple_of(i * n, n), n)], idx_smem, sem)
    cp.start()
    cp.wait()


def _experts_kernel(te_ref, tv_ref, src_hbm, x_hbm, w1_ref, w3_ref, w2_ref, y_ref,
                    idx_smem, xbuf, idx_sem, row_sem, *, tm):
    i = pl.program_id(0)
    nt = pl.num_programs(0)
    cur = lax.rem(i, 2)
    nxt = 1 - cur

    def gather(step, buf):
        _load_indices(src_hbm, step, idx_smem, idx_sem)

        def issue(r, carry):
            pltpu.make_async_copy(_token_tile(x_hbm, idx_smem[r]), _token_tile(xbuf.at[buf], r),
                                  row_sem.at[buf]).start()
            return carry

        lax.fori_loop(0, tm, issue, 0, unroll=8)

    @pl.when((i == 0) & (tv_ref[0] > 0))
    def _():
        gather(0, 0)

    @pl.when((i + 1 < nt) & (tv_ref[jnp.minimum(i + 1, nt - 1)] > 0))
    def _():
        gather(i + 1, nxt)

    @pl.when(tv_ref[i] > 0)
    def _():
        pltpu.make_async_copy(x_hbm.at[pl.ds(0, xbuf.shape[1]), :], xbuf.at[cur], row_sem.at[cur]).wait()
        x = _from_token_tiles(xbuf.at[cur], tm).astype(BF16)
        h_gate = jnp.dot(x, w1_ref[...], preferred_element_type=F32)
        h_up = jnp.dot(x, w3_ref[...], preferred_element_type=F32)
        hid = (h_gate * jax.nn.sigmoid(h_gate) * h_up).astype(BF16)
        _to_token_tiles(y_ref, jnp.dot(hid, w2_ref[...], preferred_element_type=F32))

    @pl.when(tv_ref[i] == 0)
    def _():
        y_ref[...] = jnp.zeros_like(y_ref)


def _experts(x, src, w1, w3, w2, tile_expert, tile_valid):
    tm = TM_EXPERT
    nt = tile_expert.shape[0]
    d, f = w1.shape[1:]
    rec = src.shape[0] // nt
    grid_spec = pltpu.PrefetchScalarGridSpec(
        num_scalar_prefetch=2,
        grid=(nt,),
        in_specs=[pl.BlockSpec(memory_space=pl.ANY), pl.BlockSpec(memory_space=pl.ANY),
                  pl.BlockSpec((None, d, f), lambda i, te, tv: (te[i], 0, 0)),
                  pl.BlockSpec((None, d, f), lambda i, te, tv: (te[i], 0, 0)),
                  pl.BlockSpec((None, f, d), lambda i, te, tv: (te[i], 0, 0))],
        out_specs=pl.BlockSpec((tm * V7X_SUBLANES, V7X_LANES), lambda i, te, tv: (i, 0)),
        scratch_shapes=[pltpu.SMEM((rec,), I32), pltpu.VMEM((2, tm * V7X_SUBLANES, V7X_LANES), F32),
                        pltpu.SemaphoreType.DMA, pltpu.SemaphoreType.DMA((2,))],
    )
    return pl.pallas_call(
        functools.partial(_experts_kernel, tm=tm),
        out_shape=jax.ShapeDtypeStruct((nt * tm * V7X_SUBLANES, V7X_LANES), F32),
        grid_spec=grid_spec,
        compiler_params=_params("arbitrary"),
        name="experts",
    )(tile_expert, tile_valid, src, x, w1, w3, w2)


def _combine_ple_kernel(pos_hbm, ys_hbm, h_ref, route_ref, p_ref, nw_ref, wg_ref, bg_ref, wp_ref,
                        fw_ref, o_ref, idx_smem, ybuf, idx_sem, row_sem, *, tm, final):
    i = pl.program_id(0)
    nb = pl.num_programs(0)
    cur = lax.rem(i, 2)
    nxt = 1 - cur

    def gather(step, buf):
        _load_indices(pos_hbm, step, idx_smem, idx_sem)

        def issue(tt, carry):
            for s in range(2):
                pltpu.make_async_copy(_token_tile(ys_hbm, idx_smem[s * tm + tt]),
                                      _token_tile(ybuf.at[buf, s], tt), row_sem.at[buf]).start()
            return carry

        lax.fori_loop(0, tm, issue, 0, unroll=8)

    def wait_gather(buf):
        for s in range(2):
            pltpu.make_async_copy(ys_hbm.at[pl.ds(0, ybuf.shape[2]), :], ybuf.at[buf, s],
                                  row_sem.at[buf]).wait()

    @pl.when(i == 0)
    def _():
        gather(0, 0)

    @pl.when(i + 1 < nb)
    def _():
        gather(i + 1, nxt)

    wait_gather(cur)

    route = route_ref[...]
    c1 = route[:, _R_C1:_R_C1 + 1]
    c2 = route[:, _R_C2:_R_C2 + 1]
    h2 = (h_ref[...] + c1 * _from_token_tiles(ybuf.at[cur, 0], tm)
          + c2 * _from_token_tiles(ybuf.at[cur, 1], tm))
    hn = _rms(h2, nw_ref[...]).astype(BF16)
    gate = jax.nn.sigmoid(jnp.dot(hn, wg_ref[...], preferred_element_type=F32) + bg_ref[...])
    h3 = h2 + gate * jnp.dot(p_ref[...].astype(BF16), wp_ref[...], preferred_element_type=F32)
    if final:
        h3 = _rms(h3, fw_ref[...])
    o_ref[...] = h3


def _combine_ple(pos_tiles, ys, h, route, p, nw, wg, bg, wp, fw, *, tm, final):
    t, d = h.shape
    nl = route.shape[1]
    dp = p.shape[1]
    tok = lambda i: (i, 0)
    const = lambda i: (0, 0)
    return pl.pallas_call(
        functools.partial(_combine_ple_kernel, tm=tm, final=final),
        out_shape=jax.ShapeDtypeStruct((t, d), F32),
        grid=(t // tm,),
        in_specs=[pl.BlockSpec(memory_space=pl.ANY), pl.BlockSpec(memory_space=pl.ANY),
                  pl.BlockSpec((tm, d), tok), pl.BlockSpec((tm, nl), tok), pl.BlockSpec((tm, dp), tok),
                  pl.BlockSpec((1, d), const), pl.BlockSpec((d, d), const), pl.BlockSpec((1, d), const),
                  pl.BlockSpec((dp, d), const), pl.BlockSpec((1, d), const)],
        out_specs=pl.BlockSpec((tm, d), tok),
        scratch_shapes=[pltpu.SMEM((pos_tiles.shape[0] // (t // tm),), I32),
                        pltpu.VMEM((2, 2, tm * V7X_SUBLANES, V7X_LANES), F32),
                        pltpu.SemaphoreType.DMA, pltpu.SemaphoreType.DMA((2,))],
        compiler_params=_params("arbitrary"),
        name="combine_ple_final" if final else "combine_ple",
    )(pos_tiles, ys, h, route, p, nw, wg, bg, wp, fw)


_SMEM_RECORD_WORDS = 1024


def _index_records(pos1, pos2, tm, extra=None):
    nb = pos1.shape[0] // tm
    parts = [pos1.reshape(nb, tm), pos2.reshape(nb, tm)]
    if extra is not None:
        parts.append(extra.reshape(nb, -1))
    rec = jnp.concatenate(parts, axis=1)
    pad = -rec.shape[1] % _SMEM_RECORD_WORDS
    return jnp.pad(rec, ((0, 0), (0, pad))).reshape(-1)


def _lookup(table, idx):
    ids = jnp.arange(table.shape[0], dtype=I32)
    return jnp.sum(jnp.where(idx[:, None] == ids[None, :], table[None, :], 0), axis=1)


def _bucket(ends, x):
    return jnp.minimum(jnp.sum((x[:, None] >= ends[None, :]).astype(I32), axis=1), ends.shape[0] - 1)


def kernel(x, p, norm_mix_w, w_in, rwkv_mu, rwkv_w0, rwkv_w2, rwkv_a0, rwkv_a2, rwkv_g2, rwkv_k_k, rwkv_k_a, rwkv_r_k, rwkv_ln_w, rwkv_ln_b, rwkv_v0, rwkv_v1, rwkv_v2, att_rel_bias, w_out, norm_ffn_w, router_group_w, router_group_b, router_expert_w, router_expert_b, expert_w1, expert_w3, expert_w2, norm_ple_w, ple_gate_w, ple_gate_b, ple_proj_w, final_norm_w):
    batch, seq, d = x.shape
    depth = w_in.shape[0]
    t = batch * seq
    d_r = rwkv_w0.shape[1]
    n_heads_r = d_r // HEAD_DIM
    n_rwkv_in = rwkv_mu.shape[1]
    d_a = (w_in.shape[2] - n_rwkv_in) // 3
    n_heads_a = d_a // HEAD_DIM
    n_dec, n_iclr, n_gate = rwkv_w2.shape[1], rwkv_a2.shape[1], rwkv_g2.shape[1]
    assert n_dec == n_iclr and n_gate == n_dec + n_iclr
    n_lo = n_dec + n_iclr + n_gate
    f_exp = expert_w1.shape[-1]
    assert d == V7X_SUBLANES * V7X_LANES
    n_rows = 2 * t + N_EXPERTS * TM_EXPERT
    n_tiles = n_rows // TM_EXPERT
    qb = min(QB_ATTN, seq)

    h = x.reshape(t, d)
    v_first = None
    for i in range(depth):
        wr = w_in[i, :, :n_rwkv_in].astype(BF16)
        wa = w_in[i, :, n_rwkv_in:].astype(BF16)
        wl = jnp.zeros((n_lo, 3 * d_r), F32)
        wl = wl.at[:n_dec, :d_r].set(rwkv_w2[i])
        wl = wl.at[n_dec:n_dec + n_iclr, d_r:2 * d_r].set(rwkv_a2[i])
        wl = wl.at[n_dec + n_iclr:, 2 * d_r:].set(rwkv_g2[i]).astype(BF16)
        v0 = rwkv_v0[i - 1] if i > 0 else jnp.zeros((d_r,), F32)
        vec = jnp.stack([rwkv_w0[i], rwkv_a0[i], rwkv_k_k[i], rwkv_k_a[i], rwkv_r_k[i],
                         rwkv_ln_w[i], rwkv_ln_b[i], v0])
        if i > 0:
            n_vr = rwkv_v1.shape[2]
            v1 = jnp.zeros((d_r, V7X_LANES), F32).at[:, :n_vr].set(rwkv_v1[i - 1]).astype(BF16)
            v2 = jnp.zeros((V7X_LANES, d_r), F32).at[:n_vr, :].set(rwkv_v2[i - 1]).astype(BF16)
        else:
            v1 = v2 = None
        table = _attn_table(att_rel_bias[i], qb)
        wor = w_out[i, :d_r].astype(BF16)
        woa = w_out[i, d_r:].astype(BF16)
        n_rt = N_GROUPS + N_EXPERTS
        wrt = jnp.zeros((d, V7X_LANES), F32)
        wrt = wrt.at[:, :N_GROUPS].set(router_group_w[i]).at[:, N_GROUPS:n_rt].set(router_expert_w[i])
        wrt_hi = wrt.astype(BF16)
        wrt = jnp.concatenate([wrt_hi, (wrt - wrt_hi.astype(F32)).astype(BF16)], axis=1)
        brt = jnp.zeros((1, V7X_LANES), F32)
        brt = brt.at[0, :N_GROUPS].set(router_group_b[i]).at[0, N_GROUPS:n_rt].set(router_expert_b[i])
        w1 = expert_w1[i].reshape(N_EXPERTS, d, f_exp).astype(BF16)
        w3 = expert_w3[i].reshape(N_EXPERTS, d, f_exp).astype(BF16)
        w2 = expert_w2[i].reshape(N_EXPERTS, f_exp, d).astype(BF16)

        z_r, qkv = _norm_proj(h, norm_mix_w[i][None], wr, wa)
        if i == 0:
            y_r, v_first = _rwkv(z_r, None, rwkv_mu[i][None], vec, wl, None, None,
                                 batch=batch, seq=seq, n_heads=n_heads_r)
        else:
            y_r = _rwkv(z_r, v_first, rwkv_mu[i][None], vec, wl, v1, v2,
                        batch=batch, seq=seq, n_heads=n_heads_r)
        y_a = _attn(qkv, table, batch=batch, seq=seq, n_heads=n_heads_a)

        h1, hn, route, cnt = _outproj_route(y_r, y_a, h, wor, woa, norm_ffn_w[i][None], wrt, brt)
        ri = route[:, :V7X_SUBLANES].astype(I32)
        counts = cnt[0, :N_EXPERTS].astype(I32)
        padded = ((counts + TM_EXPERT - 1) // TM_EXPERT) * TM_EXPERT
        p_end = jnp.cumsum(padded)
        p_start = p_end - padded
        pos1 = _lookup(p_start, ri[:, _R_E1]) + ri[:, _R_RANK1]
        pos2 = _lookup(p_start, ri[:, _R_E2]) + ri[:, _R_RANK2]
        tile_start = jnp.arange(n_tiles, dtype=I32) * TM_EXPERT
        tile_expert = _bucket(p_end, tile_start)
        tile_valid = (tile_start < p_end[-1]).astype(I32)

        tok = jnp.arange(t, dtype=I32)
        src = jnp.zeros((n_rows,), I32).at[pos1].set(tok, unique_indices=True)
        src = src.at[pos2].set(tok, unique_indices=True)
        src = jnp.pad(src.reshape(n_tiles, TM_EXPERT),
                      ((0, 0), (0, -TM_EXPERT % _SMEM_RECORD_WORDS))).reshape(-1)
        ys = _experts(hn, src, w1, w3, w2, tile_expert, tile_valid)

        tm_c = min(TM_COMBINE, t)
        h = _combine_ple(_index_records(pos1, pos2, tm_c), ys, h1, route, p[i].reshape(t, -1),
                         norm_ple_w[i][None], ple_gate_w[i].astype(BF16), ple_gate_b[i][None],
                         ple_proj_w[i].astype(BF16), final_norm_w[None],
                         tm=tm_c, final=(i == depth - 1))
    return h.reshape(batch, seq, d)
```

```python
import functools

import jax
import jax.numpy as jnp
from jax import lax
from jax.experimental import pallas as pl
from jax.experimental.pallas import tpu as pltpu

F32 = jnp.float32
BF16 = jnp.bfloat16
I32 = jnp.int32

CHUNK = 64
HEAD_DIM = 64
LEFT_CHUNKS = 8
MAX_REL = 256
N_GROUPS = 4
EXPERTS_PER_GROUP = 8
N_EXPERTS = N_GROUPS * EXPERTS_PER_GROUP
RMS_EPS = 1e-6
GN_EPS = 64e-5
NEG_INF = -1e30

V7X_LANES = 128
V7X_SUBLANES = 8
V7X_VMEM_LIMIT_BYTES = 48 * 1024 * 1024

TM_PROJ = 512
TB_RWKV = 256
RWKV_CHUNK_GROUP = 2
QB_ATTN = 256
TM_ROUTE = 512
TM_DISPATCH = 512
TM_EXPERT = 256
TM_COMBINE = 512


def _params(*sem):
    return pltpu.CompilerParams(dimension_semantics=sem, vmem_limit_bytes=V7X_VMEM_LIMIT_BYTES)


def _rms(x, w):
    return x * lax.rsqrt(jnp.mean(x * x, axis=-1, keepdims=True) + RMS_EPS) * w


def _mm(a, b):
    return jnp.dot(a.astype(BF16), b.astype(BF16), preferred_element_type=F32)


def _mm_nt(a, b):
    return lax.dot_general(a.astype(BF16), b.astype(BF16), (((1,), (1,)), ((), ())),
                           preferred_element_type=F32)


def _mm_tn(a, b):
    return lax.dot_general(a.astype(BF16), b.astype(BF16), (((0,), (0,)), ((), ())),
                           preferred_element_type=F32)


def _to_token_tiles(ref, x):
    m, d = x.shape
    for s in range(d // V7X_LANES):
        ref[pl.ds(s, m, stride=V7X_SUBLANES), :] = x[:, s * V7X_LANES:(s + 1) * V7X_LANES]


def _from_token_tiles(ref, m):
    return jnp.concatenate([ref[pl.ds(s, m, stride=V7X_SUBLANES), :] for s in range(V7X_SUBLANES)],
                           axis=-1)


def _token_tile(ref, row):
    return ref.at[pl.ds(pl.multiple_of(row * V7X_SUBLANES, V7X_SUBLANES), V7X_SUBLANES), :]


def _split3(x):
    hi = x.astype(BF16)
    r1 = x - hi.astype(F32)
    mid = r1.astype(BF16)
    lo = (r1 - mid.astype(F32)).astype(BF16)
    return hi, mid, lo


def _mm_exact_lhs(a_bf16, x):
    hi, mid, lo = _split3(x)
    return (jnp.dot(a_bf16, hi, preferred_element_type=F32)
            + jnp.dot(a_bf16, mid, preferred_element_type=F32)
            + jnp.dot(a_bf16, lo, preferred_element_type=F32))


def _mm_exact_rhs(x, b_bf16):
    hi, mid, lo = _split3(x)
    return (jnp.dot(hi, b_bf16, preferred_element_type=F32)
            + jnp.dot(mid, b_bf16, preferred_element_type=F32)
            + jnp.dot(lo, b_bf16, preferred_element_type=F32))


def _norm_proj_kernel(h_ref, nw_ref, wr_ref, wa_ref, zr_ref, qkv_ref):
    hn = _rms(h_ref[...], nw_ref[...]).astype(BF16)
    zr_ref[...] = jnp.dot(hn, wr_ref[...], preferred_element_type=F32)
    qkv_ref[...] = jnp.dot(hn, wa_ref[...], preferred_element_type=F32).astype(BF16)


def _norm_proj(h, nw, wr, wa):
    t, d = h.shape
    tm = min(TM_PROJ, t)
    n_r, n_a = wr.shape[1], wa.shape[1]
    return pl.pallas_call(
        _norm_proj_kernel,
        out_shape=(jax.ShapeDtypeStruct((t, n_r), F32), jax.ShapeDtypeStruct((t, n_a), BF16)),
        grid=(t // tm,),
        in_specs=[pl.BlockSpec((tm, d), lambda i: (i, 0)),
                  pl.BlockSpec((1, d), lambda i: (0, 0)),
                  pl.BlockSpec((d, n_r), lambda i: (0, 0)),
                  pl.BlockSpec((d, n_a), lambda i: (0, 0))],
        out_specs=(pl.BlockSpec((tm, n_r), lambda i: (i, 0)),
                   pl.BlockSpec((tm, n_a), lambda i: (i, 0))),
        compiler_params=_params("parallel"),
        name="norm_proj",
    )(h, nw, wr, wa)


_V_W0, _V_A0, _V_KK, _V_KA, _V_RK, _V_LNW, _V_LNB, _V_V0 = range(8)


def _rwkv_kernel(*refs, has_vres, n_heads, d_r, group):
    if has_vres:
        (z_ref, vf_ref, mu_ref, vec_ref, wl_ref, tril_ref, ones_ref, v1_ref, v2_ref, y_ref,
         s_ref, carry_ref, r_s, k_s, v_s, kk_s, a_s, lc_s, lw_s, bon_s, g_s) = refs
        vf_out_ref = None
    else:
        (z_ref, mu_ref, vec_ref, wl_ref, tril_ref, ones_ref, y_ref, vf_out_ref,
         s_ref, carry_ref, r_s, k_s, v_s, kk_s, a_s, lc_s, lw_s, bon_s, g_s) = refs
    tb = z_ref.shape[0]
    n_chunks = tb // CHUNK
    j = pl.program_id(1)

    @pl.when(j == 0)
    def _():
        s_ref[...] = jnp.zeros_like(s_ref)
        carry_ref[...] = jnp.zeros_like(carry_ref)

    z = z_ref[...]
    row = lax.broadcasted_iota(I32, z.shape, 0)
    z_prev = jnp.where(row == 0, carry_ref[0:1, :], pltpu.roll(z, 1, axis=0))
    carry_ref[0:1, :] = z[tb - 1:tb, :]
    zs = z + (z_prev - z) * mu_ref[...]

    vec = vec_ref[...]

    def vrow(i):
        return vec[i:i + 1, :]

    r = zs[:, 0:d_r]
    k = zs[:, d_r:2 * d_r]
    v = zs[:, 2 * d_r:3 * d_r]
    lo = zs[:, 3 * d_r:]
    n_lo = lo.shape[1]
    lane = lax.broadcasted_iota(I32, lo.shape, 1)
    lo_act = jnp.where(lane < n_lo // 4, jnp.tanh(lo),
                       jnp.where(lane < n_lo // 2, lo, jax.nn.sigmoid(lo)))
    lo_out = _mm(lo_act, wl_ref[...])

    if has_vres:
        vv = _mm(_mm(v, v1_ref[...]), v2_ref[...])
        v = v + (vf_ref[...] - v) * jax.nn.sigmoid(vrow(_V_V0) + vv)
    else:
        vf_out_ref[...] = v

    w_log = -jax.nn.softplus(-(vrow(_V_W0) + lo_out[:, 0:d_r])) - 0.5
    lw = -jnp.exp(w_log)
    a = jax.nn.sigmoid(vrow(_V_A0) + lo_out[:, d_r:2 * d_r])
    g = lo_out[:, 2 * d_r:3 * d_r]

    head_ones = ones_ref[...]
    kk = k * vrow(_V_KK)
    kk = kk * lax.rsqrt(jnp.maximum(_mm_exact_rhs(kk * kk, head_ones), 1e-24))
    k2 = k * (1.0 + (a - 1.0) * vrow(_V_KA))
    bonus = _mm_exact_rhs(r * k2 * vrow(_V_RK), head_ones) * v

    lc = _mm_exact_lhs(tril_ref[...], lw)

    r_s[...] = r
    k_s[...] = k2
    v_s[...] = v
    kk_s[...] = kk
    a_s[...] = a
    lc_s[...] = lc
    lw_s[...] = lw
    bon_s[...] = bonus
    g_s[...] = g

    ci = lax.broadcasted_iota(I32, (CHUNK, CHUNK), 0)
    cj = lax.broadcasted_iota(I32, (CHUNK, CHUNK), 1)
    strict = cj < ci
    lower = cj <= ci
    eye = ci == cj
    eye_f = jnp.where(eye, 1.0, 0.0)
    ln_w = vrow(_V_LNW)
    ln_b = vrow(_V_LNB)

    hs = [slice(h * HEAD_DIM, (h + 1) * HEAD_DIM) for h in range(n_heads)]

    def chunk_operands(r0):
        rs = pl.ds(r0, CHUNK)
        lc_c = lc_s[rs, :]
        lw_c = lw_s[rs, :]
        l_end = lc_s[pl.ds(r0 + CHUNK - 1, 1), :]
        p_in = jnp.exp(lc_c)
        p_prev = jnp.exp(lc_c - lw_c)
        p_inv = jnp.exp(-lc_c)
        p_end = jnp.exp(l_end - lc_c)
        p_last = jnp.exp(l_end)
        kk_c = kk_s[rs, :]
        b_c = kk_c * a_s[rs, :]
        k_c = k_s[rs, :]
        at = (-kk_c * p_prev).astype(BF16)
        bt = (b_c * p_inv).astype(BF16)
        bh = (b_c * p_end).astype(BF16)
        kt = (k_c * p_inv).astype(BF16)
        kh = (k_c * p_end).astype(BF16)
        rt = (r_s[rs, :] * p_in).astype(BF16)
        vc = v_s[rs, :].astype(BF16)
        per_head = [[x[:, sl] for sl in hs] for x in (at, bt, bh, kt, kh, rt, vc)]
        per_head.append([p_last[:, sl] for sl in hs])
        return per_head

    def group_body(gi, carry):
        g0 = pl.multiple_of(gi * (group * CHUNK), group * CHUNK)
        ops = [chunk_operands(g0 + c * CHUNK) for c in range(group)]
        at_h, bt_h, bh_h, kt_h, kh_h, rt_h, v_h, pl_h = ([x for c in range(group) for x in ops[c][q]]
                                                         for q in range(8))
        heads = range(group * n_heads)
        ar_h = [jnp.concatenate([at_h[h], rt_h[h]], axis=0) for h in heads]
        m_b = [_mm_nt(ar_h[h], bt_h[h]) for h in heads]
        m_k = [_mm_nt(ar_h[h], kt_h[h]) for h in heads]
        n_ab = [jnp.where(strict, m_b[h][:CHUNK], 0.0) for h in heads]
        a_ak = [jnp.where(strict, m_k[h][:CHUNK], 0.0) for h in heads]
        a_rb = [jnp.where(lower, m_b[h][CHUNK:], 0.0) for h in heads]
        a_rk = [jnp.where(lower, m_k[h][CHUNK:], 0.0) for h in heads]
        x_inv = [eye_f + n_ab[h] for h in heads]
        pw = [_mm(n_ab[h], n_ab[h]) for h in heads]
        akv = [_mm(a_ak[h], v_h[h]) for h in heads]
        n_sq = CHUNK.bit_length() - 2
        for it in range(n_sq):
            if it < n_sq - 1:
                st = [_mm(jnp.concatenate([x_inv[h], pw[h]], axis=0), pw[h]) for h in heads]
                x_inv = [x_inv[h] + st[h][:CHUNK] for h in heads]
                pw = [st[h][CHUNK:] for h in heads]
            else:
                st = [_mm(x_inv[h], pw[h]) for h in heads]
                x_inv = [x_inv[h] + st[h] for h in heads]
        w_h = [_mm(x_inv[h], at_h[h]) for h in heads]
        u0 = [_mm(x_inv[h], akv[h]) for h in heads]
        y0 = [_mm(a_rk[h], v_h[h]) + _mm(a_rb[h], u0[h]) for h in heads]
        r_p = [rt_h[h].astype(F32) + _mm(a_rb[h], w_h[h]) for h in heads]
        g_h = [jnp.where(eye, pl_h[h], 0.0) + _mm_tn(w_h[h], bh_h[h]) for h in heads]
        d_h = [_mm_tn(u0[h], bh_h[h]) + _mm_tn(v_h[h], kh_h[h]) for h in heads]
        s_h = [s_ref[h] for h in range(n_heads)]
        for c in range(group):
            rs = pl.ds(g0 + c * CHUNK, CHUNK)
            idx = [c * n_heads + h for h in range(n_heads)]
            y_h = [y0[i] + _mm_nt(r_p[i], s_h[h]) for h, i in enumerate(idx)]
            s_h = [_mm(s_h[h], g_h[i]) + d_h[i] for h, i in enumerate(idx)]
            y_heads = []
            for h in range(n_heads):
                mean = jnp.mean(y_h[h], axis=-1, keepdims=True)
                yc = y_h[h] - mean
                var = jnp.mean(yc * yc, axis=-1, keepdims=True)
                y_heads.append(yc * lax.rsqrt(var + GN_EPS))
            y_n = jnp.concatenate(y_heads, axis=-1)
            out = (y_n * ln_w + ln_b + bon_s[rs, :]) * g_s[rs, :]
            y_ref[rs, :] = out.astype(y_ref.dtype)
        for h in range(n_heads):
            s_ref[h] = s_h[h]
        return carry

    lax.fori_loop(0, n_chunks // group, group_body, 0)


def _rwkv(z, v_first, mu, vec, wl, v1, v2, *, batch, seq, n_heads):
    t, n_z = z.shape
    d_r = n_heads * HEAD_DIM
    tb = min(TB_RWKV, seq)
    nb = seq // tb
    has_vres = v_first is not None
    tok = lambda b, j: (b * nb + j, 0)
    const = lambda b, j: (0, 0)
    in_specs = [pl.BlockSpec((tb, n_z), tok)]
    args = [z]
    if has_vres:
        in_specs.append(pl.BlockSpec((tb, d_r), tok))
        args.append(v_first)
    ti = jnp.arange(tb)
    tril = ((ti[:, None] // CHUNK == ti[None, :] // CHUNK) & (ti[None, :] <= ti[:, None])).astype(BF16)
    hi = jnp.arange(d_r) // HEAD_DIM
    head_ones = (hi[:, None] == hi[None, :]).astype(BF16)
    in_specs += [pl.BlockSpec(mu.shape, const), pl.BlockSpec(vec.shape, const),
                 pl.BlockSpec(wl.shape, const), pl.BlockSpec(tril.shape, const),
                 pl.BlockSpec(head_ones.shape, const)]
    args += [mu, vec, wl, tril, head_ones]
    n_chunks = tb // CHUNK
    group = RWKV_CHUNK_GROUP if n_chunks % RWKV_CHUNK_GROUP == 0 else 1
    if has_vres:
        in_specs += [pl.BlockSpec(v1.shape, const), pl.BlockSpec(v2.shape, const)]
        args += [v1, v2]
        out_shape = jax.ShapeDtypeStruct((t, d_r), BF16)
        out_specs = pl.BlockSpec((tb, d_r), tok)
    else:
        out_shape = (jax.ShapeDtypeStruct((t, d_r), BF16), jax.ShapeDtypeStruct((t, d_r), F32))
        out_specs = (pl.BlockSpec((tb, d_r), tok), pl.BlockSpec((tb, d_r), tok))
    scratch = [pltpu.VMEM((n_heads, HEAD_DIM, HEAD_DIM), F32),
               pltpu.VMEM((V7X_SUBLANES, n_z), F32)]
    scratch += [pltpu.VMEM((tb, d_r), F32) for _ in range(9)]
    return pl.pallas_call(
        functools.partial(_rwkv_kernel, has_vres=has_vres, n_heads=n_heads, d_r=d_r, group=group),
        out_shape=out_shape,
        grid=(batch, nb),
        in_specs=in_specs,
        out_specs=out_specs,
        scratch_shapes=scratch,
        compiler_params=_params("arbitrary", "arbitrary"),
        name="rwkv_vres" if has_vres else "rwkv",
    )(*args)


def _attn_kernel(*refs, n_heads, n_parts):
    q_ref = refs[0]
    k_refs = refs[1:1 + n_parts]
    v_refs = refs[1 + n_parts:1 + 2 * n_parts]
    tab_ref = refs[1 + 2 * n_parts]
    o_ref = refs[2 + 2 * n_parts]
    qb = q_ref.shape[0]
    j = pl.program_id(1)
    scale = HEAD_DIM ** -0.5
    q = q_ref[...] * jnp.asarray(scale, q_ref.dtype)
    ks = [r[...] for r in k_refs]
    vs = [r[...] for r in v_refs]
    outs = []
    for h in range(n_heads):
        sl = slice(h * HEAD_DIM, (h + 1) * HEAD_DIM)
        q_h = q[:, sl]
        s_parts = []
        for p in range(n_parts):
            s = _mm_nt(q_h, ks[p][:, sl]) + tab_ref[h, :, p * qb:(p + 1) * qb]
            back = n_parts - 1 - p
            if back > 0:
                s = jnp.where(j >= back, s, NEG_INF)
            s_parts.append(s)
        m = s_parts[0].max(axis=-1, keepdims=True)
        for s in s_parts[1:]:
            m = jnp.maximum(m, s.max(axis=-1, keepdims=True))
        l = jnp.zeros_like(m)
        acc = jnp.zeros((qb, HEAD_DIM), F32)
        for p in range(n_parts):
            e = jnp.exp(s_parts[p] - m)
            l = l + e.sum(axis=-1, keepdims=True)
            acc = acc + _mm(e, vs[p][:, sl])
        outs.append(acc / l)
    o_ref[...] = jnp.concatenate(outs, axis=-1).astype(o_ref.dtype)


def _attn(qkv, table, *, batch, seq, n_heads):
    t = qkv.shape[0]
    d_a = n_heads * HEAD_DIM
    qb = min(QB_ATTN, seq)
    left = LEFT_CHUNKS * CHUNK
    assert left % qb == 0 and seq % qb == 0
    n_parts = left // qb + 1
    nb = seq // qb
    in_specs = [pl.BlockSpec((qb, d_a), lambda b, j: (b * nb + j, 0))]
    for p in range(n_parts):
        back = n_parts - 1 - p
        in_specs.append(pl.BlockSpec((qb, d_a), lambda b, j, back=back: (b * nb + jnp.maximum(j - back, 0), 1)))
    for p in range(n_parts):
        back = n_parts - 1 - p
        in_specs.append(pl.BlockSpec((qb, d_a), lambda b, j, back=back: (b * nb + jnp.maximum(j - back, 0), 2)))
    in_specs.append(pl.BlockSpec(table.shape, lambda b, j: (0, 0, 0)))
    return pl.pallas_call(
        functools.partial(_attn_kernel, n_heads=n_heads, n_parts=n_parts),
        out_shape=jax.ShapeDtypeStruct((t, d_a), BF16),
        grid=(batch, nb),
        in_specs=in_specs,
        out_specs=pl.BlockSpec((qb, d_a), lambda b, j: (b * nb + j, 0)),
        compiler_params=_params("parallel", "arbitrary"),
        name="attn",
    )(*([qkv] * (1 + 2 * n_parts)), table)


def _attn_table(rel_bias, qb):
    left = LEFT_CHUNKS * CHUNK
    n_keys = left + qb
    period = qb + n_keys - 1
    n_heads = rel_bias.shape[0]
    m = jnp.arange(period)
    rel = left - jnp.where(m < n_keys, m, m - period)
    g = rel_bias[:, jnp.clip(rel, -(CHUNK - 1), MAX_REL) + (CHUNK - 1)].astype(F32)
    flat = jnp.tile(g, (1, qb))[:, :qb * (period - 1)]
    bias = flat.reshape(n_heads, qb, period - 1)[:, :, :n_keys]
    cq = jnp.arange(qb)[:, None] // CHUNK
    ck = jnp.arange(n_keys)[None, :] // CHUNK
    valid = (ck >= cq) & (ck <= cq + LEFT_CHUNKS)
    return jnp.where(valid[None], bias, NEG_INF)


_R_E1, _R_E2, _R_C1, _R_C2, _R_RANK1, _R_RANK2 = range(6)


def _outproj_route_kernel(yr_ref, ya_ref, h_ref, wor_ref, woa_ref, nw_ref, wrt_ref, brt_ref,
                          h1_ref, hn_ref, route_ref, cnt_ref, carry_ref):
    i = pl.program_id(0)

    @pl.when(i == 0)
    def _():
        carry_ref[...] = jnp.zeros_like(carry_ref)

    h1 = (h_ref[...] + jnp.dot(yr_ref[...], wor_ref[...], preferred_element_type=F32)
          + jnp.dot(ya_ref[...], woa_ref[...], preferred_element_type=F32))
    h1_ref[...] = h1
    hn = _rms(h1, nw_ref[...])
    _to_token_tiles(hn_ref, hn)
    nl = brt_ref.shape[1]
    hn_hi = hn.astype(BF16)
    hn_lo = (hn - hn_hi.astype(F32)).astype(BF16)
    part = jnp.dot(hn_hi, wrt_ref[...], preferred_element_type=F32)
    logits = (part[:, :nl] + part[:, nl:]
              + jnp.dot(hn_lo, wrt_ref[:, :nl], preferred_element_type=F32) + brt_ref[...])
    tm = logits.shape[0]
    lane = lax.broadcasted_iota(I32, (tm, nl), 1)
    lane_f = lane.astype(F32)
    ninf = -jnp.inf
    big = float(nl)
    is_g = lane < N_GROUPS
    gl = jnp.where(is_g, logits, ninf)
    g_max = gl.max(axis=-1, keepdims=True)
    g_sel = jnp.where(gl == g_max, lane_f, big).min(axis=-1, keepdims=True)
    p_g = 1.0 / jnp.where(is_g, jnp.exp(logits - g_max), 0.0).sum(axis=-1, keepdims=True)
    e_lo = N_GROUPS + EXPERTS_PER_GROUP * g_sel
    in_grp = (lane_f >= e_lo) & (lane_f < e_lo + EXPERTS_PER_GROUP)
    el = jnp.where(in_grp, logits, ninf)
    m1 = el.max(axis=-1, keepdims=True)
    i1 = jnp.where(el == m1, lane_f, big).min(axis=-1, keepdims=True)
    el2 = jnp.where(lane_f == i1, ninf, el)
    m2 = el2.max(axis=-1, keepdims=True)
    i2 = jnp.where(el2 == m2, lane_f, big).min(axis=-1, keepdims=True)
    t2 = jnp.exp(m2 - m1)
    c1 = p_g / (1.0 + t2)
    c2 = p_g * t2 / (1.0 + t2)
    e1 = i1 - N_GROUPS
    e2 = i2 - N_GROUPS
    oh1 = lane_f == e1
    oh2 = lane_f == e2
    ohs = jnp.where(oh1 | oh2, 1.0, 0.0)
    ri = lax.broadcasted_iota(I32, (tm, tm), 0)
    rj = lax.broadcasted_iota(I32, (tm, tm), 1)
    before = jnp.where(rj < ri, 1.0, 0.0).astype(BF16)
    cnt = jnp.dot(before, ohs.astype(BF16), preferred_element_type=F32) + carry_ref[0:1, :]
    rank1 = jnp.where(oh1, cnt, 0.0).sum(axis=-1, keepdims=True)
    rank2 = jnp.where(oh2, cnt, 0.0).sum(axis=-1, keepdims=True)
    new_carry = carry_ref[0:1, :] + ohs.sum(axis=0, keepdims=True)
    carry_ref[0:1, :] = new_carry
    cnt_ref[...] = jnp.broadcast_to(new_carry, cnt_ref.shape)
    route = jnp.zeros((tm, nl), F32)
    for idx, val in ((_R_E1, e1), (_R_E2, e2), (_R_C1, c1), (_R_C2, c2),
                     (_R_RANK1, rank1), (_R_RANK2, rank2)):
        route = jnp.where(lane == idx, val, route)
    route_ref[...] = route


def _outproj_route(yr, ya, h, wor, woa, nw, wrt, brt):
    t, d = h.shape
    tm = min(TM_ROUTE, t)
    d_r, d_a = yr.shape[1], ya.shape[1]
    nl = brt.shape[1]
    tok = lambda i: (i, 0)
    const = lambda i: (0, 0)
    return pl.pallas_call(
        _outproj_route_kernel,
        out_shape=(jax.ShapeDtypeStruct((t, d), F32),
                   jax.ShapeDtypeStruct((t * V7X_SUBLANES, V7X_LANES), F32),
                   jax.ShapeDtypeStruct((t, nl), F32), jax.ShapeDtypeStruct((V7X_SUBLANES, nl), F32)),
        grid=(t // tm,),
        in_specs=[pl.BlockSpec((tm, d_r), tok), pl.BlockSpec((tm, d_a), tok), pl.BlockSpec((tm, d), tok),
                  pl.BlockSpec((d_r, d), const), pl.BlockSpec((d_a, d), const),
                  pl.BlockSpec((1, d), const), pl.BlockSpec(wrt.shape, const), pl.BlockSpec((1, nl), const)],
        out_specs=(pl.BlockSpec((tm, d), tok), pl.BlockSpec((tm * V7X_SUBLANES, V7X_LANES), tok),
                   pl.BlockSpec((tm, nl), tok), pl.BlockSpec((V7X_SUBLANES, nl), const)),
        scratch_shapes=[pltpu.VMEM((V7X_SUBLANES, nl), F32)],
        compiler_params=_params("arbitrary"),
        name="outproj_route",
    )(yr, ya, h, wor, woa, nw, wrt, brt)


def _load_indices(idx_hbm, i, idx_smem, sem):
    n = idx_smem.shape[0]
    cp = pltpu.make_async_copy(idx_hbm.at[pl.ds(pl.multiple_of(i * n, n), n)], idx_smem, sem)
    cp.start()
    cp.wait()


def _dispatch_kernel(idx_hbm, x_ref, xs_hbm, idx_smem, zero_vmem, idx_sem, row_sem, zero_sem,
                     *, tm, n_free):
    i = pl.program_id(0)
    _load_indices(idx_hbm, i, idx_smem, idx_sem)

    def issue(tt, carry):
        for s in range(2):
            pltpu.make_async_copy(_token_tile(x_ref, tt), _token_tile(xs_hbm, idx_smem[s * tm + tt]),
                                  row_sem).start()
        return carry

    lax.fori_loop(0, tm, issue, 0, unroll=8)

    zero_vmem[...] = jnp.zeros_like(zero_vmem)

    def zero_copy(dst_row):
        return pltpu.make_async_copy(zero_vmem, _token_tile(xs_hbm, dst_row), zero_sem)

    def issue_zero(n, carry):
        zero_copy(idx_smem[2 * tm + n]).start()
        return carry

    lax.fori_loop(0, n_free, issue_zero, 0, unroll=8)

    for s in range(2):
        pltpu.make_async_copy(x_ref, xs_hbm.at[pl.ds(0, x_ref.shape[0]), :], row_sem).wait()

    def drain_zero(n, carry):
        zero_copy(0).wait()
        return carry

    lax.fori_loop(0, n_free, drain_zero, 0)


def _dispatch(x, idx, n_rows, tm, n_free):
    t = x.shape[0] // V7X_SUBLANES
    nb = t // tm
    rec = idx.shape[0] // nb
    return pl.pallas_call(
        functools.partial(_dispatch_kernel, tm=tm, n_free=n_free),
        out_shape=jax.ShapeDtypeStruct((n_rows * V7X_SUBLANES, V7X_LANES), x.dtype),
        grid=(nb,),
        in_specs=[pl.BlockSpec(memory_space=pl.ANY),
                  pl.BlockSpec((tm * V7X_SUBLANES, V7X_LANES), lambda i: (i, 0))],
        out_specs=pl.BlockSpec(memory_space=pl.ANY),
        scratch_shapes=[pltpu.SMEM((rec,), I32), pltpu.VMEM((V7X_SUBLANES, V7X_LANES), x.dtype),
                        pltpu.SemaphoreType.DMA, pltpu.SemaphoreType.DMA, pltpu.SemaphoreType.DMA],
        compiler_params=_params("arbitrary"),
        name="dispatch",
    )(idx, x)


def _experts_kernel(te_ref, tv_ref, tf_ref, x_ref, w1_ref, w3_ref, w2_ref, y_ref,
                    w1_b, w3_b, w2_b, *, tm):
    i = pl.program_id(0)

    @pl.when(tf_ref[i] > 0)
    def _():
        w1_b[...] = w1_ref[...].astype(BF16)
        w3_b[...] = w3_ref[...].astype(BF16)
        w2_b[...] = w2_ref[...].astype(BF16)

    @pl.when(tv_ref[i] > 0)
    def _():
        x = _from_token_tiles(x_ref, tm).astype(BF16)
        h_gate = jnp.dot(x, w1_b[...], preferred_element_type=F32)
        h_up = jnp.dot(x, w3_b[...], preferred_element_type=F32)
        hid = (h_gate * jax.nn.sigmoid(h_gate) * h_up).astype(BF16)
        _to_token_tiles(y_ref, jnp.dot(hid, w2_b[...], preferred_element_type=F32))

    @pl.when(tv_ref[i] == 0)
    def _():
        y_ref[...] = jnp.zeros_like(y_ref)


def _experts(xs, w1, w3, w2, tile_expert, tile_valid, tile_first):
    n_rows = xs.shape[0] // V7X_SUBLANES
    tm = TM_EXPERT
    nt = n_rows // tm
    d, f = w1.shape[1:]
    tile_spec = pl.BlockSpec((tm * V7X_SUBLANES, V7X_LANES), lambda i, te, tv, tf: (i, 0))
    grid_spec = pltpu.PrefetchScalarGridSpec(
        num_scalar_prefetch=3,
        grid=(nt,),
        in_specs=[tile_spec,
                  pl.BlockSpec((None, d, f), lambda i, te, tv, tf: (te[i], 0, 0)),
                  pl.BlockSpec((None, d, f), lambda i, te, tv, tf: (te[i], 0, 0)),
                  pl.BlockSpec((None, f, d), lambda i, te, tv, tf: (te[i], 0, 0))],
        out_specs=tile_spec,
        scratch_shapes=[pltpu.VMEM((d, f), BF16), pltpu.VMEM((d, f), BF16), pltpu.VMEM((f, d), BF16)],
    )
    return pl.pallas_call(
        functools.partial(_experts_kernel, tm=tm),
        out_shape=jax.ShapeDtypeStruct(xs.shape, F32),
        grid_spec=grid_spec,
        compiler_params=_params("arbitrary"),
        name="experts",
    )(tile_expert, tile_valid, tile_first, xs, w1, w3, w2)


def _combine_ple_kernel(pos_hbm, ys_hbm, h_ref, route_ref, p_ref, nw_ref, wg_ref, bg_ref, wp_ref,
                        fw_ref, o_ref, idx_smem, ybuf, idx_sem, row_sem, *, tm, final):
    i = pl.program_id(0)
    nb = pl.num_programs(0)
    cur = lax.rem(i, 2)
    nxt = 1 - cur

    def gather(step, buf):
        _load_indices(pos_hbm, step, idx_smem, idx_sem)

        def issue(tt, carry):
            for s in range(2):
                pltpu.make_async_copy(_token_tile(ys_hbm, idx_smem[s * tm + tt]),
                                      _token_tile(ybuf.at[buf, s], tt), row_sem.at[buf]).start()
            return carry

        lax.fori_loop(0, tm, issue, 0, unroll=8)

    def wait_gather(buf):
        for s in range(2):
            pltpu.make_async_copy(ys_hbm.at[pl.ds(0, ybuf.shape[2]), :], ybuf.at[buf, s],
                                  row_sem.at[buf]).wait()

    @pl.when(i == 0)
    def _():
        gather(0, 0)

    @pl.when(i + 1 < nb)
    def _():
        gather(i + 1, nxt)

    wait_gather(cur)

    route = route_ref[...]
    c1 = route[:, _R_C1:_R_C1 + 1]
    c2 = route[:, _R_C2:_R_C2 + 1]
    h2 = (h_ref[...] + c1 * _from_token_tiles(ybuf.at[cur, 0], tm)
          + c2 * _from_token_tiles(ybuf.at[cur, 1], tm))
    hn = _rms(h2, nw_ref[...]).astype(BF16)
    gate = jax.nn.sigmoid(jnp.dot(hn, wg_ref[...], preferred_element_type=F32) + bg_ref[...])
    h3 = h2 + gate * jnp.dot(p_ref[...].astype(BF16), wp_ref[...], preferred_element_type=F32)
    if final:
        h3 = _rms(h3, fw_ref[...])
    o_ref[...] = h3


def _combine_ple(pos_tiles, ys, h, route, p, nw, wg, bg, wp, fw, *, tm, final):
    t, d = h.shape
    nl = route.shape[1]
    dp = p.shape[1]
    tok = lambda i: (i, 0)
    const = lambda i: (0, 0)
    return pl.pallas_call(
        functools.partial(_combine_ple_kernel, tm=tm, final=final),
        out_shape=jax.ShapeDtypeStruct((t, d), F32),
        grid=(t // tm,),
        in_specs=[pl.BlockSpec(memory_space=pl.ANY), pl.BlockSpec(memory_space=pl.ANY),
                  pl.BlockSpec((tm, d), tok), pl.BlockSpec((tm, nl), tok), pl.BlockSpec((tm, dp), tok),
                  pl.BlockSpec((1, d), const), pl.BlockSpec((d, d), const), pl.BlockSpec((1, d), const),
                  pl.BlockSpec((dp, d), const), pl.BlockSpec((1, d), const)],
        out_specs=pl.BlockSpec((tm, d), tok),
        scratch_shapes=[pltpu.SMEM((pos_tiles.shape[0] // (t // tm),), I32),
                        pltpu.VMEM((2, 2, tm * V7X_SUBLANES, V7X_LANES), F32),
                        pltpu.SemaphoreType.DMA, pltpu.SemaphoreType.DMA((2,))],
        compiler_params=_params("arbitrary"),
        name="combine_ple_final" if final else "combine_ple",
    )(pos_tiles, ys, h, route, p, nw, wg, bg, wp, fw)


_SMEM_RECORD_WORDS = 1024


def _index_records(pos1, pos2, tm, extra=None):
    nb = pos1.shape[0] // tm
    parts = [pos1.reshape(nb, tm), pos2.reshape(nb, tm)]
    if extra is not None:
        parts.append(extra.reshape(nb, -1))
    rec = jnp.concatenate(parts, axis=1)
    pad = -rec.shape[1] % _SMEM_RECORD_WORDS
    return jnp.pad(rec, ((0, 0), (0, pad))).reshape(-1)


def _lookup(table, idx):
    ids = jnp.arange(table.shape[0], dtype=I32)
    return jnp.sum(jnp.where(idx[None, :] == ids[:, None], table[:, None], 0), axis=0)


def _bucket(ends, x):
    return jnp.minimum(jnp.sum((x[None, :] >= ends[:, None]).astype(I32), axis=0), ends.shape[0] - 1)


def kernel(x, p, norm_mix_w, w_in, rwkv_mu, rwkv_w0, rwkv_w2, rwkv_a0, rwkv_a2, rwkv_g2, rwkv_k_k, rwkv_k_a, rwkv_r_k, rwkv_ln_w, rwkv_ln_b, rwkv_v0, rwkv_v1, rwkv_v2, att_rel_bias, w_out, norm_ffn_w, router_group_w, router_group_b, router_expert_w, router_expert_b, expert_w1, expert_w3, expert_w2, norm_ple_w, ple_gate_w, ple_gate_b, ple_proj_w, final_norm_w):
    batch, seq, d = x.shape
    depth = w_in.shape[0]
    t = batch * seq
    d_r = rwkv_w0.shape[1]
    n_heads_r = d_r // HEAD_DIM
    n_rwkv_in = rwkv_mu.shape[1]
    d_a = (w_in.shape[2] - n_rwkv_in) // 3
    n_heads_a = d_a // HEAD_DIM
    n_dec, n_iclr, n_gate = rwkv_w2.shape[1], rwkv_a2.shape[1], rwkv_g2.shape[1]
    assert n_dec == n_iclr and n_gate == n_dec + n_iclr
    n_lo = n_dec + n_iclr + n_gate
    f_exp = expert_w1.shape[-1]
    assert d == V7X_SUBLANES * V7X_LANES
    n_rows = 2 * t + N_EXPERTS * TM_EXPERT
    n_tiles = n_rows // TM_EXPERT
    qb = min(QB_ATTN, seq)

    h = x.reshape(t, d)
    v_first = None
    for i in range(depth):
        wr = w_in[i, :, :n_rwkv_in].astype(BF16)
        wa = w_in[i, :, n_rwkv_in:].astype(BF16)
        wl = jnp.zeros((n_lo, 3 * d_r), F32)
        wl = wl.at[:n_dec, :d_r].set(rwkv_w2[i])
        wl = wl.at[n_dec:n_dec + n_iclr, d_r:2 * d_r].set(rwkv_a2[i])
        wl = wl.at[n_dec + n_iclr:, 2 * d_r:].set(rwkv_g2[i]).astype(BF16)
        v0 = rwkv_v0[i - 1] if i > 0 else jnp.zeros((d_r,), F32)
        vec = jnp.stack([rwkv_w0[i], rwkv_a0[i], rwkv_k_k[i], rwkv_k_a[i], rwkv_r_k[i],
                         rwkv_ln_w[i], rwkv_ln_b[i], v0])
        if i > 0:
            n_vr = rwkv_v1.shape[2]
            v1 = jnp.zeros((d_r, V7X_LANES), F32).at[:, :n_vr].set(rwkv_v1[i - 1]).astype(BF16)
            v2 = jnp.zeros((V7X_LANES, d_r), F32).at[:n_vr, :].set(rwkv_v2[i - 1]).astype(BF16)
        else:
            v1 = v2 = None
        table = _attn_table(att_rel_bias[i], qb)
        wor = w_out[i, :d_r].astype(BF16)
        woa = w_out[i, d_r:].astype(BF16)
        n_rt = N_GROUPS + N_EXPERTS
        wrt = jnp.zeros((d, V7X_LANES), F32)
        wrt = wrt.at[:, :N_GROUPS].set(router_group_w[i]).at[:, N_GROUPS:n_rt].set(router_expert_w[i])
        wrt_hi = wrt.astype(BF16)
        wrt = jnp.concatenate([wrt_hi, (wrt - wrt_hi.astype(F32)).astype(BF16)], axis=1)
        brt = jnp.zeros((1, V7X_LANES), F32)
        brt = brt.at[0, :N_GROUPS].set(router_group_b[i]).at[0, N_GROUPS:n_rt].set(router_expert_b[i])
        w1 = expert_w1[i].reshape(N_EXPERTS, d, f_exp)
        w3 = expert_w3[i].reshape(N_EXPERTS, d, f_exp)
        w2 = expert_w2[i].reshape(N_EXPERTS, f_exp, d)

        z_r, qkv = _norm_proj(h, norm_mix_w[i][None], wr, wa)
        if i == 0:
            y_r, v_first = _rwkv(z_r, None, rwkv_mu[i][None], vec, wl, None, None,
                                 batch=batch, seq=seq, n_heads=n_heads_r)
        else:
            y_r = _rwkv(z_r, v_first, rwkv_mu[i][None], vec, wl, v1, v2,
                        batch=batch, seq=seq, n_heads=n_heads_r)
        y_a = _attn(qkv, table, batch=batch, seq=seq, n_heads=n_heads_a)

        h1, hn, route, cnt = _outproj_route(y_r, y_a, h, wor, woa, norm_ffn_w[i][None], wrt, brt)
        ri = route[:, :V7X_SUBLANES].astype(I32)
        counts = cnt[0, :N_EXPERTS].astype(I32)
        padded = ((counts + TM_EXPERT - 1) // TM_EXPERT) * TM_EXPERT
        p_end = jnp.cumsum(padded)
        p_start = p_end - padded
        pos1 = _lookup(p_start, ri[:, _R_E1]) + ri[:, _R_RANK1]
        pos2 = _lookup(p_start, ri[:, _R_E2]) + ri[:, _R_RANK2]
        tile_start = jnp.arange(n_tiles, dtype=I32) * TM_EXPERT
        tile_expert = _bucket(p_end, tile_start)
        tile_valid = (tile_start < p_end[-1]).astype(I32)

        tile_first = jnp.concatenate([jnp.ones((1,), I32),
                                      (tile_expert[1:] != tile_expert[:-1]).astype(I32)])

        n_free = n_rows - 2 * t
        n_pad = padded - counts
        f_end = jnp.cumsum(n_pad)
        kf = jnp.arange(n_free, dtype=I32)
        ef = _bucket(f_end, kf)
        pad_row = _lookup(p_start + counts - (f_end - n_pad), ef) + kf
        free_rows = jnp.where(kf < f_end[-1], pad_row, p_end[-1] + (kf - f_end[-1])).astype(I32)

        tm_d = min(TM_DISPATCH, t)
        nb_d = t // tm_d
        assert n_free % nb_d == 0
        xs = _dispatch(hn, _index_records(pos1, pos2, tm_d, free_rows), n_rows, tm_d, n_free // nb_d)
        ys = _experts(xs, w1, w3, w2, tile_expert, tile_valid, tile_first)

        tm_c = min(TM_COMBINE, t)
        h = _combine_ple(_index_records(pos1, pos2, tm_c), ys, h1, route, p[i].reshape(t, -1),
                         norm_ple_w[i][None], ple_gate_w[i].astype(BF16), ple_gate_b[i][None],
                         ple_proj_w[i].astype(BF16), final_norm_w[None],
                         tm=tm_c, final=(i == depth - 1))
    return h.reshape(batch, seq, d)
```

```python
import functools

import jax
import jax.numpy as jnp
from jax import lax
from jax.experimental import pallas as pl
from jax.experimental.pallas import tpu as pltpu

F32 = jnp.float32
BF16 = jnp.bfloat16
I32 = jnp.int32

CHUNK = 64
HEAD_DIM = 64
LEFT_CHUNKS = 8
MAX_REL = 256
N_GROUPS = 4
EXPERTS_PER_GROUP = 8
N_EXPERTS = N_GROUPS * EXPERTS_PER_GROUP
RMS_EPS = 1e-6
GN_EPS = 64e-5
NEG_INF = -1e30

V7X_LANES = 128
V7X_SUBLANES = 8
V7X_VMEM_LIMIT_BYTES = 48 * 1024 * 1024

TM_PROJ = 512
TB_RWKV = 256
RWKV_CHUNK_GROUP = 2
QB_ATTN = 256
TM_ROUTE = 512
TM_DISPATCH = 512
TM_EXPERT = 256
TM_COMBINE = 512


def _params(*sem):
    return pltpu.CompilerParams(dimension_semantics=sem, vmem_limit_bytes=V7X_VMEM_LIMIT_BYTES)


def _rms(x, w):
    return x * lax.rsqrt(jnp.mean(x * x, axis=-1, keepdims=True) + RMS_EPS) * w


def _mm(a, b):
    return jnp.dot(a.astype(BF16), b.astype(BF16), preferred_element_type=F32)


def _mm_nt(a, b):
    return lax.dot_general(a.astype(BF16), b.astype(BF16), (((1,), (1,)), ((), ())),
                           preferred_element_type=F32)


def _mm_tn(a, b):
    return lax.dot_general(a.astype(BF16), b.astype(BF16), (((0,), (0,)), ((), ())),
                           preferred_element_type=F32)


def _to_token_tiles(ref, x):
    m, d = x.shape
    for s in range(d // V7X_LANES):
        ref[pl.ds(s, m, stride=V7X_SUBLANES), :] = x[:, s * V7X_LANES:(s + 1) * V7X_LANES]


def _from_token_tiles(ref, m):
    return jnp.concatenate([ref[pl.ds(s, m, stride=V7X_SUBLANES), :] for s in range(V7X_SUBLANES)],
                           axis=-1)


def _token_tile(ref, row):
    return ref.at[pl.ds(pl.multiple_of(row * V7X_SUBLANES, V7X_SUBLANES), V7X_SUBLANES), :]


def _split3(x):
    hi = x.astype(BF16)
    r1 = x - hi.astype(F32)
    mid = r1.astype(BF16)
    lo = (r1 - mid.astype(F32)).astype(BF16)
    return hi, mid, lo


def _mm_exact_lhs(a_bf16, x):
    hi, mid, lo = _split3(x)
    return (jnp.dot(a_bf16, hi, preferred_element_type=F32)
            + jnp.dot(a_bf16, mid, preferred_element_type=F32)
            + jnp.dot(a_bf16, lo, preferred_element_type=F32))


def _mm_exact_rhs(x, b_bf16):
    hi, mid, lo = _split3(x)
    return (jnp.dot(hi, b_bf16, preferred_element_type=F32)
            + jnp.dot(mid, b_bf16, preferred_element_type=F32)
            + jnp.dot(lo, b_bf16, preferred_element_type=F32))


def _norm_proj_kernel(h_ref, nw_ref, wr_ref, wa_ref, zr_ref, qkv_ref):
    hn = _rms(h_ref[...], nw_ref[...]).astype(BF16)
    zr_ref[...] = jnp.dot(hn, wr_ref[...], preferred_element_type=F32)
    qkv_ref[...] = jnp.dot(hn, wa_ref[...], preferred_element_type=F32).astype(BF16)


def _norm_proj(h, nw, wr, wa):
    t, d = h.shape
    tm = min(TM_PROJ, t)
    n_r, n_a = wr.shape[1], wa.shape[1]
    return pl.pallas_call(
        _norm_proj_kernel,
        out_shape=(jax.ShapeDtypeStruct((t, n_r), F32), jax.ShapeDtypeStruct((t, n_a), BF16)),
        grid=(t // tm,),
        in_specs=[pl.BlockSpec((tm, d), lambda i: (i, 0)),
                  pl.BlockSpec((1, d), lambda i: (0, 0)),
                  pl.BlockSpec((d, n_r), lambda i: (0, 0)),
                  pl.BlockSpec((d, n_a), lambda i: (0, 0))],
        out_specs=(pl.BlockSpec((tm, n_r), lambda i: (i, 0)),
                   pl.BlockSpec((tm, n_a), lambda i: (i, 0))),
        compiler_params=_params("parallel"),
        name="norm_proj",
    )(h, nw, wr, wa)


_V_W0, _V_A0, _V_KK, _V_KA, _V_RK, _V_LNW, _V_LNB, _V_V0 = range(8)


def _rwkv_kernel(*refs, has_vres, n_heads, d_r, group):
    if has_vres:
        (z_ref, vf_ref, mu_ref, vec_ref, wl_ref, tril_ref, ones_ref, v1_ref, v2_ref, y_ref,
         s_ref, carry_ref, r_s, k_s, v_s, kk_s, a_s, lc_s, lw_s, bon_s, g_s) = refs
        vf_out_ref = None
    else:
        (z_ref, mu_ref, vec_ref, wl_ref, tril_ref, ones_ref, y_ref, vf_out_ref,
         s_ref, carry_ref, r_s, k_s, v_s, kk_s, a_s, lc_s, lw_s, bon_s, g_s) = refs
    tb = z_ref.shape[0]
    n_chunks = tb // CHUNK
    j = pl.program_id(1)

    @pl.when(j == 0)
    def _():
        s_ref[...] = jnp.zeros_like(s_ref)
        carry_ref[...] = jnp.zeros_like(carry_ref)

    z = z_ref[...]
    row = lax.broadcasted_iota(I32, z.shape, 0)
    z_prev = jnp.where(row == 0, carry_ref[0:1, :], pltpu.roll(z, 1, axis=0))
    carry_ref[0:1, :] = z[tb - 1:tb, :]
    zs = z + (z_prev - z) * mu_ref[...]

    vec = vec_ref[...]

    def vrow(i):
        return vec[i:i + 1, :]

    r = zs[:, 0:d_r]
    k = zs[:, d_r:2 * d_r]
    v = zs[:, 2 * d_r:3 * d_r]
    lo = zs[:, 3 * d_r:]
    n_lo = lo.shape[1]
    lane = lax.broadcasted_iota(I32, lo.shape, 1)
    lo_act = jnp.where(lane < n_lo // 4, jnp.tanh(lo),
                       jnp.where(lane < n_lo // 2, lo, jax.nn.sigmoid(lo)))
    lo_out = _mm(lo_act, wl_ref[...])

    if has_vres:
        vv = _mm(_mm(v, v1_ref[...]), v2_ref[...])
        v = v + (vf_ref[...] - v) * jax.nn.sigmoid(vrow(_V_V0) + vv)
    else:
        vf_out_ref[...] = v

    w_log = -jax.nn.softplus(-(vrow(_V_W0) + lo_out[:, 0:d_r])) - 0.5
    lw = -jnp.exp(w_log)
    a = jax.nn.sigmoid(vrow(_V_A0) + lo_out[:, d_r:2 * d_r])
    g = lo_out[:, 2 * d_r:3 * d_r]

    head_ones = ones_ref[...]
    kk = k * vrow(_V_KK)
    kk = kk * lax.rsqrt(jnp.maximum(_mm_exact_rhs(kk * kk, head_ones), 1e-24))
    k2 = k * (1.0 + (a - 1.0) * vrow(_V_KA))
    bonus = _mm_exact_rhs(r * k2 * vrow(_V_RK), head_ones) * v

    lc = _mm_exact_lhs(tril_ref[...], lw)

    r_s[...] = r
    k_s[...] = k2
    v_s[...] = v
    kk_s[...] = kk
    a_s[...] = a
    lc_s[...] = lc
    lw_s[...] = lw
    bon_s[...] = bonus
    g_s[...] = g

    ci = lax.broadcasted_iota(I32, (CHUNK, CHUNK), 0)
    cj = lax.broadcasted_iota(I32, (CHUNK, CHUNK), 1)
    strict = cj < ci
    lower = cj <= ci
    eye = ci == cj
    eye_f = jnp.where(eye, 1.0, 0.0)
    ln_w = vrow(_V_LNW)
    ln_b = vrow(_V_LNB)

    hs = [slice(h * HEAD_DIM, (h + 1) * HEAD_DIM) for h in range(n_heads)]

    def chunk_operands(r0):
        rs = pl.ds(r0, CHUNK)
        lc_c = lc_s[rs, :]
        lw_c = lw_s[rs, :]
        l_end = lc_s[pl.ds(r0 + CHUNK - 1, 1), :]
        p_in = jnp.exp(lc_c)
        p_prev = jnp.exp(lc_c - lw_c)
        p_inv = jnp.exp(-lc_c)
        p_end = jnp.exp(l_end - lc_c)
        p_last = jnp.exp(l_end)
        kk_c = kk_s[rs, :]
        b_c = kk_c * a_s[rs, :]
        k_c = k_s[rs, :]
        at = (-kk_c * p_prev).astype(BF16)
        bt = (b_c * p_inv).astype(BF16)
        bh = (b_c * p_end).astype(BF16)
        kt = (k_c * p_inv).astype(BF16)
        kh = (k_c * p_end).astype(BF16)
        rt = (r_s[rs, :] * p_in).astype(BF16)
        vc = v_s[rs, :].astype(BF16)
        per_head = [[x[:, sl] for sl in hs] for x in (at, bt, bh, kt, kh, rt, vc)]
        per_head.append([p_last[:, sl] for sl in hs])
        return per_head

    def group_body(gi, carry):
        g0 = pl.multiple_of(gi * (group * CHUNK), group * CHUNK)
        ops = [chunk_operands(g0 + c * CHUNK) for c in range(group)]
        at_h, bt_h, bh_h, kt_h, kh_h, rt_h, v_h, pl_h = ([x for c in range(group) for x in ops[c][q]]
                                                         for q in range(8))
        heads = range(group * n_heads)
        ar_h = [jnp.concatenate([at_h[h], rt_h[h]], axis=0) for h in heads]
        m_b = [_mm_nt(ar_h[h], bt_h[h]) for h in heads]
        m_k = [_mm_nt(ar_h[h], kt_h[h]) for h in heads]
        n_ab = [jnp.where(strict, m_b[h][:CHUNK], 0.0) for h in heads]
        a_ak = [jnp.where(strict, m_k[h][:CHUNK], 0.0) for h in heads]
        a_rb = [jnp.where(lower, m_b[h][CHUNK:], 0.0) for h in heads]
        a_rk = [jnp.where(lower, m_k[h][CHUNK:], 0.0) for h in heads]
        x_inv = [eye_f + n_ab[h] for h in heads]
        pw = [_mm(n_ab[h], n_ab[h]) for h in heads]
        akv = [_mm(a_ak[h], v_h[h]) for h in heads]
        n_sq = CHUNK.bit_length() - 2
        for it in range(n_sq):
            if it < n_sq - 1:
                st = [_mm(jnp.concatenate([x_inv[h], pw[h]], axis=0), pw[h]) for h in heads]
                x_inv = [x_inv[h] + st[h][:CHUNK] for h in heads]
                pw = [st[h][CHUNK:] for h in heads]
            else:
                st = [_mm(x_inv[h], pw[h]) for h in heads]
                x_inv = [x_inv[h] + st[h] for h in heads]
        w_h = [_mm(x_inv[h], at_h[h]) for h in heads]
        u0 = [_mm(x_inv[h], akv[h]) for h in heads]
        y0 = [_mm(a_rk[h], v_h[h]) + _mm(a_rb[h], u0[h]) for h in heads]
        r_p = [rt_h[h].astype(F32) + _mm(a_rb[h], w_h[h]) for h in heads]
        g_h = [jnp.where(eye, pl_h[h], 0.0) + _mm_tn(w_h[h], bh_h[h]) for h in heads]
        d_h = [_mm_tn(u0[h], bh_h[h]) + _mm_tn(v_h[h], kh_h[h]) for h in heads]
        s_h = [s_ref[h] for h in range(n_heads)]
        for c in range(group):
            rs = pl.ds(g0 + c * CHUNK, CHUNK)
            idx = [c * n_heads + h for h in range(n_heads)]
            y_h = [y0[i] + _mm_nt(r_p[i], s_h[h]) for h, i in enumerate(idx)]
            s_h = [_mm(s_h[h], g_h[i]) + d_h[i] for h, i in enumerate(idx)]
            y_heads = []
            for h in range(n_heads):
                mean = jnp.mean(y_h[h], axis=-1, keepdims=True)
                yc = y_h[h] - mean
                var = jnp.mean(yc * yc, axis=-1, keepdims=True)
                y_heads.append(yc * lax.rsqrt(var + GN_EPS))
            y_n = jnp.concatenate(y_heads, axis=-1)
            out = (y_n * ln_w + ln_b + bon_s[rs, :]) * g_s[rs, :]
            y_ref[rs, :] = out.astype(y_ref.dtype)
        for h in range(n_heads):
            s_ref[h] = s_h[h]
        return carry

    lax.fori_loop(0, n_chunks // group, group_body, 0)


def _rwkv(z, v_first, mu, vec, wl, v1, v2, *, batch, seq, n_heads):
    t, n_z = z.shape
    d_r = n_heads * HEAD_DIM
    tb = min(TB_RWKV, seq)
    nb = seq // tb
    has_vres = v_first is not None
    tok = lambda b, j: (b * nb + j, 0)
    const = lambda b, j: (0, 0)
    in_specs = [pl.BlockSpec((tb, n_z), tok)]
    args = [z]
    if has_vres:
        in_specs.append(pl.BlockSpec((tb, d_r), tok))
        args.append(v_first)
    ti = jnp.arange(tb)
    tril = ((ti[:, None] // CHUNK == ti[None, :] // CHUNK) & (ti[None, :] <= ti[:, None])).astype(BF16)
    hi = jnp.arange(d_r) // HEAD_DIM
    head_ones = (hi[:, None] == hi[None, :]).astype(BF16)
    in_specs += [pl.BlockSpec(mu.shape, const), pl.BlockSpec(vec.shape, const),
                 pl.BlockSpec(wl.shape, const), pl.BlockSpec(tril.shape, const),
                 pl.BlockSpec(head_ones.shape, const)]
    args += [mu, vec, wl, tril, head_ones]
    n_chunks = tb // CHUNK
    group = RWKV_CHUNK_GROUP if n_chunks % RWKV_CHUNK_GROUP == 0 else 1
    if has_vres:
        in_specs += [pl.BlockSpec(v1.shape, const), pl.BlockSpec(v2.shape, const)]
        args += [v1, v2]
        out_shape = jax.ShapeDtypeStruct((t, d_r), BF16)
        out_specs = pl.BlockSpec((tb, d_r), tok)
    else:
        out_shape = (jax.ShapeDtypeStruct((t, d_r), BF16), jax.ShapeDtypeStruct((t, d_r), F32))
        out_specs = (pl.BlockSpec((tb, d_r), tok), pl.BlockSpec((tb, d_r), tok))
    scratch = [pltpu.VMEM((n_heads, HEAD_DIM, HEAD_DIM), F32),
               pltpu.VMEM((V7X_SUBLANES, n_z), F32)]
    scratch += [pltpu.VMEM((tb, d_r), F32) for _ in range(9)]
    return pl.pallas_call(
        functools.partial(_rwkv_kernel, has_vres=has_vres, n_heads=n_heads, d_r=d_r, group=group),
        out_shape=out_shape,
        grid=(batch, nb),
        in_specs=in_specs,
        out_specs=out_specs,
        scratch_shapes=scratch,
        compiler_params=_params("arbitrary", "arbitrary"),
        name="rwkv_vres" if has_vres else "rwkv",
    )(*args)


def _attn_kernel(*refs, n_heads, n_parts):
    q_ref = refs[0]
    k_refs = refs[1:1 + n_parts]
    v_refs = refs[1 + n_parts:1 + 2 * n_parts]
    tab_ref = refs[1 + 2 * n_parts]
    o_ref = refs[2 + 2 * n_parts]
    qb = q_ref.shape[0]
    j = pl.program_id(1)
    scale = HEAD_DIM ** -0.5
    q = q_ref[...] * jnp.asarray(scale, q_ref.dtype)
    ks = [r[...] for r in k_refs]
    vs = [r[...] for r in v_refs]
    pw = 2 * HEAD_DIM
    lane = lax.broadcasted_iota(I32, (qb, pw), 1)
    sum_even = jnp.where(lane < HEAD_DIM, 1.0, 0.0).astype(q.dtype)
    sum_odd = jnp.where(lane < HEAD_DIM, 0.0, 1.0).astype(q.dtype)
    zero = jnp.zeros((), q.dtype)
    is_even = sum_even > zero
    outs = []
    for hp in range(n_heads // 2):
        sl = slice(hp * pw, (hp + 1) * pw)
        qq = q[:, sl]
        q_pair = (jnp.where(is_even, qq, zero), jnp.where(is_even, zero, qq))
        s_parts = [[], []]
        for p in range(n_parts):
            kk = ks[p][:, sl]
            back = n_parts - 1 - p
            for u in range(2):
                s = _mm_nt(q_pair[u], kk) + tab_ref[2 * hp + u, :, p * qb:(p + 1) * qb]
                if back > 0:
                    s = jnp.where(j >= back, s, NEG_INF)
                s_parts[u].append(s)
        m = []
        for u in range(2):
            mm = s_parts[u][0]
            for s in s_parts[u][1:]:
                mm = jnp.maximum(mm, s)
            m.append(mm.max(axis=-1, keepdims=True))
        acc = jnp.zeros((qb, 2 * pw), F32)
        for p in range(n_parts):
            vv = vs[p][:, sl]
            rhs = jnp.concatenate(
                [jnp.concatenate([jnp.where(is_even, vv, zero), sum_even], axis=1),
                 jnp.concatenate([jnp.where(is_even, zero, vv), sum_odd], axis=1)], axis=0)
            e = jnp.concatenate([jnp.exp(s_parts[u][p] - m[u]).astype(BF16) for u in range(2)], axis=1)
            acc = acc + jnp.dot(e, rhs, preferred_element_type=F32)
        outs.append(acc[:, :pw] / acc[:, pw:])
    o_ref[...] = jnp.concatenate(outs, axis=-1).astype(o_ref.dtype)


def _attn(qkv, table, *, batch, seq, n_heads):
    t = qkv.shape[0]
    d_a = n_heads * HEAD_DIM
    qb = min(QB_ATTN, seq)
    left = LEFT_CHUNKS * CHUNK
    assert left % qb == 0 and seq % qb == 0
    n_parts = left // qb + 1
    nb = seq // qb
    in_specs = [pl.BlockSpec((qb, d_a), lambda b, j: (b * nb + j, 0))]
    for p in range(n_parts):
        back = n_parts - 1 - p
        in_specs.append(pl.BlockSpec((qb, d_a), lambda b, j, back=back: (b * nb + jnp.maximum(j - back, 0), 1)))
    for p in range(n_parts):
        back = n_parts - 1 - p
        in_specs.append(pl.BlockSpec((qb, d_a), lambda b, j, back=back: (b * nb + jnp.maximum(j - back, 0), 2)))
    in_specs.append(pl.BlockSpec(table.shape, lambda b, j: (0, 0, 0)))
    return pl.pallas_call(
        functools.partial(_attn_kernel, n_heads=n_heads, n_parts=n_parts),
        out_shape=jax.ShapeDtypeStruct((t, d_a), BF16),
        grid=(batch, nb),
        in_specs=in_specs,
        out_specs=pl.BlockSpec((qb, d_a), lambda b, j: (b * nb + j, 0)),
        compiler_params=_params("parallel", "arbitrary"),
        name="attn",
    )(*([qkv] * (1 + 2 * n_parts)), table)


def _attn_table(rel_bias, qb):
    left = LEFT_CHUNKS * CHUNK
    n_keys = left + qb
    period = qb + n_keys - 1
    n_heads = rel_bias.shape[0]
    m = jnp.arange(period)
    rel = left - jnp.where(m < n_keys, m, m - period)
    g = rel_bias[:, jnp.clip(rel, -(CHUNK - 1), MAX_REL) + (CHUNK - 1)].astype(F32)
    flat = jnp.tile(g, (1, qb))[:, :qb * (period - 1)]
    bias = flat.reshape(n_heads, qb, period - 1)[:, :, :n_keys]
    cq = jnp.arange(qb)[:, None] // CHUNK
    ck = jnp.arange(n_keys)[None, :] // CHUNK
    valid = (ck >= cq) & (ck <= cq + LEFT_CHUNKS)
    return jnp.where(valid[None], bias, NEG_INF)


_R_E1, _R_E2, _R_C1, _R_C2, _R_RANK1, _R_RANK2 = range(6)


def _outproj_route_kernel(yr_ref, ya_ref, h_ref, wor_ref, woa_ref, nw_ref, wrt_ref, brt_ref,
                          h1_ref, hn_ref, route_ref, route_t_ref, cnt_ref, carry_ref):
    i = pl.program_id(0)

    @pl.when(i == 0)
    def _():
        carry_ref[...] = jnp.zeros_like(carry_ref)

    h1 = (h_ref[...] + jnp.dot(yr_ref[...], wor_ref[...], preferred_element_type=F32)
          + jnp.dot(ya_ref[...], woa_ref[...], preferred_element_type=F32))
    h1_ref[...] = h1
    hn = _rms(h1, nw_ref[...])
    _to_token_tiles(hn_ref, hn)
    nl = brt_ref.shape[1]
    hn_hi = hn.astype(BF16)
    hn_lo = (hn - hn_hi.astype(F32)).astype(BF16)
    part = jnp.dot(hn_hi, wrt_ref[...], preferred_element_type=F32)
    logits = (part[:, :nl] + part[:, nl:]
              + jnp.dot(hn_lo, wrt_ref[:, :nl], preferred_element_type=F32) + brt_ref[...])
    tm = logits.shape[0]
    lane = lax.broadcasted_iota(I32, (tm, nl), 1)
    lane_f = lane.astype(F32)
    ninf = -jnp.inf
    big = float(nl)
    is_g = lane < N_GROUPS
    gl = jnp.where(is_g, logits, ninf)
    g_max = gl.max(axis=-1, keepdims=True)
    g_sel = jnp.where(gl == g_max, lane_f, big).min(axis=-1, keepdims=True)
    p_g = 1.0 / jnp.where(is_g, jnp.exp(logits - g_max), 0.0).sum(axis=-1, keepdims=True)
    e_lo = N_GROUPS + EXPERTS_PER_GROUP * g_sel
    in_grp = (lane_f >= e_lo) & (lane_f < e_lo + EXPERTS_PER_GROUP)
    el = jnp.where(in_grp, logits, ninf)
    m1 = el.max(axis=-1, keepdims=True)
    i1 = jnp.where(el == m1, lane_f, big).min(axis=-1, keepdims=True)
    el2 = jnp.where(lane_f == i1, ninf, el)
    m2 = el2.max(axis=-1, keepdims=True)
    i2 = jnp.where(el2 == m2, lane_f, big).min(axis=-1, keepdims=True)
    t2 = jnp.exp(m2 - m1)
    c1 = p_g / (1.0 + t2)
    c2 = p_g * t2 / (1.0 + t2)
    e1 = i1 - N_GROUPS
    e2 = i2 - N_GROUPS
    oh1 = lane_f == e1
    oh2 = lane_f == e2
    ohs = jnp.where(oh1 | oh2, 1.0, 0.0)
    ri = lax.broadcasted_iota(I32, (tm, tm), 0)
    rj = lax.broadcasted_iota(I32, (tm, tm), 1)
    before = jnp.where(rj < ri, 1.0, 0.0).astype(BF16)
    cnt = jnp.dot(before, ohs.astype(BF16), preferred_element_type=F32) + carry_ref[0:1, :]
    rank1 = jnp.where(oh1, cnt, 0.0).sum(axis=-1, keepdims=True)
    rank2 = jnp.where(oh2, cnt, 0.0).sum(axis=-1, keepdims=True)
    new_carry = carry_ref[0:1, :] + ohs.sum(axis=0, keepdims=True)
    carry_ref[0:1, :] = new_carry
    cnt_ref[...] = jnp.broadcast_to(new_carry, cnt_ref.shape)
    route = jnp.zeros((tm, nl), F32)
    for idx, val in ((_R_E1, e1), (_R_E2, e2), (_R_C1, c1), (_R_C2, c2),
                     (_R_RANK1, rank1), (_R_RANK2, rank2)):
        route = jnp.where(lane == idx, val, route)
    route_ref[...] = route
    route_t_ref[...] = route.T[:route_t_ref.shape[0], :]


def _outproj_route(yr, ya, h, wor, woa, nw, wrt, brt):
    t, d = h.shape
    tm = min(TM_ROUTE, t)
    d_r, d_a = yr.shape[1], ya.shape[1]
    nl = brt.shape[1]
    tok = lambda i: (i, 0)
    const = lambda i: (0, 0)
    return pl.pallas_call(
        _outproj_route_kernel,
        out_shape=(jax.ShapeDtypeStruct((t, d), F32),
                   jax.ShapeDtypeStruct((t * V7X_SUBLANES, V7X_LANES), F32),
                   jax.ShapeDtypeStruct((t, nl), F32), jax.ShapeDtypeStruct((V7X_SUBLANES, t), F32),
                   jax.ShapeDtypeStruct((V7X_SUBLANES, nl), F32)),
        grid=(t // tm,),
        in_specs=[pl.BlockSpec((tm, d_r), tok), pl.BlockSpec((tm, d_a), tok), pl.BlockSpec((tm, d), tok),
                  pl.BlockSpec((d_r, d), const), pl.BlockSpec((d_a, d), const),
                  pl.BlockSpec((1, d), const), pl.BlockSpec(wrt.shape, const), pl.BlockSpec((1, nl), const)],
        out_specs=(pl.BlockSpec((tm, d), tok), pl.BlockSpec((tm * V7X_SUBLANES, V7X_LANES), tok),
                   pl.BlockSpec((tm, nl), tok), pl.BlockSpec((V7X_SUBLANES, tm), lambda i: (0, i)),
                   pl.BlockSpec((V7X_SUBLANES, nl), const)),
        scratch_shapes=[pltpu.VMEM((V7X_SUBLANES, nl), F32)],
        compiler_params=_params("arbitrary"),
        name="outproj_route",
    )(yr, ya, h, wor, woa, nw, wrt, brt)


def _load_indices(idx_hbm, i, idx_smem, sem):
    n = idx_smem.shape[0]
    cp = pltpu.make_async_copy(idx_hbm.at[pl.ds(pl.multiple_of(i * n, n), n)], idx_smem, sem)
    cp.start()
    cp.wait()


def _dispatch_kernel(idx_hbm, x_ref, xs_hbm, idx_smem, zero_vmem, idx_sem, row_sem, zero_sem,
                     *, tm, n_free):
    i = pl.program_id(0)
    _load_indices(idx_hbm, i, idx_smem, idx_sem)

    def issue(tt, carry):
        for s in range(2):
            pltpu.make_async_copy(_token_tile(x_ref, tt), _token_tile(xs_hbm, idx_smem[s * tm + tt]),
                                  row_sem).start()
        return carry

    lax.fori_loop(0, tm, issue, 0, unroll=8)

    zero_vmem[...] = jnp.zeros_like(zero_vmem)

    def zero_copy(dst_row):
        return pltpu.make_async_copy(zero_vmem, _token_tile(xs_hbm, dst_row), zero_sem)

    def issue_zero(n, carry):
        zero_copy(idx_smem[2 * tm + n]).start()
        return carry

    lax.fori_loop(0, n_free, issue_zero, 0, unroll=8)

    for s in range(2):
        pltpu.make_async_copy(x_ref, xs_hbm.at[pl.ds(0, x_ref.shape[0]), :], row_sem).wait()

    def drain_zero(n, carry):
        zero_copy(0).wait()
        return carry

    lax.fori_loop(0, n_free, drain_zero, 0)


def _dispatch(x, idx, n_rows, tm, n_free):
    t = x.shape[0] // V7X_SUBLANES
    nb = t // tm
    rec = idx.shape[0] // nb
    return pl.pallas_call(
        functools.partial(_dispatch_kernel, tm=tm, n_free=n_free),
        out_shape=jax.ShapeDtypeStruct((n_rows * V7X_SUBLANES, V7X_LANES), x.dtype),
        grid=(nb,),
        in_specs=[pl.BlockSpec(memory_space=pl.ANY),
                  pl.BlockSpec((tm * V7X_SUBLANES, V7X_LANES), lambda i: (i, 0))],
        out_specs=pl.BlockSpec(memory_space=pl.ANY),
        scratch_shapes=[pltpu.SMEM((rec,), I32), pltpu.VMEM((V7X_SUBLANES, V7X_LANES), x.dtype),
                        pltpu.SemaphoreType.DMA, pltpu.SemaphoreType.DMA, pltpu.SemaphoreType.DMA],
        compiler_params=_params("arbitrary"),
        name="dispatch",
    )(idx, x)


def _experts_kernel(te_ref, tv_ref, tf_ref, x_ref, w1_ref, w3_ref, w2_ref, y_ref,
                    w1_b, w3_b, w2_b, *, tm):
    i = pl.program_id(0)

    @pl.when(tf_ref[i] > 0)
    def _():
        w1_b[...] = w1_ref[...].astype(BF16)
        w3_b[...] = w3_ref[...].astype(BF16)
        w2_b[...] = w2_ref[...].astype(BF16)

    @pl.when(tv_ref[i] > 0)
    def _():
        x = _from_token_tiles(x_ref, tm).astype(BF16)
        h_gate = jnp.dot(x, w1_b[...], preferred_element_type=F32)
        h_up = jnp.dot(x, w3_b[...], preferred_element_type=F32)
        hid = (h_gate * jax.nn.sigmoid(h_gate) * h_up).astype(BF16)
        _to_token_tiles(y_ref, jnp.dot(hid, w2_b[...], preferred_element_type=F32))

    @pl.when(tv_ref[i] == 0)
    def _():
        y_ref[...] = jnp.zeros_like(y_ref)


def _experts(xs, w1, w3, w2, tile_expert, tile_valid, tile_first):
    n_rows = xs.shape[0] // V7X_SUBLANES
    tm = TM_EXPERT
    nt = n_rows // tm
    d, f = w1.shape[1:]
    tile_spec = pl.BlockSpec((tm * V7X_SUBLANES, V7X_LANES), lambda i, te, tv, tf: (i, 0))
    grid_spec = pltpu.PrefetchScalarGridSpec(
        num_scalar_prefetch=3,
        grid=(nt,),
        in_specs=[tile_spec,
                  pl.BlockSpec((None, d, f), lambda i, te, tv, tf: (te[i], 0, 0)),
                  pl.BlockSpec((None, d, f), lambda i, te, tv, tf: (te[i], 0, 0)),
                  pl.BlockSpec((None, f, d), lambda i, te, tv, tf: (te[i], 0, 0))],
        out_specs=tile_spec,
        scratch_shapes=[pltpu.VMEM((d, f), BF16), pltpu.VMEM((d, f), BF16), pltpu.VMEM((f, d), BF16)],
    )
    return pl.pallas_call(
        functools.partial(_experts_kernel, tm=tm),
        out_shape=jax.ShapeDtypeStruct(xs.shape, F32),
        grid_spec=grid_spec,
        compiler_params=_params("arbitrary"),
        name="experts",
    )(tile_expert, tile_valid, tile_first, xs, w1, w3, w2)


def _combine_ple_kernel(pos_hbm, ys_hbm, h_ref, route_ref, p_ref, nw_ref, wg_ref, bg_ref, wp_ref,
                        fw_ref, o_ref, idx_smem, ybuf, idx_sem, row_sem, *, tm, final):
    i = pl.program_id(0)
    nb = pl.num_programs(0)
    cur = lax.rem(i, 2)
    nxt = 1 - cur

    def gather(step, buf):
        _load_indices(pos_hbm, step, idx_smem, idx_sem)

        def issue(tt, carry):
            for s in range(2):
                pltpu.make_async_copy(_token_tile(ys_hbm, idx_smem[s * tm + tt]),
                                      _token_tile(ybuf.at[buf, s], tt), row_sem.at[buf]).start()
            return carry

        lax.fori_loop(0, tm, issue, 0, unroll=8)

    def wait_gather(buf):
        for s in range(2):
            pltpu.make_async_copy(ys_hbm.at[pl.ds(0, ybuf.shape[2]), :], ybuf.at[buf, s],
                                  row_sem.at[buf]).wait()

    @pl.when(i == 0)
    def _():
        gather(0, 0)

    @pl.when(i + 1 < nb)
    def _():
        gather(i + 1, nxt)

    wait_gather(cur)

    route = route_ref[...]
    c1 = route[:, _R_C1:_R_C1 + 1]
    c2 = route[:, _R_C2:_R_C2 + 1]
    h2 = (h_ref[...] + c1 * _from_token_tiles(ybuf.at[cur, 0], tm)
          + c2 * _from_token_tiles(ybuf.at[cur, 1], tm))
    hn = _rms(h2, nw_ref[...]).astype(BF16)
    gate = jax.nn.sigmoid(jnp.dot(hn, wg_ref[...], preferred_element_type=F32) + bg_ref[...])
    h3 = h2 + gate * jnp.dot(p_ref[...].astype(BF16), wp_ref[...], preferred_element_type=F32)
    if final:
        h3 = _rms(h3, fw_ref[...])
    o_ref[...] = h3


def _combine_ple(pos_tiles, ys, h, route, p, nw, wg, bg, wp, fw, *, tm, final):
    t, d = h.shape
    nl = route.shape[1]
    dp = p.shape[1]
    tok = lambda i: (i, 0)
    const = lambda i: (0, 0)
    return pl.pallas_call(
        functools.partial(_combine_ple_kernel, tm=tm, final=final),
        out_shape=jax.ShapeDtypeStruct((t, d), F32),
        grid=(t // tm,),
        in_specs=[pl.BlockSpec(memory_space=pl.ANY), pl.BlockSpec(memory_space=pl.ANY),
                  pl.BlockSpec((tm, d), tok), pl.BlockSpec((tm, nl), tok), pl.BlockSpec((tm, dp), tok),
                  pl.BlockSpec((1, d), const), pl.BlockSpec((d, d), const), pl.BlockSpec((1, d), const),
                  pl.BlockSpec((dp, d), const), pl.BlockSpec((1, d), const)],
        out_specs=pl.BlockSpec((tm, d), tok),
        scratch_shapes=[pltpu.SMEM((pos_tiles.shape[0] // (t // tm),), I32),
                        pltpu.VMEM((2, 2, tm * V7X_SUBLANES, V7X_LANES), F32),
                        pltpu.SemaphoreType.DMA, pltpu.SemaphoreType.DMA((2,))],
        compiler_params=_params("arbitrary"),
        name="combine_ple_final" if final else "combine_ple",
    )(pos_tiles, ys, h, route, p, nw, wg, bg, wp, fw)


_SMEM_RECORD_WORDS = 1024


def _index_records(pos1, pos2, tm, extra=None):
    nb = pos1.shape[0] // tm
    parts = [pos1.reshape(nb, tm), pos2.reshape(nb, tm)]
    if extra is not None:
        parts.append(extra.reshape(nb, -1))
    rec = jnp.concatenate(parts, axis=1)
    pad = -rec.shape[1] % _SMEM_RECORD_WORDS
    return jnp.pad(rec, ((0, 0), (0, pad))).reshape(-1)


def _lookup(table, idx):
    ids = jnp.arange(table.shape[0], dtype=I32)
    return jnp.sum(jnp.where(idx[None, :] == ids[:, None], table[:, None], 0), axis=0)


def _bucket(ends, x):
    return jnp.minimum(jnp.sum((x[None, :] >= ends[:, None]).astype(I32), axis=0), ends.shape[0] - 1)


def kernel(x, p, norm_mix_w, w_in, rwkv_mu, rwkv_w0, rwkv_w2, rwkv_a0, rwkv_a2, rwkv_g2, rwkv_k_k, rwkv_k_a, rwkv_r_k, rwkv_ln_w, rwkv_ln_b, rwkv_v0, rwkv_v1, rwkv_v2, att_rel_bias, w_out, norm_ffn_w, router_group_w, router_group_b, router_expert_w, router_expert_b, expert_w1, expert_w3, expert_w2, norm_ple_w, ple_gate_w, ple_gate_b, ple_proj_w, final_norm_w):
    batch, seq, d = x.shape
    depth = w_in.shape[0]
    t = batch * seq
    d_r = rwkv_w0.shape[1]
    n_heads_r = d_r // HEAD_DIM
    n_rwkv_in = rwkv_mu.shape[1]
    d_a = (w_in.shape[2] - n_rwkv_in) // 3
    n_heads_a = d_a // HEAD_DIM
    n_dec, n_iclr, n_gate = rwkv_w2.shape[1], rwkv_a2.shape[1], rwkv_g2.shape[1]
    assert n_dec == n_iclr and n_gate == n_dec + n_iclr
    n_lo = n_dec + n_iclr + n_gate
    f_exp = expert_w1.shape[-1]
    assert d == V7X_SUBLANES * V7X_LANES
    n_rows = 2 * t + N_EXPERTS * TM_EXPERT
    n_tiles = n_rows // TM_EXPERT
    qb = min(QB_ATTN, seq)

    w1_all = expert_w1.reshape(depth * N_EXPERTS, d, f_exp)
    w3_all = expert_w3.reshape(depth * N_EXPERTS, d, f_exp)
    w2_all = expert_w2.reshape(depth * N_EXPERTS, f_exp, d)

    h = x.reshape(t, d)
    v_first = None
    for i in range(depth):
        wr = w_in[i, :, :n_rwkv_in].astype(BF16)
        wa = w_in[i, :, n_rwkv_in:].astype(BF16)
        wl = jnp.zeros((n_lo, 3 * d_r), F32)
        wl = wl.at[:n_dec, :d_r].set(rwkv_w2[i])
        wl = wl.at[n_dec:n_dec + n_iclr, d_r:2 * d_r].set(rwkv_a2[i])
        wl = wl.at[n_dec + n_iclr:, 2 * d_r:].set(rwkv_g2[i]).astype(BF16)
        v0 = rwkv_v0[i - 1] if i > 0 else jnp.zeros((d_r,), F32)
        vec = jnp.stack([rwkv_w0[i], rwkv_a0[i], rwkv_k_k[i], rwkv_k_a[i], rwkv_r_k[i],
                         rwkv_ln_w[i], rwkv_ln_b[i], v0])
        if i > 0:
            n_vr = rwkv_v1.shape[2]
            v1 = jnp.zeros((d_r, V7X_LANES), F32).at[:, :n_vr].set(rwkv_v1[i - 1]).astype(BF16)
            v2 = jnp.zeros((V7X_LANES, d_r), F32).at[:n_vr, :].set(rwkv_v2[i - 1]).astype(BF16)
        else:
            v1 = v2 = None
        table = _attn_table(att_rel_bias[i], qb)
        wor = w_out[i, :d_r].astype(BF16)
        woa = w_out[i, d_r:].astype(BF16)
        n_rt = N_GROUPS + N_EXPERTS
        wrt = jnp.zeros((d, V7X_LANES), F32)
        wrt = wrt.at[:, :N_GROUPS].set(router_group_w[i]).at[:, N_GROUPS:n_rt].set(router_expert_w[i])
        wrt_hi = wrt.astype(BF16)
        wrt = jnp.concatenate([wrt_hi, (wrt - wrt_hi.astype(F32)).astype(BF16)], axis=1)
        brt = jnp.zeros((1, V7X_LANES), F32)
        brt = brt.at[0, :N_GROUPS].set(router_group_b[i]).at[0, N_GROUPS:n_rt].set(router_expert_b[i])

        z_r, qkv = _norm_proj(h, norm_mix_w[i][None], wr, wa)
        if i == 0:
            y_r, v_first = _rwkv(z_r, None, rwkv_mu[i][None], vec, wl, None, None,
                                 batch=batch, seq=seq, n_heads=n_heads_r)
        else:
            y_r = _rwkv(z_r, v_first, rwkv_mu[i][None], vec, wl, v1, v2,
                        batch=batch, seq=seq, n_heads=n_heads_r)
        y_a = _attn(qkv, table, batch=batch, seq=seq, n_heads=n_heads_a)

        h1, hn, route, route_t, cnt = _outproj_route(y_r, y_a, h, wor, woa, norm_ffn_w[i][None], wrt, brt)
        ri = route_t.astype(I32)
        counts = cnt[0, :N_EXPERTS].astype(I32)
        padded = ((counts + TM_EXPERT - 1) // TM_EXPERT) * TM_EXPERT
        p_end = jnp.cumsum(padded)
        p_start = p_end - padded
        pos1 = _lookup(p_start, ri[_R_E1]) + ri[_R_RANK1]
        pos2 = _lookup(p_start, ri[_R_E2]) + ri[_R_RANK2]
        tile_start = jnp.arange(n_tiles, dtype=I32) * TM_EXPERT
        tile_expert = _bucket(p_end, tile_start)
        tile_valid = (tile_start < p_end[-1]).astype(I32)

        tile_first = jnp.concatenate([jnp.ones((1,), I32),
                                      (tile_expert[1:] != tile_expert[:-1]).astype(I32)])

        n_free = n_rows - 2 * t
        n_pad = padded - counts
        f_end = jnp.cumsum(n_pad)
        kf = jnp.arange(n_free, dtype=I32)
        ef = _bucket(f_end, kf)
        pad_row = _lookup(p_start + counts - (f_end - n_pad), ef) + kf
        free_rows = jnp.where(kf < f_end[-1], pad_row, p_end[-1] + (kf - f_end[-1])).astype(I32)

        tm_d = min(TM_DISPATCH, t)
        nb_d = t // tm_d
        assert n_free % nb_d == 0
        xs = _dispatch(hn, _index_records(pos1, pos2, tm_d, free_rows), n_rows, tm_d, n_free // nb_d)
        ys = _experts(xs, w1_all, w3_all, w2_all, tile_expert + i * N_EXPERTS, tile_valid, tile_first)

        tm_c = min(TM_COMBINE, t)
        h = _combine_ple(_index_records(pos1, pos2, tm_c), ys, h1, route, p[i].reshape(t, -1),
                         norm_ple_w[i][None], ple_gate_w[i].astype(BF16), ple_gate_b[i][None],
                         ple_proj_w[i].astype(BF16), final_norm_w[None],
                         tm=tm_c, final=(i == depth - 1))
    return h.reshape(batch, seq, d)
```

```python
import functools

import jax
import jax.numpy as jnp
from jax import lax
from jax.experimental import pallas as pl
from jax.experimental.pallas import tpu as pltpu

F32 = jnp.float32
BF16 = jnp.bfloat16
I32 = jnp.int32

CHUNK = 64
HEAD_DIM = 64
LEFT_CHUNKS = 8
MAX_REL = 256
N_GROUPS = 4
EXPERTS_PER_GROUP = 8
N_EXPERTS = N_GROUPS * EXPERTS_PER_GROUP
RMS_EPS = 1e-6
GN_EPS = 64e-5
NEG_INF = -1e30

V7X_LANES = 128
V7X_SUBLANES = 8
V7X_VMEM_LIMIT_BYTES = 48 * 1024 * 1024

TM_PROJ = 512
TB_RWKV = 256
RWKV_CHUNK_GROUP = 2
QB_ATTN = 256
TM_ROUTE = 512
TM_DISPATCH = 512
TM_EXPERT = 256
TM_COMBINE = 512


def _params(*sem):
    return pltpu.CompilerParams(dimension_semantics=sem, vmem_limit_bytes=V7X_VMEM_LIMIT_BYTES)


def _rms(x, w):
    return x * lax.rsqrt(jnp.mean(x * x, axis=-1, keepdims=True) + RMS_EPS) * w


def _mm(a, b):
    return jnp.dot(a.astype(BF16), b.astype(BF16), preferred_element_type=F32)


def _mm_nt(a, b):
    return lax.dot_general(a.astype(BF16), b.astype(BF16), (((1,), (1,)), ((), ())),
                           preferred_element_type=F32)


def _mm_tn(a, b):
    return lax.dot_general(a.astype(BF16), b.astype(BF16), (((0,), (0,)), ((), ())),
                           preferred_element_type=F32)


def _to_token_tiles(ref, x):
    m, d = x.shape
    for s in range(d // V7X_LANES):
        ref[pl.ds(s, m, stride=V7X_SUBLANES), :] = x[:, s * V7X_LANES:(s + 1) * V7X_LANES]


def _from_token_tiles(ref, m):
    return jnp.concatenate([ref[pl.ds(s, m, stride=V7X_SUBLANES), :] for s in range(V7X_SUBLANES)],
                           axis=-1)


def _token_tile(ref, row):
    return ref.at[pl.ds(pl.multiple_of(row * V7X_SUBLANES, V7X_SUBLANES), V7X_SUBLANES), :]


def _split3(x):
    hi = x.astype(BF16)
    r1 = x - hi.astype(F32)
    mid = r1.astype(BF16)
    lo = (r1 - mid.astype(F32)).astype(BF16)
    return hi, mid, lo


def _mm_exact_lhs(a_bf16, x):
    hi, mid, lo = _split3(x)
    return (jnp.dot(a_bf16, hi, preferred_element_type=F32)
            + jnp.dot(a_bf16, mid, preferred_element_type=F32)
            + jnp.dot(a_bf16, lo, preferred_element_type=F32))


def _mm_exact_rhs(x, b_bf16):
    hi, mid, lo = _split3(x)
    return (jnp.dot(hi, b_bf16, preferred_element_type=F32)
            + jnp.dot(mid, b_bf16, preferred_element_type=F32)
            + jnp.dot(lo, b_bf16, preferred_element_type=F32))


def _norm_proj_kernel(h_ref, nw_ref, wr_ref, wa_ref, zr_ref, qkv_ref):
    hn = _rms(h_ref[...], nw_ref[...]).astype(BF16)
    zr_ref[...] = jnp.dot(hn, wr_ref[...], preferred_element_type=F32)
    qkv_ref[...] = jnp.dot(hn, wa_ref[...], preferred_element_type=F32).astype(BF16)


def _norm_proj(h, nw, wr, wa):
    t, d = h.shape
    tm = min(TM_PROJ, t)
    n_r, n_a = wr.shape[1], wa.shape[1]
    return pl.pallas_call(
        _norm_proj_kernel,
        out_shape=(jax.ShapeDtypeStruct((t, n_r), F32), jax.ShapeDtypeStruct((t, n_a), BF16)),
        grid=(t // tm,),
        in_specs=[pl.BlockSpec((tm, d), lambda i: (i, 0)),
                  pl.BlockSpec((1, d), lambda i: (0, 0)),
                  pl.BlockSpec((d, n_r), lambda i: (0, 0)),
                  pl.BlockSpec((d, n_a), lambda i: (0, 0))],
        out_specs=(pl.BlockSpec((tm, n_r), lambda i: (i, 0)),
                   pl.BlockSpec((tm, n_a), lambda i: (i, 0))),
        compiler_params=_params("parallel"),
        name="norm_proj",
    )(h, nw, wr, wa)


_V_W0, _V_A0, _V_KK, _V_KA, _V_RK, _V_LNW, _V_LNB, _V_V0 = range(8)


def _rwkv_kernel(*refs, has_vres, n_heads, d_r, group):
    if has_vres:
        (z_ref, vf_ref, mu_ref, vec_ref, wl_ref, tril_ref, ones_ref, v1_ref, v2_ref, y_ref,
         s_ref, carry_ref, r_s, k_s, v_s, kk_s, a_s, lc_s, lw_s, bon_s, g_s) = refs
        vf_out_ref = None
    else:
        (z_ref, mu_ref, vec_ref, wl_ref, tril_ref, ones_ref, y_ref, vf_out_ref,
         s_ref, carry_ref, r_s, k_s, v_s, kk_s, a_s, lc_s, lw_s, bon_s, g_s) = refs
    tb = z_ref.shape[0]
    n_chunks = tb // CHUNK
    j = pl.program_id(1)

    @pl.when(j == 0)
    def _():
        s_ref[...] = jnp.zeros_like(s_ref)
        carry_ref[...] = jnp.zeros_like(carry_ref)

    z = z_ref[...]
    row = lax.broadcasted_iota(I32, z.shape, 0)
    z_prev = jnp.where(row == 0, carry_ref[0:1, :], pltpu.roll(z, 1, axis=0))
    carry_ref[0:1, :] = z[tb - 1:tb, :]
    zs = z + (z_prev - z) * mu_ref[...]

    vec = vec_ref[...]

    def vrow(i):
        return vec[i:i + 1, :]

    r = zs[:, 0:d_r]
    k = zs[:, d_r:2 * d_r]
    v = zs[:, 2 * d_r:3 * d_r]
    lo = zs[:, 3 * d_r:]
    n_lo = lo.shape[1]
    lane = lax.broadcasted_iota(I32, lo.shape, 1)
    lo_act = jnp.where(lane < n_lo // 4, jnp.tanh(lo),
                       jnp.where(lane < n_lo // 2, lo, jax.nn.sigmoid(lo)))
    lo_out = _mm(lo_act, wl_ref[...])

    if has_vres:
        vv = _mm(_mm(v, v1_ref[...]), v2_ref[...])
        v = v + (vf_ref[...] - v) * jax.nn.sigmoid(vrow(_V_V0) + vv)
    else:
        vf_out_ref[...] = v

    w_log = -jax.nn.softplus(-(vrow(_V_W0) + lo_out[:, 0:d_r])) - 0.5
    lw = -jnp.exp(w_log)
    a = jax.nn.sigmoid(vrow(_V_A0) + lo_out[:, d_r:2 * d_r])
    g = lo_out[:, 2 * d_r:3 * d_r]

    head_ones = ones_ref[...]
    kk = k * vrow(_V_KK)
    kk = kk * lax.rsqrt(jnp.maximum(_mm_exact_rhs(kk * kk, head_ones), 1e-24))
    k2 = k * (1.0 + (a - 1.0) * vrow(_V_KA))
    bonus = _mm_exact_rhs(r * k2 * vrow(_V_RK), head_ones) * v

    lc = _mm_exact_lhs(tril_ref[...], lw)

    r_s[...] = r
    k_s[...] = k2
    v_s[...] = v
    kk_s[...] = kk
    a_s[...] = a
    lc_s[...] = lc
    lw_s[...] = lw
    bon_s[...] = bonus
    g_s[...] = g

    ci = lax.broadcasted_iota(I32, (CHUNK, CHUNK), 0)
    cj = lax.broadcasted_iota(I32, (CHUNK, CHUNK), 1)
    strict = cj < ci
    lower = cj <= ci
    eye = ci == cj
    eye_f = jnp.where(eye, 1.0, 0.0)
    ln_w = vrow(_V_LNW)
    ln_b = vrow(_V_LNB)

    hs = [slice(h * HEAD_DIM, (h + 1) * HEAD_DIM) for h in range(n_heads)]

    def chunk_operands(r0):
        rs = pl.ds(r0, CHUNK)
        lc_c = lc_s[rs, :]
        lw_c = lw_s[rs, :]
        l_end = lc_s[pl.ds(r0 + CHUNK - 1, 1), :]
        p_in = jnp.exp(lc_c)
        p_prev = jnp.exp(lc_c - lw_c)
        p_inv = jnp.exp(-lc_c)
        p_end = jnp.exp(l_end - lc_c)
        p_last = jnp.exp(l_end)
        kk_c = kk_s[rs, :]
        b_c = kk_c * a_s[rs, :]
        k_c = k_s[rs, :]
        at = (-kk_c * p_prev).astype(BF16)
        bt = (b_c * p_inv).astype(BF16)
        bh = (b_c * p_end).astype(BF16)
        kt = (k_c * p_inv).astype(BF16)
        kh = (k_c * p_end).astype(BF16)
        rt = (r_s[rs, :] * p_in).astype(BF16)
        vc = v_s[rs, :].astype(BF16)
        per_head = [[x[:, sl] for sl in hs] for x in (at, bt, bh, kt, kh, rt, vc)]
        per_head.append([p_last[:, sl] for sl in hs])
        return per_head

    def group_body(gi, carry):
        g0 = pl.multiple_of(gi * (group * CHUNK), group * CHUNK)
        ops = [chunk_operands(g0 + c * CHUNK) for c in range(group)]
        at_h, bt_h, bh_h, kt_h, kh_h, rt_h, v_h, pl_h = ([x for c in range(group) for x in ops[c][q]]
                                                         for q in range(8))
        heads = range(group * n_heads)
        ar_h = [jnp.concatenate([at_h[h], rt_h[h]], axis=0) for h in heads]
        m_b = [_mm_nt(ar_h[h], bt_h[h]) for h in heads]
        m_k = [_mm_nt(ar_h[h], kt_h[h]) for h in heads]
        n_ab = [jnp.where(strict, m_b[h][:CHUNK], 0.0) for h in heads]
        a_ak = [jnp.where(strict, m_k[h][:CHUNK], 0.0) for h in heads]
        a_rb = [jnp.where(lower, m_b[h][CHUNK:], 0.0) for h in heads]
        a_rk = [jnp.where(lower, m_k[h][CHUNK:], 0.0) for h in heads]
        x_inv = [eye_f + n_ab[h] for h in heads]
        pw = [_mm(n_ab[h], n_ab[h]) for h in heads]
        akv = [_mm(a_ak[h], v_h[h]) for h in heads]
        n_sq = CHUNK.bit_length() - 2
        for it in range(n_sq):
            if it < n_sq - 1:
                st = [_mm(jnp.concatenate([x_inv[h], pw[h]], axis=0), pw[h]) for h in heads]
                x_inv = [x_inv[h] + st[h][:CHUNK] for h in heads]
                pw = [st[h][CHUNK:] for h in heads]
            else:
                st = [_mm(x_inv[h], pw[h]) for h in heads]
                x_inv = [x_inv[h] + st[h] for h in heads]
        w_h = [_mm(x_inv[h], at_h[h]) for h in heads]
        u0 = [_mm(x_inv[h], akv[h]) for h in heads]
        y0 = [_mm(a_rk[h], v_h[h]) + _mm(a_rb[h], u0[h]) for h in heads]
        r_p = [rt_h[h].astype(F32) + _mm(a_rb[h], w_h[h]) for h in heads]
        g_h = [jnp.where(eye, pl_h[h], 0.0) + _mm_tn(w_h[h], bh_h[h]) for h in heads]
        d_h = [_mm_tn(u0[h], bh_h[h]) + _mm_tn(v_h[h], kh_h[h]) for h in heads]
        s_h = [s_ref[h] for h in range(n_heads)]
        for c in range(group):
            rs = pl.ds(g0 + c * CHUNK, CHUNK)
            idx = [c * n_heads + h for h in range(n_heads)]
            y_h = [y0[i] + _mm_nt(r_p[i], s_h[h]) for h, i in enumerate(idx)]
            s_h = [_mm(s_h[h], g_h[i]) + d_h[i] for h, i in enumerate(idx)]
            y_heads = []
            for h in range(n_heads):
                mean = jnp.mean(y_h[h], axis=-1, keepdims=True)
                yc = y_h[h] - mean
                var = jnp.mean(yc * yc, axis=-1, keepdims=True)
                y_heads.append(yc * lax.rsqrt(var + GN_EPS))
            y_n = jnp.concatenate(y_heads, axis=-1)
            out = (y_n * ln_w + ln_b + bon_s[rs, :]) * g_s[rs, :]
            y_ref[rs, :] = out.astype(y_ref.dtype)
        for h in range(n_heads):
            s_ref[h] = s_h[h]
        return carry

    lax.fori_loop(0, n_chunks // group, group_body, 0)


def _rwkv(z, v_first, mu, vec, wl, v1, v2, *, batch, seq, n_heads):
    t, n_z = z.shape
    d_r = n_heads * HEAD_DIM
    tb = min(TB_RWKV, seq)
    nb = seq // tb
    has_vres = v_first is not None
    tok = lambda b, j: (b * nb + j, 0)
    const = lambda b, j: (0, 0)
    in_specs = [pl.BlockSpec((tb, n_z), tok)]
    args = [z]
    if has_vres:
        in_specs.append(pl.BlockSpec((tb, d_r), tok))
        args.append(v_first)
    ti = jnp.arange(tb)
    tril = ((ti[:, None] // CHUNK == ti[None, :] // CHUNK) & (ti[None, :] <= ti[:, None])).astype(BF16)
    hi = jnp.arange(d_r) // HEAD_DIM
    head_ones = (hi[:, None] == hi[None, :]).astype(BF16)
    in_specs += [pl.BlockSpec(mu.shape, const), pl.BlockSpec(vec.shape, const),
                 pl.BlockSpec(wl.shape, const), pl.BlockSpec(tril.shape, const),
                 pl.BlockSpec(head_ones.shape, const)]
    args += [mu, vec, wl, tril, head_ones]
    n_chunks = tb // CHUNK
    group = RWKV_CHUNK_GROUP if n_chunks % RWKV_CHUNK_GROUP == 0 else 1
    if has_vres:
        in_specs += [pl.BlockSpec(v1.shape, const), pl.BlockSpec(v2.shape, const)]
        args += [v1, v2]
        out_shape = jax.ShapeDtypeStruct((t, d_r), BF16)
        out_specs = pl.BlockSpec((tb, d_r), tok)
    else:
        out_shape = (jax.ShapeDtypeStruct((t, d_r), BF16), jax.ShapeDtypeStruct((t, d_r), F32))
        out_specs = (pl.BlockSpec((tb, d_r), tok), pl.BlockSpec((tb, d_r), tok))
    scratch = [pltpu.VMEM((n_heads, HEAD_DIM, HEAD_DIM), F32),
               pltpu.VMEM((V7X_SUBLANES, n_z), F32)]
    scratch += [pltpu.VMEM((tb, d_r), F32) for _ in range(9)]
    return pl.pallas_call(
        functools.partial(_rwkv_kernel, has_vres=has_vres, n_heads=n_heads, d_r=d_r, group=group),
        out_shape=out_shape,
        grid=(batch, nb),
        in_specs=in_specs,
        out_specs=out_specs,
        scratch_shapes=scratch,
        compiler_params=_params("arbitrary", "arbitrary"),
        name="rwkv_vres" if has_vres else "rwkv",
    )(*args)


def _attn_kernel(*refs, n_heads, n_parts):
    q_ref = refs[0]
    k_refs = refs[1:1 + n_parts]
    v_refs = refs[1 + n_parts:1 + 2 * n_parts]
    tab_ref = refs[1 + 2 * n_parts]
    o_ref = refs[2 + 2 * n_parts]
    qb = q_ref.shape[0]
    j = pl.program_id(1)
    scale = HEAD_DIM ** -0.5
    q = q_ref[...] * jnp.asarray(scale, q_ref.dtype)
    ks = [r[...] for r in k_refs]
    vs = [r[...] for r in v_refs]
    pw = 2 * HEAD_DIM
    lane = lax.broadcasted_iota(I32, (qb, pw), 1)
    sum_even = jnp.where(lane < HEAD_DIM, 1.0, 0.0).astype(q.dtype)
    sum_odd = jnp.where(lane < HEAD_DIM, 0.0, 1.0).astype(q.dtype)
    zero = jnp.zeros((), q.dtype)
    is_even = sum_even > zero
    outs = []
    for hp in range(n_heads // 2):
        sl = slice(hp * pw, (hp + 1) * pw)
        qq = q[:, sl]
        q_pair = (jnp.where(is_even, qq, zero), jnp.where(is_even, zero, qq))
        s_parts = [[], []]
        for p in range(n_parts):
            kk = ks[p][:, sl]
            back = n_parts - 1 - p
            for u in range(2):
                s = _mm_nt(q_pair[u], kk) + tab_ref[2 * hp + u, :, p * qb:(p + 1) * qb]
                if back > 0:
                    s = jnp.where(j >= back, s, NEG_INF)
                s_parts[u].append(s)
        m = []
        for u in range(2):
            mm = s_parts[u][0]
            for s in s_parts[u][1:]:
                mm = jnp.maximum(mm, s)
            m.append(mm.max(axis=-1, keepdims=True))
        acc = jnp.zeros((qb, 2 * pw), F32)
        for p in range(n_parts):
            vv = vs[p][:, sl]
            rhs = jnp.concatenate(
                [jnp.concatenate([jnp.where(is_even, vv, zero), sum_even], axis=1),
                 jnp.concatenate([jnp.where(is_even, zero, vv), sum_odd], axis=1)], axis=0)
            e = jnp.concatenate([jnp.exp(s_parts[u][p] - m[u]).astype(BF16) for u in range(2)], axis=1)
            acc = acc + jnp.dot(e, rhs, preferred_element_type=F32)
        outs.append(acc[:, :pw] / acc[:, pw:])
    o_ref[...] = jnp.concatenate(outs, axis=-1).astype(o_ref.dtype)


def _attn(qkv, table, *, batch, seq, n_heads):
    t = qkv.shape[0]
    d_a = n_heads * HEAD_DIM
    qb = min(QB_ATTN, seq)
    left = LEFT_CHUNKS * CHUNK
    assert left % qb == 0 and seq % qb == 0
    n_parts = left // qb + 1
    nb = seq // qb
    in_specs = [pl.BlockSpec((qb, d_a), lambda b, j: (b * nb + j, 0))]
    for p in range(n_parts):
        back = n_parts - 1 - p
        in_specs.append(pl.BlockSpec((qb, d_a), lambda b, j, back=back: (b * nb + jnp.maximum(j - back, 0), 1)))
    for p in range(n_parts):
        back = n_parts - 1 - p
        in_specs.append(pl.BlockSpec((qb, d_a), lambda b, j, back=back: (b * nb + jnp.maximum(j - back, 0), 2)))
    in_specs.append(pl.BlockSpec(table.shape, lambda b, j: (0, 0, 0)))
    return pl.pallas_call(
        functools.partial(_attn_kernel, n_heads=n_heads, n_parts=n_parts),
        out_shape=jax.ShapeDtypeStruct((t, d_a), BF16),
        grid=(batch, nb),
        in_specs=in_specs,
        out_specs=pl.BlockSpec((qb, d_a), lambda b, j: (b * nb + j, 0)),
        compiler_params=_params("parallel", "arbitrary"),
        name="attn",
    )(*([qkv] * (1 + 2 * n_parts)), table)


def _attn_table(rel_bias, qb):
    left = LEFT_CHUNKS * CHUNK
    n_keys = left + qb
    period = qb + n_keys - 1
    n_heads = rel_bias.shape[0]
    m = jnp.arange(period)
    rel = left - jnp.where(m < n_keys, m, m - period)
    g = rel_bias[:, jnp.clip(rel, -(CHUNK - 1), MAX_REL) + (CHUNK - 1)].astype(F32)
    flat = jnp.tile(g, (1, qb))[:, :qb * (period - 1)]
    bias = flat.reshape(n_heads, qb, period - 1)[:, :, :n_keys]
    cq = jnp.arange(qb)[:, None] // CHUNK
    ck = jnp.arange(n_keys)[None, :] // CHUNK
    valid = (ck >= cq) & (ck <= cq + LEFT_CHUNKS)
    return jnp.where(valid[None], bias, NEG_INF)


_R_E1, _R_E2, _R_C1, _R_C2, _R_RANK1, _R_RANK2 = range(6)


def _outproj_route_kernel(yr_ref, ya_ref, h_ref, wor_ref, woa_ref, nw_ref, wrt_ref, brt_ref,
                          h1_ref, hn_ref, route_ref, route_t_ref, cnt_ref, carry_ref):
    i = pl.program_id(0)

    @pl.when(i == 0)
    def _():
        carry_ref[...] = jnp.zeros_like(carry_ref)

    h1 = (h_ref[...] + jnp.dot(yr_ref[...], wor_ref[...], preferred_element_type=F32)
          + jnp.dot(ya_ref[...], woa_ref[...], preferred_element_type=F32))
    h1_ref[...] = h1
    hn = _rms(h1, nw_ref[...])
    _to_token_tiles(hn_ref, hn)
    nl = brt_ref.shape[1]
    hn_hi = hn.astype(BF16)
    hn_lo = (hn - hn_hi.astype(F32)).astype(BF16)
    part = jnp.dot(hn_hi, wrt_ref[...], preferred_element_type=F32)
    logits = (part[:, :nl] + part[:, nl:]
              + jnp.dot(hn_lo, wrt_ref[:, :nl], preferred_element_type=F32) + brt_ref[...])
    tm = logits.shape[0]
    lane = lax.broadcasted_iota(I32, (tm, nl), 1)
    lane_f = lane.astype(F32)
    ninf = -jnp.inf
    big = float(nl)
    is_g = lane < N_GROUPS
    gl = jnp.where(is_g, logits, ninf)
    g_max = gl.max(axis=-1, keepdims=True)
    g_sel = jnp.where(gl == g_max, lane_f, big).min(axis=-1, keepdims=True)
    p_g = 1.0 / jnp.where(is_g, jnp.exp(logits - g_max), 0.0).sum(axis=-1, keepdims=True)
    e_lo = N_GROUPS + EXPERTS_PER_GROUP * g_sel
    in_grp = (lane_f >= e_lo) & (lane_f < e_lo + EXPERTS_PER_GROUP)
    el = jnp.where(in_grp, logits, ninf)
    m1 = el.max(axis=-1, keepdims=True)
    i1 = jnp.where(el == m1, lane_f, big).min(axis=-1, keepdims=True)
    el2 = jnp.where(lane_f == i1, ninf, el)
    m2 = el2.max(axis=-1, keepdims=True)
    i2 = jnp.where(el2 == m2, lane_f, big).min(axis=-1, keepdims=True)
    t2 = jnp.exp(m2 - m1)
    c1 = p_g / (1.0 + t2)
    c2 = p_g * t2 / (1.0 + t2)
    e1 = i1 - N_GROUPS
    e2 = i2 - N_GROUPS
    oh1 = lane_f == e1
    oh2 = lane_f == e2
    ohs = jnp.where(oh1 | oh2, 1.0, 0.0)
    ri = lax.broadcasted_iota(I32, (tm, tm), 0)
    rj = lax.broadcasted_iota(I32, (tm, tm), 1)
    before = jnp.where(rj < ri, 1.0, 0.0).astype(BF16)
    cnt = jnp.dot(before, ohs.astype(BF16), preferred_element_type=F32) + carry_ref[0:1, :]
    rank1 = jnp.where(oh1, cnt, 0.0).sum(axis=-1, keepdims=True)
    rank2 = jnp.where(oh2, cnt, 0.0).sum(axis=-1, keepdims=True)
    new_carry = carry_ref[0:1, :] + ohs.sum(axis=0, keepdims=True)
    carry_ref[0:1, :] = new_carry
    cnt_ref[...] = jnp.broadcast_to(new_carry, cnt_ref.shape)
    route = jnp.zeros((tm, nl), F32)
    for idx, val in ((_R_E1, e1), (_R_E2, e2), (_R_C1, c1), (_R_C2, c2),
                     (_R_RANK1, rank1), (_R_RANK2, rank2)):
        route = jnp.where(lane == idx, val, route)
    route_ref[...] = route
    route_t_ref[...] = route.T[:route_t_ref.shape[0], :]


def _outproj_route(yr, ya, h, wor, woa, nw, wrt, brt):
    t, d = h.shape
    tm = min(TM_ROUTE, t)
    d_r, d_a = yr.shape[1], ya.shape[1]
    nl = brt.shape[1]
    tok = lambda i: (i, 0)
    const = lambda i: (0, 0)
    return pl.pallas_call(
        _outproj_route_kernel,
        out_shape=(jax.ShapeDtypeStruct((t, d), F32),
                   jax.ShapeDtypeStruct((t * V7X_SUBLANES, V7X_LANES), F32),
                   jax.ShapeDtypeStruct((t, nl), F32), jax.ShapeDtypeStruct((V7X_SUBLANES, t), F32),
                   jax.ShapeDtypeStruct((V7X_SUBLANES, nl), F32)),
        grid=(t // tm,),
        in_specs=[pl.BlockSpec((tm, d_r), tok), pl.BlockSpec((tm, d_a), tok), pl.BlockSpec((tm, d), tok),
                  pl.BlockSpec((d_r, d), const), pl.BlockSpec((d_a, d), const),
                  pl.BlockSpec((1, d), const), pl.BlockSpec(wrt.shape, const), pl.BlockSpec((1, nl), const)],
        out_specs=(pl.BlockSpec((tm, d), tok), pl.BlockSpec((tm * V7X_SUBLANES, V7X_LANES), tok),
                   pl.BlockSpec((tm, nl), tok), pl.BlockSpec((V7X_SUBLANES, tm), lambda i: (0, i)),
                   pl.BlockSpec((V7X_SUBLANES, nl), const)),
        scratch_shapes=[pltpu.VMEM((V7X_SUBLANES, nl), F32)],
        compiler_params=_params("arbitrary"),
        name="outproj_route",
    )(yr, ya, h, wor, woa, nw, wrt, brt)


def _load_indices(idx_hbm, i, idx_smem, sem):
    n = idx_smem.shape[0]
    cp = pltpu.make_async_copy(idx_hbm.at[pl.ds(pl.multiple_of(i * n, n), n)], idx_smem, sem)
    cp.start()
    cp.wait()


def _dispatch_kernel(idx_hbm, x_ref, xs_hbm, idx_smem, zero_vmem, idx_sem, row_sem, zero_sem,
                     *, tm, n_free):
    i = pl.program_id(0)
    _load_indices(idx_hbm, i, idx_smem, idx_sem)

    def issue(tt, carry):
        for s in range(2):
            pltpu.make_async_copy(_token_tile(x_ref, tt), _token_tile(xs_hbm, idx_smem[s * tm + tt]),
                                  row_sem).start(priority=s)
        return carry

    lax.fori_loop(0, tm, issue, 0, unroll=8)

    zero_vmem[...] = jnp.zeros_like(zero_vmem)

    def zero_copy(dst_row):
        return pltpu.make_async_copy(zero_vmem, _token_tile(xs_hbm, dst_row), zero_sem)

    def issue_zero(n, carry):
        zero_copy(idx_smem[2 * tm + n]).start()
        return carry

    lax.fori_loop(0, n_free, issue_zero, 0, unroll=8)

    for s in range(2):
        pltpu.make_async_copy(x_ref, xs_hbm.at[pl.ds(0, x_ref.shape[0]), :], row_sem).wait()

    def drain_zero(n, carry):
        zero_copy(0).wait()
        return carry

    lax.fori_loop(0, n_free, drain_zero, 0)


def _dispatch(x, idx, n_rows, tm, n_free):
    t = x.shape[0] // V7X_SUBLANES
    nb = t // tm
    rec = idx.shape[0] // nb
    return pl.pallas_call(
        functools.partial(_dispatch_kernel, tm=tm, n_free=n_free),
        out_shape=jax.ShapeDtypeStruct((n_rows * V7X_SUBLANES, V7X_LANES), x.dtype),
        grid=(nb,),
        in_specs=[pl.BlockSpec(memory_space=pl.ANY),
                  pl.BlockSpec((tm * V7X_SUBLANES, V7X_LANES), lambda i: (i, 0))],
        out_specs=pl.BlockSpec(memory_space=pl.ANY),
        scratch_shapes=[pltpu.SMEM((rec,), I32), pltpu.VMEM((V7X_SUBLANES, V7X_LANES), x.dtype),
                        pltpu.SemaphoreType.DMA, pltpu.SemaphoreType.DMA, pltpu.SemaphoreType.DMA],
        compiler_params=_params("arbitrary"),
        name="dispatch",
    )(idx, x)


def _experts_kernel(te_ref, tv_ref, tf_ref, x_ref, w1_ref, w3_ref, w2_ref, y_ref,
                    w1_b, w3_b, w2_b, *, tm):
    i = pl.program_id(0)

    @pl.when(tf_ref[i] > 0)
    def _():
        w1_b[...] = w1_ref[...].astype(BF16)
        w3_b[...] = w3_ref[...].astype(BF16)
        w2_b[...] = w2_ref[...].astype(BF16)

    @pl.when(tv_ref[i] > 0)
    def _():
        x = _from_token_tiles(x_ref, tm).astype(BF16)
        h_gate = jnp.dot(x, w1_b[...], preferred_element_type=F32)
        h_up = jnp.dot(x, w3_b[...], preferred_element_type=F32)
        hid = (h_gate * jax.nn.sigmoid(h_gate) * h_up).astype(BF16)
        _to_token_tiles(y_ref, jnp.dot(hid, w2_b[...], preferred_element_type=F32))

    @pl.when(tv_ref[i] == 0)
    def _():
        y_ref[...] = jnp.zeros_like(y_ref)


def _experts(xs, w1, w3, w2, tile_expert, tile_valid, tile_first):
    n_rows = xs.shape[0] // V7X_SUBLANES
    tm = TM_EXPERT
    nt = n_rows // tm
    d, f = w1.shape[1:]
    tile_spec = pl.BlockSpec((tm * V7X_SUBLANES, V7X_LANES), lambda i, te, tv, tf: (i, 0))
    grid_spec = pltpu.PrefetchScalarGridSpec(
        num_scalar_prefetch=3,
        grid=(nt,),
        in_specs=[tile_spec,
                  pl.BlockSpec((None, d, f), lambda i, te, tv, tf: (te[i], 0, 0)),
                  pl.BlockSpec((None, d, f), lambda i, te, tv, tf: (te[i], 0, 0)),
                  pl.BlockSpec((None, f, d), lambda i, te, tv, tf: (te[i], 0, 0))],
        out_specs=tile_spec,
        scratch_shapes=[pltpu.VMEM((d, f), BF16), pltpu.VMEM((d, f), BF16), pltpu.VMEM((f, d), BF16)],
    )
    return pl.pallas_call(
        functools.partial(_experts_kernel, tm=tm),
        out_shape=jax.ShapeDtypeStruct(xs.shape, F32),
        grid_spec=grid_spec,
        compiler_params=_params("arbitrary"),
        name="experts",
    )(tile_expert, tile_valid, tile_first, xs, w1, w3, w2)


def _combine_ple_kernel(pos_hbm, ys_hbm, h_ref, route_ref, p_ref, nw_ref, wg_ref, bg_ref, wp_ref,
                        fw_ref, o_ref, idx_smem, ybuf, idx_sem, row_sem, *, tm, final):
    i = pl.program_id(0)
    nb = pl.num_programs(0)
    cur = lax.rem(i, 2)
    nxt = 1 - cur

    def gather(step, buf):
        _load_indices(pos_hbm, step, idx_smem, idx_sem)

        def issue(tt, carry):
            for s in range(2):
                pltpu.make_async_copy(_token_tile(ys_hbm, idx_smem[s * tm + tt]),
                                      _token_tile(ybuf.at[buf, s], tt), row_sem.at[buf]).start(priority=s)
            return carry

        lax.fori_loop(0, tm, issue, 0, unroll=8)

    def wait_gather(buf):
        for s in range(2):
            pltpu.make_async_copy(ys_hbm.at[pl.ds(0, ybuf.shape[2]), :], ybuf.at[buf, s],
                                  row_sem.at[buf]).wait()

    @pl.when(i == 0)
    def _():
        gather(0, 0)

    @pl.when(i + 1 < nb)
    def _():
        gather(i + 1, nxt)

    wait_gather(cur)

    route = route_ref[...]
    c1 = route[:, _R_C1:_R_C1 + 1]
    c2 = route[:, _R_C2:_R_C2 + 1]
    h2 = (h_ref[...] + c1 * _from_token_tiles(ybuf.at[cur, 0], tm)
          + c2 * _from_token_tiles(ybuf.at[cur, 1], tm))
    hn = _rms(h2, nw_ref[...]).astype(BF16)
    gate = jax.nn.sigmoid(jnp.dot(hn, wg_ref[...], preferred_element_type=F32) + bg_ref[...])
    h3 = h2 + gate * jnp.dot(p_ref[...].astype(BF16), wp_ref[...], preferred_element_type=F32)
    if final:
        h3 = _rms(h3, fw_ref[...])
    o_ref[...] = h3


def _combine_ple(pos_tiles, ys, h, route, p, nw, wg, bg, wp, fw, *, tm, final):
    t, d = h.shape
    nl = route.shape[1]
    dp = p.shape[1]
    tok = lambda i: (i, 0)
    const = lambda i: (0, 0)
    return pl.pallas_call(
        functools.partial(_combine_ple_kernel, tm=tm, final=final),
        out_shape=jax.ShapeDtypeStruct((t, d), F32),
        grid=(t // tm,),
        in_specs=[pl.BlockSpec(memory_space=pl.ANY), pl.BlockSpec(memory_space=pl.ANY),
                  pl.BlockSpec((tm, d), tok), pl.BlockSpec((tm, nl), tok), pl.BlockSpec((tm, dp), tok),
                  pl.BlockSpec((1, d), const), pl.BlockSpec((d, d), const), pl.BlockSpec((1, d), const),
                  pl.BlockSpec((dp, d), const), pl.BlockSpec((1, d), const)],
        out_specs=pl.BlockSpec((tm, d), tok),
        scratch_shapes=[pltpu.SMEM((pos_tiles.shape[0] // (t // tm),), I32),
                        pltpu.VMEM((2, 2, tm * V7X_SUBLANES, V7X_LANES), F32),
                        pltpu.SemaphoreType.DMA, pltpu.SemaphoreType.DMA((2,))],
        compiler_params=_params("arbitrary"),
        name="combine_ple_final" if final else "combine_ple",
    )(pos_tiles, ys, h, route, p, nw, wg, bg, wp, fw)


_SMEM_RECORD_WORDS = 1024


def _index_records(pos1, pos2, tm, extra=None):
    nb = pos1.shape[0] // tm
    parts = [pos1.reshape(nb, tm), pos2.reshape(nb, tm)]
    if extra is not None:
        parts.append(extra.reshape(nb, -1))
    rec = jnp.concatenate(parts, axis=1)
    pad = -rec.shape[1] % _SMEM_RECORD_WORDS
    return jnp.pad(rec, ((0, 0), (0, pad))).reshape(-1)


def _lookup(table, idx):
    ids = jnp.arange(table.shape[0], dtype=I32)
    return jnp.sum(jnp.where(idx[None, :] == ids[:, None], table[:, None], 0), axis=0)


def _bucket(ends, x):
    return jnp.minimum(jnp.sum((x[None, :] >= ends[:, None]).astype(I32), axis=0), ends.shape[0] - 1)


def kernel(x, p, norm_mix_w, w_in, rwkv_mu, rwkv_w0, rwkv_w2, rwkv_a0, rwkv_a2, rwkv_g2, rwkv_k_k, rwkv_k_a, rwkv_r_k, rwkv_ln_w, rwkv_ln_b, rwkv_v0, rwkv_v1, rwkv_v2, att_rel_bias, w_out, norm_ffn_w, router_group_w, router_group_b, router_expert_w, router_expert_b, expert_w1, expert_w3, expert_w2, norm_ple_w, ple_gate_w, ple_gate_b, ple_proj_w, final_norm_w):
    batch, seq, d = x.shape
    depth = w_in.shape[0]
    t = batch * seq
    d_r = rwkv_w0.shape[1]
    n_heads_r = d_r // HEAD_DIM
    n_rwkv_in = rwkv_mu.shape[1]
    d_a = (w_in.shape[2] - n_rwkv_in) // 3
    n_heads_a = d_a // HEAD_DIM
    n_dec, n_iclr, n_gate = rwkv_w2.shape[1], rwkv_a2.shape[1], rwkv_g2.shape[1]
    assert n_dec == n_iclr and n_gate == n_dec + n_iclr
    n_lo = n_dec + n_iclr + n_gate
    f_exp = expert_w1.shape[-1]
    assert d == V7X_SUBLANES * V7X_LANES
    n_rows = 2 * t + N_EXPERTS * TM_EXPERT
    n_tiles = n_rows // TM_EXPERT
    qb = min(QB_ATTN, seq)

    w1_all = expert_w1.reshape(depth * N_EXPERTS, d, f_exp)
    w3_all = expert_w3.reshape(depth * N_EXPERTS, d, f_exp)
    w2_all = expert_w2.reshape(depth * N_EXPERTS, f_exp, d)

    h = x.reshape(t, d)
    v_first = None
    for i in range(depth):
        wr = w_in[i, :, :n_rwkv_in].astype(BF16)
        wa = w_in[i, :, n_rwkv_in:].astype(BF16)
        wl = jnp.zeros((n_lo, 3 * d_r), F32)
        wl = wl.at[:n_dec, :d_r].set(rwkv_w2[i])
        wl = wl.at[n_dec:n_dec + n_iclr, d_r:2 * d_r].set(rwkv_a2[i])
        wl = wl.at[n_dec + n_iclr:, 2 * d_r:].set(rwkv_g2[i]).astype(BF16)
        v0 = rwkv_v0[i - 1] if i > 0 else jnp.zeros((d_r,), F32)
        vec = jnp.stack([rwkv_w0[i], rwkv_a0[i], rwkv_k_k[i], rwkv_k_a[i], rwkv_r_k[i],
                         rwkv_ln_w[i], rwkv_ln_b[i], v0])
        if i > 0:
            n_vr = rwkv_v1.shape[2]
            v1 = jnp.zeros((d_r, V7X_LANES), F32).at[:, :n_vr].set(rwkv_v1[i - 1]).astype(BF16)
            v2 = jnp.zeros((V7X_LANES, d_r), F32).at[:n_vr, :].set(rwkv_v2[i - 1]).astype(BF16)
        else:
            v1 = v2 = None
        table = _attn_table(att_rel_bias[i], qb)
        wor = w_out[i, :d_r].astype(BF16)
        woa = w_out[i, d_r:].astype(BF16)
        n_rt = N_GROUPS + N_EXPERTS
        wrt = jnp.zeros((d, V7X_LANES), F32)
        wrt = wrt.at[:, :N_GROUPS].set(router_group_w[i]).at[:, N_GROUPS:n_rt].set(router_expert_w[i])
        wrt_hi = wrt.astype(BF16)
        wrt = jnp.concatenate([wrt_hi, (wrt - wrt_hi.astype(F32)).astype(BF16)], axis=1)
        brt = jnp.zeros((1, V7X_LANES), F32)
        brt = brt.at[0, :N_GROUPS].set(router_group_b[i]).at[0, N_GROUPS:n_rt].set(router_expert_b[i])

        z_r, qkv = _norm_proj(h, norm_mix_w[i][None], wr, wa)
        if i == 0:
            y_r, v_first = _rwkv(z_r, None, rwkv_mu[i][None], vec, wl, None, None,
                                 batch=batch, seq=seq, n_heads=n_heads_r)
        else:
            y_r = _rwkv(z_r, v_first, rwkv_mu[i][None], vec, wl, v1, v2,
                        batch=batch, seq=seq, n_heads=n_heads_r)
        y_a = _attn(qkv, table, batch=batch, seq=seq, n_heads=n_heads_a)

        h1, hn, route, route_t, cnt = _outproj_route(y_r, y_a, h, wor, woa, norm_ffn_w[i][None], wrt, brt)
        ri = route_t.astype(I32)
        counts = cnt[0, :N_EXPERTS].astype(I32)
        padded = ((counts + TM_EXPERT - 1) // TM_EXPERT) * TM_EXPERT
        p_end = jnp.cumsum(padded)
        p_start = p_end - padded
        pos1 = _lookup(p_start, ri[_R_E1]) + ri[_R_RANK1]
        pos2 = _lookup(p_start, ri[_R_E2]) + ri[_R_RANK2]
        tile_start = jnp.arange(n_tiles, dtype=I32) * TM_EXPERT
        tile_expert = _bucket(p_end, tile_start)
        tile_valid = (tile_start < p_end[-1]).astype(I32)

        tile_first = jnp.concatenate([jnp.ones((1,), I32),
                                      (tile_expert[1:] != tile_expert[:-1]).astype(I32)])

        n_free = n_rows - 2 * t
        n_pad = padded - counts
        f_end = jnp.cumsum(n_pad)
        kf = jnp.arange(n_free, dtype=I32)
        ef = _bucket(f_end, kf)
        pad_row = _lookup(p_start + counts - (f_end - n_pad), ef) + kf
        free_rows = jnp.where(kf < f_end[-1], pad_row, p_end[-1] + (kf - f_end[-1])).astype(I32)

        tm_d = min(TM_DISPATCH, t)
        nb_d = t // tm_d
        assert n_free % nb_d == 0
        xs = _dispatch(hn, _index_records(pos1, pos2, tm_d, free_rows), n_rows, tm_d, n_free // nb_d)
        ys = _experts(xs, w1_all, w3_all, w2_all, tile_expert + i * N_EXPERTS, tile_valid, tile_first)

        tm_c = min(TM_COMBINE, t)
        h = _combine_ple(_index_records(pos1, pos2, tm_c), ys, h1, route, p[i].reshape(t, -1),
                         norm_ple_w[i][None], ple_gate_w[i].astype(BF16), ple_gate_b[i][None],
                         ple_proj_w[i].astype(BF16), final_norm_w[None],
                         tm=tm_c, final=(i == depth - 1))
    return h.reshape(batch, seq, d)
```

```python
import functools

import jax
import jax.numpy as jnp
from jax import lax
from jax.experimental import pallas as pl
from jax.experimental.pallas import tpu as pltpu

F32 = jnp.float32
BF16 = jnp.bfloat16
I32 = jnp.int32

CHUNK = 64
HEAD_DIM = 64
LEFT_CHUNKS = 8
MAX_REL = 256
N_GROUPS = 4
EXPERTS_PER_GROUP = 8
N_EXPERTS = N_GROUPS * EXPERTS_PER_GROUP
RMS_EPS = 1e-6
GN_EPS = 64e-5
NEG_INF = -1e30

V7X_LANES = 128
V7X_SUBLANES = 8
V7X_VMEM_LIMIT_BYTES = 48 * 1024 * 1024

TM_PROJ = 512
TB_RWKV = 256
RWKV_CHUNK_GROUP = 2
QB_ATTN = 256
TM_ROUTE = 512
TM_DISPATCH = 512
TM_EXPERT = 256
TM_COMBINE = 512


def _params(*sem):
    return pltpu.CompilerParams(dimension_semantics=sem, vmem_limit_bytes=V7X_VMEM_LIMIT_BYTES)


def _rms(x, w):
    return x * lax.rsqrt(jnp.mean(x * x, axis=-1, keepdims=True) + RMS_EPS) * w


def _mm(a, b):
    return jnp.dot(a.astype(BF16), b.astype(BF16), preferred_element_type=F32)


def _mm_nt(a, b):
    return lax.dot_general(a.astype(BF16), b.astype(BF16), (((1,), (1,)), ((), ())),
                           preferred_element_type=F32)


def _mm_tn(a, b):
    return lax.dot_general(a.astype(BF16), b.astype(BF16), (((0,), (0,)), ((), ())),
                           preferred_element_type=F32)


def _to_token_tiles(ref, x):
    m, d = x.shape
    for s in range(d // V7X_LANES):
        ref[pl.ds(s, m, stride=V7X_SUBLANES), :] = x[:, s * V7X_LANES:(s + 1) * V7X_LANES]


def _from_token_tiles(ref, m):
    return jnp.concatenate([ref[pl.ds(s, m, stride=V7X_SUBLANES), :] for s in range(V7X_SUBLANES)],
                           axis=-1)


def _token_tile(ref, row):
    return ref.at[pl.ds(pl.multiple_of(row * V7X_SUBLANES, V7X_SUBLANES), V7X_SUBLANES), :]


def _split3(x):
    hi = x.astype(BF16)
    r1 = x - hi.astype(F32)
    mid = r1.astype(BF16)
    lo = (r1 - mid.astype(F32)).astype(BF16)
    return hi, mid, lo


def _mm_exact_lhs(a_bf16, x):
    hi, mid, lo = _split3(x)
    return (jnp.dot(a_bf16, hi, preferred_element_type=F32)
            + jnp.dot(a_bf16, mid, preferred_element_type=F32)
            + jnp.dot(a_bf16, lo, preferred_element_type=F32))


def _mm_split2_rhs(x, b_bf16):
    hi = x.astype(BF16)
    lo = (x - hi.astype(F32)).astype(BF16)
    return (jnp.dot(hi, b_bf16, preferred_element_type=F32)
            + jnp.dot(lo, b_bf16, preferred_element_type=F32))


def _norm_proj_kernel(h_ref, nw_ref, wr_ref, wa_ref, zr_ref, qkv_ref):
    hn = _rms(h_ref[...], nw_ref[...]).astype(BF16)
    zr_ref[...] = jnp.dot(hn, wr_ref[...], preferred_element_type=F32)
    qkv_ref[...] = jnp.dot(hn, wa_ref[...], preferred_element_type=F32).astype(BF16)


def _norm_proj(h, nw, wr, wa):
    t, d = h.shape
    tm = min(TM_PROJ, t)
    n_r, n_a = wr.shape[1], wa.shape[1]
    return pl.pallas_call(
        _norm_proj_kernel,
        out_shape=(jax.ShapeDtypeStruct((t, n_r), F32), jax.ShapeDtypeStruct((t, n_a), BF16)),
        grid=(t // tm,),
        in_specs=[pl.BlockSpec((tm, d), lambda i: (i, 0)),
                  pl.BlockSpec((1, d), lambda i: (0, 0)),
                  pl.BlockSpec((d, n_r), lambda i: (0, 0)),
                  pl.BlockSpec((d, n_a), lambda i: (0, 0))],
        out_specs=(pl.BlockSpec((tm, n_r), lambda i: (i, 0)),
                   pl.BlockSpec((tm, n_a), lambda i: (i, 0))),
        compiler_params=_params("parallel"),
        name="norm_proj",
    )(h, nw, wr, wa)


_V_W0, _V_A0, _V_KK, _V_KA, _V_RK, _V_LNW, _V_LNB, _V_V0 = range(8)


def _rwkv_kernel(*refs, has_vres, n_heads, d_r, group, nb):
    if has_vres:
        (z_ref, vf_ref, mu_ref, vec_ref, wl_ref, tril_ref, ones_ref, v1_ref, v2_ref, y_ref,
         s_ref, carry_ref, r_s, k_s, v_s, kk_s, a_s, lc_s, lw_s, bon_s, g_s) = refs
        vf_out_ref = None
    else:
        (z_ref, mu_ref, vec_ref, wl_ref, tril_ref, ones_ref, y_ref, vf_out_ref,
         s_ref, carry_ref, r_s, k_s, v_s, kk_s, a_s, lc_s, lw_s, bon_s, g_s) = refs
    per_token = (r_s, k_s, v_s, kk_s, a_s, lc_s, lw_s, bon_s, g_s)
    tb = z_ref.shape[0]
    gr = group * CHUNK
    k_step = pl.program_id(0)
    cur = lax.rem(k_step, 2)
    prv = 1 - cur

    @pl.when(k_step == 0)
    def _():
        s_ref[...] = jnp.zeros_like(s_ref)
        for ref in per_token:
            ref[...] = jnp.zeros_like(ref)

    @pl.when(lax.rem(k_step, nb) == 0)
    def _():
        carry_ref[...] = jnp.zeros_like(carry_ref)

    vec = vec_ref[...]

    def vrow(i):
        return vec[i:i + 1, :]

    ln_w = vrow(_V_LNW)
    ln_b = vrow(_V_LNB)
    mu = mu_ref[...]
    head_ones = ones_ref[...]
    tril = tril_ref[...]
    ci = lax.broadcasted_iota(I32, (CHUNK, CHUNK), 0)
    cj = lax.broadcasted_iota(I32, (CHUNK, CHUNK), 1)
    strict = cj < ci
    lower = cj <= ci
    eye = ci == cj
    eye_f = jnp.where(eye, 1.0, 0.0)
    hs = [slice(h * HEAD_DIM, (h + 1) * HEAD_DIM) for h in range(n_heads)]
    prev_first = lax.rem(k_step + nb - 1, nb) == 0

    def token_work(rows):
        z = z_ref[rows, :]
        row = lax.broadcasted_iota(I32, z.shape, 0)
        z_prev = jnp.where(row == 0, carry_ref[0:1, :], pltpu.roll(z, 1, axis=0))
        carry_ref[0:1, :] = z[gr - 1:gr, :]
        zs = z + (z_prev - z) * mu
        r = zs[:, 0:d_r]
        k = zs[:, d_r:2 * d_r]
        v = zs[:, 2 * d_r:3 * d_r]
        lo = zs[:, 3 * d_r:]
        n_lo = lo.shape[1]
        lane = lax.broadcasted_iota(I32, lo.shape, 1)
        lo_act = jnp.where(lane < n_lo // 4, jnp.tanh(lo),
                           jnp.where(lane < n_lo // 2, lo, jax.nn.sigmoid(lo)))
        lo_out = _mm(lo_act, wl_ref[...])
        if has_vres:
            vv = _mm(_mm(v, v1_ref[...]), v2_ref[...])
            v = v + (vf_ref[rows, :] - v) * jax.nn.sigmoid(vrow(_V_V0) + vv)
        w_log = -jax.nn.softplus(-(vrow(_V_W0) + lo_out[:, 0:d_r])) - 0.5
        lw = -jnp.exp(w_log)
        a = jax.nn.sigmoid(vrow(_V_A0) + lo_out[:, d_r:2 * d_r])
        kk = k * vrow(_V_KK)
        kk = kk * lax.rsqrt(jnp.maximum(_mm_split2_rhs(kk * kk, head_ones), 1e-24))
        k2 = k * (1.0 + (a - 1.0) * vrow(_V_KA))
        r_s[cur, rows, :] = r
        k_s[cur, rows, :] = k2
        v_s[cur, rows, :] = v
        kk_s[cur, rows, :] = kk
        a_s[cur, rows, :] = a
        lc_s[cur, rows, :] = _mm_exact_lhs(tril, lw)
        lw_s[cur, rows, :] = lw
        bon_s[cur, rows, :] = _mm_split2_rhs(r * k2 * vrow(_V_RK), head_ones) * v
        g_s[cur, rows, :] = lo_out[:, 2 * d_r:3 * d_r]

    def chunk_operands(r0):
        rs = pl.ds(r0, CHUNK)
        lc_c = lc_s[prv, rs, :]
        lw_c = lw_s[prv, rs, :]
        l_end = lc_s[prv, pl.ds(r0 + CHUNK - 1, 1), :]
        p_in = jnp.exp(lc_c)
        p_prev = jnp.exp(lc_c - lw_c)
        p_inv = jnp.exp(-lc_c)
        p_end = jnp.exp(l_end - lc_c)
        p_last = jnp.exp(l_end)
        kk_c = kk_s[prv, rs, :]
        b_c = kk_c * a_s[prv, rs, :]
        k_c = k_s[prv, rs, :]
        at = (-kk_c * p_prev).astype(BF16)
        bt = (b_c * p_inv).astype(BF16)
        bh = (b_c * p_end).astype(BF16)
        kt = (k_c * p_inv).astype(BF16)
        kh = (k_c * p_end).astype(BF16)
        rt = (r_s[prv, rs, :] * p_in).astype(BF16)
        vc = v_s[prv, rs, :].astype(BF16)
        per_head = [[x[:, sl] for sl in hs] for x in (at, bt, bh, kt, kh, rt, vc)]
        per_head.append([p_last[:, sl] for sl in hs])
        return per_head

    def group_body(gi, carry):
        g0 = pl.multiple_of(gi * gr, gr)
        rows = pl.ds(g0, gr)
        if vf_out_ref is not None:
            vf_out_ref[rows, :] = v_s[prv, rows, :]
        ops = [chunk_operands(g0 + c * CHUNK) for c in range(group)]
        at_h, bt_h, bh_h, kt_h, kh_h, rt_h, v_h, pl_h = ([x for c in range(group) for x in ops[c][q]]
                                                         for q in range(8))
        heads = range(group * n_heads)
        ar_h = [jnp.concatenate([at_h[h], rt_h[h]], axis=0) for h in heads]
        m_b = [_mm_nt(ar_h[h], bt_h[h]) for h in heads]
        m_k = [_mm_nt(ar_h[h], kt_h[h]) for h in heads]
        n_ab = [jnp.where(strict, m_b[h][:CHUNK], 0.0) for h in heads]
        a_ak = [jnp.where(strict, m_k[h][:CHUNK], 0.0) for h in heads]
        a_rb = [jnp.where(lower, m_b[h][CHUNK:], 0.0) for h in heads]
        a_rk = [jnp.where(lower, m_k[h][CHUNK:], 0.0) for h in heads]
        x_inv = [eye_f + n_ab[h] for h in heads]
        pw = [_mm(n_ab[h], n_ab[h]) for h in heads]
        akv = [_mm(a_ak[h], v_h[h]) for h in heads]
        n_sq = CHUNK.bit_length() - 2
        for it in range(n_sq):
            if it < n_sq - 1:
                st = [_mm(jnp.concatenate([x_inv[h], pw[h]], axis=0), pw[h]) for h in heads]
                x_inv = [x_inv[h] + st[h][:CHUNK] for h in heads]
                pw = [st[h][CHUNK:] for h in heads]
            else:
                st = [_mm(x_inv[h], pw[h]) for h in heads]
                x_inv = [x_inv[h] + st[h] for h in heads]
        w_h = [_mm(x_inv[h], at_h[h]) for h in heads]
        u0 = [_mm(x_inv[h], akv[h]) for h in heads]
        y0 = [_mm(a_rk[h], v_h[h]) + _mm(a_rb[h], u0[h]) for h in heads]
        r_p = [rt_h[h].astype(F32) + _mm(a_rb[h], w_h[h]) for h in heads]
        g_h = [jnp.where(eye, pl_h[h], 0.0) + _mm_tn(w_h[h], bh_h[h]) for h in heads]
        d_h = [_mm_tn(u0[h], bh_h[h]) + _mm_tn(v_h[h], kh_h[h]) for h in heads]
        fresh = prev_first & (gi == 0)
        s_h = [jnp.where(fresh, 0.0, s_ref[h]) for h in range(n_heads)]
        for c in range(group):
            rs = pl.ds(g0 + c * CHUNK, CHUNK)
            idx = [c * n_heads + h for h in range(n_heads)]
            y_h = [y0[i] + _mm_nt(r_p[i], s_h[h]) for h, i in enumerate(idx)]
            s_h = [_mm(s_h[h], g_h[i]) + d_h[i] for h, i in enumerate(idx)]
            y_heads = []
            for h in range(n_heads):
                mean = jnp.mean(y_h[h], axis=-1, keepdims=True)
                yc = y_h[h] - mean
                var = jnp.mean(yc * yc, axis=-1, keepdims=True)
                y_heads.append(yc * lax.rsqrt(var + GN_EPS))
            y_n = jnp.concatenate(y_heads, axis=-1)
            out = (y_n * ln_w + ln_b + bon_s[prv, rs, :]) * g_s[prv, rs, :]
            y_ref[rs, :] = out.astype(y_ref.dtype)
        for h in range(n_heads):
            s_ref[h] = s_h[h]
        token_work(rows)
        return carry

    lax.fori_loop(0, tb // gr, group_body, 0)


def _rwkv(z, v_first, mu, vec, wl, v1, v2, *, batch, seq, n_heads):
    t, n_z = z.shape
    d_r = n_heads * HEAD_DIM
    tb = min(TB_RWKV, seq)
    nb = seq // tb
    n_blocks = batch * nb
    has_vres = v_first is not None
    tok_in = lambda k: (jnp.minimum(k, n_blocks - 1), 0)
    tok_out = lambda k: (jnp.maximum(k - 1, 0), 0)
    const = lambda k: (0, 0)
    in_specs = [pl.BlockSpec((tb, n_z), tok_in)]
    args = [z]
    if has_vres:
        in_specs.append(pl.BlockSpec((tb, d_r), tok_in))
        args.append(v_first)
    n_chunks = tb // CHUNK
    group = RWKV_CHUNK_GROUP if n_chunks % RWKV_CHUNK_GROUP == 0 else 1
    ti = jnp.arange(group * CHUNK)
    tril = ((ti[:, None] // CHUNK == ti[None, :] // CHUNK) & (ti[None, :] <= ti[:, None])).astype(BF16)
    hi = jnp.arange(d_r) // HEAD_DIM
    head_ones = (hi[:, None] == hi[None, :]).astype(BF16)
    in_specs += [pl.BlockSpec(mu.shape, const), pl.BlockSpec(vec.shape, const),
                 pl.BlockSpec(wl.shape, const), pl.BlockSpec(tril.shape, const),
                 pl.BlockSpec(head_ones.shape, const)]
    args += [mu, vec, wl, tril, head_ones]
    if has_vres:
        in_specs += [pl.BlockSpec(v1.shape, const), pl.BlockSpec(v2.shape, const)]
        args += [v1, v2]
        out_shape = jax.ShapeDtypeStruct((t, d_r), BF16)
        out_specs = pl.BlockSpec((tb, d_r), tok_out)
    else:
        out_shape = (jax.ShapeDtypeStruct((t, d_r), BF16), jax.ShapeDtypeStruct((t, d_r), F32))
        out_specs = (pl.BlockSpec((tb, d_r), tok_out), pl.BlockSpec((tb, d_r), tok_out))
    scratch = [pltpu.VMEM((n_heads, HEAD_DIM, HEAD_DIM), F32),
               pltpu.VMEM((V7X_SUBLANES, n_z), F32)]
    scratch += [pltpu.VMEM((2, tb, d_r), F32) for _ in range(9)]
    return pl.pallas_call(
        functools.partial(_rwkv_kernel, has_vres=has_vres, n_heads=n_heads, d_r=d_r, group=group, nb=nb),
        out_shape=out_shape,
        grid=(n_blocks + 1,),
        in_specs=in_specs,
        out_specs=out_specs,
        scratch_shapes=scratch,
        compiler_params=_params("arbitrary"),
        name="rwkv_vres" if has_vres else "rwkv",
    )(*args)


def _attn_kernel(*refs, n_heads, n_parts):
    q_ref = refs[0]
    k_refs = refs[1:1 + n_parts]
    v_refs = refs[1 + n_parts:1 + 2 * n_parts]
    tab_ref = refs[1 + 2 * n_parts]
    o_ref = refs[2 + 2 * n_parts]
    qb = q_ref.shape[0]
    j = pl.program_id(1)
    scale = HEAD_DIM ** -0.5
    q = q_ref[...] * jnp.asarray(scale, q_ref.dtype)
    ks = [r[...] for r in k_refs]
    vs = [r[...] for r in v_refs]
    pw = 2 * HEAD_DIM
    lane = lax.broadcasted_iota(I32, (qb, pw), 1)
    sum_even = jnp.where(lane < HEAD_DIM, 1.0, 0.0).astype(q.dtype)
    sum_odd = jnp.where(lane < HEAD_DIM, 0.0, 1.0).astype(q.dtype)
    zero = jnp.zeros((), q.dtype)
    is_even = sum_even > zero
    outs = []
    for hp in range(n_heads // 2):
        sl = slice(hp * pw, (hp + 1) * pw)
        qq = q[:, sl]
        q_pair = (jnp.where(is_even, qq, zero), jnp.where(is_even, zero, qq))
        s_parts = [[], []]
        for p in range(n_parts):
            kk = ks[p][:, sl]
            back = n_parts - 1 - p
            for u in range(2):
                s = _mm_nt(q_pair[u], kk) + tab_ref[2 * hp + u, :, p * qb:(p + 1) * qb]
                if back > 0:
                    s = jnp.where(j >= back, s, NEG_INF)
                s_parts[u].append(s)
        m = []
        for u in range(2):
            mm = s_parts[u][0]
            for s in s_parts[u][1:]:
                mm = jnp.maximum(mm, s)
            m.append(mm.max(axis=-1, keepdims=True))
        acc = jnp.zeros((qb, 2 * pw), F32)
        for p in range(n_parts):
            vv = vs[p][:, sl]
            rhs = jnp.concatenate(
                [jnp.concatenate([jnp.where(is_even, vv, zero), sum_even], axis=1),
                 jnp.concatenate([jnp.where(is_even, zero, vv), sum_odd], axis=1)], axis=0)
            e = jnp.concatenate([jnp.exp(s_parts[u][p] - m[u]).astype(BF16) for u in range(2)], axis=1)
            acc = acc + jnp.dot(e, rhs, preferred_element_type=F32)
        outs.append(acc[:, :pw] / acc[:, pw:])
    o_ref[...] = jnp.concatenate(outs, axis=-1).astype(o_ref.dtype)


def _attn(qkv, table, *, batch, seq, n_heads):
    t = qkv.shape[0]
    d_a = n_heads * HEAD_DIM
    qb = min(QB_ATTN, seq)
    left = LEFT_CHUNKS * CHUNK
    assert left % qb == 0 and seq % qb == 0
    n_parts = left // qb + 1
    nb = seq // qb
    in_specs = [pl.BlockSpec((qb, d_a), lambda b, j: (b * nb + j, 0))]
    for p in range(n_parts):
        back = n_parts - 1 - p
        in_specs.append(pl.BlockSpec((qb, d_a), lambda b, j, back=back: (b * nb + jnp.maximum(j - back, 0), 1)))
    for p in range(n_parts):
        back = n_parts - 1 - p
        in_specs.append(pl.BlockSpec((qb, d_a), lambda b, j, back=back: (b * nb + jnp.maximum(j - back, 0), 2)))
    in_specs.append(pl.BlockSpec(table.shape, lambda b, j: (0, 0, 0)))
    return pl.pallas_call(
        functools.partial(_attn_kernel, n_heads=n_heads, n_parts=n_parts),
        out_shape=jax.ShapeDtypeStruct((t, d_a), BF16),
        grid=(batch, nb),
        in_specs=in_specs,
        out_specs=pl.BlockSpec((qb, d_a), lambda b, j: (b * nb + j, 0)),
        compiler_params=_params("parallel", "arbitrary"),
        name="attn",
    )(*([qkv] * (1 + 2 * n_parts)), table)


def _attn_table(rel_bias, qb):
    left = LEFT_CHUNKS * CHUNK
    n_keys = left + qb
    period = qb + n_keys - 1
    n_heads = rel_bias.shape[0]
    m = jnp.arange(period)
    rel = left - jnp.where(m < n_keys, m, m - period)
    g = rel_bias[:, jnp.clip(rel, -(CHUNK - 1), MAX_REL) + (CHUNK - 1)].astype(F32)
    flat = jnp.tile(g, (1, qb))[:, :qb * (period - 1)]
    bias = flat.reshape(n_heads, qb, period - 1)[:, :, :n_keys]
    cq = jnp.arange(qb)[:, None] // CHUNK
    ck = jnp.arange(n_keys)[None, :] // CHUNK
    valid = (ck >= cq) & (ck <= cq + LEFT_CHUNKS)
    return jnp.where(valid[None], bias, NEG_INF)


_R_E1, _R_E2, _R_C1, _R_C2, _R_RANK1, _R_RANK2 = range(6)


def _outproj_route_kernel(yr_ref, ya_ref, h_ref, wor_ref, woa_ref, nw_ref, wrt_ref, brt_ref,
                          h1_ref, hn_ref, route_ref, route_t_ref, cnt_ref, carry_ref):
    i = pl.program_id(0)

    @pl.when(i == 0)
    def _():
        carry_ref[...] = jnp.zeros_like(carry_ref)

    h1 = (h_ref[...] + jnp.dot(yr_ref[...], wor_ref[...], preferred_element_type=F32)
          + jnp.dot(ya_ref[...], woa_ref[...], preferred_element_type=F32))
    h1_ref[...] = h1
    hn = _rms(h1, nw_ref[...])
    _to_token_tiles(hn_ref, hn)
    nl = brt_ref.shape[1]
    hn_hi = hn.astype(BF16)
    hn_lo = (hn - hn_hi.astype(F32)).astype(BF16)
    part = jnp.dot(hn_hi, wrt_ref[...], preferred_element_type=F32)
    logits = (part[:, :nl] + part[:, nl:]
              + jnp.dot(hn_lo, wrt_ref[:, :nl], preferred_element_type=F32) + brt_ref[...])
    tm = logits.shape[0]
    lane = lax.broadcasted_iota(I32, (tm, nl), 1)
    lane_f = lane.astype(F32)
    ninf = -jnp.inf
    big = float(nl)
    is_g = lane < N_GROUPS
    gl = jnp.where(is_g, logits, ninf)
    g_max = gl.max(axis=-1, keepdims=True)
    g_sel = jnp.where(gl == g_max, lane_f, big).min(axis=-1, keepdims=True)
    p_g = 1.0 / jnp.where(is_g, jnp.exp(logits - g_max), 0.0).sum(axis=-1, keepdims=True)
    e_lo = N_GROUPS + EXPERTS_PER_GROUP * g_sel
    in_grp = (lane_f >= e_lo) & (lane_f < e_lo + EXPERTS_PER_GROUP)
    el = jnp.where(in_grp, logits, ninf)
    m1 = el.max(axis=-1, keepdims=True)
    i1 = jnp.where(el == m1, lane_f, big).min(axis=-1, keepdims=True)
    el2 = jnp.where(lane_f == i1, ninf, el)
    m2 = el2.max(axis=-1, keepdims=True)
    i2 = jnp.where(el2 == m2, lane_f, big).min(axis=-1, keepdims=True)
    t2 = jnp.exp(m2 - m1)
    c1 = p_g / (1.0 + t2)
    c2 = p_g * t2 / (1.0 + t2)
    e1 = i1 - N_GROUPS
    e2 = i2 - N_GROUPS
    oh1 = lane_f == e1
    oh2 = lane_f == e2
    ohs = jnp.where(oh1 | oh2, 1.0, 0.0)
    ri = lax.broadcasted_iota(I32, (tm, tm), 0)
    rj = lax.broadcasted_iota(I32, (tm, tm), 1)
    before = jnp.where(rj < ri, 1.0, 0.0).astype(BF16)
    cnt = jnp.dot(before, ohs.astype(BF16), preferred_element_type=F32) + carry_ref[0:1, :]
    rank1 = jnp.where(oh1, cnt, 0.0).sum(axis=-1, keepdims=True)
    rank2 = jnp.where(oh2, cnt, 0.0).sum(axis=-1, keepdims=True)
    new_carry = carry_ref[0:1, :] + ohs.sum(axis=0, keepdims=True)
    carry_ref[0:1, :] = new_carry
    cnt_ref[...] = jnp.broadcast_to(new_carry, cnt_ref.shape)
    route = jnp.zeros((tm, nl), F32)
    for idx, val in ((_R_E1, e1), (_R_E2, e2), (_R_C1, c1), (_R_C2, c2),
                     (_R_RANK1, rank1), (_R_RANK2, rank2)):
        route = jnp.where(lane == idx, val, route)
    route_ref[...] = route
    route_t_ref[...] = route.T[:route_t_ref.shape[0], :]


def _outproj_route(yr, ya, h, wor, woa, nw, wrt, brt):
    t, d = h.shape
    tm = min(TM_ROUTE, t)
    d_r, d_a = yr.shape[1], ya.shape[1]
    nl = brt.shape[1]
    tok = lambda i: (i, 0)
    const = lambda i: (0, 0)
    return pl.pallas_call(
        _outproj_route_kernel,
        out_shape=(jax.ShapeDtypeStruct((t, d), F32),
                   jax.ShapeDtypeStruct((t * V7X_SUBLANES, V7X_LANES), F32),
                   jax.ShapeDtypeStruct((t, nl), F32), jax.ShapeDtypeStruct((V7X_SUBLANES, t), F32),
                   jax.ShapeDtypeStruct((V7X_SUBLANES, nl), F32)),
        grid=(t // tm,),
        in_specs=[pl.BlockSpec((tm, d_r), tok), pl.BlockSpec((tm, d_a), tok), pl.BlockSpec((tm, d), tok),
                  pl.BlockSpec((d_r, d), const), pl.BlockSpec((d_a, d), const),
                  pl.BlockSpec((1, d), const), pl.BlockSpec(wrt.shape, const), pl.BlockSpec((1, nl), const)],
        out_specs=(pl.BlockSpec((tm, d), tok), pl.BlockSpec((tm * V7X_SUBLANES, V7X_LANES), tok),
                   pl.BlockSpec((tm, nl), tok), pl.BlockSpec((V7X_SUBLANES, tm), lambda i: (0, i)),
                   pl.BlockSpec((V7X_SUBLANES, nl), const)),
        scratch_shapes=[pltpu.VMEM((V7X_SUBLANES, nl), F32)],
        compiler_params=_params("arbitrary"),
        name="outproj_route",
    )(yr, ya, h, wor, woa, nw, wrt, brt)


def _load_indices(idx_hbm, i, idx_smem, sem):
    n = idx_smem.shape[0]
    cp = pltpu.make_async_copy(idx_hbm.at[pl.ds(pl.multiple_of(i * n, n), n)], idx_smem, sem)
    cp.start()
    cp.wait()


def _dispatch_kernel(idx_hbm, x_ref, xs_hbm, idx_smem, zero_vmem, idx_sem, row_sem, zero_sem,
                     *, tm, n_free):
    i = pl.program_id(0)
    _load_indices(idx_hbm, i, idx_smem, idx_sem)

    def issue(tt, carry):
        for s in range(2):
            pltpu.make_async_copy(_token_tile(x_ref, tt), _token_tile(xs_hbm, idx_smem[s * tm + tt]),
                                  row_sem).start(priority=s)
        return carry

    lax.fori_loop(0, tm, issue, 0, unroll=8)

    zero_vmem[...] = jnp.zeros_like(zero_vmem)

    def zero_copy(dst_row):
        return pltpu.make_async_copy(zero_vmem, _token_tile(xs_hbm, dst_row), zero_sem)

    def issue_zero(n, carry):
        zero_copy(idx_smem[2 * tm + n]).start()
        return carry

    lax.fori_loop(0, n_free, issue_zero, 0, unroll=8)

    for s in range(2):
        pltpu.make_async_copy(x_ref, xs_hbm.at[pl.ds(0, x_ref.shape[0]), :], row_sem).wait()

    def drain_zero(n, carry):
        zero_copy(0).wait()
        return carry

    lax.fori_loop(0, n_free, drain_zero, 0)


def _dispatch(x, idx, n_rows, tm, n_free):
    t = x.shape[0] // V7X_SUBLANES
    nb = t // tm
    rec = idx.shape[0] // nb
    return pl.pallas_call(
        functools.partial(_dispatch_kernel, tm=tm, n_free=n_free),
        out_shape=jax.ShapeDtypeStruct((n_rows * V7X_SUBLANES, V7X_LANES), x.dtype),
        grid=(nb,),
        in_specs=[pl.BlockSpec(memory_space=pl.ANY),
                  pl.BlockSpec((tm * V7X_SUBLANES, V7X_LANES), lambda i: (i, 0))],
        out_specs=pl.BlockSpec(memory_space=pl.ANY),
        scratch_shapes=[pltpu.SMEM((rec,), I32), pltpu.VMEM((V7X_SUBLANES, V7X_LANES), x.dtype),
                        pltpu.SemaphoreType.DMA, pltpu.SemaphoreType.DMA, pltpu.SemaphoreType.DMA],
        compiler_params=_params("arbitrary"),
        name="dispatch",
    )(idx, x)


def _experts_kernel(te_ref, tv_ref, tf_ref, x_ref, w1_ref, w3_ref, w2_ref, y_ref,
                    w1_b, w3_b, w2_b, *, tm):
    i = pl.program_id(0)

    @pl.when(tf_ref[i] > 0)
    def _():
        w1_b[...] = w1_ref[...].astype(BF16)
        w3_b[...] = w3_ref[...].astype(BF16)
        w2_b[...] = w2_ref[...].astype(BF16)

    @pl.when(tv_ref[i] > 0)
    def _():
        x = _from_token_tiles(x_ref, tm).astype(BF16)
        h_gate = jnp.dot(x, w1_b[...], preferred_element_type=F32)
        h_up = jnp.dot(x, w3_b[...], preferred_element_type=F32)
        hid = (h_gate * jax.nn.sigmoid(h_gate) * h_up).astype(BF16)
        _to_token_tiles(y_ref, jnp.dot(hid, w2_b[...], preferred_element_type=F32))

    @pl.when(tv_ref[i] == 0)
    def _():
        y_ref[...] = jnp.zeros_like(y_ref)


def _experts(xs, w1, w3, w2, tile_expert, tile_valid, tile_first):
    n_rows = xs.shape[0] // V7X_SUBLANES
    tm = TM_EXPERT
    nt = n_rows // tm
    d, f = w1.shape[1:]
    tile_spec = pl.BlockSpec((tm * V7X_SUBLANES, V7X_LANES), lambda i, te, tv, tf: (i, 0))
    grid_spec = pltpu.PrefetchScalarGridSpec(
        num_scalar_prefetch=3,
        grid=(nt,),
        in_specs=[tile_spec,
                  pl.BlockSpec((None, d, f), lambda i, te, tv, tf: (te[i], 0, 0)),
                  pl.BlockSpec((None, d, f), lambda i, te, tv, tf: (te[i], 0, 0)),
                  pl.BlockSpec((None, f, d), lambda i, te, tv, tf: (te[i], 0, 0))],
        out_specs=tile_spec,
        scratch_shapes=[pltpu.VMEM((d, f), BF16), pltpu.VMEM((d, f), BF16), pltpu.VMEM((f, d), BF16)],
    )
    return pl.pallas_call(
        functools.partial(_experts_kernel, tm=tm),
        out_shape=jax.ShapeDtypeStruct(xs.shape, F32),
        grid_spec=grid_spec,
        compiler_params=_params("arbitrary"),
        name="experts",
    )(tile_expert, tile_valid, tile_first, xs, w1, w3, w2)


def _combine_ple_kernel(pos_hbm, ys_hbm, h_ref, route_ref, p_ref, nw_ref, wg_ref, bg_ref, wp_ref,
                        fw_ref, o_ref, idx_smem, ybuf, idx_sem, row_sem, *, tm, final):
    i = pl.program_id(0)
    nb = pl.num_programs(0)
    cur = lax.rem(i, 2)
    nxt = 1 - cur

    def gather(step, buf):
        _load_indices(pos_hbm, step, idx_smem, idx_sem)

        def issue(tt, carry):
            for s in range(2):
                pltpu.make_async_copy(_token_tile(ys_hbm, idx_smem[s * tm + tt]),
                                      _token_tile(ybuf.at[buf, s], tt), row_sem.at[buf]).start(priority=s)
            return carry

        lax.fori_loop(0, tm, issue, 0, unroll=8)

    def wait_gather(buf):
        for s in range(2):
            pltpu.make_async_copy(ys_hbm.at[pl.ds(0, ybuf.shape[2]), :], ybuf.at[buf, s],
                                  row_sem.at[buf]).wait()

    @pl.when(i == 0)
    def _():
        gather(0, 0)

    @pl.when(i + 1 < nb)
    def _():
        gather(i + 1, nxt)

    wait_gather(cur)

    route = route_ref[...]
    c1 = route[:, _R_C1:_R_C1 + 1]
    c2 = route[:, _R_C2:_R_C2 + 1]
    h2 = (h_ref[...] + c1 * _from_token_tiles(ybuf.at[cur, 0], tm)
          + c2 * _from_token_tiles(ybuf.at[cur, 1], tm))
    hn = _rms(h2, nw_ref[...]).astype(BF16)
    gate = jax.nn.sigmoid(jnp.dot(hn, wg_ref[...], preferred_element_type=F32) + bg_ref[...])
    h3 = h2 + gate * jnp.dot(p_ref[...].astype(BF16), wp_ref[...], preferred_element_type=F32)
    if final:
        h3 = _rms(h3, fw_ref[...])
    o_ref[...] = h3


def _combine_ple(pos_tiles, ys, h, route, p, nw, wg, bg, wp, fw, *, tm, final):
    t, d = h.shape
    nl = route.shape[1]
    dp = p.shape[1]
    tok = lambda i: (i, 0)
    const = lambda i: (0, 0)
    return pl.pallas_call(
        functools.partial(_combine_ple_kernel, tm=tm, final=final),
        out_shape=jax.ShapeDtypeStruct((t, d), F32),
        grid=(t // tm,),
        in_specs=[pl.BlockSpec(memory_space=pl.ANY), pl.BlockSpec(memory_space=pl.ANY),
                  pl.BlockSpec((tm, d), tok), pl.BlockSpec((tm, nl), tok), pl.BlockSpec((tm, dp), tok),
                  pl.BlockSpec((1, d), const), pl.BlockSpec((d, d), const), pl.BlockSpec((1, d), const),
                  pl.BlockSpec((dp, d), const), pl.BlockSpec((1, d), const)],
        out_specs=pl.BlockSpec((tm, d), tok),
        scratch_shapes=[pltpu.SMEM((pos_tiles.shape[0] // (t // tm),), I32),
                        pltpu.VMEM((2, 2, tm * V7X_SUBLANES, V7X_LANES), F32),
                        pltpu.SemaphoreType.DMA, pltpu.SemaphoreType.DMA((2,))],
        compiler_params=_params("arbitrary"),
        name="combine_ple_final" if final else "combine_ple",
    )(pos_tiles, ys, h, route, p, nw, wg, bg, wp, fw)


_SMEM_RECORD_WORDS = 1024


def _index_records(pos1, pos2, tm, extra=None):
    nb = pos1.shape[0] // tm
    parts = [pos1.reshape(nb, tm), pos2.reshape(nb, tm)]
    if extra is not None:
        parts.append(extra.reshape(nb, -1))
    rec = jnp.concatenate(parts, axis=1)
    pad = -rec.shape[1] % _SMEM_RECORD_WORDS
    return jnp.pad(rec, ((0, 0), (0, pad))).reshape(-1)


def _lookup(table, idx):
    ids = jnp.arange(table.shape[0], dtype=I32)
    return jnp.sum(jnp.where(idx[None, :] == ids[:, None], table[:, None], 0), axis=0)


def _bucket(ends, x):
    return jnp.minimum(jnp.sum((x[None, :] >= ends[:, None]).astype(I32), axis=0), ends.shape[0] - 1)


def kernel(x, p, norm_mix_w, w_in, rwkv_mu, rwkv_w0, rwkv_w2, rwkv_a0, rwkv_a2, rwkv_g2, rwkv_k_k, rwkv_k_a, rwkv_r_k, rwkv_ln_w, rwkv_ln_b, rwkv_v0, rwkv_v1, rwkv_v2, att_rel_bias, w_out, norm_ffn_w, router_group_w, router_group_b, router_expert_w, router_expert_b, expert_w1, expert_w3, expert_w2, norm_ple_w, ple_gate_w, ple_gate_b, ple_proj_w, final_norm_w):
    batch, seq, d = x.shape
    depth = w_in.shape[0]
    t = batch * seq
    d_r = rwkv_w0.shape[1]
    n_heads_r = d_r // HEAD_DIM
    n_rwkv_in = rwkv_mu.shape[1]
    d_a = (w_in.shape[2] - n_rwkv_in) // 3
    n_heads_a = d_a // HEAD_DIM
    n_dec, n_iclr, n_gate = rwkv_w2.shape[1], rwkv_a2.shape[1], rwkv_g2.shape[1]
    assert n_dec == n_iclr and n_gate == n_dec + n_iclr
    n_lo = n_dec + n_iclr + n_gate
    f_exp = expert_w1.shape[-1]
    assert d == V7X_SUBLANES * V7X_LANES
    n_rows = 2 * t + N_EXPERTS * TM_EXPERT
    n_tiles = n_rows // TM_EXPERT
    qb = min(QB_ATTN, seq)

    w1_all = expert_w1.reshape(depth * N_EXPERTS, d, f_exp)
    w3_all = expert_w3.reshape(depth * N_EXPERTS, d, f_exp)
    w2_all = expert_w2.reshape(depth * N_EXPERTS, f_exp, d)

    h = x.reshape(t, d)
    v_first = None
    for i in range(depth):
        wr = w_in[i, :, :n_rwkv_in].astype(BF16)
        wa = w_in[i, :, n_rwkv_in:].astype(BF16)
        wl = jnp.zeros((n_lo, 3 * d_r), F32)
        wl = wl.at[:n_dec, :d_r].set(rwkv_w2[i])
        wl = wl.at[n_dec:n_dec + n_iclr, d_r:2 * d_r].set(rwkv_a2[i])
        wl = wl.at[n_dec + n_iclr:, 2 * d_r:].set(rwkv_g2[i]).astype(BF16)
        v0 = rwkv_v0[i - 1] if i > 0 else jnp.zeros((d_r,), F32)
        vec = jnp.stack([rwkv_w0[i], rwkv_a0[i], rwkv_k_k[i], rwkv_k_a[i], rwkv_r_k[i],
                         rwkv_ln_w[i], rwkv_ln_b[i], v0])
        if i > 0:
            n_vr = rwkv_v1.shape[2]
            v1 = jnp.zeros((d_r, V7X_LANES), F32).at[:, :n_vr].set(rwkv_v1[i - 1]).astype(BF16)
            v2 = jnp.zeros((V7X_LANES, d_r), F32).at[:n_vr, :].set(rwkv_v2[i - 1]).astype(BF16)
        else:
            v1 = v2 = None
        table = _attn_table(att_rel_bias[i], qb)
        wor = w_out[i, :d_r].astype(BF16)
        woa = w_out[i, d_r:].astype(BF16)
        n_rt = N_GROUPS + N_EXPERTS
        wrt = jnp.zeros((d, V7X_LANES), F32)
        wrt = wrt.at[:, :N_GROUPS].set(router_group_w[i]).at[:, N_GROUPS:n_rt].set(router_expert_w[i])
        wrt_hi = wrt.astype(BF16)
        wrt = jnp.concatenate([wrt_hi, (wrt - wrt_hi.astype(F32)).astype(BF16)], axis=1)
        brt = jnp.zeros((1, V7X_LANES), F32)
        brt = brt.at[0, :N_GROUPS].set(router_group_b[i]).at[0, N_GROUPS:n_rt].set(router_expert_b[i])

        z_r, qkv = _norm_proj(h, norm_mix_w[i][None], wr, wa)
        if i == 0:
            y_r, v_first = _rwkv(z_r, None, rwkv_mu[i][None], vec, wl, None, None,
                                 batch=batch, seq=seq, n_heads=n_heads_r)
        else:
            y_r = _rwkv(z_r, v_first, rwkv_mu[i][None], vec, wl, v1, v2,
                        batch=batch, seq=seq, n_heads=n_heads_r)
        y_a = _attn(qkv, table, batch=batch, seq=seq, n_heads=n_heads_a)

        h1, hn, route, route_t, cnt = _outproj_route(y_r, y_a, h, wor, woa, norm_ffn_w[i][None], wrt, brt)
        ri = route_t.astype(I32)
        counts = cnt[0, :N_EXPERTS].astype(I32)
        padded = ((counts + TM_EXPERT - 1) // TM_EXPERT) * TM_EXPERT
        p_end = jnp.cumsum(padded)
        p_start = p_end - padded
        pos1 = _lookup(p_start, ri[_R_E1]) + ri[_R_RANK1]
        pos2 = _lookup(p_start, ri[_R_E2]) + ri[_R_RANK2]
        tile_start = jnp.arange(n_tiles, dtype=I32) * TM_EXPERT
        tile_expert = _bucket(p_end, tile_start)
        tile_valid = (tile_start < p_end[-1]).astype(I32)

        tile_first = jnp.concatenate([jnp.ones((1,), I32),
                                      (tile_expert[1:] != tile_expert[:-1]).astype(I32)])

        n_free = n_rows - 2 * t
        n_pad = padded - counts
        f_end = jnp.cumsum(n_pad)
        kf = jnp.arange(n_free, dtype=I32)
        ef = _bucket(f_end, kf)
        pad_row = _lookup(p_start + counts - (f_end - n_pad), ef) + kf
        free_rows = jnp.where(kf < f_end[-1], pad_row, p_end[-1] + (kf - f_end[-1])).astype(I32)

        tm_d = min(TM_DISPATCH, t)
        nb_d = t // tm_d
        assert n_free % nb_d == 0
        xs = _dispatch(hn, _index_records(pos1, pos2, tm_d, free_rows), n_rows, tm_d, n_free // nb_d)
        ys = _experts(xs, w1_all, w3_all, w2_all, tile_expert + i * N_EXPERTS, tile_valid, tile_first)

        tm_c = min(TM_COMBINE, t)
        h = _combine_ple(_index_records(pos1, pos2, tm_c), ys, h1, route, p[i].reshape(t, -1),
                         norm_ple_w[i][None], ple_gate_w[i].astype(BF16), ple_gate_b[i][None],
                         ple_proj_w[i].astype(BF16), final_norm_w[None],
                         tm=tm_c, final=(i == depth - 1))
    return h.reshape(batch, seq, d)
```

```python
import functools

import jax
import jax.numpy as jnp
from jax import lax
from jax.experimental import pallas as pl
from jax.experimental.pallas import tpu as pltpu

F32 = jnp.float32
BF16 = jnp.bfloat16
I32 = jnp.int32

CHUNK = 64
HEAD_DIM = 64
LEFT_CHUNKS = 8
MAX_REL = 256
N_GROUPS = 4
EXPERTS_PER_GROUP = 8
N_EXPERTS = N_GROUPS * EXPERTS_PER_GROUP
RMS_EPS = 1e-6
GN_EPS = 64e-5
NEG_INF = -1e30

V7X_LANES = 128
V7X_SUBLANES = 8
V7X_VMEM_LIMIT_BYTES = 48 * 1024 * 1024

TM_PROJ = 512
TB_RWKV = 256
RWKV_CHUNK_GROUP = 2
QB_ATTN = 256
TM_ROUTE = 512
TM_DISPATCH = 512
TM_EXPERT = 256
TM_COMBINE = 512


def _params(*sem):
    return pltpu.CompilerParams(dimension_semantics=sem, vmem_limit_bytes=V7X_VMEM_LIMIT_BYTES)


def _rms(x, w):
    return x * lax.rsqrt(jnp.mean(x * x, axis=-1, keepdims=True) + RMS_EPS) * w


def _mm(a, b):
    return jnp.dot(a.astype(BF16), b.astype(BF16), preferred_element_type=F32)


def _mm_nt(a, b):
    return lax.dot_general(a.astype(BF16), b.astype(BF16), (((1,), (1,)), ((), ())),
                           preferred_element_type=F32)


def _mm_tn(a, b):
    return lax.dot_general(a.astype(BF16), b.astype(BF16), (((0,), (0,)), ((), ())),
                           preferred_element_type=F32)


def _to_token_tiles(ref, x):
    m, d = x.shape
    for s in range(d // V7X_LANES):
        ref[pl.ds(s, m, stride=V7X_SUBLANES), :] = x[:, s * V7X_LANES:(s + 1) * V7X_LANES]


def _from_token_tiles(ref, m):
    return jnp.concatenate([ref[pl.ds(s, m, stride=V7X_SUBLANES), :] for s in range(V7X_SUBLANES)],
                           axis=-1)


def _token_tile(ref, row):
    return ref.at[pl.ds(pl.multiple_of(row * V7X_SUBLANES, V7X_SUBLANES), V7X_SUBLANES), :]


def _split3(x):
    hi = x.astype(BF16)
    r1 = x - hi.astype(F32)
    mid = r1.astype(BF16)
    lo = (r1 - mid.astype(F32)).astype(BF16)
    return hi, mid, lo


def _mm_exact_lhs(a_bf16, x):
    hi, mid, lo = _split3(x)
    return (jnp.dot(a_bf16, hi, preferred_element_type=F32)
            + jnp.dot(a_bf16, mid, preferred_element_type=F32)
            + jnp.dot(a_bf16, lo, preferred_element_type=F32))


def _mm_split2_rhs(x, b_bf16):
    hi = x.astype(BF16)
    lo = (x - hi.astype(F32)).astype(BF16)
    return (jnp.dot(hi, b_bf16, preferred_element_type=F32)
            + jnp.dot(lo, b_bf16, preferred_element_type=F32))


def _norm_proj_kernel(h_ref, nw_ref, wr_ref, wa_ref, zr_ref, qkv_ref):
    hn = _rms(h_ref[...], nw_ref[...]).astype(BF16)
    zr_ref[...] = jnp.dot(hn, wr_ref[...], preferred_element_type=F32)
    qkv_ref[...] = jnp.dot(hn, wa_ref[...], preferred_element_type=F32).astype(BF16)


def _norm_proj(h, nw, wr, wa):
    t, d = h.shape
    tm = min(TM_PROJ, t)
    n_r, n_a = wr.shape[1], wa.shape[1]
    return pl.pallas_call(
        _norm_proj_kernel,
        out_shape=(jax.ShapeDtypeStruct((t, n_r), F32), jax.ShapeDtypeStruct((t, n_a), BF16)),
        grid=(t // tm,),
        in_specs=[pl.BlockSpec((tm, d), lambda i: (i, 0)),
                  pl.BlockSpec((1, d), lambda i: (0, 0)),
                  pl.BlockSpec((d, n_r), lambda i: (0, 0)),
                  pl.BlockSpec((d, n_a), lambda i: (0, 0))],
        out_specs=(pl.BlockSpec((tm, n_r), lambda i: (i, 0)),
                   pl.BlockSpec((tm, n_a), lambda i: (i, 0))),
        compiler_params=_params("parallel"),
        name="norm_proj",
    )(h, nw, wr, wa)


_V_W0, _V_A0, _V_KK, _V_KA, _V_RK, _V_LNW, _V_LNB, _V_V0 = range(8)


def _rwkv_kernel(*refs, has_vres, n_heads, d_r, group, nb):
    if has_vres:
        (z_ref, vf_ref, mu_ref, vec_ref, wl_ref, tril_ref, ones_ref, v1_ref, v2_ref, y_ref,
         s_ref, carry_ref, r_s, k_s, v_s, kk_s, a_s, lc_s, lw_s, bon_s, g_s) = refs
        vf_out_ref = None
    else:
        (z_ref, mu_ref, vec_ref, wl_ref, tril_ref, ones_ref, y_ref, vf_out_ref,
         s_ref, carry_ref, r_s, k_s, v_s, kk_s, a_s, lc_s, lw_s, bon_s, g_s) = refs
    per_token = (r_s, k_s, v_s, kk_s, a_s, lc_s, lw_s, bon_s, g_s)
    tb = z_ref.shape[0]
    gr = group * CHUNK
    k_step = pl.program_id(0)
    cur = lax.rem(k_step, 2)
    prv = 1 - cur

    @pl.when(k_step == 0)
    def _():
        s_ref[...] = jnp.zeros_like(s_ref)
        for ref in per_token:
            ref[...] = jnp.zeros_like(ref)

    @pl.when(lax.rem(k_step, nb) == 0)
    def _():
        carry_ref[...] = jnp.zeros_like(carry_ref)

    vec = vec_ref[...]

    def vrow(i):
        return vec[i:i + 1, :]

    ln_w = vrow(_V_LNW)
    ln_b = vrow(_V_LNB)
    mu = mu_ref[...]
    head_ones = ones_ref[...]
    tril = tril_ref[...]
    ci = lax.broadcasted_iota(I32, (CHUNK, CHUNK), 0)
    cj = lax.broadcasted_iota(I32, (CHUNK, CHUNK), 1)
    strict = cj < ci
    lower = cj <= ci
    eye = ci == cj
    eye_f = jnp.where(eye, 1.0, 0.0)
    hs = [slice(h * HEAD_DIM, (h + 1) * HEAD_DIM) for h in range(n_heads)]
    prev_first = lax.rem(k_step + nb - 1, nb) == 0

    def token_work(rows):
        z = z_ref[rows, :]
        row = lax.broadcasted_iota(I32, z.shape, 0)
        z_prev = jnp.where(row == 0, carry_ref[0:1, :], pltpu.roll(z, 1, axis=0))
        carry_ref[0:1, :] = z[gr - 1:gr, :]
        zs = z + (z_prev - z) * mu
        r = zs[:, 0:d_r]
        k = zs[:, d_r:2 * d_r]
        v = zs[:, 2 * d_r:3 * d_r]
        lo = zs[:, 3 * d_r:]
        n_lo = lo.shape[1]
        lane = lax.broadcasted_iota(I32, lo.shape, 1)
        lo_act = jnp.where(lane < n_lo // 4, jnp.tanh(lo),
                           jnp.where(lane < n_lo // 2, lo, jax.nn.sigmoid(lo)))
        lo_out = _mm(lo_act, wl_ref[...])
        if has_vres:
            vv = _mm(_mm(v, v1_ref[...]), v2_ref[...])
            v = v + (vf_ref[rows, :] - v) * jax.nn.sigmoid(vrow(_V_V0) + vv)
        w_log = -jax.nn.softplus(-(vrow(_V_W0) + lo_out[:, 0:d_r])) - 0.5
        lw = -jnp.exp(w_log)
        a = jax.nn.sigmoid(vrow(_V_A0) + lo_out[:, d_r:2 * d_r])
        kk = k * vrow(_V_KK)
        kk = kk * lax.rsqrt(jnp.maximum(_mm_split2_rhs(kk * kk, head_ones), 1e-24))
        k2 = k * (1.0 + (a - 1.0) * vrow(_V_KA))
        r_s[cur, rows, :] = r
        k_s[cur, rows, :] = k2
        v_s[cur, rows, :] = v
        kk_s[cur, rows, :] = kk
        a_s[cur, rows, :] = a
        lc_s[cur, rows, :] = _mm_exact_lhs(tril, lw)
        lw_s[cur, rows, :] = lw
        bon_s[cur, rows, :] = _mm_split2_rhs(r * k2 * vrow(_V_RK), head_ones) * v
        g_s[cur, rows, :] = lo_out[:, 2 * d_r:3 * d_r]

    def chunk_operands(r0):
        rs = pl.ds(r0, CHUNK)
        lc_c = lc_s[prv, rs, :]
        lw_c = lw_s[prv, rs, :]
        l_end = lc_s[prv, pl.ds(r0 + CHUNK - 1, 1), :]
        p_in = jnp.exp(lc_c)
        p_prev = jnp.exp(lc_c - lw_c)
        p_inv = jnp.exp(-lc_c)
        p_end = jnp.exp(l_end - lc_c)
        p_last = jnp.exp(l_end)
        kk_c = kk_s[prv, rs, :]
        b_c = kk_c * a_s[prv, rs, :]
        k_c = k_s[prv, rs, :]
        at = (-kk_c * p_prev).astype(BF16)
        bt = (b_c * p_inv).astype(BF16)
        bh = (b_c * p_end).astype(BF16)
        kt = (k_c * p_inv).astype(BF16)
        kh = (k_c * p_end).astype(BF16)
        rt = (r_s[prv, rs, :] * p_in).astype(BF16)
        vc = v_s[prv, rs, :].astype(BF16)
        per_head = [[x[:, sl] for sl in hs] for x in (at, bt, bh, kt, kh, rt, vc)]
        per_head.append([p_last[:, sl] for sl in hs])
        return per_head

    def group_body(gi, carry):
        g0 = pl.multiple_of(gi * gr, gr)
        rows = pl.ds(g0, gr)
        if vf_out_ref is not None:
            vf_out_ref[rows, :] = v_s[prv, rows, :]
        ops = [chunk_operands(g0 + c * CHUNK) for c in range(group)]
        at_h, bt_h, bh_h, kt_h, kh_h, rt_h, v_h, pl_h = ([x for c in range(group) for x in ops[c][q]]
                                                         for q in range(8))
        heads = range(group * n_heads)
        ar_h = [jnp.concatenate([at_h[h], rt_h[h]], axis=0) for h in heads]
        m_b = [_mm_nt(ar_h[h], bt_h[h]) for h in heads]
        m_k = [_mm_nt(ar_h[h], kt_h[h]) for h in heads]
        n_ab = [jnp.where(strict, m_b[h][:CHUNK], 0.0) for h in heads]
        a_ak = [jnp.where(strict, m_k[h][:CHUNK], 0.0) for h in heads]
        a_rb = [jnp.where(lower, m_b[h][CHUNK:], 0.0) for h in heads]
        a_rk = [jnp.where(lower, m_k[h][CHUNK:], 0.0) for h in heads]
        x_inv = [eye_f + n_ab[h] for h in heads]
        pw = [_mm(n_ab[h], n_ab[h]) for h in heads]
        akv = [_mm(a_ak[h], v_h[h]) for h in heads]
        n_sq = CHUNK.bit_length() - 2
        for it in range(n_sq):
            if it < n_sq - 1:
                st = [_mm(jnp.concatenate([x_inv[h], pw[h]], axis=0), pw[h]) for h in heads]
                x_inv = [x_inv[h] + st[h][:CHUNK] for h in heads]
                pw = [st[h][CHUNK:] for h in heads]
            else:
                st = [_mm(x_inv[h], pw[h]) for h in heads]
                x_inv = [x_inv[h] + st[h] for h in heads]
        w_h = [_mm(x_inv[h], at_h[h]) for h in heads]
        u0 = [_mm(x_inv[h], akv[h]) for h in heads]
        y0 = [_mm(a_rk[h], v_h[h]) + _mm(a_rb[h], u0[h]) for h in heads]
        r_p = [rt_h[h].astype(F32) + _mm(a_rb[h], w_h[h]) for h in heads]
        g_h = [jnp.where(eye, pl_h[h], 0.0) + _mm_tn(w_h[h], bh_h[h]) for h in heads]
        d_h = [_mm_tn(u0[h], bh_h[h]) + _mm_tn(v_h[h], kh_h[h]) for h in heads]
        fresh = prev_first & (gi == 0)
        s_h = [jnp.where(fresh, 0.0, s_ref[h]) for h in range(n_heads)]
        for c in range(group):
            rs = pl.ds(g0 + c * CHUNK, CHUNK)
            idx = [c * n_heads + h for h in range(n_heads)]
            y_h = [y0[i] + _mm_nt(r_p[i], s_h[h]) for h, i in enumerate(idx)]
            s_h = [_mm(s_h[h], g_h[i]) + d_h[i] for h, i in enumerate(idx)]
            y_heads = []
            for h in range(n_heads):
                mean = jnp.mean(y_h[h], axis=-1, keepdims=True)
                yc = y_h[h] - mean
                var = jnp.mean(yc * yc, axis=-1, keepdims=True)
                y_heads.append(yc * lax.rsqrt(var + GN_EPS))
            y_n = jnp.concatenate(y_heads, axis=-1)
            out = (y_n * ln_w + ln_b + bon_s[prv, rs, :]) * g_s[prv, rs, :]
            y_ref[rs, :] = out.astype(y_ref.dtype)
        for h in range(n_heads):
            s_ref[h] = s_h[h]
        token_work(rows)
        return carry

    lax.fori_loop(0, tb // gr, group_body, 0)


def _rwkv(z, v_first, mu, vec, wl, v1, v2, *, batch, seq, n_heads):
    t, n_z = z.shape
    d_r = n_heads * HEAD_DIM
    tb = min(TB_RWKV, seq)
    nb = seq // tb
    n_blocks = batch * nb
    has_vres = v_first is not None
    tok_in = lambda k: (jnp.minimum(k, n_blocks - 1), 0)
    tok_out = lambda k: (jnp.maximum(k - 1, 0), 0)
    const = lambda k: (0, 0)
    in_specs = [pl.BlockSpec((tb, n_z), tok_in)]
    args = [z]
    if has_vres:
        in_specs.append(pl.BlockSpec((tb, d_r), tok_in))
        args.append(v_first)
    n_chunks = tb // CHUNK
    group = RWKV_CHUNK_GROUP if n_chunks % RWKV_CHUNK_GROUP == 0 else 1
    ti = jnp.arange(group * CHUNK)
    tril = ((ti[:, None] // CHUNK == ti[None, :] // CHUNK) & (ti[None, :] <= ti[:, None])).astype(BF16)
    hi = jnp.arange(d_r) // HEAD_DIM
    head_ones = (hi[:, None] == hi[None, :]).astype(BF16)
    in_specs += [pl.BlockSpec(mu.shape, const), pl.BlockSpec(vec.shape, const),
                 pl.BlockSpec(wl.shape, const), pl.BlockSpec(tril.shape, const),
                 pl.BlockSpec(head_ones.shape, const)]
    args += [mu, vec, wl, tril, head_ones]
    if has_vres:
        in_specs += [pl.BlockSpec(v1.shape, const), pl.BlockSpec(v2.shape, const)]
        args += [v1, v2]
        out_shape = jax.ShapeDtypeStruct((t, d_r), BF16)
        out_specs = pl.BlockSpec((tb, d_r), tok_out)
    else:
        out_shape = (jax.ShapeDtypeStruct((t, d_r), BF16), jax.ShapeDtypeStruct((t, d_r), F32))
        out_specs = (pl.BlockSpec((tb, d_r), tok_out), pl.BlockSpec((tb, d_r), tok_out))
    scratch = [pltpu.VMEM((n_heads, HEAD_DIM, HEAD_DIM), F32),
               pltpu.VMEM((V7X_SUBLANES, n_z), F32)]
    scratch += [pltpu.VMEM((2, tb, d_r), F32) for _ in range(9)]
    return pl.pallas_call(
        functools.partial(_rwkv_kernel, has_vres=has_vres, n_heads=n_heads, d_r=d_r, group=group, nb=nb),
        out_shape=out_shape,
        grid=(n_blocks + 1,),
        in_specs=in_specs,
        out_specs=out_specs,
        scratch_shapes=scratch,
        compiler_params=_params("arbitrary"),
        name="rwkv_vres" if has_vres else "rwkv",
    )(*args)


def _attn_kernel(*refs, n_heads, n_parts):
    q_ref = refs[0]
    k_refs = refs[1:1 + n_parts]
    v_refs = refs[1 + n_parts:1 + 2 * n_parts]
    tab_ref = refs[1 + 2 * n_parts]
    o_ref = refs[2 + 2 * n_parts]
    qb = q_ref.shape[0]
    j = pl.program_id(1)
    scale = HEAD_DIM ** -0.5
    q = q_ref[...] * jnp.asarray(scale, q_ref.dtype)
    ks = [r[...] for r in k_refs]
    vs = [r[...] for r in v_refs]
    pw = 2 * HEAD_DIM
    lane = lax.broadcasted_iota(I32, (qb, pw), 1)
    sum_even = jnp.where(lane < HEAD_DIM, 1.0, 0.0).astype(q.dtype)
    sum_odd = jnp.where(lane < HEAD_DIM, 0.0, 1.0).astype(q.dtype)
    zero = jnp.zeros((), q.dtype)
    is_even = sum_even > zero
    outs = []
    for hp in range(n_heads // 2):
        sl = slice(hp * pw, (hp + 1) * pw)
        qq = q[:, sl]
        q_pair = (jnp.where(is_even, qq, zero), jnp.where(is_even, zero, qq))
        s_parts = [[], []]
        for p in range(n_parts):
            kk = ks[p][:, sl]
            back = n_parts - 1 - p
            for u in range(2):
                s = _mm_nt(q_pair[u], kk) + tab_ref[2 * hp + u, :, p * qb:(p + 1) * qb]
                if back > 0:
                    s = jnp.where(j >= back, s, NEG_INF)
                s_parts[u].append(s)
        m = []
        for u in range(2):
            mm = s_parts[u][0]
            for s in s_parts[u][1:]:
                mm = jnp.maximum(mm, s)
            m.append(mm.max(axis=-1, keepdims=True))
        acc = jnp.zeros((qb, 2 * pw), F32)
        for p in range(n_parts):
            vv = vs[p][:, sl]
            rhs = jnp.concatenate(
                [jnp.concatenate([jnp.where(is_even, vv, zero), sum_even], axis=1),
                 jnp.concatenate([jnp.where(is_even, zero, vv), sum_odd], axis=1)], axis=0)
            e = jnp.concatenate([jnp.exp(s_parts[u][p] - m[u]).astype(BF16) for u in range(2)], axis=1)
            acc = acc + jnp.dot(e, rhs, preferred_element_type=F32)
        outs.append(acc[:, :pw] / acc[:, pw:])
    o_ref[...] = jnp.concatenate(outs, axis=-1).astype(o_ref.dtype)


def _attn(qkv, table, *, batch, seq, n_heads):
    t = qkv.shape[0]
    d_a = n_heads * HEAD_DIM
    qb = min(QB_ATTN, seq)
    left = LEFT_CHUNKS * CHUNK
    assert left % qb == 0 and seq % qb == 0
    n_parts = left // qb + 1
    nb = seq // qb
    in_specs = [pl.BlockSpec((qb, d_a), lambda b, j: (b * nb + j, 0))]
    for p in range(n_parts):
        back = n_parts - 1 - p
        in_specs.append(pl.BlockSpec((qb, d_a), lambda b, j, back=back: (b * nb + jnp.maximum(j - back, 0), 1)))
    for p in range(n_parts):
        back = n_parts - 1 - p
        in_specs.append(pl.BlockSpec((qb, d_a), lambda b, j, back=back: (b * nb + jnp.maximum(j - back, 0), 2)))
    in_specs.append(pl.BlockSpec(table.shape, lambda b, j: (0, 0, 0)))
    return pl.pallas_call(
        functools.partial(_attn_kernel, n_heads=n_heads, n_parts=n_parts),
        out_shape=jax.ShapeDtypeStruct((t, d_a), BF16),
        grid=(batch, nb),
        in_specs=in_specs,
        out_specs=pl.BlockSpec((qb, d_a), lambda b, j: (b * nb + j, 0)),
        compiler_params=_params("parallel", "arbitrary"),
        name="attn",
    )(*([qkv] * (1 + 2 * n_parts)), table)


def _attn_table(rel_bias, qb):
    left = LEFT_CHUNKS * CHUNK
    n_keys = left + qb
    period = qb + n_keys - 1
    n_heads = rel_bias.shape[0]
    m = jnp.arange(period)
    rel = left - jnp.where(m < n_keys, m, m - period)
    g = rel_bias[:, jnp.clip(rel, -(CHUNK - 1), MAX_REL) + (CHUNK - 1)].astype(F32)
    flat = jnp.tile(g, (1, qb))[:, :qb * (period - 1)]
    bias = flat.reshape(n_heads, qb, period - 1)[:, :, :n_keys]
    cq = jnp.arange(qb)[:, None] // CHUNK
    ck = jnp.arange(n_keys)[None, :] // CHUNK
    valid = (ck >= cq) & (ck <= cq + LEFT_CHUNKS)
    return jnp.where(valid[None], bias, NEG_INF)


_R_E1, _R_E2, _R_C1, _R_C2, _R_RANK1, _R_RANK2, _R_LRANK1, _R_LRANK2 = range(8)


def _outproj_route_kernel(yr_ref, ya_ref, h_ref, wor_ref, woa_ref, nw_ref, wrt_ref, brt_ref,
                          h1_ref, hn_ref, route_ref, route_t_ref, cnt_ref, tstat_ref, carry_ref):
    i = pl.program_id(0)

    @pl.when(i == 0)
    def _():
        carry_ref[...] = jnp.zeros_like(carry_ref)

    h1 = (h_ref[...] + jnp.dot(yr_ref[...], wor_ref[...], preferred_element_type=F32)
          + jnp.dot(ya_ref[...], woa_ref[...], preferred_element_type=F32))
    h1_ref[...] = h1
    hn = _rms(h1, nw_ref[...])
    _to_token_tiles(hn_ref, hn)
    nl = brt_ref.shape[1]
    hn_hi = hn.astype(BF16)
    hn_lo = (hn - hn_hi.astype(F32)).astype(BF16)
    part = jnp.dot(hn_hi, wrt_ref[...], preferred_element_type=F32)
    logits = (part[:, :nl] + part[:, nl:]
              + jnp.dot(hn_lo, wrt_ref[:, :nl], preferred_element_type=F32) + brt_ref[...])
    tm = logits.shape[0]
    lane = lax.broadcasted_iota(I32, (tm, nl), 1)
    lane_f = lane.astype(F32)
    ninf = -jnp.inf
    big = float(nl)
    is_g = lane < N_GROUPS
    gl = jnp.where(is_g, logits, ninf)
    g_max = gl.max(axis=-1, keepdims=True)
    g_sel = jnp.where(gl == g_max, lane_f, big).min(axis=-1, keepdims=True)
    p_g = 1.0 / jnp.where(is_g, jnp.exp(logits - g_max), 0.0).sum(axis=-1, keepdims=True)
    e_lo = N_GROUPS + EXPERTS_PER_GROUP * g_sel
    in_grp = (lane_f >= e_lo) & (lane_f < e_lo + EXPERTS_PER_GROUP)
    el = jnp.where(in_grp, logits, ninf)
    m1 = el.max(axis=-1, keepdims=True)
    i1 = jnp.where(el == m1, lane_f, big).min(axis=-1, keepdims=True)
    el2 = jnp.where(lane_f == i1, ninf, el)
    m2 = el2.max(axis=-1, keepdims=True)
    i2 = jnp.where(el2 == m2, lane_f, big).min(axis=-1, keepdims=True)
    t2 = jnp.exp(m2 - m1)
    c1 = p_g / (1.0 + t2)
    c2 = p_g * t2 / (1.0 + t2)
    e1 = i1 - N_GROUPS
    e2 = i2 - N_GROUPS
    oh1 = lane_f == e1
    oh2 = lane_f == e2
    ohs = jnp.where(oh1 | oh2, 1.0, 0.0)
    ri = lax.broadcasted_iota(I32, (tm, tm), 0)
    rj = lax.broadcasted_iota(I32, (tm, tm), 1)
    before = jnp.where(rj < ri, 1.0, 0.0).astype(BF16)
    old_carry = carry_ref[0:1, :]
    cnt_tile = jnp.dot(before, ohs.astype(BF16), preferred_element_type=F32)
    cnt = cnt_tile + old_carry
    rank1 = jnp.where(oh1, cnt, 0.0).sum(axis=-1, keepdims=True)
    rank2 = jnp.where(oh2, cnt, 0.0).sum(axis=-1, keepdims=True)
    lrank1 = jnp.where(oh1, cnt_tile, 0.0).sum(axis=-1, keepdims=True)
    lrank2 = jnp.where(oh2, cnt_tile, 0.0).sum(axis=-1, keepdims=True)
    tile_cnt = ohs.sum(axis=0, keepdims=True)
    new_carry = old_carry + tile_cnt
    carry_ref[0:1, :] = new_carry
    cnt_ref[...] = jnp.broadcast_to(new_carry, cnt_ref.shape)
    srow = lax.broadcasted_iota(I32, tstat_ref.shape, 0)
    tstat_ref[...] = jnp.where(srow == 0, tile_cnt, jnp.where(srow == 1, old_carry, 0.0))
    route = jnp.zeros((tm, nl), F32)
    for idx, val in ((_R_E1, e1), (_R_E2, e2), (_R_C1, c1), (_R_C2, c2),
                     (_R_RANK1, rank1), (_R_RANK2, rank2), (_R_LRANK1, lrank1), (_R_LRANK2, lrank2)):
        route = jnp.where(lane == idx, val, route)
    route_ref[...] = route
    route_t_ref[...] = route.T[:route_t_ref.shape[0], :]


def _outproj_route(yr, ya, h, wor, woa, nw, wrt, brt):
    t, d = h.shape
    tm = min(TM_ROUTE, t)
    d_r, d_a = yr.shape[1], ya.shape[1]
    nl = brt.shape[1]
    tok = lambda i: (i, 0)
    const = lambda i: (0, 0)
    return pl.pallas_call(
        _outproj_route_kernel,
        out_shape=(jax.ShapeDtypeStruct((t, d), F32),
                   jax.ShapeDtypeStruct((t * V7X_SUBLANES, V7X_LANES), F32),
                   jax.ShapeDtypeStruct((t, nl), F32), jax.ShapeDtypeStruct((V7X_SUBLANES, t), F32),
                   jax.ShapeDtypeStruct((V7X_SUBLANES, nl), F32),
                   jax.ShapeDtypeStruct((t // tm * V7X_SUBLANES, nl), F32)),
        grid=(t // tm,),
        in_specs=[pl.BlockSpec((tm, d_r), tok), pl.BlockSpec((tm, d_a), tok), pl.BlockSpec((tm, d), tok),
                  pl.BlockSpec((d_r, d), const), pl.BlockSpec((d_a, d), const),
                  pl.BlockSpec((1, d), const), pl.BlockSpec(wrt.shape, const), pl.BlockSpec((1, nl), const)],
        out_specs=(pl.BlockSpec((tm, d), tok), pl.BlockSpec((tm * V7X_SUBLANES, V7X_LANES), tok),
                   pl.BlockSpec((tm, nl), tok), pl.BlockSpec((V7X_SUBLANES, tm), lambda i: (0, i)),
                   pl.BlockSpec((V7X_SUBLANES, nl), const), pl.BlockSpec((V7X_SUBLANES, nl), tok)),
        scratch_shapes=[pltpu.VMEM((V7X_SUBLANES, nl), F32)],
        compiler_params=_params("arbitrary"),
        name="outproj_route",
    )(yr, ya, h, wor, woa, nw, wrt, brt)


def _load_indices(idx_hbm, i, idx_smem, sem):
    n = idx_smem.shape[0]
    cp = pltpu.make_async_copy(idx_hbm.at[pl.ds(pl.multiple_of(i * n, n), n)], idx_smem, sem)
    cp.start()
    cp.wait()


def _tiles(ref, row, n):
    start = row * V7X_SUBLANES
    if not isinstance(row, int):
        start = pl.multiple_of(start, V7X_SUBLANES)
    return ref.at[pl.ds(start, n * V7X_SUBLANES), :]


def _start_run_copies(length, n_bits, copy_of):
    for b in range(n_bits):
        @pl.when(((length >> b) & 1) == 1)
        def _(b=b):
            done = (length >> (b + 1)) << (b + 1)
            copy_of(done, 1 << b, b).start(priority=b % 2)


def _dispatch_kernel(idx_hbm, zinfo_hbm, x_ref, xs_hbm, idx_smem, zinfo_smem, xloc, zero_vmem,
                     idx_sem, run_sem, zero_sem, *, tm, n_free):
    i = pl.program_id(0)
    nb = pl.num_programs(0)
    buf = lax.rem(i, 2)
    zero_rows = zero_vmem.shape[0] // V7X_SUBLANES

    @pl.when(i == 0)
    def _():
        cp = pltpu.make_async_copy(zinfo_hbm, zinfo_smem, idx_sem)
        cp.start()
        cp.wait()
        zero_vmem[...] = jnp.zeros_like(zero_vmem)

        def pad_runs(e, carry):
            start = zinfo_smem[e]
            _start_run_copies(
                zinfo_smem[N_EXPERTS + e], zero_rows.bit_length(),
                lambda done, n, b: pltpu.make_async_copy(_tiles(zero_vmem, 0, n),
                                                         _tiles(xs_hbm, start + done, n), zero_sem))
            return carry

        lax.fori_loop(0, N_EXPERTS, pad_runs, 0)
        tail_start = zinfo_smem[2 * N_EXPERTS]

        def tail_block(n, carry):
            pltpu.make_async_copy(zero_vmem, _tiles(xs_hbm, tail_start + n * zero_rows, zero_rows),
                                  zero_sem).start()
            return carry

        lax.fori_loop(0, zinfo_smem[2 * N_EXPERTS + 1], tail_block, 0)

    _load_indices(idx_hbm, i, idx_smem, idx_sem)

    def wait_runs(b):
        pltpu.make_async_copy(xloc.at[b], xs_hbm.at[pl.ds(0, xloc.shape[1]), :], run_sem.at[b]).wait()

    @pl.when(i >= 2)
    def _():
        wait_runs(buf)

    def place(tt, carry):
        row = x_ref[pl.ds(pl.multiple_of(tt * V7X_SUBLANES, V7X_SUBLANES), V7X_SUBLANES), :]
        for s in range(2):
            lp = idx_smem[s * tm + tt]
            xloc[buf, pl.ds(pl.multiple_of(lp * V7X_SUBLANES, V7X_SUBLANES), V7X_SUBLANES), :] = row
        return carry

    lax.fori_loop(0, tm, place, 0, unroll=8)

    def expert_run(e, carry):
        dst = idx_smem[2 * tm + N_EXPERTS + e]
        off = idx_smem[2 * tm + 2 * N_EXPERTS + e]
        _start_run_copies(
            idx_smem[2 * tm + e], tm.bit_length(),
            lambda done, n, b: pltpu.make_async_copy(_tiles(xloc.at[buf], off + done, n),
                                                     _tiles(xs_hbm, dst + done, n), run_sem.at[buf]))
        return carry

    lax.fori_loop(0, N_EXPERTS, expert_run, 0)

    @pl.when(i == nb - 1)
    def _():
        wait_runs(buf)

        @pl.when(nb >= 2)
        def _():
            wait_runs(1 - buf)

        pltpu.make_async_copy(xs_hbm.at[pl.ds(0, n_free * V7X_SUBLANES), :],
                              xs_hbm.at[pl.ds(0, n_free * V7X_SUBLANES), :], zero_sem).wait()


def _dispatch(x, idx, zinfo, n_rows, tm, n_free):
    t = x.shape[0] // V7X_SUBLANES
    nb = t // tm
    rec = idx.shape[0] // nb
    return pl.pallas_call(
        functools.partial(_dispatch_kernel, tm=tm, n_free=n_free),
        out_shape=jax.ShapeDtypeStruct((n_rows * V7X_SUBLANES, V7X_LANES), x.dtype),
        grid=(nb,),
        in_specs=[pl.BlockSpec(memory_space=pl.ANY), pl.BlockSpec(memory_space=pl.ANY),
                  pl.BlockSpec((tm * V7X_SUBLANES, V7X_LANES), lambda i: (i, 0))],
        out_specs=pl.BlockSpec(memory_space=pl.ANY),
        scratch_shapes=[pltpu.SMEM((rec,), I32), pltpu.SMEM(zinfo.shape, I32),
                        pltpu.VMEM((2, 2 * tm * V7X_SUBLANES, V7X_LANES), x.dtype),
                        pltpu.VMEM((TM_EXPERT // 2 * V7X_SUBLANES, V7X_LANES), x.dtype),
                        pltpu.SemaphoreType.DMA, pltpu.SemaphoreType.DMA((2,)), pltpu.SemaphoreType.DMA],
        compiler_params=_params("arbitrary"),
        name="dispatch",
    )(idx, zinfo, x)


def _experts_kernel(te_ref, tv_ref, tf_ref, x_ref, w1_ref, w3_ref, w2_ref, y_ref,
                    w1_b, w3_b, w2_b, *, tm):
    i = pl.program_id(0)

    @pl.when(tf_ref[i] > 0)
    def _():
        w1_b[...] = w1_ref[...].astype(BF16)
        w3_b[...] = w3_ref[...].astype(BF16)
        w2_b[...] = w2_ref[...].astype(BF16)

    @pl.when(tv_ref[i] > 0)
    def _():
        x = _from_token_tiles(x_ref, tm).astype(BF16)
        h_gate = jnp.dot(x, w1_b[...], preferred_element_type=F32)
        h_up = jnp.dot(x, w3_b[...], preferred_element_type=F32)
        hid = (h_gate * jax.nn.sigmoid(h_gate) * h_up).astype(BF16)
        _to_token_tiles(y_ref, jnp.dot(hid, w2_b[...], preferred_element_type=F32))

    @pl.when(tv_ref[i] == 0)
    def _():
        y_ref[...] = jnp.zeros_like(y_ref)


def _experts(xs, w1, w3, w2, tile_expert, tile_valid, tile_first):
    n_rows = xs.shape[0] // V7X_SUBLANES
    tm = TM_EXPERT
    nt = n_rows // tm
    d, f = w1.shape[1:]
    tile_spec = pl.BlockSpec((tm * V7X_SUBLANES, V7X_LANES), lambda i, te, tv, tf: (i, 0))
    grid_spec = pltpu.PrefetchScalarGridSpec(
        num_scalar_prefetch=3,
        grid=(nt,),
        in_specs=[tile_spec,
                  pl.BlockSpec((None, d, f), lambda i, te, tv, tf: (te[i], 0, 0)),
                  pl.BlockSpec((None, d, f), lambda i, te, tv, tf: (te[i], 0, 0)),
                  pl.BlockSpec((None, f, d), lambda i, te, tv, tf: (te[i], 0, 0))],
        out_specs=tile_spec,
        scratch_shapes=[pltpu.VMEM((d, f), BF16), pltpu.VMEM((d, f), BF16), pltpu.VMEM((f, d), BF16)],
    )
    return pl.pallas_call(
        functools.partial(_experts_kernel, tm=tm),
        out_shape=jax.ShapeDtypeStruct(xs.shape, F32),
        grid_spec=grid_spec,
        compiler_params=_params("arbitrary"),
        name="experts",
    )(tile_expert, tile_valid, tile_first, xs, w1, w3, w2)


def _combine_ple_kernel(pos_hbm, ys_hbm, h_ref, route_ref, p_ref, nw_ref, wg_ref, bg_ref, wp_ref,
                        fw_ref, o_ref, idx_smem, ybuf, idx_sem, row_sem, *, tm, final):
    i = pl.program_id(0)
    nb = pl.num_programs(0)
    cur = lax.rem(i, 2)
    nxt = 1 - cur

    def gather(step, buf):
        _load_indices(pos_hbm, step, idx_smem, idx_sem)

        def issue(tt, carry):
            for s in range(2):
                pltpu.make_async_copy(_token_tile(ys_hbm, idx_smem[s * tm + tt]),
                                      _token_tile(ybuf.at[buf, s], tt), row_sem.at[buf]).start(priority=s)
            return carry

        lax.fori_loop(0, tm, issue, 0, unroll=8)

    def wait_gather(buf):
        for s in range(2):
            pltpu.make_async_copy(ys_hbm.at[pl.ds(0, ybuf.shape[2]), :], ybuf.at[buf, s],
                                  row_sem.at[buf]).wait()

    @pl.when(i == 0)
    def _():
        gather(0, 0)

    @pl.when(i + 1 < nb)
    def _():
        gather(i + 1, nxt)

    wait_gather(cur)

    route = route_ref[...]
    c1 = route[:, _R_C1:_R_C1 + 1]
    c2 = route[:, _R_C2:_R_C2 + 1]
    h2 = (h_ref[...] + c1 * _from_token_tiles(ybuf.at[cur, 0], tm)
          + c2 * _from_token_tiles(ybuf.at[cur, 1], tm))
    hn = _rms(h2, nw_ref[...]).astype(BF16)
    gate = jax.nn.sigmoid(jnp.dot(hn, wg_ref[...], preferred_element_type=F32) + bg_ref[...])
    h3 = h2 + gate * jnp.dot(p_ref[...].astype(BF16), wp_ref[...], preferred_element_type=F32)
    if final:
        h3 = _rms(h3, fw_ref[...])
    o_ref[...] = h3


def _combine_ple(pos_tiles, ys, h, route, p, nw, wg, bg, wp, fw, *, tm, final):
    t, d = h.shape
    nl = route.shape[1]
    dp = p.shape[1]
    tok = lambda i: (i, 0)
    const = lambda i: (0, 0)
    return pl.pallas_call(
        functools.partial(_combine_ple_kernel, tm=tm, final=final),
        out_shape=jax.ShapeDtypeStruct((t, d), F32),
        grid=(t // tm,),
        in_specs=[pl.BlockSpec(memory_space=pl.ANY), pl.BlockSpec(memory_space=pl.ANY),
                  pl.BlockSpec((tm, d), tok), pl.BlockSpec((tm, nl), tok), pl.BlockSpec((tm, dp), tok),
                  pl.BlockSpec((1, d), const), pl.BlockSpec((d, d), const), pl.BlockSpec((1, d), const),
                  pl.BlockSpec((dp, d), const), pl.BlockSpec((1, d), const)],
        out_specs=pl.BlockSpec((tm, d), tok),
        scratch_shapes=[pltpu.SMEM((pos_tiles.shape[0] // (t // tm),), I32),
                        pltpu.VMEM((2, 2, tm * V7X_SUBLANES, V7X_LANES), F32),
                        pltpu.SemaphoreType.DMA, pltpu.SemaphoreType.DMA((2,))],
        compiler_params=_params("arbitrary"),
        name="combine_ple_final" if final else "combine_ple",
    )(pos_tiles, ys, h, route, p, nw, wg, bg, wp, fw)


_SMEM_RECORD_WORDS = 1024


def _index_records(pos1, pos2, tm, extra=None):
    nb = pos1.shape[0] // tm
    parts = [pos1.reshape(nb, tm), pos2.reshape(nb, tm)]
    if extra is not None:
        parts.append(extra.reshape(nb, -1))
    rec = jnp.concatenate(parts, axis=1)
    pad = -rec.shape[1] % _SMEM_RECORD_WORDS
    return jnp.pad(rec, ((0, 0), (0, pad))).reshape(-1)


def _lookup(table, idx):
    ids = jnp.arange(table.shape[0], dtype=I32)
    return jnp.sum(jnp.where(idx[None, :] == ids[:, None], table[:, None], 0), axis=0)


def _bucket(ends, x):
    return jnp.minimum(jnp.sum((x[None, :] >= ends[:, None]).astype(I32), axis=0), ends.shape[0] - 1)


def kernel(x, p, norm_mix_w, w_in, rwkv_mu, rwkv_w0, rwkv_w2, rwkv_a0, rwkv_a2, rwkv_g2, rwkv_k_k, rwkv_k_a, rwkv_r_k, rwkv_ln_w, rwkv_ln_b, rwkv_v0, rwkv_v1, rwkv_v2, att_rel_bias, w_out, norm_ffn_w, router_group_w, router_group_b, router_expert_w, router_expert_b, expert_w1, expert_w3, expert_w2, norm_ple_w, ple_gate_w, ple_gate_b, ple_proj_w, final_norm_w):
    batch, seq, d = x.shape
    depth = w_in.shape[0]
    t = batch * seq
    d_r = rwkv_w0.shape[1]
    n_heads_r = d_r // HEAD_DIM
    n_rwkv_in = rwkv_mu.shape[1]
    d_a = (w_in.shape[2] - n_rwkv_in) // 3
    n_heads_a = d_a // HEAD_DIM
    n_dec, n_iclr, n_gate = rwkv_w2.shape[1], rwkv_a2.shape[1], rwkv_g2.shape[1]
    assert n_dec == n_iclr and n_gate == n_dec + n_iclr
    n_lo = n_dec + n_iclr + n_gate
    f_exp = expert_w1.shape[-1]
    assert d == V7X_SUBLANES * V7X_LANES
    n_rows = 2 * t + N_EXPERTS * TM_EXPERT
    n_tiles = n_rows // TM_EXPERT
    qb = min(QB_ATTN, seq)

    w1_all = expert_w1.reshape(depth * N_EXPERTS, d, f_exp)
    w3_all = expert_w3.reshape(depth * N_EXPERTS, d, f_exp)
    w2_all = expert_w2.reshape(depth * N_EXPERTS, f_exp, d)

    h = x.reshape(t, d)
    v_first = None
    for i in range(depth):
        wr = w_in[i, :, :n_rwkv_in].astype(BF16)
        wa = w_in[i, :, n_rwkv_in:].astype(BF16)
        wl = jnp.zeros((n_lo, 3 * d_r), F32)
        wl = wl.at[:n_dec, :d_r].set(rwkv_w2[i])
        wl = wl.at[n_dec:n_dec + n_iclr, d_r:2 * d_r].set(rwkv_a2[i])
        wl = wl.at[n_dec + n_iclr:, 2 * d_r:].set(rwkv_g2[i]).astype(BF16)
        v0 = rwkv_v0[i - 1] if i > 0 else jnp.zeros((d_r,), F32)
        vec = jnp.stack([rwkv_w0[i], rwkv_a0[i], rwkv_k_k[i], rwkv_k_a[i], rwkv_r_k[i],
                         rwkv_ln_w[i], rwkv_ln_b[i], v0])
        if i > 0:
            n_vr = rwkv_v1.shape[2]
            v1 = jnp.zeros((d_r, V7X_LANES), F32).at[:, :n_vr].set(rwkv_v1[i - 1]).astype(BF16)
            v2 = jnp.zeros((V7X_LANES, d_r), F32).at[:n_vr, :].set(rwkv_v2[i - 1]).astype(BF16)
        else:
            v1 = v2 = None
        table = _attn_table(att_rel_bias[i], qb)
        wor = w_out[i, :d_r].astype(BF16)
        woa = w_out[i, d_r:].astype(BF16)
        n_rt = N_GROUPS + N_EXPERTS
        wrt = jnp.zeros((d, V7X_LANES), F32)
        wrt = wrt.at[:, :N_GROUPS].set(router_group_w[i]).at[:, N_GROUPS:n_rt].set(router_expert_w[i])
        wrt_hi = wrt.astype(BF16)
        wrt = jnp.concatenate([wrt_hi, (wrt - wrt_hi.astype(F32)).astype(BF16)], axis=1)
        brt = jnp.zeros((1, V7X_LANES), F32)
        brt = brt.at[0, :N_GROUPS].set(router_group_b[i]).at[0, N_GROUPS:n_rt].set(router_expert_b[i])

        z_r, qkv = _norm_proj(h, norm_mix_w[i][None], wr, wa)
        if i == 0:
            y_r, v_first = _rwkv(z_r, None, rwkv_mu[i][None], vec, wl, None, None,
                                 batch=batch, seq=seq, n_heads=n_heads_r)
        else:
            y_r = _rwkv(z_r, v_first, rwkv_mu[i][None], vec, wl, v1, v2,
                        batch=batch, seq=seq, n_heads=n_heads_r)
        y_a = _attn(qkv, table, batch=batch, seq=seq, n_heads=n_heads_a)

        h1, hn, route, route_t, cnt, tstat = _outproj_route(y_r, y_a, h, wor, woa, norm_ffn_w[i][None],
                                                            wrt, brt)
        ri = route_t.astype(I32)
        counts = cnt[0, :N_EXPERTS].astype(I32)
        padded = ((counts + TM_EXPERT - 1) // TM_EXPERT) * TM_EXPERT
        p_end = jnp.cumsum(padded)
        p_start = p_end - padded
        pos1 = _lookup(p_start, ri[_R_E1]) + ri[_R_RANK1]
        pos2 = _lookup(p_start, ri[_R_E2]) + ri[_R_RANK2]
        tile_start = jnp.arange(n_tiles, dtype=I32) * TM_EXPERT
        tile_expert = _bucket(p_end, tile_start)
        tile_valid = (tile_start < p_end[-1]).astype(I32)

        tile_first = jnp.concatenate([jnp.ones((1,), I32),
                                      (tile_expert[1:] != tile_expert[:-1]).astype(I32)])

        tm_d = min(TM_DISPATCH, t)
        nb_d = t // tm_d
        assert tm_d == min(TM_ROUTE, t)
        ts = tstat.reshape(nb_d, V7X_SUBLANES, -1)[:, :2, :N_EXPERTS].astype(I32)
        tile_cnt, tile_before = ts[:, 0], ts[:, 1]
        local_start = jnp.cumsum(tile_cnt, axis=1) - tile_cnt
        run_dst = p_start[None, :] + tile_before
        ls_tok = jnp.repeat(local_start.T, tm_d, axis=1)
        ids = jnp.arange(N_EXPERTS, dtype=I32)[:, None]
        lpos1 = jnp.sum(jnp.where(ri[_R_E1][None, :] == ids, ls_tok, 0), axis=0) + ri[_R_LRANK1]
        lpos2 = jnp.sum(jnp.where(ri[_R_E2][None, :] == ids, ls_tok, 0), axis=0) + ri[_R_LRANK2]
        idx_d = _index_records(lpos1, lpos2, tm_d,
                               jnp.concatenate([tile_cnt, run_dst, local_start], axis=1))
        n_free = n_rows - 2 * t
        zero_rows = TM_EXPERT // 2
        zinfo = jnp.concatenate([p_start + counts, padded - counts,
                                 jnp.stack([p_end[-1], (n_rows - p_end[-1]) // zero_rows])])
        zinfo = jnp.pad(zinfo, (0, -zinfo.shape[0] % _SMEM_RECORD_WORDS)).astype(I32)
        xs = _dispatch(hn, idx_d, zinfo, n_rows, tm_d, n_free)
        ys = _experts(xs, w1_all, w3_all, w2_all, tile_expert + i * N_EXPERTS, tile_valid, tile_first)

        tm_c = min(TM_COMBINE, t)
        h = _combine_ple(_index_records(pos1, pos2, tm_c), ys, h1, route, p[i].reshape(t, -1),
                         norm_ple_w[i][None], ple_gate_w[i].astype(BF16), ple_gate_b[i][None],
                         ple_proj_w[i].astype(BF16), final_norm_w[None],
                         tm=tm_c, final=(i == depth - 1))
    return h.reshape(batch, seq, d)
```

```python
import functools

import jax
import jax.numpy as jnp
from jax import lax
from jax.experimental import pallas as pl
from jax.experimental.pallas import tpu as pltpu

F32 = jnp.float32
BF16 = jnp.bfloat16
I32 = jnp.int32

CHUNK = 64
HEAD_DIM = 64
LEFT_CHUNKS = 8
MAX_REL = 256
N_GROUPS = 4
EXPERTS_PER_GROUP = 8
N_EXPERTS = N_GROUPS * EXPERTS_PER_GROUP
RMS_EPS = 1e-6
GN_EPS = 64e-5
NEG_INF = -1e30

V7X_LANES = 128
V7X_SUBLANES = 8
V7X_VMEM_LIMIT_BYTES = 48 * 1024 * 1024

TM_PROJ = 512
TB_RWKV = 256
RWKV_CHUNK_GROUP = 2
QB_ATTN = 256
TM_ROUTE = 512
TM_DISPATCH = 512
TM_EXPERT = 256
TM_COMBINE = 512


def _params(*sem):
    return pltpu.CompilerParams(dimension_semantics=sem, vmem_limit_bytes=V7X_VMEM_LIMIT_BYTES)


def _rms(x, w):
    return x * lax.rsqrt(jnp.mean(x * x, axis=-1, keepdims=True) + RMS_EPS) * w


def _mm(a, b):
    return jnp.dot(a.astype(BF16), b.astype(BF16), preferred_element_type=F32)


def _mm_nt(a, b):
    return lax.dot_general(a.astype(BF16), b.astype(BF16), (((1,), (1,)), ((), ())),
                           preferred_element_type=F32)


def _mm_tn(a, b):
    return lax.dot_general(a.astype(BF16), b.astype(BF16), (((0,), (0,)), ((), ())),
                           preferred_element_type=F32)


def _to_token_tiles(ref, x):
    m, d = x.shape
    for s in range(d // V7X_LANES):
        ref[pl.ds(s, m, stride=V7X_SUBLANES), :] = x[:, s * V7X_LANES:(s + 1) * V7X_LANES]


def _from_token_tiles(ref, m):
    return jnp.concatenate([ref[pl.ds(s, m, stride=V7X_SUBLANES), :] for s in range(V7X_SUBLANES)],
                           axis=-1)


def _token_tile(ref, row):
    return ref.at[pl.ds(pl.multiple_of(row * V7X_SUBLANES, V7X_SUBLANES), V7X_SUBLANES), :]


def _split3(x):
    hi = x.astype(BF16)
    r1 = x - hi.astype(F32)
    mid = r1.astype(BF16)
    lo = (r1 - mid.astype(F32)).astype(BF16)
    return hi, mid, lo


def _mm_exact_lhs(a_bf16, x):
    hi, mid, lo = _split3(x)
    return (jnp.dot(a_bf16, hi, preferred_element_type=F32)
            + jnp.dot(a_bf16, mid, preferred_element_type=F32)
            + jnp.dot(a_bf16, lo, preferred_element_type=F32))


def _mm_split2_rhs(x, b_bf16):
    hi = x.astype(BF16)
    lo = (x - hi.astype(F32)).astype(BF16)
    return (jnp.dot(hi, b_bf16, preferred_element_type=F32)
            + jnp.dot(lo, b_bf16, preferred_element_type=F32))


def _norm_proj_kernel(h_ref, nw_ref, wr_ref, wa_ref, zr_ref, qkv_ref):
    hn = _rms(h_ref[...], nw_ref[...]).astype(BF16)
    zr_ref[...] = jnp.dot(hn, wr_ref[...], preferred_element_type=F32)
    qkv_ref[...] = jnp.dot(hn, wa_ref[...], preferred_element_type=F32).astype(BF16)


def _norm_proj(h, nw, wr, wa):
    t, d = h.shape
    tm = min(TM_PROJ, t)
    n_r, n_a = wr.shape[1], wa.shape[1]
    return pl.pallas_call(
        _norm_proj_kernel,
        out_shape=(jax.ShapeDtypeStruct((t, n_r), F32), jax.ShapeDtypeStruct((t, n_a), BF16)),
        grid=(t // tm,),
        in_specs=[pl.BlockSpec((tm, d), lambda i: (i, 0)),
                  pl.BlockSpec((1, d), lambda i: (0, 0)),
                  pl.BlockSpec((d, n_r), lambda i: (0, 0)),
                  pl.BlockSpec((d, n_a), lambda i: (0, 0))],
        out_specs=(pl.BlockSpec((tm, n_r), lambda i: (i, 0)),
                   pl.BlockSpec((tm, n_a), lambda i: (i, 0))),
        compiler_params=_params("parallel"),
        name="norm_proj",
    )(h, nw, wr, wa)


_V_W0, _V_A0, _V_KK, _V_KA, _V_RK, _V_LNW, _V_LNB, _V_V0 = range(8)


def _rwkv_kernel(*refs, has_vres, n_heads, d_r, group, nb):
    if has_vres:
        (z_ref, vf_ref, mu_ref, vec_ref, wl_ref, tril_ref, ones_ref, v1_ref, v2_ref, y_ref,
         s_ref, carry_ref, r_s, k_s, v_s, kk_s, a_s, lc_s, lw_s, bon_s, g_s) = refs
        vf_out_ref = None
    else:
        (z_ref, mu_ref, vec_ref, wl_ref, tril_ref, ones_ref, y_ref, vf_out_ref,
         s_ref, carry_ref, r_s, k_s, v_s, kk_s, a_s, lc_s, lw_s, bon_s, g_s) = refs
    per_token = (r_s, k_s, v_s, kk_s, a_s, lc_s, lw_s, bon_s, g_s)
    tb = z_ref.shape[0]
    gr = group * CHUNK
    k_step = pl.program_id(0)
    cur = lax.rem(k_step, 2)
    prv = 1 - cur

    @pl.when(k_step == 0)
    def _():
        s_ref[...] = jnp.zeros_like(s_ref)
        for ref in per_token:
            ref[...] = jnp.zeros_like(ref)

    @pl.when(lax.rem(k_step, nb) == 0)
    def _():
        carry_ref[...] = jnp.zeros_like(carry_ref)

    vec = vec_ref[...]

    def vrow(i):
        return vec[i:i + 1, :]

    ln_w = vrow(_V_LNW)
    ln_b = vrow(_V_LNB)
    mu = mu_ref[...]
    head_ones = ones_ref[...]
    tril = tril_ref[...]
    ci = lax.broadcasted_iota(I32, (CHUNK, CHUNK), 0)
    cj = lax.broadcasted_iota(I32, (CHUNK, CHUNK), 1)
    strict = cj < ci
    lower = cj <= ci
    eye = ci == cj
    eye_f = jnp.where(eye, 1.0, 0.0)
    hs = [slice(h * HEAD_DIM, (h + 1) * HEAD_DIM) for h in range(n_heads)]
    prev_first = lax.rem(k_step + nb - 1, nb) == 0

    def token_work(rows):
        z = z_ref[rows, :]
        row = lax.broadcasted_iota(I32, z.shape, 0)
        z_prev = jnp.where(row == 0, carry_ref[0:1, :], pltpu.roll(z, 1, axis=0))
        carry_ref[0:1, :] = z[gr - 1:gr, :]
        zs = z + (z_prev - z) * mu
        r = zs[:, 0:d_r]
        k = zs[:, d_r:2 * d_r]
        v = zs[:, 2 * d_r:3 * d_r]
        lo = zs[:, 3 * d_r:]
        n_lo = lo.shape[1]
        lane = lax.broadcasted_iota(I32, lo.shape, 1)
        lo_act = jnp.where(lane < n_lo // 4, jnp.tanh(lo),
                           jnp.where(lane < n_lo // 2, lo, jax.nn.sigmoid(lo)))
        lo_out = _mm(lo_act, wl_ref[...])
        if has_vres:
            vv = _mm(_mm(v, v1_ref[...]), v2_ref[...])
            v = v + (vf_ref[rows, :] - v) * jax.nn.sigmoid(vrow(_V_V0) + vv)
        w_log = -jax.nn.softplus(-(vrow(_V_W0) + lo_out[:, 0:d_r])) - 0.5
        lw = -jnp.exp(w_log)
        a = jax.nn.sigmoid(vrow(_V_A0) + lo_out[:, d_r:2 * d_r])
        kk = k * vrow(_V_KK)
        kk = kk * lax.rsqrt(jnp.maximum(_mm_split2_rhs(kk * kk, head_ones), 1e-24))
        k2 = k * (1.0 + (a - 1.0) * vrow(_V_KA))
        r_s[cur, rows, :] = r
        k_s[cur, rows, :] = k2
        v_s[cur, rows, :] = v
        kk_s[cur, rows, :] = kk
        a_s[cur, rows, :] = a
        lc_s[cur, rows, :] = _mm_exact_lhs(tril, lw)
        lw_s[cur, rows, :] = lw
        bon_s[cur, rows, :] = _mm_split2_rhs(r * k2 * vrow(_V_RK), head_ones) * v
        g_s[cur, rows, :] = lo_out[:, 2 * d_r:3 * d_r]

    def chunk_operands(r0):
        rs = pl.ds(r0, CHUNK)
        lc_c = lc_s[prv, rs, :]
        lw_c = lw_s[prv, rs, :]
        l_end = lc_s[prv, pl.ds(r0 + CHUNK - 1, 1), :]
        p_in = jnp.exp(lc_c)
        p_prev = jnp.exp(lc_c - lw_c)
        p_inv = jnp.exp(-lc_c)
        p_end = jnp.exp(l_end - lc_c)
        p_last = jnp.exp(l_end)
        kk_c = kk_s[prv, rs, :]
        b_c = kk_c * a_s[prv, rs, :]
        k_c = k_s[prv, rs, :]
        at = (-kk_c * p_prev).astype(BF16)
        bt = (b_c * p_inv).astype(BF16)
        bh = (b_c * p_end).astype(BF16)
        kt = (k_c * p_inv).astype(BF16)
        kh = (k_c * p_end).astype(BF16)
        rt = (r_s[prv, rs, :] * p_in).astype(BF16)
        vc = v_s[prv, rs, :].astype(BF16)
        per_head = [[x[:, sl] for sl in hs] for x in (at, bt, bh, kt, kh, rt, vc)]
        per_head.append([p_last[:, sl] for sl in hs])
        return per_head

    def group_body(gi, carry):
        g0 = pl.multiple_of(gi * gr, gr)
        rows = pl.ds(g0, gr)
        if vf_out_ref is not None:
            vf_out_ref[rows, :] = v_s[prv, rows, :]
        ops = [chunk_operands(g0 + c * CHUNK) for c in range(group)]
        at_h, bt_h, bh_h, kt_h, kh_h, rt_h, v_h, pl_h = ([x for c in range(group) for x in ops[c][q]]
                                                         for q in range(8))
        heads = range(group * n_heads)
        ar_h = [jnp.concatenate([at_h[h], rt_h[h]], axis=0) for h in heads]
        m_b = [_mm_nt(ar_h[h], bt_h[h]) for h in heads]
        m_k = [_mm_nt(ar_h[h], kt_h[h]) for h in heads]
        n_ab = [jnp.where(strict, m_b[h][:CHUNK], 0.0) for h in heads]
        a_ak = [jnp.where(strict, m_k[h][:CHUNK], 0.0) for h in heads]
        a_rb = [jnp.where(lower, m_b[h][CHUNK:], 0.0) for h in heads]
        a_rk = [jnp.where(lower, m_k[h][CHUNK:], 0.0) for h in heads]
        x_inv = [eye_f + n_ab[h] for h in heads]
        pw = [_mm(n_ab[h], n_ab[h]) for h in heads]
        akv = [_mm(a_ak[h], v_h[h]) for h in heads]
        n_sq = CHUNK.bit_length() - 2
        for it in range(n_sq):
            if it < n_sq - 1:
                st = [_mm(jnp.concatenate([x_inv[h], pw[h]], axis=0), pw[h]) for h in heads]
                x_inv = [x_inv[h] + st[h][:CHUNK] for h in heads]
                pw = [st[h][CHUNK:] for h in heads]
            else:
                st = [_mm(x_inv[h], pw[h]) for h in heads]
                x_inv = [x_inv[h] + st[h] for h in heads]
        w_h = [_mm(x_inv[h], at_h[h]) for h in heads]
        u0 = [_mm(x_inv[h], akv[h]) for h in heads]
        y0 = [_mm(a_rk[h], v_h[h]) + _mm(a_rb[h], u0[h]) for h in heads]
        r_p = [rt_h[h].astype(F32) + _mm(a_rb[h], w_h[h]) for h in heads]
        g_h = [jnp.where(eye, pl_h[h], 0.0) + _mm_tn(w_h[h], bh_h[h]) for h in heads]
        d_h = [_mm_tn(u0[h], bh_h[h]) + _mm_tn(v_h[h], kh_h[h]) for h in heads]
        fresh = prev_first & (gi == 0)
        s_h = [jnp.where(fresh, 0.0, s_ref[h]) for h in range(n_heads)]
        for c in range(group):
            rs = pl.ds(g0 + c * CHUNK, CHUNK)
            idx = [c * n_heads + h for h in range(n_heads)]
            y_h = [y0[i] + _mm_nt(r_p[i], s_h[h]) for h, i in enumerate(idx)]
            s_h = [_mm(s_h[h], g_h[i]) + d_h[i] for h, i in enumerate(idx)]
            y_heads = []
            for h in range(n_heads):
                mean = jnp.mean(y_h[h], axis=-1, keepdims=True)
                yc = y_h[h] - mean
                var = jnp.mean(yc * yc, axis=-1, keepdims=True)
                y_heads.append(yc * lax.rsqrt(var + GN_EPS))
            y_n = jnp.concatenate(y_heads, axis=-1)
            out = (y_n * ln_w + ln_b + bon_s[prv, rs, :]) * g_s[prv, rs, :]
            y_ref[rs, :] = out.astype(y_ref.dtype)
        for h in range(n_heads):
            s_ref[h] = s_h[h]
        token_work(rows)
        return carry

    lax.fori_loop(0, tb // gr, group_body, 0)


def _rwkv(z, v_first, mu, vec, wl, v1, v2, *, batch, seq, n_heads):
    t, n_z = z.shape
    d_r = n_heads * HEAD_DIM
    tb = min(TB_RWKV, seq)
    nb = seq // tb
    n_blocks = batch * nb
    has_vres = v_first is not None
    tok_in = lambda k: (jnp.minimum(k, n_blocks - 1), 0)
    tok_out = lambda k: (jnp.maximum(k - 1, 0), 0)
    const = lambda k: (0, 0)
    in_specs = [pl.BlockSpec((tb, n_z), tok_in)]
    args = [z]
    if has_vres:
        in_specs.append(pl.BlockSpec((tb, d_r), tok_in))
        args.append(v_first)
    n_chunks = tb // CHUNK
    group = RWKV_CHUNK_GROUP if n_chunks % RWKV_CHUNK_GROUP == 0 else 1
    ti = jnp.arange(group * CHUNK)
    tril = ((ti[:, None] // CHUNK == ti[None, :] // CHUNK) & (ti[None, :] <= ti[:, None])).astype(BF16)
    hi = jnp.arange(d_r) // HEAD_DIM
    head_ones = (hi[:, None] == hi[None, :]).astype(BF16)
    in_specs += [pl.BlockSpec(mu.shape, const), pl.BlockSpec(vec.shape, const),
                 pl.BlockSpec(wl.shape, const), pl.BlockSpec(tril.shape, const),
                 pl.BlockSpec(head_ones.shape, const)]
    args += [mu, vec, wl, tril, head_ones]
    if has_vres:
        in_specs += [pl.BlockSpec(v1.shape, const), pl.BlockSpec(v2.shape, const)]
        args += [v1, v2]
        out_shape = jax.ShapeDtypeStruct((t, d_r), BF16)
        out_specs = pl.BlockSpec((tb, d_r), tok_out)
    else:
        out_shape = (jax.ShapeDtypeStruct((t, d_r), BF16), jax.ShapeDtypeStruct((t, d_r), F32))
        out_specs = (pl.BlockSpec((tb, d_r), tok_out), pl.BlockSpec((tb, d_r), tok_out))
    scratch = [pltpu.VMEM((n_heads, HEAD_DIM, HEAD_DIM), F32),
               pltpu.VMEM((V7X_SUBLANES, n_z), F32)]
    scratch += [pltpu.VMEM((2, tb, d_r), F32) for _ in range(9)]
    return pl.pallas_call(
        functools.partial(_rwkv_kernel, has_vres=has_vres, n_heads=n_heads, d_r=d_r, group=group, nb=nb),
        out_shape=out_shape,
        grid=(n_blocks + 1,),
        in_specs=in_specs,
        out_specs=out_specs,
        scratch_shapes=scratch,
        compiler_params=_params("arbitrary"),
        name="rwkv_vres" if has_vres else "rwkv",
    )(*args)


def _attn_kernel(*refs, n_heads, n_parts):
    q_ref = refs[0]
    k_refs = refs[1:1 + n_parts]
    v_refs = refs[1 + n_parts:1 + 2 * n_parts]
    tab_ref = refs[1 + 2 * n_parts]
    o_ref = refs[2 + 2 * n_parts]
    qb = q_ref.shape[0]
    j = pl.program_id(1)
    scale = HEAD_DIM ** -0.5
    q = q_ref[...] * jnp.asarray(scale, q_ref.dtype)
    ks = [r[...] for r in k_refs]
    vs = [r[...] for r in v_refs]
    pw = 2 * HEAD_DIM
    lane = lax.broadcasted_iota(I32, (qb, pw), 1)
    sum_even = jnp.where(lane < HEAD_DIM, 1.0, 0.0).astype(q.dtype)
    sum_odd = jnp.where(lane < HEAD_DIM, 0.0, 1.0).astype(q.dtype)
    zero = jnp.zeros((), q.dtype)
    is_even = sum_even > zero
    outs = []
    for hp in range(n_heads // 2):
        sl = slice(hp * pw, (hp + 1) * pw)
        qq = q[:, sl]
        q_pair = (jnp.where(is_even, qq, zero), jnp.where(is_even, zero, qq))
        s_parts = [[], []]
        for p in range(n_parts):
            kk = ks[p][:, sl]
            back = n_parts - 1 - p
            for u in range(2):
                s = _mm_nt(q_pair[u], kk) + tab_ref[2 * hp + u, :, p * qb:(p + 1) * qb]
                if back > 0:
                    s = jnp.where(j >= back, s, NEG_INF)
                s_parts[u].append(s)
        m = []
        for u in range(2):
            mm = s_parts[u][0]
            for s in s_parts[u][1:]:
                mm = jnp.maximum(mm, s)
            m.append(mm.max(axis=-1, keepdims=True))
        acc = jnp.zeros((qb, 2 * pw), F32)
        for p in range(n_parts):
            vv = vs[p][:, sl]
            rhs = jnp.concatenate(
                [jnp.concatenate([jnp.where(is_even, vv, zero), sum_even], axis=1),
                 jnp.concatenate([jnp.where(is_even, zero, vv), sum_odd], axis=1)], axis=0)
            e = jnp.concatenate([jnp.exp(s_parts[u][p] - m[u]).astype(BF16) for u in range(2)], axis=1)
            acc = acc + jnp.dot(e, rhs, preferred_element_type=F32)
        outs.append(acc[:, :pw] / acc[:, pw:])
    o_ref[...] = jnp.concatenate(outs, axis=-1).astype(o_ref.dtype)


def _attn(qkv, table, *, batch, seq, n_heads):
    t = qkv.shape[0]
    d_a = n_heads * HEAD_DIM
    qb = min(QB_ATTN, seq)
    left = LEFT_CHUNKS * CHUNK
    assert left % qb == 0 and seq % qb == 0
    n_parts = left // qb + 1
    nb = seq // qb
    in_specs = [pl.BlockSpec((qb, d_a), lambda b, j: (b * nb + j, 0))]
    for p in range(n_parts):
        back = n_parts - 1 - p
        in_specs.append(pl.BlockSpec((qb, d_a), lambda b, j, back=back: (b * nb + jnp.maximum(j - back, 0), 1)))
    for p in range(n_parts):
        back = n_parts - 1 - p
        in_specs.append(pl.BlockSpec((qb, d_a), lambda b, j, back=back: (b * nb + jnp.maximum(j - back, 0), 2)))
    in_specs.append(pl.BlockSpec(table.shape, lambda b, j: (0, 0, 0)))
    return pl.pallas_call(
        functools.partial(_attn_kernel, n_heads=n_heads, n_parts=n_parts),
        out_shape=jax.ShapeDtypeStruct((t, d_a), BF16),
        grid=(batch, nb),
        in_specs=in_specs,
        out_specs=pl.BlockSpec((qb, d_a), lambda b, j: (b * nb + j, 0)),
        compiler_params=_params("parallel", "arbitrary"),
        name="attn",
    )(*([qkv] * (1 + 2 * n_parts)), table)


def _attn_table(rel_bias, qb):
    left = LEFT_CHUNKS * CHUNK
    n_keys = left + qb
    period = qb + n_keys - 1
    n_heads = rel_bias.shape[0]
    m = jnp.arange(period)
    rel = left - jnp.where(m < n_keys, m, m - period)
    g = rel_bias[:, jnp.clip(rel, -(CHUNK - 1), MAX_REL) + (CHUNK - 1)].astype(F32)
    flat = jnp.tile(g, (1, qb))[:, :qb * (period - 1)]
    bias = flat.reshape(n_heads, qb, period - 1)[:, :, :n_keys]
    cq = jnp.arange(qb)[:, None] // CHUNK
    ck = jnp.arange(n_keys)[None, :] // CHUNK
    valid = (ck >= cq) & (ck <= cq + LEFT_CHUNKS)
    return jnp.where(valid[None], bias, NEG_INF)


_R_E1, _R_E2, _R_C1, _R_C2, _R_RANK1, _R_RANK2, _R_LRANK1, _R_LRANK2 = range(8)


def _outproj_route_kernel(yr_ref, ya_ref, h_ref, wor_ref, woa_ref, nw_ref, wrt_ref, brt_ref,
                          h1_ref, hn_ref, route_ref, route_t_ref, cnt_ref, tstat_ref, carry_ref):
    i = pl.program_id(0)

    @pl.when(i == 0)
    def _():
        carry_ref[...] = jnp.zeros_like(carry_ref)

    h1 = (h_ref[...] + jnp.dot(yr_ref[...], wor_ref[...], preferred_element_type=F32)
          + jnp.dot(ya_ref[...], woa_ref[...], preferred_element_type=F32))
    h1_ref[...] = h1
    hn = _rms(h1, nw_ref[...])
    _to_token_tiles(hn_ref, hn)
    nl = brt_ref.shape[1]
    hn_hi = hn.astype(BF16)
    hn_lo = (hn - hn_hi.astype(F32)).astype(BF16)
    part = jnp.dot(hn_hi, wrt_ref[...], preferred_element_type=F32)
    logits = (part[:, :nl] + part[:, nl:]
              + jnp.dot(hn_lo, wrt_ref[:, :nl], preferred_element_type=F32) + brt_ref[...])
    tm = logits.shape[0]
    lane = lax.broadcasted_iota(I32, (tm, nl), 1)
    lane_f = lane.astype(F32)
    ninf = -jnp.inf
    big = float(nl)
    is_g = lane < N_GROUPS
    gl = jnp.where(is_g, logits, ninf)
    g_max = gl.max(axis=-1, keepdims=True)
    g_sel = jnp.where(gl == g_max, lane_f, big).min(axis=-1, keepdims=True)
    p_g = 1.0 / jnp.where(is_g, jnp.exp(logits - g_max), 0.0).sum(axis=-1, keepdims=True)
    e_lo = N_GROUPS + EXPERTS_PER_GROUP * g_sel
    in_grp = (lane_f >= e_lo) & (lane_f < e_lo + EXPERTS_PER_GROUP)
    el = jnp.where(in_grp, logits, ninf)
    m1 = el.max(axis=-1, keepdims=True)
    i1 = jnp.where(el == m1, lane_f, big).min(axis=-1, keepdims=True)
    el2 = jnp.where(lane_f == i1, ninf, el)
    m2 = el2.max(axis=-1, keepdims=True)
    i2 = jnp.where(el2 == m2, lane_f, big).min(axis=-1, keepdims=True)
    t2 = jnp.exp(m2 - m1)
    c1 = p_g / (1.0 + t2)
    c2 = p_g * t2 / (1.0 + t2)
    e1 = i1 - N_GROUPS
    e2 = i2 - N_GROUPS
    oh1 = lane_f == e1
    oh2 = lane_f == e2
    ohs = jnp.where(oh1 | oh2, 1.0, 0.0)
    ri = lax.broadcasted_iota(I32, (tm, tm), 0)
    rj = lax.broadcasted_iota(I32, (tm, tm), 1)
    before = jnp.where(rj < ri, 1.0, 0.0).astype(BF16)
    old_carry = carry_ref[0:1, :]
    cnt_tile = jnp.dot(before, ohs.astype(BF16), preferred_element_type=F32)
    cnt = cnt_tile + old_carry
    rank1 = jnp.where(oh1, cnt, 0.0).sum(axis=-1, keepdims=True)
    rank2 = jnp.where(oh2, cnt, 0.0).sum(axis=-1, keepdims=True)
    lrank1 = jnp.where(oh1, cnt_tile, 0.0).sum(axis=-1, keepdims=True)
    lrank2 = jnp.where(oh2, cnt_tile, 0.0).sum(axis=-1, keepdims=True)
    tile_cnt = ohs.sum(axis=0, keepdims=True)
    new_carry = old_carry + tile_cnt
    carry_ref[0:1, :] = new_carry
    cnt_ref[...] = jnp.broadcast_to(new_carry, cnt_ref.shape)
    srow = lax.broadcasted_iota(I32, tstat_ref.shape, 0)
    tstat_ref[...] = jnp.where(srow == 0, tile_cnt, jnp.where(srow == 1, old_carry, 0.0))
    route = jnp.zeros((tm, nl), F32)
    for idx, val in ((_R_E1, e1), (_R_E2, e2), (_R_C1, c1), (_R_C2, c2),
                     (_R_RANK1, rank1), (_R_RANK2, rank2), (_R_LRANK1, lrank1), (_R_LRANK2, lrank2)):
        route = jnp.where(lane == idx, val, route)
    route_ref[...] = route
    route_t_ref[...] = route.T[:route_t_ref.shape[0], :]


def _outproj_route(yr, ya, h, wor, woa, nw, wrt, brt):
    t, d = h.shape
    tm = min(TM_ROUTE, t)
    d_r, d_a = yr.shape[1], ya.shape[1]
    nl = brt.shape[1]
    tok = lambda i: (i, 0)
    const = lambda i: (0, 0)
    return pl.pallas_call(
        _outproj_route_kernel,
        out_shape=(jax.ShapeDtypeStruct((t, d), F32),
                   jax.ShapeDtypeStruct((t * V7X_SUBLANES, V7X_LANES), F32),
                   jax.ShapeDtypeStruct((t, nl), F32), jax.ShapeDtypeStruct((V7X_SUBLANES, t), F32),
                   jax.ShapeDtypeStruct((V7X_SUBLANES, nl), F32),
                   jax.ShapeDtypeStruct((t // tm * V7X_SUBLANES, nl), F32)),
        grid=(t // tm,),
        in_specs=[pl.BlockSpec((tm, d_r), tok), pl.BlockSpec((tm, d_a), tok), pl.BlockSpec((tm, d), tok),
                  pl.BlockSpec((d_r, d), const), pl.BlockSpec((d_a, d), const),
                  pl.BlockSpec((1, d), const), pl.BlockSpec(wrt.shape, const), pl.BlockSpec((1, nl), const)],
        out_specs=(pl.BlockSpec((tm, d), tok), pl.BlockSpec((tm * V7X_SUBLANES, V7X_LANES), tok),
                   pl.BlockSpec((tm, nl), tok), pl.BlockSpec((V7X_SUBLANES, tm), lambda i: (0, i)),
                   pl.BlockSpec((V7X_SUBLANES, nl), const), pl.BlockSpec((V7X_SUBLANES, nl), tok)),
        scratch_shapes=[pltpu.VMEM((V7X_SUBLANES, nl), F32)],
        compiler_params=_params("arbitrary"),
        name="outproj_route",
    )(yr, ya, h, wor, woa, nw, wrt, brt)


def _load_indices(idx_hbm, i, idx_smem, sem):
    n = idx_smem.shape[0]
    cp = pltpu.make_async_copy(idx_hbm.at[pl.ds(pl.multiple_of(i * n, n), n)], idx_smem, sem)
    cp.start()
    cp.wait()


def _tiles(ref, row, n):
    start = row * V7X_SUBLANES
    if not isinstance(row, int):
        start = pl.multiple_of(start, V7X_SUBLANES)
    return ref.at[pl.ds(start, n * V7X_SUBLANES), :]


def _start_run_copies(length, n_bits, copy_of):
    for b in range(n_bits):
        @pl.when(((length >> b) & 1) == 1)
        def _(b=b):
            done = (length >> (b + 1)) << (b + 1)
            copy_of(done, 1 << b, b).start(priority=b % 2)


def _dispatch_kernel(idx_hbm, zinfo_hbm, x_ref, xs_hbm, idx_smem, zinfo_smem, xloc, zero_vmem,
                     idx_sem, run_sem, zero_sem, *, tm, n_free):
    i = pl.program_id(0)
    nb = pl.num_programs(0)
    buf = lax.rem(i, 2)
    zero_rows = zero_vmem.shape[0] // V7X_SUBLANES

    @pl.when(i == 0)
    def _():
        cp = pltpu.make_async_copy(zinfo_hbm, zinfo_smem, idx_sem)
        cp.start()
        cp.wait()
        zero_vmem[...] = jnp.zeros_like(zero_vmem)

        def pad_runs(e, carry):
            start = zinfo_smem[e]
            _start_run_copies(
                zinfo_smem[N_EXPERTS + e], zero_rows.bit_length(),
                lambda done, n, b: pltpu.make_async_copy(_tiles(zero_vmem, 0, n),
                                                         _tiles(xs_hbm, start + done, n), zero_sem))
            return carry

        lax.fori_loop(0, N_EXPERTS, pad_runs, 0)
        tail_start = zinfo_smem[2 * N_EXPERTS]

        def tail_block(n, carry):
            pltpu.make_async_copy(zero_vmem, _tiles(xs_hbm, tail_start + n * zero_rows, zero_rows),
                                  zero_sem).start()
            return carry

        lax.fori_loop(0, zinfo_smem[2 * N_EXPERTS + 1], tail_block, 0)

    _load_indices(idx_hbm, i, idx_smem, idx_sem)

    def wait_runs(b):
        pltpu.make_async_copy(xloc.at[b], xs_hbm.at[pl.ds(0, xloc.shape[1]), :], run_sem.at[b]).wait()

    @pl.when(i >= 2)
    def _():
        wait_runs(buf)

    def place(tt, carry):
        row = x_ref[pl.ds(pl.multiple_of(tt * V7X_SUBLANES, V7X_SUBLANES), V7X_SUBLANES), :]
        for s in range(2):
            lp = idx_smem[s * tm + tt]
            xloc[buf, pl.ds(pl.multiple_of(lp * V7X_SUBLANES, V7X_SUBLANES), V7X_SUBLANES), :] = row
        return carry

    lax.fori_loop(0, tm, place, 0, unroll=8)

    def expert_run(e, carry):
        dst = idx_smem[2 * tm + N_EXPERTS + e]
        off = idx_smem[2 * tm + 2 * N_EXPERTS + e]
        _start_run_copies(
            idx_smem[2 * tm + e], tm.bit_length(),
            lambda done, n, b: pltpu.make_async_copy(_tiles(xloc.at[buf], off + done, n),
                                                     _tiles(xs_hbm, dst + done, n), run_sem.at[buf]))
        return carry

    lax.fori_loop(0, N_EXPERTS, expert_run, 0)

    @pl.when(i == nb - 1)
    def _():
        wait_runs(buf)

        @pl.when(nb >= 2)
        def _():
            wait_runs(1 - buf)

        pltpu.make_async_copy(xs_hbm.at[pl.ds(0, n_free * V7X_SUBLANES), :],
                              xs_hbm.at[pl.ds(0, n_free * V7X_SUBLANES), :], zero_sem).wait()


def _dispatch(x, idx, zinfo, n_rows, tm, n_free):
    t = x.shape[0] // V7X_SUBLANES
    nb = t // tm
    rec = idx.shape[0] // nb
    return pl.pallas_call(
        functools.partial(_dispatch_kernel, tm=tm, n_free=n_free),
        out_shape=jax.ShapeDtypeStruct((n_rows * V7X_SUBLANES, V7X_LANES), x.dtype),
        grid=(nb,),
        in_specs=[pl.BlockSpec(memory_space=pl.ANY), pl.BlockSpec(memory_space=pl.ANY),
                  pl.BlockSpec((tm * V7X_SUBLANES, V7X_LANES), lambda i: (i, 0))],
        out_specs=pl.BlockSpec(memory_space=pl.ANY),
        scratch_shapes=[pltpu.SMEM((rec,), I32), pltpu.SMEM(zinfo.shape, I32),
                        pltpu.VMEM((2, 2 * tm * V7X_SUBLANES, V7X_LANES), x.dtype),
                        pltpu.VMEM((TM_EXPERT // 2 * V7X_SUBLANES, V7X_LANES), x.dtype),
                        pltpu.SemaphoreType.DMA, pltpu.SemaphoreType.DMA((2,)), pltpu.SemaphoreType.DMA],
        compiler_params=_params("arbitrary"),
        name="dispatch",
    )(idx, zinfo, x)


def _experts_kernel(te_ref, tv_ref, tf_ref, x_ref, w1_ref, w3_ref, w2_ref, y_ref,
                    w1_b, w3_b, w2_b, *, tm):
    i = pl.program_id(0)

    @pl.when(tf_ref[i] > 0)
    def _():
        w1_b[...] = w1_ref[...].astype(BF16)
        w3_b[...] = w3_ref[...].astype(BF16)
        w2_b[...] = w2_ref[...].astype(BF16)

    @pl.when(tv_ref[i] > 0)
    def _():
        x = _from_token_tiles(x_ref, tm).astype(BF16)
        h_gate = jnp.dot(x, w1_b[...], preferred_element_type=F32)
        h_up = jnp.dot(x, w3_b[...], preferred_element_type=F32)
        hid = (h_gate * jax.nn.sigmoid(h_gate) * h_up).astype(BF16)
        _to_token_tiles(y_ref, jnp.dot(hid, w2_b[...], preferred_element_type=F32))

    @pl.when(tv_ref[i] == 0)
    def _():
        y_ref[...] = jnp.zeros_like(y_ref)


def _experts(xs, w1, w3, w2, tile_expert, tile_valid, tile_first):
    n_rows = xs.shape[0] // V7X_SUBLANES
    tm = TM_EXPERT
    nt = n_rows // tm
    d, f = w1.shape[1:]
    tile_spec = pl.BlockSpec((tm * V7X_SUBLANES, V7X_LANES), lambda i, te, tv, tf: (i, 0))
    grid_spec = pltpu.PrefetchScalarGridSpec(
        num_scalar_prefetch=3,
        grid=(nt,),
        in_specs=[tile_spec,
                  pl.BlockSpec((None, d, f), lambda i, te, tv, tf: (te[i], 0, 0)),
                  pl.BlockSpec((None, d, f), lambda i, te, tv, tf: (te[i], 0, 0)),
                  pl.BlockSpec((None, f, d), lambda i, te, tv, tf: (te[i], 0, 0))],
        out_specs=tile_spec,
        scratch_shapes=[pltpu.VMEM((d, f), BF16), pltpu.VMEM((d, f), BF16), pltpu.VMEM((f, d), BF16)],
    )
    return pl.pallas_call(
        functools.partial(_experts_kernel, tm=tm),
        out_shape=jax.ShapeDtypeStruct(xs.shape, F32),
        grid_spec=grid_spec,
        compiler_params=_params("arbitrary"),
        name="experts",
    )(tile_expert, tile_valid, tile_first, xs, w1, w3, w2)


def _combine_ple_kernel(pos_hbm, ys_hbm, h_ref, route_ref, p_ref, nw_ref, wg_ref, bg_ref, wp_ref,
                        fw_ref, o_ref, idx_smem, ysort, ybuf, idx_sem, run_sem, *, tm, final):
    i = pl.program_id(0)
    nb = pl.num_programs(0)
    cur = lax.rem(i, 2)
    nxt = 1 - cur

    def gather(step, buf):
        _load_indices(pos_hbm, step, idx_smem, idx_sem)

        def expert_run(e, carry):
            src = idx_smem[2 * tm + N_EXPERTS + e]
            off = idx_smem[2 * tm + 2 * N_EXPERTS + e]
            _start_run_copies(
                idx_smem[2 * tm + e], tm.bit_length(),
                lambda done, n, b: pltpu.make_async_copy(_tiles(ys_hbm, src + done, n),
                                                         _tiles(ysort.at[buf], off + done, n),
                                                         run_sem.at[buf]))
            return carry

        lax.fori_loop(0, N_EXPERTS, expert_run, 0)

    @pl.when(i == 0)
    def _():
        gather(0, 0)

    pltpu.make_async_copy(ys_hbm.at[pl.ds(0, ysort.shape[1]), :], ysort.at[cur], run_sem.at[cur]).wait()

    def place(tt, carry):
        for s in range(2):
            lp = idx_smem[s * tm + tt]
            ybuf[s, pl.ds(pl.multiple_of(tt * V7X_SUBLANES, V7X_SUBLANES), V7X_SUBLANES), :] = (
                ysort[cur, pl.ds(pl.multiple_of(lp * V7X_SUBLANES, V7X_SUBLANES), V7X_SUBLANES), :])
        return carry

    lax.fori_loop(0, tm, place, 0, unroll=8)

    @pl.when(i + 1 < nb)
    def _():
        gather(i + 1, nxt)

    route = route_ref[...]
    c1 = route[:, _R_C1:_R_C1 + 1]
    c2 = route[:, _R_C2:_R_C2 + 1]
    h2 = (h_ref[...] + c1 * _from_token_tiles(ybuf.at[0], tm)
          + c2 * _from_token_tiles(ybuf.at[1], tm))
    hn = _rms(h2, nw_ref[...]).astype(BF16)
    gate = jax.nn.sigmoid(jnp.dot(hn, wg_ref[...], preferred_element_type=F32) + bg_ref[...])
    h3 = h2 + gate * jnp.dot(p_ref[...].astype(BF16), wp_ref[...], preferred_element_type=F32)
    if final:
        h3 = _rms(h3, fw_ref[...])
    o_ref[...] = h3


def _combine_ple(pos_tiles, ys, h, route, p, nw, wg, bg, wp, fw, *, tm, final):
    t, d = h.shape
    nl = route.shape[1]
    dp = p.shape[1]
    tok = lambda i: (i, 0)
    const = lambda i: (0, 0)
    return pl.pallas_call(
        functools.partial(_combine_ple_kernel, tm=tm, final=final),
        out_shape=jax.ShapeDtypeStruct((t, d), F32),
        grid=(t // tm,),
        in_specs=[pl.BlockSpec(memory_space=pl.ANY), pl.BlockSpec(memory_space=pl.ANY),
                  pl.BlockSpec((tm, d), tok), pl.BlockSpec((tm, nl), tok), pl.BlockSpec((tm, dp), tok),
                  pl.BlockSpec((1, d), const), pl.BlockSpec((d, d), const), pl.BlockSpec((1, d), const),
                  pl.BlockSpec((dp, d), const), pl.BlockSpec((1, d), const)],
        out_specs=pl.BlockSpec((tm, d), tok),
        scratch_shapes=[pltpu.SMEM((pos_tiles.shape[0] // (t // tm),), I32),
                        pltpu.VMEM((2, 2 * tm * V7X_SUBLANES, V7X_LANES), F32),
                        pltpu.VMEM((2, tm * V7X_SUBLANES, V7X_LANES), F32),
                        pltpu.SemaphoreType.DMA, pltpu.SemaphoreType.DMA((2,))],
        compiler_params=_params("arbitrary"),
        name="combine_ple_final" if final else "combine_ple",
    )(pos_tiles, ys, h, route, p, nw, wg, bg, wp, fw)


_SMEM_RECORD_WORDS = 1024


def _index_records(pos1, pos2, tm, extra=None):
    nb = pos1.shape[0] // tm
    parts = [pos1.reshape(nb, tm), pos2.reshape(nb, tm)]
    if extra is not None:
        parts.append(extra.reshape(nb, -1))
    rec = jnp.concatenate(parts, axis=1)
    pad = -rec.shape[1] % _SMEM_RECORD_WORDS
    return jnp.pad(rec, ((0, 0), (0, pad))).reshape(-1)


def _lookup(table, idx):
    ids = jnp.arange(table.shape[0], dtype=I32)
    return jnp.sum(jnp.where(idx[None, :] == ids[:, None], table[:, None], 0), axis=0)


def _bucket(ends, x):
    return jnp.minimum(jnp.sum((x[None, :] >= ends[:, None]).astype(I32), axis=0), ends.shape[0] - 1)


def kernel(x, p, norm_mix_w, w_in, rwkv_mu, rwkv_w0, rwkv_w2, rwkv_a0, rwkv_a2, rwkv_g2, rwkv_k_k, rwkv_k_a, rwkv_r_k, rwkv_ln_w, rwkv_ln_b, rwkv_v0, rwkv_v1, rwkv_v2, att_rel_bias, w_out, norm_ffn_w, router_group_w, router_group_b, router_expert_w, router_expert_b, expert_w1, expert_w3, expert_w2, norm_ple_w, ple_gate_w, ple_gate_b, ple_proj_w, final_norm_w):
    batch, seq, d = x.shape
    depth = w_in.shape[0]
    t = batch * seq
    d_r = rwkv_w0.shape[1]
    n_heads_r = d_r // HEAD_DIM
    n_rwkv_in = rwkv_mu.shape[1]
    d_a = (w_in.shape[2] - n_rwkv_in) // 3
    n_heads_a = d_a // HEAD_DIM
    n_dec, n_iclr, n_gate = rwkv_w2.shape[1], rwkv_a2.shape[1], rwkv_g2.shape[1]
    assert n_dec == n_iclr and n_gate == n_dec + n_iclr
    n_lo = n_dec + n_iclr + n_gate
    f_exp = expert_w1.shape[-1]
    assert d == V7X_SUBLANES * V7X_LANES
    n_rows = 2 * t + N_EXPERTS * TM_EXPERT
    n_tiles = n_rows // TM_EXPERT
    qb = min(QB_ATTN, seq)

    w1_all = expert_w1.reshape(depth * N_EXPERTS, d, f_exp)
    w3_all = expert_w3.reshape(depth * N_EXPERTS, d, f_exp)
    w2_all = expert_w2.reshape(depth * N_EXPERTS, f_exp, d)

    h = x.reshape(t, d)
    v_first = None
    for i in range(depth):
        wr = w_in[i, :, :n_rwkv_in].astype(BF16)
        wa = w_in[i, :, n_rwkv_in:].astype(BF16)
        wl = jnp.zeros((n_lo, 3 * d_r), F32)
        wl = wl.at[:n_dec, :d_r].set(rwkv_w2[i])
        wl = wl.at[n_dec:n_dec + n_iclr, d_r:2 * d_r].set(rwkv_a2[i])
        wl = wl.at[n_dec + n_iclr:, 2 * d_r:].set(rwkv_g2[i]).astype(BF16)
        v0 = rwkv_v0[i - 1] if i > 0 else jnp.zeros((d_r,), F32)
        vec = jnp.stack([rwkv_w0[i], rwkv_a0[i], rwkv_k_k[i], rwkv_k_a[i], rwkv_r_k[i],
                         rwkv_ln_w[i], rwkv_ln_b[i], v0])
        if i > 0:
            n_vr = rwkv_v1.shape[2]
            v1 = jnp.zeros((d_r, V7X_LANES), F32).at[:, :n_vr].set(rwkv_v1[i - 1]).astype(BF16)
            v2 = jnp.zeros((V7X_LANES, d_r), F32).at[:n_vr, :].set(rwkv_v2[i - 1]).astype(BF16)
        else:
            v1 = v2 = None
        table = _attn_table(att_rel_bias[i], qb)
        wor = w_out[i, :d_r].astype(BF16)
        woa = w_out[i, d_r:].astype(BF16)
        n_rt = N_GROUPS + N_EXPERTS
        wrt = jnp.zeros((d, V7X_LANES), F32)
        wrt = wrt.at[:, :N_GROUPS].set(router_group_w[i]).at[:, N_GROUPS:n_rt].set(router_expert_w[i])
        wrt_hi = wrt.astype(BF16)
        wrt = jnp.concatenate([wrt_hi, (wrt - wrt_hi.astype(F32)).astype(BF16)], axis=1)
        brt = jnp.zeros((1, V7X_LANES), F32)
        brt = brt.at[0, :N_GROUPS].set(router_group_b[i]).at[0, N_GROUPS:n_rt].set(router_expert_b[i])

        z_r, qkv = _norm_proj(h, norm_mix_w[i][None], wr, wa)
        if i == 0:
            y_r, v_first = _rwkv(z_r, None, rwkv_mu[i][None], vec, wl, None, None,
                                 batch=batch, seq=seq, n_heads=n_heads_r)
        else:
            y_r = _rwkv(z_r, v_first, rwkv_mu[i][None], vec, wl, v1, v2,
                        batch=batch, seq=seq, n_heads=n_heads_r)
        y_a = _attn(qkv, table, batch=batch, seq=seq, n_heads=n_heads_a)

        h1, hn, route, route_t, cnt, tstat = _outproj_route(y_r, y_a, h, wor, woa, norm_ffn_w[i][None],
                                                            wrt, brt)
        ri = route_t.astype(I32)
        counts = cnt[0, :N_EXPERTS].astype(I32)
        padded = ((counts + TM_EXPERT - 1) // TM_EXPERT) * TM_EXPERT
        p_end = jnp.cumsum(padded)
        p_start = p_end - padded
        tile_start = jnp.arange(n_tiles, dtype=I32) * TM_EXPERT
        tile_expert = _bucket(p_end, tile_start)
        tile_valid = (tile_start < p_end[-1]).astype(I32)

        tile_first = jnp.concatenate([jnp.ones((1,), I32),
                                      (tile_expert[1:] != tile_expert[:-1]).astype(I32)])

        tm_d = min(TM_DISPATCH, t)
        nb_d = t // tm_d
        assert tm_d == min(TM_ROUTE, t)
        ts = tstat.reshape(nb_d, V7X_SUBLANES, -1)[:, :2, :N_EXPERTS].astype(I32)
        tile_cnt, tile_before = ts[:, 0], ts[:, 1]
        local_start = jnp.cumsum(tile_cnt, axis=1) - tile_cnt
        run_dst = p_start[None, :] + tile_before
        ls_tok = jnp.repeat(local_start.T, tm_d, axis=1)
        ids = jnp.arange(N_EXPERTS, dtype=I32)[:, None]
        lpos1 = jnp.sum(jnp.where(ri[_R_E1][None, :] == ids, ls_tok, 0), axis=0) + ri[_R_LRANK1]
        lpos2 = jnp.sum(jnp.where(ri[_R_E2][None, :] == ids, ls_tok, 0), axis=0) + ri[_R_LRANK2]
        idx_d = _index_records(lpos1, lpos2, tm_d,
                               jnp.concatenate([tile_cnt, run_dst, local_start], axis=1))
        n_free = n_rows - 2 * t
        zero_rows = TM_EXPERT // 2
        zinfo = jnp.concatenate([p_start + counts, padded - counts,
                                 jnp.stack([p_end[-1], (n_rows - p_end[-1]) // zero_rows])])
        zinfo = jnp.pad(zinfo, (0, -zinfo.shape[0] % _SMEM_RECORD_WORDS)).astype(I32)
        xs = _dispatch(hn, idx_d, zinfo, n_rows, tm_d, n_free)
        ys = _experts(xs, w1_all, w3_all, w2_all, tile_expert + i * N_EXPERTS, tile_valid, tile_first)

        tm_c = min(TM_COMBINE, t)
        assert tm_c == tm_d
        h = _combine_ple(idx_d, ys, h1, route, p[i].reshape(t, -1),
                         norm_ple_w[i][None], ple_gate_w[i].astype(BF16), ple_gate_b[i][None],
                         ple_proj_w[i].astype(BF16), final_norm_w[None],
                         tm=tm_c, final=(i == depth - 1))
    return h.reshape(batch, seq, d)
```

```python
import functools

import jax
import jax.numpy as jnp
from jax import lax
from jax.experimental import pallas as pl
from jax.experimental.pallas import tpu as pltpu

F32 = jnp.float32
BF16 = jnp.bfloat16
I32 = jnp.int32

CHUNK = 64
HEAD_DIM = 64
LEFT_CHUNKS = 8
MAX_REL = 256
N_GROUPS = 4
EXPERTS_PER_GROUP = 8
N_EXPERTS = N_GROUPS * EXPERTS_PER_GROUP
RMS_EPS = 1e-6
GN_EPS = 64e-5
NEG_INF = -1e30

V7X_LANES = 128
V7X_SUBLANES = 8
V7X_VMEM_LIMIT_BYTES = 48 * 1024 * 1024

TM_PROJ = 512
TB_RWKV = 512
RWKV_CHUNK_GROUP = 2
QB_ATTN = 256
TM_ROUTE = 512
TM_DISPATCH = 512
TM_EXPERT = 512
TM_COMBINE = 512


def _params(*sem):
    return pltpu.CompilerParams(dimension_semantics=sem, vmem_limit_bytes=V7X_VMEM_LIMIT_BYTES)


def _rms(x, w):
    return x * lax.rsqrt(jnp.mean(x * x, axis=-1, keepdims=True) + RMS_EPS) * w


def _mm(a, b):
    return jnp.dot(a.astype(BF16), b.astype(BF16), preferred_element_type=F32)


def _mm_nt(a, b):
    return lax.dot_general(a.astype(BF16), b.astype(BF16), (((1,), (1,)), ((), ())),
                           preferred_element_type=F32)


def _mm_tn(a, b):
    return lax.dot_general(a.astype(BF16), b.astype(BF16), (((0,), (0,)), ((), ())),
                           preferred_element_type=F32)


def _to_token_tiles(ref, x):
    m, d = x.shape
    for s in range(d // V7X_LANES):
        ref[pl.ds(s, m, stride=V7X_SUBLANES), :] = x[:, s * V7X_LANES:(s + 1) * V7X_LANES]


def _from_token_tiles(ref, m):
    return jnp.concatenate([ref[pl.ds(s, m, stride=V7X_SUBLANES), :] for s in range(V7X_SUBLANES)],
                           axis=-1)


def _token_tile(ref, row):
    return ref.at[pl.ds(pl.multiple_of(row * V7X_SUBLANES, V7X_SUBLANES), V7X_SUBLANES), :]


def _split3(x):
    hi = x.astype(BF16)
    r1 = x - hi.astype(F32)
    mid = r1.astype(BF16)
    lo = (r1 - mid.astype(F32)).astype(BF16)
    return hi, mid, lo


def _mm_exact_lhs(a_bf16, x):
    hi, mid, lo = _split3(x)
    return (jnp.dot(a_bf16, hi, preferred_element_type=F32)
            + jnp.dot(a_bf16, mid, preferred_element_type=F32)
            + jnp.dot(a_bf16, lo, preferred_element_type=F32))


def _mm_split2_rhs(x, b_bf16):
    hi = x.astype(BF16)
    lo = (x - hi.astype(F32)).astype(BF16)
    return (jnp.dot(hi, b_bf16, preferred_element_type=F32)
            + jnp.dot(lo, b_bf16, preferred_element_type=F32))


def _norm_proj_kernel(h_ref, nw_ref, wr_ref, wa_ref, zr_ref, qkv_ref):
    hn = _rms(h_ref[...], nw_ref[...]).astype(BF16)
    zr_ref[...] = jnp.dot(hn, wr_ref[...], preferred_element_type=F32)
    qkv_ref[...] = jnp.dot(hn, wa_ref[...], preferred_element_type=F32).astype(BF16)


def _norm_proj(h, nw, wr, wa):
    t, d = h.shape
    tm = min(TM_PROJ, t)
    n_r, n_a = wr.shape[1], wa.shape[1]
    return pl.pallas_call(
        _norm_proj_kernel,
        out_shape=(jax.ShapeDtypeStruct((t, n_r), F32), jax.ShapeDtypeStruct((t, n_a), BF16)),
        grid=(t // tm,),
        in_specs=[pl.BlockSpec((tm, d), lambda i: (i, 0)),
                  pl.BlockSpec((1, d), lambda i: (0, 0)),
                  pl.BlockSpec((d, n_r), lambda i: (0, 0)),
                  pl.BlockSpec((d, n_a), lambda i: (0, 0))],
        out_specs=(pl.BlockSpec((tm, n_r), lambda i: (i, 0)),
                   pl.BlockSpec((tm, n_a), lambda i: (i, 0))),
        compiler_params=_params("parallel"),
        name="norm_proj",
    )(h, nw, wr, wa)


_V_W0, _V_A0, _V_KK, _V_KA, _V_RK, _V_LNW, _V_LNB, _V_V0 = range(8)


def _rwkv_kernel(*refs, has_vres, n_heads, d_r, group, nb):
    if has_vres:
        (z_ref, vf_ref, mu_ref, vec_ref, wl_ref, tril_ref, ones_ref, v1_ref, v2_ref, y_ref,
         s_ref, carry_ref, r_s, k_s, v_s, kk_s, a_s, lc_s, lw_s, bon_s, g_s) = refs
        vf_out_ref = None
    else:
        (z_ref, mu_ref, vec_ref, wl_ref, tril_ref, ones_ref, y_ref, vf_out_ref,
         s_ref, carry_ref, r_s, k_s, v_s, kk_s, a_s, lc_s, lw_s, bon_s, g_s) = refs
    per_token = (r_s, k_s, v_s, kk_s, a_s, lc_s, lw_s, bon_s, g_s)
    tb = z_ref.shape[0]
    gr = group * CHUNK
    k_step = pl.program_id(0)
    cur = lax.rem(k_step, 2)
    prv = 1 - cur

    @pl.when(k_step == 0)
    def _():
        s_ref[...] = jnp.zeros_like(s_ref)
        for ref in per_token:
            ref[...] = jnp.zeros_like(ref)

    @pl.when(lax.rem(k_step, nb) == 0)
    def _():
        carry_ref[...] = jnp.zeros_like(carry_ref)

    vec = vec_ref[...]

    def vrow(i):
        return vec[i:i + 1, :]

    ln_w = vrow(_V_LNW)
    ln_b = vrow(_V_LNB)
    mu = mu_ref[...]
    head_ones = ones_ref[...]
    tril = tril_ref[...]
    ci = lax.broadcasted_iota(I32, (CHUNK, CHUNK), 0)
    cj = lax.broadcasted_iota(I32, (CHUNK, CHUNK), 1)
    strict = cj < ci
    lower = cj <= ci
    eye = ci == cj
    eye_f = jnp.where(eye, 1.0, 0.0)
    hs = [slice(h * HEAD_DIM, (h + 1) * HEAD_DIM) for h in range(n_heads)]
    prev_first = lax.rem(k_step + nb - 1, nb) == 0

    def token_work(rows):
        z = z_ref[rows, :]
        row = lax.broadcasted_iota(I32, z.shape, 0)
        z_prev = jnp.where(row == 0, carry_ref[0:1, :], pltpu.roll(z, 1, axis=0))
        carry_ref[0:1, :] = z[gr - 1:gr, :]
        zs = z + (z_prev - z) * mu
        r = zs[:, 0:d_r]
        k = zs[:, d_r:2 * d_r]
        v = zs[:, 2 * d_r:3 * d_r]
        lo = zs[:, 3 * d_r:]
        n_lo = lo.shape[1]
        lane = lax.broadcasted_iota(I32, lo.shape, 1)
        lo_act = jnp.where(lane < n_lo // 4, jnp.tanh(lo),
                           jnp.where(lane < n_lo // 2, lo, jax.nn.sigmoid(lo)))
        lo_out = _mm(lo_act, wl_ref[...])
        if has_vres:
            vv = _mm(_mm(v, v1_ref[...]), v2_ref[...])
            v = v + (vf_ref[rows, :] - v) * jax.nn.sigmoid(vrow(_V_V0) + vv)
        w_log = -jax.nn.softplus(-(vrow(_V_W0) + lo_out[:, 0:d_r])) - 0.5
        lw = -jnp.exp(w_log)
        a = jax.nn.sigmoid(vrow(_V_A0) + lo_out[:, d_r:2 * d_r])
        kk = k * vrow(_V_KK)
        kk = kk * lax.rsqrt(jnp.maximum(_mm_split2_rhs(kk * kk, head_ones), 1e-24))
        k2 = k * (1.0 + (a - 1.0) * vrow(_V_KA))
        r_s[cur, rows, :] = r
        k_s[cur, rows, :] = k2
        v_s[cur, rows, :] = v
        kk_s[cur, rows, :] = kk
        a_s[cur, rows, :] = a
        lc_s[cur, rows, :] = _mm_exact_lhs(tril, lw)
        lw_s[cur, rows, :] = lw
        bon_s[cur, rows, :] = _mm_split2_rhs(r * k2 * vrow(_V_RK), head_ones) * v
        g_s[cur, rows, :] = lo_out[:, 2 * d_r:3 * d_r]

    def chunk_operands(r0):
        rs = pl.ds(r0, CHUNK)
        lc_c = lc_s[prv, rs, :]
        lw_c = lw_s[prv, rs, :]
        l_end = lc_s[prv, pl.ds(r0 + CHUNK - 1, 1), :]
        p_in = jnp.exp(lc_c)
        p_prev = jnp.exp(lc_c - lw_c)
        p_inv = jnp.exp(-lc_c)
        p_end = jnp.exp(l_end - lc_c)
        p_last = jnp.exp(l_end)
        kk_c = kk_s[prv, rs, :]
        b_c = kk_c * a_s[prv, rs, :]
        k_c = k_s[prv, rs, :]
        at = (-kk_c * p_prev).astype(BF16)
        bt = (b_c * p_inv).astype(BF16)
        bh = (b_c * p_end).astype(BF16)
        kt = (k_c * p_inv).astype(BF16)
        kh = (k_c * p_end).astype(BF16)
        rt = (r_s[prv, rs, :] * p_in).astype(BF16)
        vc = v_s[prv, rs, :].astype(BF16)
        per_head = [[x[:, sl] for sl in hs] for x in (at, bt, bh, kt, kh, rt, vc)]
        per_head.append([p_last[:, sl] for sl in hs])
        return per_head

    def group_body(gi, carry):
        g0 = pl.multiple_of(gi * gr, gr)
        rows = pl.ds(g0, gr)
        if vf_out_ref is not None:
            vf_out_ref[rows, :] = v_s[prv, rows, :]
        ops = [chunk_operands(g0 + c * CHUNK) for c in range(group)]
        at_h, bt_h, bh_h, kt_h, kh_h, rt_h, v_h, pl_h = ([x for c in range(group) for x in ops[c][q]]
                                                         for q in range(8))
        heads = range(group * n_heads)
        ar_h = [jnp.concatenate([at_h[h], rt_h[h]], axis=0) for h in heads]
        m_b = [_mm_nt(ar_h[h], bt_h[h]) for h in heads]
        m_k = [_mm_nt(ar_h[h], kt_h[h]) for h in heads]
        n_ab = [jnp.where(strict, m_b[h][:CHUNK], 0.0) for h in heads]
        a_ak = [jnp.where(strict, m_k[h][:CHUNK], 0.0) for h in heads]
        a_rb = [jnp.where(lower, m_b[h][CHUNK:], 0.0) for h in heads]
        a_rk = [jnp.where(lower, m_k[h][CHUNK:], 0.0) for h in heads]
        x_inv = [eye_f + n_ab[h] for h in heads]
        pw = [_mm(n_ab[h], n_ab[h]) for h in heads]
        akv = [_mm(a_ak[h], v_h[h]) for h in heads]
        n_sq = CHUNK.bit_length() - 2
        for it in range(n_sq):
            if it < n_sq - 1:
                st = [_mm(jnp.concatenate([x_inv[h], pw[h]], axis=0), pw[h]) for h in heads]
                x_inv = [x_inv[h] + st[h][:CHUNK] for h in heads]
                pw = [st[h][CHUNK:] for h in heads]
            else:
                st = [_mm(x_inv[h], pw[h]) for h in heads]
                x_inv = [x_inv[h] + st[h] for h in heads]
        w_h = [_mm(x_inv[h], at_h[h]) for h in heads]
        u0 = [_mm(x_inv[h], akv[h]) for h in heads]
        y0 = [_mm(a_rk[h], v_h[h]) + _mm(a_rb[h], u0[h]) for h in heads]
        r_p = [rt_h[h].astype(F32) + _mm(a_rb[h], w_h[h]) for h in heads]
        g_h = [jnp.where(eye, pl_h[h], 0.0) + _mm_tn(w_h[h], bh_h[h]) for h in heads]
        d_h = [_mm_tn(u0[h], bh_h[h]) + _mm_tn(v_h[h], kh_h[h]) for h in heads]
        fresh = prev_first & (gi == 0)
        s_h = [jnp.where(fresh, 0.0, s_ref[h]) for h in range(n_heads)]
        for c in range(group):
            rs = pl.ds(g0 + c * CHUNK, CHUNK)
            idx = [c * n_heads + h for h in range(n_heads)]
            y_h = [y0[i] + _mm_nt(r_p[i], s_h[h]) for h, i in enumerate(idx)]
            s_h = [_mm(s_h[h], g_h[i]) + d_h[i] for h, i in enumerate(idx)]
            y_heads = []
            for h in range(n_heads):
                mean = jnp.mean(y_h[h], axis=-1, keepdims=True)
                yc = y_h[h] - mean
                var = jnp.mean(yc * yc, axis=-1, keepdims=True)
                y_heads.append(yc * lax.rsqrt(var + GN_EPS))
            y_n = jnp.concatenate(y_heads, axis=-1)
            out = (y_n * ln_w + ln_b + bon_s[prv, rs, :]) * g_s[prv, rs, :]
            y_ref[rs, :] = out.astype(y_ref.dtype)
        for h in range(n_heads):
            s_ref[h] = s_h[h]
        token_work(rows)
        return carry

    lax.fori_loop(0, tb // gr, group_body, 0)


def _rwkv(z, v_first, mu, vec, wl, v1, v2, *, batch, seq, n_heads):
    t, n_z = z.shape
    d_r = n_heads * HEAD_DIM
    tb = min(TB_RWKV, seq)
    nb = seq // tb
    n_blocks = batch * nb
    has_vres = v_first is not None
    tok_in = lambda k: (jnp.minimum(k, n_blocks - 1), 0)
    tok_out = lambda k: (jnp.maximum(k - 1, 0), 0)
    const = lambda k: (0, 0)
    in_specs = [pl.BlockSpec((tb, n_z), tok_in)]
    args = [z]
    if has_vres:
        in_specs.append(pl.BlockSpec((tb, d_r), tok_in))
        args.append(v_first)
    n_chunks = tb // CHUNK
    group = RWKV_CHUNK_GROUP if n_chunks % RWKV_CHUNK_GROUP == 0 else 1
    ti = jnp.arange(group * CHUNK)
    tril = ((ti[:, None] // CHUNK == ti[None, :] // CHUNK) & (ti[None, :] <= ti[:, None])).astype(BF16)
    hi = jnp.arange(d_r) // HEAD_DIM
    head_ones = (hi[:, None] == hi[None, :]).astype(BF16)
    in_specs += [pl.BlockSpec(mu.shape, const), pl.BlockSpec(vec.shape, const),
                 pl.BlockSpec(wl.shape, const), pl.BlockSpec(tril.shape, const),
                 pl.BlockSpec(head_ones.shape, const)]
    args += [mu, vec, wl, tril, head_ones]
    if has_vres:
        in_specs += [pl.BlockSpec(v1.shape, const), pl.BlockSpec(v2.shape, const)]
        args += [v1, v2]
        out_shape = jax.ShapeDtypeStruct((t, d_r), BF16)
        out_specs = pl.BlockSpec((tb, d_r), tok_out)
    else:
        out_shape = (jax.ShapeDtypeStruct((t, d_r), BF16), jax.ShapeDtypeStruct((t, d_r), F32))
        out_specs = (pl.BlockSpec((tb, d_r), tok_out), pl.BlockSpec((tb, d_r), tok_out))
    scratch = [pltpu.VMEM((n_heads, HEAD_DIM, HEAD_DIM), F32),
               pltpu.VMEM((V7X_SUBLANES, n_z), F32)]
    scratch += [pltpu.VMEM((2, tb, d_r), F32) for _ in range(9)]
    return pl.pallas_call(
        functools.partial(_rwkv_kernel, has_vres=has_vres, n_heads=n_heads, d_r=d_r, group=group, nb=nb),
        out_shape=out_shape,
        grid=(n_blocks + 1,),
        in_specs=in_specs,
        out_specs=out_specs,
        scratch_shapes=scratch,
        compiler_params=_params("arbitrary"),
        name="rwkv_vres" if has_vres else "rwkv",
    )(*args)


def _attn_kernel(*refs, n_heads, n_parts):
    q_ref = refs[0]
    k_refs = refs[1:1 + n_parts]
    v_refs = refs[1 + n_parts:1 + 2 * n_parts]
    tab_ref = refs[1 + 2 * n_parts]
    o_ref = refs[2 + 2 * n_parts]
    qb = q_ref.shape[0]
    j = pl.program_id(1)
    scale = HEAD_DIM ** -0.5
    q = q_ref[...] * jnp.asarray(scale, q_ref.dtype)
    ks = [r[...] for r in k_refs]
    vs = [r[...] for r in v_refs]
    pw = 2 * HEAD_DIM
    lane = lax.broadcasted_iota(I32, (qb, pw), 1)
    sum_even = jnp.where(lane < HEAD_DIM, 1.0, 0.0).astype(q.dtype)
    sum_odd = jnp.where(lane < HEAD_DIM, 0.0, 1.0).astype(q.dtype)
    zero = jnp.zeros((), q.dtype)
    is_even = sum_even > zero
    outs = []
    for hp in range(n_heads // 2):
        sl = slice(hp * pw, (hp + 1) * pw)
        qq = q[:, sl]
        q_pair = (jnp.where(is_even, qq, zero), jnp.where(is_even, zero, qq))
        s_parts = [[], []]
        for p in range(n_parts):
            kk = ks[p][:, sl]
            back = n_parts - 1 - p
            for u in range(2):
                s = _mm_nt(q_pair[u], kk) + tab_ref[2 * hp + u, :, p * qb:(p + 1) * qb]
                if back > 0:
                    s = jnp.where(j >= back, s, NEG_INF)
                s_parts[u].append(s)
        m = []
        for u in range(2):
            mm = s_parts[u][0]
            for s in s_parts[u][1:]:
                mm = jnp.maximum(mm, s)
            m.append(mm.max(axis=-1, keepdims=True))
        acc = jnp.zeros((qb, 2 * pw), F32)
        for p in range(n_parts):
            vv = vs[p][:, sl]
            rhs = jnp.concatenate(
                [jnp.concatenate([jnp.where(is_even, vv, zero), sum_even], axis=1),
                 jnp.concatenate([jnp.where(is_even, zero, vv), sum_odd], axis=1)], axis=0)
            e = jnp.concatenate([jnp.exp(s_parts[u][p] - m[u]).astype(BF16) for u in range(2)], axis=1)
            acc = acc + jnp.dot(e, rhs, preferred_element_type=F32)
        outs.append(acc[:, :pw] / acc[:, pw:])
    o_ref[...] = jnp.concatenate(outs, axis=-1).astype(o_ref.dtype)


def _attn(qkv, table, *, batch, seq, n_heads):
    t = qkv.shape[0]
    d_a = n_heads * HEAD_DIM
    qb = min(QB_ATTN, seq)
    left = LEFT_CHUNKS * CHUNK
    assert left % qb == 0 and seq % qb == 0
    n_parts = left // qb + 1
    nb = seq // qb
    in_specs = [pl.BlockSpec((qb, d_a), lambda b, j: (b * nb + j, 0))]
    for p in range(n_parts):
        back = n_parts - 1 - p
        in_specs.append(pl.BlockSpec((qb, d_a), lambda b, j, back=back: (b * nb + jnp.maximum(j - back, 0), 1)))
    for p in range(n_parts):
        back = n_parts - 1 - p
        in_specs.append(pl.BlockSpec((qb, d_a), lambda b, j, back=back: (b * nb + jnp.maximum(j - back, 0), 2)))
    in_specs.append(pl.BlockSpec(table.shape, lambda b, j: (0, 0, 0)))
    return pl.pallas_call(
        functools.partial(_attn_kernel, n_heads=n_heads, n_parts=n_parts),
        out_shape=jax.ShapeDtypeStruct((t, d_a), BF16),
        grid=(batch, nb),
        in_specs=in_specs,
        out_specs=pl.BlockSpec((qb, d_a), lambda b, j: (b * nb + j, 0)),
        compiler_params=_params("parallel", "arbitrary"),
        name="attn",
    )(*([qkv] * (1 + 2 * n_parts)), table)


def _attn_table(rel_bias, qb):
    left = LEFT_CHUNKS * CHUNK
    n_keys = left + qb
    period = qb + n_keys - 1
    n_heads = rel_bias.shape[0]
    m = jnp.arange(period)
    rel = left - jnp.where(m < n_keys, m, m - period)
    g = rel_bias[:, jnp.clip(rel, -(CHUNK - 1), MAX_REL) + (CHUNK - 1)].astype(F32)
    flat = jnp.tile(g, (1, qb))[:, :qb * (period - 1)]
    bias = flat.reshape(n_heads, qb, period - 1)[:, :, :n_keys]
    cq = jnp.arange(qb)[:, None] // CHUNK
    ck = jnp.arange(n_keys)[None, :] // CHUNK
    valid = (ck >= cq) & (ck <= cq + LEFT_CHUNKS)
    return jnp.where(valid[None], bias, NEG_INF)


_R_E1, _R_E2, _R_C1, _R_C2, _R_RANK1, _R_RANK2, _R_LRANK1, _R_LRANK2 = range(8)


def _outproj_route_kernel(yr_ref, ya_ref, h_ref, wor_ref, woa_ref, nw_ref, wrt_ref, brt_ref,
                          h1_ref, hn_ref, route_ref, route_t_ref, cnt_ref, tstat_ref, carry_ref):
    i = pl.program_id(0)

    @pl.when(i == 0)
    def _():
        carry_ref[...] = jnp.zeros_like(carry_ref)

    h1 = (h_ref[...] + jnp.dot(yr_ref[...], wor_ref[...], preferred_element_type=F32)
          + jnp.dot(ya_ref[...], woa_ref[...], preferred_element_type=F32))
    h1_ref[...] = h1
    hn = _rms(h1, nw_ref[...])
    _to_token_tiles(hn_ref, hn)
    nl = brt_ref.shape[1]
    hn_hi = hn.astype(BF16)
    hn_lo = (hn - hn_hi.astype(F32)).astype(BF16)
    part = jnp.dot(hn_hi, wrt_ref[...], preferred_element_type=F32)
    logits = (part[:, :nl] + part[:, nl:]
              + jnp.dot(hn_lo, wrt_ref[:, :nl], preferred_element_type=F32) + brt_ref[...])
    tm = logits.shape[0]
    lane = lax.broadcasted_iota(I32, (tm, nl), 1)
    lane_f = lane.astype(F32)
    ninf = -jnp.inf
    big = float(nl)
    is_g = lane < N_GROUPS
    gl = jnp.where(is_g, logits, ninf)
    g_max = gl.max(axis=-1, keepdims=True)
    g_sel = jnp.where(gl == g_max, lane_f, big).min(axis=-1, keepdims=True)
    p_g = 1.0 / jnp.where(is_g, jnp.exp(logits - g_max), 0.0).sum(axis=-1, keepdims=True)
    e_lo = N_GROUPS + EXPERTS_PER_GROUP * g_sel
    in_grp = (lane_f >= e_lo) & (lane_f < e_lo + EXPERTS_PER_GROUP)
    el = jnp.where(in_grp, logits, ninf)
    m1 = el.max(axis=-1, keepdims=True)
    i1 = jnp.where(el == m1, lane_f, big).min(axis=-1, keepdims=True)
    el2 = jnp.where(lane_f == i1, ninf, el)
    m2 = el2.max(axis=-1, keepdims=True)
    i2 = jnp.where(el2 == m2, lane_f, big).min(axis=-1, keepdims=True)
    t2 = jnp.exp(m2 - m1)
    c1 = p_g / (1.0 + t2)
    c2 = p_g * t2 / (1.0 + t2)
    e1 = i1 - N_GROUPS
    e2 = i2 - N_GROUPS
    oh1 = lane_f == e1
    oh2 = lane_f == e2
    ohs = jnp.where(oh1 | oh2, 1.0, 0.0)
    ri = lax.broadcasted_iota(I32, (tm, tm), 0)
    rj = lax.broadcasted_iota(I32, (tm, tm), 1)
    before = jnp.where(rj < ri, 1.0, 0.0).astype(BF16)
    old_carry = carry_ref[0:1, :]
    cnt_tile = jnp.dot(before, ohs.astype(BF16), preferred_element_type=F32)
    cnt = cnt_tile + old_carry
    rank1 = jnp.where(oh1, cnt, 0.0).sum(axis=-1, keepdims=True)
    rank2 = jnp.where(oh2, cnt, 0.0).sum(axis=-1, keepdims=True)
    lrank1 = jnp.where(oh1, cnt_tile, 0.0).sum(axis=-1, keepdims=True)
    lrank2 = jnp.where(oh2, cnt_tile, 0.0).sum(axis=-1, keepdims=True)
    tile_cnt = ohs.sum(axis=0, keepdims=True)
    new_carry = old_carry + tile_cnt
    carry_ref[0:1, :] = new_carry
    cnt_ref[...] = jnp.broadcast_to(new_carry, cnt_ref.shape)
    srow = lax.broadcasted_iota(I32, tstat_ref.shape, 0)
    tstat_ref[...] = jnp.where(srow == 0, tile_cnt, jnp.where(srow == 1, old_carry, 0.0))
    route = jnp.zeros((tm, nl), F32)
    for idx, val in ((_R_E1, e1), (_R_E2, e2), (_R_C1, c1), (_R_C2, c2),
                     (_R_RANK1, rank1), (_R_RANK2, rank2), (_R_LRANK1, lrank1), (_R_LRANK2, lrank2)):
        route = jnp.where(lane == idx, val, route)
    route_ref[...] = route
    route_t_ref[...] = route.T[:route_t_ref.shape[0], :]


def _outproj_route(yr, ya, h, wor, woa, nw, wrt, brt):
    t, d = h.shape
    tm = min(TM_ROUTE, t)
    d_r, d_a = yr.shape[1], ya.shape[1]
    nl = brt.shape[1]
    tok = lambda i: (i, 0)
    const = lambda i: (0, 0)
    return pl.pallas_call(
        _outproj_route_kernel,
        out_shape=(jax.ShapeDtypeStruct((t, d), F32),
                   jax.ShapeDtypeStruct((t * V7X_SUBLANES, V7X_LANES), F32),
                   jax.ShapeDtypeStruct((t, nl), F32), jax.ShapeDtypeStruct((V7X_SUBLANES, t), F32),
                   jax.ShapeDtypeStruct((V7X_SUBLANES, nl), F32),
                   jax.ShapeDtypeStruct((t // tm * V7X_SUBLANES, nl), F32)),
        grid=(t // tm,),
        in_specs=[pl.BlockSpec((tm, d_r), tok), pl.BlockSpec((tm, d_a), tok), pl.BlockSpec((tm, d), tok),
                  pl.BlockSpec((d_r, d), const), pl.BlockSpec((d_a, d), const),
                  pl.BlockSpec((1, d), const), pl.BlockSpec(wrt.shape, const), pl.BlockSpec((1, nl), const)],
        out_specs=(pl.BlockSpec((tm, d), tok), pl.BlockSpec((tm * V7X_SUBLANES, V7X_LANES), tok),
                   pl.BlockSpec((tm, nl), tok), pl.BlockSpec((V7X_SUBLANES, tm), lambda i: (0, i)),
                   pl.BlockSpec((V7X_SUBLANES, nl), const), pl.BlockSpec((V7X_SUBLANES, nl), tok)),
        scratch_shapes=[pltpu.VMEM((V7X_SUBLANES, nl), F32)],
        compiler_params=_params("arbitrary"),
        name="outproj_route",
    )(yr, ya, h, wor, woa, nw, wrt, brt)


def _load_indices(idx_hbm, i, idx_smem, sem):
    n = idx_smem.shape[0]
    cp = pltpu.make_async_copy(idx_hbm.at[pl.ds(pl.multiple_of(i * n, n), n)], idx_smem, sem)
    cp.start()
    cp.wait()


def _tiles(ref, row, n):
    start = row * V7X_SUBLANES
    if not isinstance(row, int):
        start = pl.multiple_of(start, V7X_SUBLANES)
    return ref.at[pl.ds(start, n * V7X_SUBLANES), :]


def _start_run_copies(length, n_bits, copy_of):
    for b in range(n_bits):
        @pl.when(((length >> b) & 1) == 1)
        def _(b=b):
            done = (length >> (b + 1)) << (b + 1)
            copy_of(done, 1 << b, b).start(priority=b % 2)


def _dispatch_kernel(idx_hbm, zinfo_hbm, x_ref, xs_hbm, idx_smem, zinfo_smem, xloc, zero_vmem,
                     idx_sem, run_sem, zero_sem, *, tm, n_free):
    i = pl.program_id(0)
    nb = pl.num_programs(0)
    buf = lax.rem(i, 2)
    zero_rows = zero_vmem.shape[0] // V7X_SUBLANES

    @pl.when(i == 0)
    def _():
        cp = pltpu.make_async_copy(zinfo_hbm, zinfo_smem, idx_sem)
        cp.start()
        cp.wait()
        zero_vmem[...] = jnp.zeros_like(zero_vmem)

        def pad_runs(e, carry):
            start = zinfo_smem[e]
            _start_run_copies(
                zinfo_smem[N_EXPERTS + e], zero_rows.bit_length(),
                lambda done, n, b: pltpu.make_async_copy(_tiles(zero_vmem, 0, n),
                                                         _tiles(xs_hbm, start + done, n), zero_sem))
            return carry

        lax.fori_loop(0, N_EXPERTS, pad_runs, 0)
        tail_start = zinfo_smem[2 * N_EXPERTS]

        def tail_block(n, carry):
            pltpu.make_async_copy(zero_vmem, _tiles(xs_hbm, tail_start + n * zero_rows, zero_rows),
                                  zero_sem).start()
            return carry

        lax.fori_loop(0, zinfo_smem[2 * N_EXPERTS + 1], tail_block, 0)

    _load_indices(idx_hbm, i, idx_smem, idx_sem)

    def wait_runs(b):
        pltpu.make_async_copy(xloc.at[b], xs_hbm.at[pl.ds(0, xloc.shape[1]), :], run_sem.at[b]).wait()

    @pl.when(i >= 2)
    def _():
        wait_runs(buf)

    def place(tt, carry):
        row = x_ref[pl.ds(pl.multiple_of(tt * V7X_SUBLANES, V7X_SUBLANES), V7X_SUBLANES), :]
        for s in range(2):
            lp = idx_smem[s * tm + tt]
            xloc[buf, pl.ds(pl.multiple_of(lp * V7X_SUBLANES, V7X_SUBLANES), V7X_SUBLANES), :] = row
        return carry

    lax.fori_loop(0, tm, place, 0, unroll=8)

    def expert_run(e, carry):
        dst = idx_smem[2 * tm + N_EXPERTS + e]
        off = idx_smem[2 * tm + 2 * N_EXPERTS + e]
        _start_run_copies(
            idx_smem[2 * tm + e], tm.bit_length(),
            lambda done, n, b: pltpu.make_async_copy(_tiles(xloc.at[buf], off + done, n),
                                                     _tiles(xs_hbm, dst + done, n), run_sem.at[buf]))
        return carry

    lax.fori_loop(0, N_EXPERTS, expert_run, 0)

    @pl.when(i == nb - 1)
    def _():
        wait_runs(buf)

        @pl.when(nb >= 2)
        def _():
            wait_runs(1 - buf)

        pltpu.make_async_copy(xs_hbm.at[pl.ds(0, n_free * V7X_SUBLANES), :],
                              xs_hbm.at[pl.ds(0, n_free * V7X_SUBLANES), :], zero_sem).wait()


def _dispatch(x, idx, zinfo, n_rows, tm, n_free):
    t = x.shape[0] // V7X_SUBLANES
    nb = t // tm
    rec = idx.shape[0] // nb
    return pl.pallas_call(
        functools.partial(_dispatch_kernel, tm=tm, n_free=n_free),
        out_shape=jax.ShapeDtypeStruct((n_rows * V7X_SUBLANES, V7X_LANES), x.dtype),
        grid=(nb,),
        in_specs=[pl.BlockSpec(memory_space=pl.ANY), pl.BlockSpec(memory_space=pl.ANY),
                  pl.BlockSpec((tm * V7X_SUBLANES, V7X_LANES), lambda i: (i, 0))],
        out_specs=pl.BlockSpec(memory_space=pl.ANY),
        scratch_shapes=[pltpu.SMEM((rec,), I32), pltpu.SMEM(zinfo.shape, I32),
                        pltpu.VMEM((2, 2 * tm * V7X_SUBLANES, V7X_LANES), x.dtype),
                        pltpu.VMEM((TM_EXPERT // 2 * V7X_SUBLANES, V7X_LANES), x.dtype),
                        pltpu.SemaphoreType.DMA, pltpu.SemaphoreType.DMA((2,)), pltpu.SemaphoreType.DMA],
        compiler_params=_params("arbitrary"),
        name="dispatch",
    )(idx, zinfo, x)


def _experts_kernel(te_ref, tv_ref, tf_ref, x_ref, w1_ref, w3_ref, w2_ref, y_ref,
                    w1_b, w3_b, w2_b, *, tm):
    i = pl.program_id(0)

    @pl.when(tf_ref[i] > 0)
    def _():
        w1_b[...] = w1_ref[...].astype(BF16)
        w3_b[...] = w3_ref[...].astype(BF16)
        w2_b[...] = w2_ref[...].astype(BF16)

    @pl.when(tv_ref[i] > 0)
    def _():
        x = _from_token_tiles(x_ref, tm).astype(BF16)
        h_gate = jnp.dot(x, w1_b[...], preferred_element_type=F32)
        h_up = jnp.dot(x, w3_b[...], preferred_element_type=F32)
        hid = (h_gate * jax.nn.sigmoid(h_gate) * h_up).astype(BF16)
        _to_token_tiles(y_ref, jnp.dot(hid, w2_b[...], preferred_element_type=F32))

    @pl.when(tv_ref[i] == 0)
    def _():
        y_ref[...] = jnp.zeros_like(y_ref)


def _experts(xs, w1, w3, w2, tile_expert, tile_valid, tile_first):
    n_rows = xs.shape[0] // V7X_SUBLANES
    tm = TM_EXPERT
    nt = n_rows // tm
    d, f = w1.shape[1:]
    tile_spec = pl.BlockSpec((tm * V7X_SUBLANES, V7X_LANES), lambda i, te, tv, tf: (i, 0))
    grid_spec = pltpu.PrefetchScalarGridSpec(
        num_scalar_prefetch=3,
        grid=(nt,),
        in_specs=[tile_spec,
                  pl.BlockSpec((None, d, f), lambda i, te, tv, tf: (te[i], 0, 0)),
                  pl.BlockSpec((None, d, f), lambda i, te, tv, tf: (te[i], 0, 0)),
                  pl.BlockSpec((None, f, d), lambda i, te, tv, tf: (te[i], 0, 0))],
        out_specs=tile_spec,
        scratch_shapes=[pltpu.VMEM((d, f), BF16), pltpu.VMEM((d, f), BF16), pltpu.VMEM((f, d), BF16)],
    )
    return pl.pallas_call(
        functools.partial(_experts_kernel, tm=tm),
        out_shape=jax.ShapeDtypeStruct(xs.shape, F32),
        grid_spec=grid_spec,
        compiler_params=_params("arbitrary"),
        name="experts",
    )(tile_expert, tile_valid, tile_first, xs, w1, w3, w2)


def _combine_ple_kernel(pos_hbm, ys_hbm, h_ref, route_ref, p_ref, nw_ref, wg_ref, bg_ref, wp_ref,
                        fw_ref, o_ref, idx_smem, ybuf, idx_sem, row_sem, *, tm, final):
    i = pl.program_id(0)
    nb = pl.num_programs(0)
    cur = lax.rem(i, 2)
    nxt = 1 - cur

    def gather(step, buf):
        _load_indices(pos_hbm, step, idx_smem, idx_sem)

        def issue(tt, carry):
            for s in range(2):
                pltpu.make_async_copy(_token_tile(ys_hbm, idx_smem[s * tm + tt]),
                                      _token_tile(ybuf.at[buf, s], tt), row_sem.at[buf]).start(priority=s)
            return carry

        lax.fori_loop(0, tm, issue, 0, unroll=8)

    def wait_gather(buf):
        for s in range(2):
            pltpu.make_async_copy(ys_hbm.at[pl.ds(0, ybuf.shape[2]), :], ybuf.at[buf, s],
                                  row_sem.at[buf]).wait()

    @pl.when(i == 0)
    def _():
        gather(0, 0)

    @pl.when(i + 1 < nb)
    def _():
        gather(i + 1, nxt)

    wait_gather(cur)

    route = route_ref[...]
    c1 = route[:, _R_C1:_R_C1 + 1]
    c2 = route[:, _R_C2:_R_C2 + 1]
    h2 = (h_ref[...] + c1 * _from_token_tiles(ybuf.at[cur, 0], tm)
          + c2 * _from_token_tiles(ybuf.at[cur, 1], tm))
    hn = _rms(h2, nw_ref[...]).astype(BF16)
    gate = jax.nn.sigmoid(jnp.dot(hn, wg_ref[...], preferred_element_type=F32) + bg_ref[...])
    h3 = h2 + gate * jnp.dot(p_ref[...].astype(BF16), wp_ref[...], preferred_element_type=F32)
    if final:
        h3 = _rms(h3, fw_ref[...])
    o_ref[...] = h3


def _combine_ple(pos_tiles, ys, h, route, p, nw, wg, bg, wp, fw, *, tm, final):
    t, d = h.shape
    nl = route.shape[1]
    dp = p.shape[1]
    tok = lambda i: (i, 0)
    const = lambda i: (0, 0)
    return pl.pallas_call(
        functools.partial(_combine_ple_kernel, tm=tm, final=final),
        out_shape=jax.ShapeDtypeStruct((t, d), F32),
        grid=(t // tm,),
        in_specs=[pl.BlockSpec(memory_space=pl.ANY), pl.BlockSpec(memory_space=pl.ANY),
                  pl.BlockSpec((tm, d), tok), pl.BlockSpec((tm, nl), tok), pl.BlockSpec((tm, dp), tok),
                  pl.BlockSpec((1, d), const), pl.BlockSpec((d, d), const), pl.BlockSpec((1, d), const),
                  pl.BlockSpec((dp, d), const), pl.BlockSpec((1, d), const)],
        out_specs=pl.BlockSpec((tm, d), tok),
        scratch_shapes=[pltpu.SMEM((pos_tiles.shape[0] // (t // tm),), I32),
                        pltpu.VMEM((2, 2, tm * V7X_SUBLANES, V7X_LANES), F32),
                        pltpu.SemaphoreType.DMA, pltpu.SemaphoreType.DMA((2,))],
        compiler_params=_params("arbitrary"),
        name="combine_ple_final" if final else "combine_ple",
    )(pos_tiles, ys, h, route, p, nw, wg, bg, wp, fw)


_SMEM_RECORD_WORDS = 1024


def _index_records(pos1, pos2, tm, extra=None):
    nb = pos1.shape[0] // tm
    parts = [pos1.reshape(nb, tm), pos2.reshape(nb, tm)]
    if extra is not None:
        parts.append(extra.reshape(nb, -1))
    rec = jnp.concatenate(parts, axis=1)
    pad = -rec.shape[1] % _SMEM_RECORD_WORDS
    return jnp.pad(rec, ((0, 0), (0, pad))).reshape(-1)


def _lookup(table, idx):
    ids = jnp.arange(table.shape[0], dtype=I32)
    return jnp.sum(jnp.where(idx[None, :] == ids[:, None], table[:, None], 0), axis=0)


def _bucket(ends, x):
    return jnp.minimum(jnp.sum((x[None, :] >= ends[:, None]).astype(I32), axis=0), ends.shape[0] - 1)


def kernel(x, p, norm_mix_w, w_in, rwkv_mu, rwkv_w0, rwkv_w2, rwkv_a0, rwkv_a2, rwkv_g2, rwkv_k_k, rwkv_k_a, rwkv_r_k, rwkv_ln_w, rwkv_ln_b, rwkv_v0, rwkv_v1, rwkv_v2, att_rel_bias, w_out, norm_ffn_w, router_group_w, router_group_b, router_expert_w, router_expert_b, expert_w1, expert_w3, expert_w2, norm_ple_w, ple_gate_w, ple_gate_b, ple_proj_w, final_norm_w):
    batch, seq, d = x.shape
    depth = w_in.shape[0]
    t = batch * seq
    d_r = rwkv_w0.shape[1]
    n_heads_r = d_r // HEAD_DIM
    n_rwkv_in = rwkv_mu.shape[1]
    d_a = (w_in.shape[2] - n_rwkv_in) // 3
    n_heads_a = d_a // HEAD_DIM
    n_dec, n_iclr, n_gate = rwkv_w2.shape[1], rwkv_a2.shape[1], rwkv_g2.shape[1]
    assert n_dec == n_iclr and n_gate == n_dec + n_iclr
    n_lo = n_dec + n_iclr + n_gate
    f_exp = expert_w1.shape[-1]
    assert d == V7X_SUBLANES * V7X_LANES
    n_rows = 2 * t + N_EXPERTS * TM_EXPERT
    n_tiles = n_rows // TM_EXPERT
    qb = min(QB_ATTN, seq)

    w1_all = expert_w1.reshape(depth * N_EXPERTS, d, f_exp)
    w3_all = expert_w3.reshape(depth * N_EXPERTS, d, f_exp)
    w2_all = expert_w2.reshape(depth * N_EXPERTS, f_exp, d)

    h = x.reshape(t, d)
    v_first = None
    for i in range(depth):
        wr = w_in[i, :, :n_rwkv_in].astype(BF16)
        wa = w_in[i, :, n_rwkv_in:].astype(BF16)
        wl = jnp.zeros((n_lo, 3 * d_r), F32)
        wl = wl.at[:n_dec, :d_r].set(rwkv_w2[i])
        wl = wl.at[n_dec:n_dec + n_iclr, d_r:2 * d_r].set(rwkv_a2[i])
        wl = wl.at[n_dec + n_iclr:, 2 * d_r:].set(rwkv_g2[i]).astype(BF16)
        v0 = rwkv_v0[i - 1] if i > 0 else jnp.zeros((d_r,), F32)
        vec = jnp.stack([rwkv_w0[i], rwkv_a0[i], rwkv_k_k[i], rwkv_k_a[i], rwkv_r_k[i],
                         rwkv_ln_w[i], rwkv_ln_b[i], v0])
        if i > 0:
            n_vr = rwkv_v1.shape[2]
            v1 = jnp.zeros((d_r, V7X_LANES), F32).at[:, :n_vr].set(rwkv_v1[i - 1]).astype(BF16)
            v2 = jnp.zeros((V7X_LANES, d_r), F32).at[:n_vr, :].set(rwkv_v2[i - 1]).astype(BF16)
        else:
            v1 = v2 = None
        table = _attn_table(att_rel_bias[i], qb)
        wor = w_out[i, :d_r].astype(BF16)
        woa = w_out[i, d_r:].astype(BF16)
        n_rt = N_GROUPS + N_EXPERTS
        wrt = jnp.zeros((d, V7X_LANES), F32)
        wrt = wrt.at[:, :N_GROUPS].set(router_group_w[i]).at[:, N_GROUPS:n_rt].set(router_expert_w[i])
        wrt_hi = wrt.astype(BF16)
        wrt = jnp.concatenate([wrt_hi, (wrt - wrt_hi.astype(F32)).astype(BF16)], axis=1)
        brt = jnp.zeros((1, V7X_LANES), F32)
        brt = brt.at[0, :N_GROUPS].set(router_group_b[i]).at[0, N_GROUPS:n_rt].set(router_expert_b[i])

        z_r, qkv = _norm_proj(h, norm_mix_w[i][None], wr, wa)
        if i == 0:
            y_r, v_first = _rwkv(z_r, None, rwkv_mu[i][None], vec, wl, None, None,
                                 batch=batch, seq=seq, n_heads=n_heads_r)
        else:
            y_r = _rwkv(z_r, v_first, rwkv_mu[i][None], vec, wl, v1, v2,
                        batch=batch, seq=seq, n_heads=n_heads_r)
        y_a = _attn(qkv, table, batch=batch, seq=seq, n_heads=n_heads_a)

        h1, hn, route, route_t, cnt, tstat = _outproj_route(y_r, y_a, h, wor, woa, norm_ffn_w[i][None],
                                                            wrt, brt)
        ri = route_t.astype(I32)
        counts = cnt[0, :N_EXPERTS].astype(I32)
        padded = ((counts + TM_EXPERT - 1) // TM_EXPERT) * TM_EXPERT
        p_end = jnp.cumsum(padded)
        p_start = p_end - padded
        pos1 = _lookup(p_start, ri[_R_E1]) + ri[_R_RANK1]
        pos2 = _lookup(p_start, ri[_R_E2]) + ri[_R_RANK2]
        tile_start = jnp.arange(n_tiles, dtype=I32) * TM_EXPERT
        tile_expert = _bucket(p_end, tile_start)
        tile_valid = (tile_start < p_end[-1]).astype(I32)

        tile_first = jnp.concatenate([jnp.ones((1,), I32),
                                      (tile_expert[1:] != tile_expert[:-1]).astype(I32)])

        tm_d = min(TM_DISPATCH, t)
        nb_d = t // tm_d
        assert tm_d == min(TM_ROUTE, t)
        ts = tstat.reshape(nb_d, V7X_SUBLANES, -1)[:, :2, :N_EXPERTS].astype(I32)
        tile_cnt, tile_before = ts[:, 0], ts[:, 1]
        local_start = jnp.cumsum(tile_cnt, axis=1) - tile_cnt
        run_dst = p_start[None, :] + tile_before
        ls_tok = jnp.repeat(local_start.T, tm_d, axis=1)
        ids = jnp.arange(N_EXPERTS, dtype=I32)[:, None]
        lpos1 = jnp.sum(jnp.where(ri[_R_E1][None, :] == ids, ls_tok, 0), axis=0) + ri[_R_LRANK1]
        lpos2 = jnp.sum(jnp.where(ri[_R_E2][None, :] == ids, ls_tok, 0), axis=0) + ri[_R_LRANK2]
        idx_d = _index_records(lpos1, lpos2, tm_d,
                               jnp.concatenate([tile_cnt, run_dst, local_start], axis=1))
        n_free = n_rows - 2 * t
        zero_rows = TM_EXPERT // 2
        zinfo = jnp.concatenate([p_start + counts, padded - counts,
                                 jnp.stack([p_end[-1], (n_rows - p_end[-1]) // zero_rows])])
        zinfo = jnp.pad(zinfo, (0, -zinfo.shape[0] % _SMEM_RECORD_WORDS)).astype(I32)
        xs = _dispatch(hn, idx_d, zinfo, n_rows, tm_d, n_free)
        ys = _experts(xs, w1_all, w3_all, w2_all, tile_expert + i * N_EXPERTS, tile_valid, tile_first)

        tm_c = min(TM_COMBINE, t)
        h = _combine_ple(_index_records(pos1, pos2, tm_c), ys, h1, route, p[i].reshape(t, -1),
                         norm_ple_w[i][None], ple_gate_w[i].astype(BF16), ple_gate_b[i][None],
                         ple_proj_w[i].astype(BF16), final_norm_w[None],
                         tm=tm_c, final=(i == depth - 1))
    return h.reshape(batch, seq, d)
```

```python
import functools

import jax
import jax.numpy as jnp
from jax import lax
from jax.experimental import pallas as pl
from jax.experimental.pallas import tpu as pltpu

F32 = jnp.float32
BF16 = jnp.bfloat16
I32 = jnp.int32

CHUNK = 64
HEAD_DIM = 64
LEFT_CHUNKS = 8
MAX_REL = 256
N_GROUPS = 4
EXPERTS_PER_GROUP = 8
N_EXPERTS = N_GROUPS * EXPERTS_PER_GROUP
RMS_EPS = 1e-6
GN_EPS = 64e-5
NEG_INF = -1e30

V7X_LANES = 128
V7X_SUBLANES = 8
V7X_VMEM_LIMIT_BYTES = 48 * 1024 * 1024

TM_PROJ = 512
TB_RWKV = 512
RWKV_CHUNK_GROUP = 4
QB_ATTN = 256
TM_ROUTE = 512
TM_DISPATCH = 512
TM_EXPERT = 512
TM_COMBINE = 512


def _params(*sem):
    return pltpu.CompilerParams(dimension_semantics=sem, vmem_limit_bytes=V7X_VMEM_LIMIT_BYTES)


def _rms(x, w):
    return x * lax.rsqrt(jnp.mean(x * x, axis=-1, keepdims=True) + RMS_EPS) * w


def _mm(a, b):
    return jnp.dot(a.astype(BF16), b.astype(BF16), preferred_element_type=F32)


def _mm_nt(a, b):
    return lax.dot_general(a.astype(BF16), b.astype(BF16), (((1,), (1,)), ((), ())),
                           preferred_element_type=F32)


def _mm_tn(a, b):
    return lax.dot_general(a.astype(BF16), b.astype(BF16), (((0,), (0,)), ((), ())),
                           preferred_element_type=F32)


def _to_token_tiles(ref, x):
    m, d = x.shape
    for s in range(d // V7X_LANES):
        ref[pl.ds(s, m, stride=V7X_SUBLANES), :] = x[:, s * V7X_LANES:(s + 1) * V7X_LANES]


def _from_token_tiles(ref, m):
    return jnp.concatenate([ref[pl.ds(s, m, stride=V7X_SUBLANES), :] for s in range(V7X_SUBLANES)],
                           axis=-1)


def _token_tile(ref, row):
    return ref.at[pl.ds(pl.multiple_of(row * V7X_SUBLANES, V7X_SUBLANES), V7X_SUBLANES), :]


def _split3(x):
    hi = x.astype(BF16)
    r1 = x - hi.astype(F32)
    mid = r1.astype(BF16)
    lo = (r1 - mid.astype(F32)).astype(BF16)
    return hi, mid, lo


def _mm_exact_lhs(a_bf16, x):
    hi, mid, lo = _split3(x)
    return (jnp.dot(a_bf16, hi, preferred_element_type=F32)
            + jnp.dot(a_bf16, mid, preferred_element_type=F32)
            + jnp.dot(a_bf16, lo, preferred_element_type=F32))


def _mm_split2_rhs(x, b_bf16):
    hi = x.astype(BF16)
    lo = (x - hi.astype(F32)).astype(BF16)
    return (jnp.dot(hi, b_bf16, preferred_element_type=F32)
            + jnp.dot(lo, b_bf16, preferred_element_type=F32))


def _norm_proj_kernel(h_ref, nw_ref, wr_ref, wa_ref, zr_ref, qkv_ref):
    hn = _rms(h_ref[...], nw_ref[...]).astype(BF16)
    zr_ref[...] = jnp.dot(hn, wr_ref[...], preferred_element_type=F32)
    qkv_ref[...] = jnp.dot(hn, wa_ref[...], preferred_element_type=F32).astype(BF16)


def _norm_proj(h, nw, wr, wa):
    t, d = h.shape
    tm = min(TM_PROJ, t)
    n_r, n_a = wr.shape[1], wa.shape[1]
    return pl.pallas_call(
        _norm_proj_kernel,
        out_shape=(jax.ShapeDtypeStruct((t, n_r), F32), jax.ShapeDtypeStruct((t, n_a), BF16)),
        grid=(t // tm,),
        in_specs=[pl.BlockSpec((tm, d), lambda i: (i, 0)),
                  pl.BlockSpec((1, d), lambda i: (0, 0)),
                  pl.BlockSpec((d, n_r), lambda i: (0, 0)),
                  pl.BlockSpec((d, n_a), lambda i: (0, 0))],
        out_specs=(pl.BlockSpec((tm, n_r), lambda i: (i, 0)),
                   pl.BlockSpec((tm, n_a), lambda i: (i, 0))),
        compiler_params=_params("parallel"),
        name="norm_proj",
    )(h, nw, wr, wa)


_V_W0, _V_A0, _V_KK, _V_KA, _V_RK, _V_LNW, _V_LNB, _V_V0 = range(8)


def _rwkv_kernel(*refs, has_vres, n_heads, d_r, group, nb):
    if has_vres:
        (z_ref, vf_ref, mu_ref, vec_ref, wl_ref, tril_ref, ones_ref, v1_ref, v2_ref, y_ref,
         s_ref, carry_ref, r_s, k_s, v_s, kk_s, a_s, lc_s, lw_s, bon_s, g_s) = refs
        vf_out_ref = None
    else:
        (z_ref, mu_ref, vec_ref, wl_ref, tril_ref, ones_ref, y_ref, vf_out_ref,
         s_ref, carry_ref, r_s, k_s, v_s, kk_s, a_s, lc_s, lw_s, bon_s, g_s) = refs
    per_token = (r_s, k_s, v_s, kk_s, a_s, lc_s, lw_s, bon_s, g_s)
    tb = z_ref.shape[0]
    gr = group * CHUNK
    k_step = pl.program_id(0)
    cur = lax.rem(k_step, 2)
    prv = 1 - cur

    @pl.when(k_step == 0)
    def _():
        s_ref[...] = jnp.zeros_like(s_ref)
        for ref in per_token:
            ref[...] = jnp.zeros_like(ref)

    @pl.when(lax.rem(k_step, nb) == 0)
    def _():
        carry_ref[...] = jnp.zeros_like(carry_ref)

    vec = vec_ref[...]

    def vrow(i):
        return vec[i:i + 1, :]

    ln_w = vrow(_V_LNW)
    ln_b = vrow(_V_LNB)
    mu = mu_ref[...]
    head_ones = ones_ref[...]
    tril = tril_ref[...]
    ci = lax.broadcasted_iota(I32, (CHUNK, CHUNK), 0)
    cj = lax.broadcasted_iota(I32, (CHUNK, CHUNK), 1)
    strict = cj < ci
    lower = cj <= ci
    eye = ci == cj
    eye_f = jnp.where(eye, 1.0, 0.0)
    hs = [slice(h * HEAD_DIM, (h + 1) * HEAD_DIM) for h in range(n_heads)]
    prev_first = lax.rem(k_step + nb - 1, nb) == 0

    def token_work(rows):
        z = z_ref[rows, :]
        row = lax.broadcasted_iota(I32, z.shape, 0)
        z_prev = jnp.where(row == 0, carry_ref[0:1, :], pltpu.roll(z, 1, axis=0))
        carry_ref[0:1, :] = z[gr - 1:gr, :]
        zs = z + (z_prev - z) * mu
        r = zs[:, 0:d_r]
        k = zs[:, d_r:2 * d_r]
        v = zs[:, 2 * d_r:3 * d_r]
        lo = zs[:, 3 * d_r:]
        n_lo = lo.shape[1]
        lane = lax.broadcasted_iota(I32, lo.shape, 1)
        lo_act = jnp.where(lane < n_lo // 4, jnp.tanh(lo),
                           jnp.where(lane < n_lo // 2, lo, jax.nn.sigmoid(lo)))
        lo_out = _mm(lo_act, wl_ref[...])
        if has_vres:
            vv = _mm(_mm(v, v1_ref[...]), v2_ref[...])
            v = v + (vf_ref[rows, :] - v) * jax.nn.sigmoid(vrow(_V_V0) + vv)
        w_log = -jax.nn.softplus(-(vrow(_V_W0) + lo_out[:, 0:d_r])) - 0.5
        lw = -jnp.exp(w_log)
        a = jax.nn.sigmoid(vrow(_V_A0) + lo_out[:, d_r:2 * d_r])
        kk = k * vrow(_V_KK)
        kk = kk * lax.rsqrt(jnp.maximum(_mm_split2_rhs(kk * kk, head_ones), 1e-24))
        k2 = k * (1.0 + (a - 1.0) * vrow(_V_KA))
        r_s[cur, rows, :] = r
        k_s[cur, rows, :] = k2
        v_s[cur, rows, :] = v
        kk_s[cur, rows, :] = kk
        a_s[cur, rows, :] = a
        lc_s[cur, rows, :] = _mm_exact_lhs(tril, lw)
        lw_s[cur, rows, :] = lw
        bon_s[cur, rows, :] = _mm_split2_rhs(r * k2 * vrow(_V_RK), head_ones) * v
        g_s[cur, rows, :] = lo_out[:, 2 * d_r:3 * d_r]

    def chunk_operands(r0):
        rs = pl.ds(r0, CHUNK)
        lc_c = lc_s[prv, rs, :]
        lw_c = lw_s[prv, rs, :]
        l_end = lc_s[prv, pl.ds(r0 + CHUNK - 1, 1), :]
        p_in = jnp.exp(lc_c)
        p_prev = jnp.exp(lc_c - lw_c)
        p_inv = jnp.exp(-lc_c)
        p_end = jnp.exp(l_end - lc_c)
        p_last = jnp.exp(l_end)
        kk_c = kk_s[prv, rs, :]
        b_c = kk_c * a_s[prv, rs, :]
        k_c = k_s[prv, rs, :]
        at = (-kk_c * p_prev).astype(BF16)
        bt = (b_c * p_inv).astype(BF16)
        bh = (b_c * p_end).astype(BF16)
        kt = (k_c * p_inv).astype(BF16)
        kh = (k_c * p_end).astype(BF16)
        rt = (r_s[prv, rs, :] * p_in).astype(BF16)
        vc = v_s[prv, rs, :].astype(BF16)
        per_head = [[x[:, sl] for sl in hs] for x in (at, bt, bh, kt, kh, rt, vc)]
        per_head.append([p_last[:, sl] for sl in hs])
        return per_head

    def group_body(gi, carry):
        g0 = pl.multiple_of(gi * gr, gr)
        rows = pl.ds(g0, gr)
        if vf_out_ref is not None:
            vf_out_ref[rows, :] = v_s[prv, rows, :]
        ops = [chunk_operands(g0 + c * CHUNK) for c in range(group)]
        at_h, bt_h, bh_h, kt_h, kh_h, rt_h, v_h, pl_h = ([x for c in range(group) for x in ops[c][q]]
                                                         for q in range(8))
        heads = range(group * n_heads)
        ar_h = [jnp.concatenate([at_h[h], rt_h[h]], axis=0) for h in heads]
        m_b = [_mm_nt(ar_h[h], bt_h[h]) for h in heads]
        m_k = [_mm_nt(ar_h[h], kt_h[h]) for h in heads]
        n_ab = [jnp.where(strict, m_b[h][:CHUNK], 0.0) for h in heads]
        a_ak = [jnp.where(strict, m_k[h][:CHUNK], 0.0) for h in heads]
        a_rb = [jnp.where(lower, m_b[h][CHUNK:], 0.0) for h in heads]
        a_rk = [jnp.where(lower, m_k[h][CHUNK:], 0.0) for h in heads]
        x_inv = [eye_f + n_ab[h] for h in heads]
        pw = [_mm(n_ab[h], n_ab[h]) for h in heads]
        akv = [_mm(a_ak[h], v_h[h]) for h in heads]
        n_sq = CHUNK.bit_length() - 2
        for it in range(n_sq):
            if it < n_sq - 1:
                st = [_mm(jnp.concatenate([x_inv[h], pw[h]], axis=0), pw[h]) for h in heads]
                x_inv = [x_inv[h] + st[h][:CHUNK] for h in heads]
                pw = [st[h][CHUNK:] for h in heads]
            else:
                st = [_mm(x_inv[h], pw[h]) for h in heads]
                x_inv = [x_inv[h] + st[h] for h in heads]
        w_h = [_mm(x_inv[h], at_h[h]) for h in heads]
        u0 = [_mm(x_inv[h], akv[h]) for h in heads]
        y0 = [_mm(a_rk[h], v_h[h]) + _mm(a_rb[h], u0[h]) for h in heads]
        r_p = [rt_h[h].astype(F32) + _mm(a_rb[h], w_h[h]) for h in heads]
        g_h = [jnp.where(eye, pl_h[h], 0.0) + _mm_tn(w_h[h], bh_h[h]) for h in heads]
        d_h = [_mm_tn(u0[h], bh_h[h]) + _mm_tn(v_h[h], kh_h[h]) for h in heads]
        fresh = prev_first & (gi == 0)
        s_h = [jnp.where(fresh, 0.0, s_ref[h]) for h in range(n_heads)]
        for c in range(group):
            rs = pl.ds(g0 + c * CHUNK, CHUNK)
            idx = [c * n_heads + h for h in range(n_heads)]
            y_h = [y0[i] + _mm_nt(r_p[i], s_h[h]) for h, i in enumerate(idx)]
            s_h = [_mm(s_h[h], g_h[i]) + d_h[i] for h, i in enumerate(idx)]
            y_heads = []
            for h in range(n_heads):
                mean = jnp.mean(y_h[h], axis=-1, keepdims=True)
                yc = y_h[h] - mean
                var = jnp.mean(yc * yc, axis=-1, keepdims=True)
                y_heads.append(yc * lax.rsqrt(var + GN_EPS))
            y_n = jnp.concatenate(y_heads, axis=-1)
            out = (y_n * ln_w + ln_b + bon_s[prv, rs, :]) * g_s[prv, rs, :]
            y_ref[rs, :] = out.astype(y_ref.dtype)
        for h in range(n_heads):
            s_ref[h] = s_h[h]
        token_work(rows)
        return carry

    lax.fori_loop(0, tb // gr, group_body, 0)


def _rwkv(z, v_first, mu, vec, wl, v1, v2, *, batch, seq, n_heads):
    t, n_z = z.shape
    d_r = n_heads * HEAD_DIM
    tb = min(TB_RWKV, seq)
    nb = seq // tb
    n_blocks = batch * nb
    has_vres = v_first is not None
    tok_in = lambda k: (jnp.minimum(k, n_blocks - 1), 0)
    tok_out = lambda k: (jnp.maximum(k - 1, 0), 0)
    const = lambda k: (0, 0)
    in_specs = [pl.BlockSpec((tb, n_z), tok_in)]
    args = [z]
    if has_vres:
        in_specs.append(pl.BlockSpec((tb, d_r), tok_in))
        args.append(v_first)
    n_chunks = tb // CHUNK
    group = RWKV_CHUNK_GROUP if n_chunks % RWKV_CHUNK_GROUP == 0 else 1
    ti = jnp.arange(group * CHUNK)
    tril = ((ti[:, None] // CHUNK == ti[None, :] // CHUNK) & (ti[None, :] <= ti[:, None])).astype(BF16)
    hi = jnp.arange(d_r) // HEAD_DIM
    head_ones = (hi[:, None] == hi[None, :]).astype(BF16)
    in_specs += [pl.BlockSpec(mu.shape, const), pl.BlockSpec(vec.shape, const),
                 pl.BlockSpec(wl.shape, const), pl.BlockSpec(tril.shape, const),
                 pl.BlockSpec(head_ones.shape, const)]
    args += [mu, vec, wl, tril, head_ones]
    if has_vres:
        in_specs += [pl.BlockSpec(v1.shape, const), pl.BlockSpec(v2.shape, const)]
        args += [v1, v2]
        out_shape = jax.ShapeDtypeStruct((t, d_r), BF16)
        out_specs = pl.BlockSpec((tb, d_r), tok_out)
    else:
        out_shape = (jax.ShapeDtypeStruct((t, d_r), BF16), jax.ShapeDtypeStruct((t, d_r), F32))
        out_specs = (pl.BlockSpec((tb, d_r), tok_out), pl.BlockSpec((tb, d_r), tok_out))
    scratch = [pltpu.VMEM((n_heads, HEAD_DIM, HEAD_DIM), F32),
               pltpu.VMEM((V7X_SUBLANES, n_z), F32)]
    scratch += [pltpu.VMEM((2, tb, d_r), F32) for _ in range(9)]
    return pl.pallas_call(
        functools.partial(_rwkv_kernel, has_vres=has_vres, n_heads=n_heads, d_r=d_r, group=group, nb=nb),
        out_shape=out_shape,
        grid=(n_blocks + 1,),
        in_specs=in_specs,
        out_specs=out_specs,
        scratch_shapes=scratch,
        compiler_params=_params("arbitrary"),
        name="rwkv_vres" if has_vres else "rwkv",
    )(*args)


def _attn_kernel(*refs, n_heads, n_parts):
    q_ref = refs[0]
    k_refs = refs[1:1 + n_parts]
    v_refs = refs[1 + n_parts:1 + 2 * n_parts]
    tab_ref = refs[1 + 2 * n_parts]
    o_ref = refs[2 + 2 * n_parts]
    qb = q_ref.shape[0]
    j = pl.program_id(1)
    scale = HEAD_DIM ** -0.5
    q = q_ref[...] * jnp.asarray(scale, q_ref.dtype)
    ks = [r[...] for r in k_refs]
    vs = [r[...] for r in v_refs]
    pw = 2 * HEAD_DIM
    lane = lax.broadcasted_iota(I32, (qb, pw), 1)
    sum_even = jnp.where(lane < HEAD_DIM, 1.0, 0.0).astype(q.dtype)
    sum_odd = jnp.where(lane < HEAD_DIM, 0.0, 1.0).astype(q.dtype)
    zero = jnp.zeros((), q.dtype)
    is_even = sum_even > zero
    outs = []
    for hp in range(n_heads // 2):
        sl = slice(hp * pw, (hp + 1) * pw)
        qq = q[:, sl]
        q_pair = (jnp.where(is_even, qq, zero), jnp.where(is_even, zero, qq))
        s_parts = [[], []]
        for p in range(n_parts):
            kk = ks[p][:, sl]
            back = n_parts - 1 - p
            for u in range(2):
                s = _mm_nt(q_pair[u], kk) + tab_ref[2 * hp + u, :, p * qb:(p + 1) * qb]
                if back > 0:
                    s = jnp.where(j >= back, s, NEG_INF)
                s_parts[u].append(s)
        m = []
        for u in range(2):
            mm = s_parts[u][0]
            for s in s_parts[u][1:]:
                mm = jnp.maximum(mm, s)
            m.append(mm.max(axis=-1, keepdims=True))
        acc = jnp.zeros((qb, 2 * pw), F32)
        for p in range(n_parts):
            vv = vs[p][:, sl]
            rhs = jnp.concatenate(
                [jnp.concatenate([jnp.where(is_even, vv, zero), sum_even], axis=1),
                 jnp.concatenate([jnp.where(is_even, zero, vv), sum_odd], axis=1)], axis=0)
            e = jnp.concatenate([jnp.exp(s_parts[u][p] - m[u]).astype(BF16) for u in range(2)], axis=1)
            acc = acc + jnp.dot(e, rhs, preferred_element_type=F32)
        outs.append(acc[:, :pw] / acc[:, pw:])
    o_ref[...] = jnp.concatenate(outs, axis=-1).astype(o_ref.dtype)


def _attn(qkv, table, *, batch, seq, n_heads):
    t = qkv.shape[0]
    d_a = n_heads * HEAD_DIM
    qb = min(QB_ATTN, seq)
    left = LEFT_CHUNKS * CHUNK
    assert left % qb == 0 and seq % qb == 0
    n_parts = left // qb + 1
    nb = seq // qb
    in_specs = [pl.BlockSpec((qb, d_a), lambda b, j: (b * nb + j, 0))]
    for p in range(n_parts):
        back = n_parts - 1 - p
        in_specs.append(pl.BlockSpec((qb, d_a), lambda b, j, back=back: (b * nb + jnp.maximum(j - back, 0), 1)))
    for p in range(n_parts):
        back = n_parts - 1 - p
        in_specs.append(pl.BlockSpec((qb, d_a), lambda b, j, back=back: (b * nb + jnp.maximum(j - back, 0), 2)))
    in_specs.append(pl.BlockSpec(table.shape, lambda b, j: (0, 0, 0)))
    return pl.pallas_call(
        functools.partial(_attn_kernel, n_heads=n_heads, n_parts=n_parts),
        out_shape=jax.ShapeDtypeStruct((t, d_a), BF16),
        grid=(batch, nb),
        in_specs=in_specs,
        out_specs=pl.BlockSpec((qb, d_a), lambda b, j: (b * nb + j, 0)),
        compiler_params=_params("parallel", "arbitrary"),
        name="attn",
    )(*([qkv] * (1 + 2 * n_parts)), table)


def _attn_table(rel_bias, qb):
    left = LEFT_CHUNKS * CHUNK
    n_keys = left + qb
    period = qb + n_keys - 1
    n_heads = rel_bias.shape[0]
    m = jnp.arange(period)
    rel = left - jnp.where(m < n_keys, m, m - period)
    g = rel_bias[:, jnp.clip(rel, -(CHUNK - 1), MAX_REL) + (CHUNK - 1)].astype(F32)
    flat = jnp.tile(g, (1, qb))[:, :qb * (period - 1)]
    bias = flat.reshape(n_heads, qb, period - 1)[:, :, :n_keys]
    cq = jnp.arange(qb)[:, None] // CHUNK
    ck = jnp.arange(n_keys)[None, :] // CHUNK
    valid = (ck >= cq) & (ck <= cq + LEFT_CHUNKS)
    return jnp.where(valid[None], bias, NEG_INF)


_R_E1, _R_E2, _R_C1, _R_C2, _R_RANK1, _R_RANK2, _R_LRANK1, _R_LRANK2 = range(8)


def _outproj_route_kernel(yr_ref, ya_ref, h_ref, wor_ref, woa_ref, nw_ref, wrt_ref, brt_ref,
                          h1_ref, hn_ref, route_ref, route_t_ref, cnt_ref, tstat_ref, carry_ref):
    i = pl.program_id(0)

    @pl.when(i == 0)
    def _():
        carry_ref[...] = jnp.zeros_like(carry_ref)

    h1 = (h_ref[...] + jnp.dot(yr_ref[...], wor_ref[...], preferred_element_type=F32)
          + jnp.dot(ya_ref[...], woa_ref[...], preferred_element_type=F32))
    h1_ref[...] = h1
    hn = _rms(h1, nw_ref[...])
    _to_token_tiles(hn_ref, hn)
    nl = brt_ref.shape[1]
    hn_hi = hn.astype(BF16)
    hn_lo = (hn - hn_hi.astype(F32)).astype(BF16)
    part = jnp.dot(hn_hi, wrt_ref[...], preferred_element_type=F32)
    logits = (part[:, :nl] + part[:, nl:]
              + jnp.dot(hn_lo, wrt_ref[:, :nl], preferred_element_type=F32) + brt_ref[...])
    tm = logits.shape[0]
    lane = lax.broadcasted_iota(I32, (tm, nl), 1)
    lane_f = lane.astype(F32)
    ninf = -jnp.inf
    big = float(nl)
    is_g = lane < N_GROUPS
    gl = jnp.where(is_g, logits, ninf)
    g_max = gl.max(axis=-1, keepdims=True)
    g_sel = jnp.where(gl == g_max, lane_f, big).min(axis=-1, keepdims=True)
    p_g = 1.0 / jnp.where(is_g, jnp.exp(logits - g_max), 0.0).sum(axis=-1, keepdims=True)
    e_lo = N_GROUPS + EXPERTS_PER_GROUP * g_sel
    in_grp = (lane_f >= e_lo) & (lane_f < e_lo + EXPERTS_PER_GROUP)
    el = jnp.where(in_grp, logits, ninf)
    m1 = el.max(axis=-1, keepdims=True)
    i1 = jnp.where(el == m1, lane_f, big).min(axis=-1, keepdims=True)
    el2 = jnp.where(lane_f == i1, ninf, el)
    m2 = el2.max(axis=-1, keepdims=True)
    i2 = jnp.where(el2 == m2, lane_f, big).min(axis=-1, keepdims=True)
    t2 = jnp.exp(m2 - m1)
    c1 = p_g / (1.0 + t2)
    c2 = p_g * t2 / (1.0 + t2)
    e1 = i1 - N_GROUPS
    e2 = i2 - N_GROUPS
    oh1 = lane_f == e1
    oh2 = lane_f == e2
    ohs = jnp.where(oh1 | oh2, 1.0, 0.0)
    ri = lax.broadcasted_iota(I32, (tm, tm), 0)
    rj = lax.broadcasted_iota(I32, (tm, tm), 1)
    before = jnp.where(rj < ri, 1.0, 0.0).astype(BF16)
    old_carry = carry_ref[0:1, :]
    cnt_tile = jnp.dot(before, ohs.astype(BF16), preferred_element_type=F32)
    cnt = cnt_tile + old_carry
    rank1 = jnp.where(oh1, cnt, 0.0).sum(axis=-1, keepdims=True)
    rank2 = jnp.where(oh2, cnt, 0.0).sum(axis=-1, keepdims=True)
    lrank1 = jnp.where(oh1, cnt_tile, 0.0).sum(axis=-1, keepdims=True)
    lrank2 = jnp.where(oh2, cnt_tile, 0.0).sum(axis=-1, keepdims=True)
    tile_cnt = ohs.sum(axis=0, keepdims=True)
    new_carry = old_carry + tile_cnt
    carry_ref[0:1, :] = new_carry
    cnt_ref[...] = jnp.broadcast_to(new_carry, cnt_ref.shape)
    srow = lax.broadcasted_iota(I32, tstat_ref.shape, 0)
    tstat_ref[...] = jnp.where(srow == 0, tile_cnt, jnp.where(srow == 1, old_carry, 0.0))
    route = jnp.zeros((tm, nl), F32)
    for idx, val in ((_R_E1, e1), (_R_E2, e2), (_R_C1, c1), (_R_C2, c2),
                     (_R_RANK1, rank1), (_R_RANK2, rank2), (_R_LRANK1, lrank1), (_R_LRANK2, lrank2)):
        route = jnp.where(lane == idx, val, route)
    route_ref[...] = route
    route_t_ref[...] = route.T[:route_t_ref.shape[0], :]


def _outproj_route(yr, ya, h, wor, woa, nw, wrt, brt):
    t, d = h.shape
    tm = min(TM_ROUTE, t)
    d_r, d_a = yr.shape[1], ya.shape[1]
    nl = brt.shape[1]
    tok = lambda i: (i, 0)
    const = lambda i: (0, 0)
    return pl.pallas_call(
        _outproj_route_kernel,
        out_shape=(jax.ShapeDtypeStruct((t, d), F32),
                   jax.ShapeDtypeStruct((t * V7X_SUBLANES, V7X_LANES), F32),
                   jax.ShapeDtypeStruct((t, nl), F32), jax.ShapeDtypeStruct((V7X_SUBLANES, t), F32),
                   jax.ShapeDtypeStruct((V7X_SUBLANES, nl), F32),
                   jax.ShapeDtypeStruct((t // tm * V7X_SUBLANES, nl), F32)),
        grid=(t // tm,),
        in_specs=[pl.BlockSpec((tm, d_r), tok), pl.BlockSpec((tm, d_a), tok), pl.BlockSpec((tm, d), tok),
                  pl.BlockSpec((d_r, d), const), pl.BlockSpec((d_a, d), const),
                  pl.BlockSpec((1, d), const), pl.BlockSpec(wrt.shape, const), pl.BlockSpec((1, nl), const)],
        out_specs=(pl.BlockSpec((tm, d), tok), pl.BlockSpec((tm * V7X_SUBLANES, V7X_LANES), tok),
                   pl.BlockSpec((tm, nl), tok), pl.BlockSpec((V7X_SUBLANES, tm), lambda i: (0, i)),
                   pl.BlockSpec((V7X_SUBLANES, nl), const), pl.BlockSpec((V7X_SUBLANES, nl), tok)),
        scratch_shapes=[pltpu.VMEM((V7X_SUBLANES, nl), F32)],
        compiler_params=_params("arbitrary"),
        name="outproj_route",
    )(yr, ya, h, wor, woa, nw, wrt, brt)


def _load_indices(idx_hbm, i, idx_smem, sem):
    n = idx_smem.shape[0]
    cp = pltpu.make_async_copy(idx_hbm.at[pl.ds(pl.multiple_of(i * n, n), n)], idx_smem, sem)
    cp.start()
    cp.wait()


def _tiles(ref, row, n):
    start = row * V7X_SUBLANES
    if not isinstance(row, int):
        start = pl.multiple_of(start, V7X_SUBLANES)
    return ref.at[pl.ds(start, n * V7X_SUBLANES), :]


def _start_run_copies(length, n_bits, copy_of):
    for b in range(n_bits):
        @pl.when(((length >> b) & 1) == 1)
        def _(b=b):
            done = (length >> (b + 1)) << (b + 1)
            copy_of(done, 1 << b, b).start(priority=b % 2)


def _dispatch_kernel(idx_hbm, zinfo_hbm, x_ref, xs_hbm, idx_smem, zinfo_smem, xloc, zero_vmem,
                     idx_sem, run_sem, zero_sem, *, tm, n_free):
    i = pl.program_id(0)
    nb = pl.num_programs(0)
    buf = lax.rem(i, 2)
    zero_rows = zero_vmem.shape[0] // V7X_SUBLANES

    @pl.when(i == 0)
    def _():
        cp = pltpu.make_async_copy(zinfo_hbm, zinfo_smem, idx_sem)
        cp.start()
        cp.wait()
        zero_vmem[...] = jnp.zeros_like(zero_vmem)

        def pad_runs(e, carry):
            start = zinfo_smem[e]
            _start_run_copies(
                zinfo_smem[N_EXPERTS + e], zero_rows.bit_length(),
                lambda done, n, b: pltpu.make_async_copy(_tiles(zero_vmem, 0, n),
                                                         _tiles(xs_hbm, start + done, n), zero_sem))
            return carry

        lax.fori_loop(0, N_EXPERTS, pad_runs, 0)
        tail_start = zinfo_smem[2 * N_EXPERTS]

        def tail_block(n, carry):
            pltpu.make_async_copy(zero_vmem, _tiles(xs_hbm, tail_start + n * zero_rows, zero_rows),
                                  zero_sem).start()
            return carry

        lax.fori_loop(0, zinfo_smem[2 * N_EXPERTS + 1], tail_block, 0)

    _load_indices(idx_hbm, i, idx_smem, idx_sem)

    def wait_runs(b):
        pltpu.make_async_copy(xloc.at[b], xs_hbm.at[pl.ds(0, xloc.shape[1]), :], run_sem.at[b]).wait()

    @pl.when(i >= 2)
    def _():
        wait_runs(buf)

    def place(tt, carry):
        row = x_ref[pl.ds(pl.multiple_of(tt * V7X_SUBLANES, V7X_SUBLANES), V7X_SUBLANES), :]
        for s in range(2):
            lp = idx_smem[s * tm + tt]
            xloc[buf, pl.ds(pl.multiple_of(lp * V7X_SUBLANES, V7X_SUBLANES), V7X_SUBLANES), :] = row
        return carry

    lax.fori_loop(0, tm, place, 0, unroll=8)

    def expert_run(e, carry):
        dst = idx_smem[2 * tm + N_EXPERTS + e]
        off = idx_smem[2 * tm + 2 * N_EXPERTS + e]
        _start_run_copies(
            idx_smem[2 * tm + e], tm.bit_length(),
            lambda done, n, b: pltpu.make_async_copy(_tiles(xloc.at[buf], off + done, n),
                                                     _tiles(xs_hbm, dst + done, n), run_sem.at[buf]))
        return carry

    lax.fori_loop(0, N_EXPERTS, expert_run, 0)

    @pl.when(i == nb - 1)
    def _():
        wait_runs(buf)

        @pl.when(nb >= 2)
        def _():
            wait_runs(1 - buf)

        pltpu.make_async_copy(xs_hbm.at[pl.ds(0, n_free * V7X_SUBLANES), :],
                              xs_hbm.at[pl.ds(0, n_free * V7X_SUBLANES), :], zero_sem).wait()


def _dispatch(x, idx, zinfo, n_rows, tm, n_free):
    t = x.shape[0] // V7X_SUBLANES
    nb = t // tm
    rec = idx.shape[0] // nb
    return pl.pallas_call(
        functools.partial(_dispatch_kernel, tm=tm, n_free=n_free),
        out_shape=jax.ShapeDtypeStruct((n_rows * V7X_SUBLANES, V7X_LANES), x.dtype),
        grid=(nb,),
        in_specs=[pl.BlockSpec(memory_space=pl.ANY), pl.BlockSpec(memory_space=pl.ANY),
                  pl.BlockSpec((tm * V7X_SUBLANES, V7X_LANES), lambda i: (i, 0))],
        out_specs=pl.BlockSpec(memory_space=pl.ANY),
        scratch_shapes=[pltpu.SMEM((rec,), I32), pltpu.SMEM(zinfo.shape, I32),
                        pltpu.VMEM((2, 2 * tm * V7X_SUBLANES, V7X_LANES), x.dtype),
                        pltpu.VMEM((TM_EXPERT // 2 * V7X_SUBLANES, V7X_LANES), x.dtype),
                        pltpu.SemaphoreType.DMA, pltpu.SemaphoreType.DMA((2,)), pltpu.SemaphoreType.DMA],
        compiler_params=_params("arbitrary"),
        name="dispatch",
    )(idx, zinfo, x)


def _experts_kernel(te_ref, tv_ref, tf_ref, x_ref, w1_ref, w3_ref, w2_ref, y_ref,
                    w1_b, w3_b, w2_b, *, tm):
    i = pl.program_id(0)

    @pl.when(tf_ref[i] > 0)
    def _():
        w1_b[...] = w1_ref[...].astype(BF16)
        w3_b[...] = w3_ref[...].astype(BF16)
        w2_b[...] = w2_ref[...].astype(BF16)

    @pl.when(tv_ref[i] > 0)
    def _():
        x = _from_token_tiles(x_ref, tm).astype(BF16)
        h_gate = jnp.dot(x, w1_b[...], preferred_element_type=F32)
        h_up = jnp.dot(x, w3_b[...], preferred_element_type=F32)
        hid = (h_gate * jax.nn.sigmoid(h_gate) * h_up).astype(BF16)
        _to_token_tiles(y_ref, jnp.dot(hid, w2_b[...], preferred_element_type=F32))

    @pl.when(tv_ref[i] == 0)
    def _():
        y_ref[...] = jnp.zeros_like(y_ref)


def _experts(xs, w1, w3, w2, tile_expert, tile_valid, tile_first):
    n_rows = xs.shape[0] // V7X_SUBLANES
    tm = TM_EXPERT
    nt = n_rows // tm
    d, f = w1.shape[1:]
    tile_spec = pl.BlockSpec((tm * V7X_SUBLANES, V7X_LANES), lambda i, te, tv, tf: (i, 0))
    grid_spec = pltpu.PrefetchScalarGridSpec(
        num_scalar_prefetch=3,
        grid=(nt,),
        in_specs=[tile_spec,
                  pl.BlockSpec((None, d, f), lambda i, te, tv, tf: (te[i], 0, 0)),
                  pl.BlockSpec((None, d, f), lambda i, te, tv, tf: (te[i], 0, 0)),
                  pl.BlockSpec((None, f, d), lambda i, te, tv, tf: (te[i], 0, 0))],
        out_specs=tile_spec,
        scratch_shapes=[pltpu.VMEM((d, f), BF16), pltpu.VMEM((d, f), BF16), pltpu.VMEM((f, d), BF16)],
    )
    return pl.pallas_call(
        functools.partial(_experts_kernel, tm=tm),
        out_shape=jax.ShapeDtypeStruct(xs.shape, F32),
        grid_spec=grid_spec,
        compiler_params=_params("arbitrary"),
        name="experts",
    )(tile_expert, tile_valid, tile_first, xs, w1, w3, w2)


def _combine_ple_kernel(pos_hbm, ys_hbm, h_ref, route_ref, p_ref, nw_ref, wg_ref, bg_ref, wp_ref,
                        fw_ref, o_ref, idx_smem, ybuf, idx_sem, row_sem, *, tm, final):
    i = pl.program_id(0)
    nb = pl.num_programs(0)
    cur = lax.rem(i, 2)
    nxt = 1 - cur

    def gather(step, buf):
        _load_indices(pos_hbm, step, idx_smem, idx_sem)

        def issue(tt, carry):
            for s in range(2):
                pltpu.make_async_copy(_token_tile(ys_hbm, idx_smem[s * tm + tt]),
                                      _token_tile(ybuf.at[buf, s], tt), row_sem.at[buf]).start(priority=s)
            return carry

        lax.fori_loop(0, tm, issue, 0, unroll=8)

    def wait_gather(buf):
        for s in range(2):
            pltpu.make_async_copy(ys_hbm.at[pl.ds(0, ybuf.shape[2]), :], ybuf.at[buf, s],
                                  row_sem.at[buf]).wait()

    @pl.when(i == 0)
    def _():
        gather(0, 0)

    @pl.when(i + 1 < nb)
    def _():
        gather(i + 1, nxt)

    wait_gather(cur)

    route = route_ref[...]
    c1 = route[:, _R_C1:_R_C1 + 1]
    c2 = route[:, _R_C2:_R_C2 + 1]
    h2 = (h_ref[...] + c1 * _from_token_tiles(ybuf.at[cur, 0], tm)
          + c2 * _from_token_tiles(ybuf.at[cur, 1], tm))
    hn = _rms(h2, nw_ref[...]).astype(BF16)
    gate = jax.nn.sigmoid(jnp.dot(hn, wg_ref[...], preferred_element_type=F32) + bg_ref[...])
    h3 = h2 + gate * jnp.dot(p_ref[...].astype(BF16), wp_ref[...], preferred_element_type=F32)
    if final:
        h3 = _rms(h3, fw_ref[...])
    o_ref[...] = h3


def _combine_ple(pos_tiles, ys, h, route, p, nw, wg, bg, wp, fw, *, tm, final, layer):
    t, d = h.shape
    nl = route.shape[1]
    dp = p.shape[1]
    tok = lambda i: (i, 0)
    const = lambda i: (0, 0)
    return pl.pallas_call(
        functools.partial(_combine_ple_kernel, tm=tm, final=final),
        out_shape=jax.ShapeDtypeStruct((t, d), F32),
        grid=(t // tm,),
        in_specs=[pl.BlockSpec(memory_space=pl.ANY), pl.BlockSpec(memory_space=pl.ANY),
                  pl.BlockSpec((tm, d), tok), pl.BlockSpec((tm, nl), tok),
                  pl.BlockSpec((tm, dp), lambda i: (i + layer * (t // tm), 0)),
                  pl.BlockSpec((1, d), const), pl.BlockSpec((d, d), const), pl.BlockSpec((1, d), const),
                  pl.BlockSpec((dp, d), const), pl.BlockSpec((1, d), const)],
        out_specs=pl.BlockSpec((tm, d), tok),
        scratch_shapes=[pltpu.SMEM((pos_tiles.shape[0] // (t // tm),), I32),
                        pltpu.VMEM((2, 2, tm * V7X_SUBLANES, V7X_LANES), F32),
                        pltpu.SemaphoreType.DMA, pltpu.SemaphoreType.DMA((2,))],
        compiler_params=_params("arbitrary"),
        name="combine_ple_final" if final else "combine_ple",
    )(pos_tiles, ys, h, route, p, nw, wg, bg, wp, fw)


_SMEM_RECORD_WORDS = 1024


def _index_records(pos1, pos2, tm, extra=None):
    nb = pos1.shape[0] // tm
    parts = [pos1.reshape(nb, tm), pos2.reshape(nb, tm)]
    if extra is not None:
        parts.append(extra.reshape(nb, -1))
    rec = jnp.concatenate(parts, axis=1)
    pad = -rec.shape[1] % _SMEM_RECORD_WORDS
    return jnp.pad(rec, ((0, 0), (0, pad))).reshape(-1)


def _lookup(table, idx):
    ids = jnp.arange(table.shape[0], dtype=I32)
    return jnp.sum(jnp.where(idx[None, :] == ids[:, None], table[:, None], 0), axis=0)


def _bucket(ends, x):
    return jnp.minimum(jnp.sum((x[None, :] >= ends[:, None]).astype(I32), axis=0), ends.shape[0] - 1)


def kernel(x, p, norm_mix_w, w_in, rwkv_mu, rwkv_w0, rwkv_w2, rwkv_a0, rwkv_a2, rwkv_g2, rwkv_k_k, rwkv_k_a, rwkv_r_k, rwkv_ln_w, rwkv_ln_b, rwkv_v0, rwkv_v1, rwkv_v2, att_rel_bias, w_out, norm_ffn_w, router_group_w, router_group_b, router_expert_w, router_expert_b, expert_w1, expert_w3, expert_w2, norm_ple_w, ple_gate_w, ple_gate_b, ple_proj_w, final_norm_w):
    batch, seq, d = x.shape
    depth = w_in.shape[0]
    t = batch * seq
    d_r = rwkv_w0.shape[1]
    n_heads_r = d_r // HEAD_DIM
    n_rwkv_in = rwkv_mu.shape[1]
    d_a = (w_in.shape[2] - n_rwkv_in) // 3
    n_heads_a = d_a // HEAD_DIM
    n_dec, n_iclr, n_gate = rwkv_w2.shape[1], rwkv_a2.shape[1], rwkv_g2.shape[1]
    assert n_dec == n_iclr and n_gate == n_dec + n_iclr
    n_lo = n_dec + n_iclr + n_gate
    f_exp = expert_w1.shape[-1]
    assert d == V7X_SUBLANES * V7X_LANES
    n_rows = 2 * t + N_EXPERTS * TM_EXPERT
    n_tiles = n_rows // TM_EXPERT
    qb = min(QB_ATTN, seq)

    w1_all = expert_w1.reshape(depth * N_EXPERTS, d, f_exp)
    w3_all = expert_w3.reshape(depth * N_EXPERTS, d, f_exp)
    w2_all = expert_w2.reshape(depth * N_EXPERTS, f_exp, d)

    h = x.reshape(t, d)
    v_first = None
    for i in range(depth):
        wr = w_in[i, :, :n_rwkv_in].astype(BF16)
        wa = w_in[i, :, n_rwkv_in:].astype(BF16)
        wl = jnp.zeros((n_lo, 3 * d_r), F32)
        wl = wl.at[:n_dec, :d_r].set(rwkv_w2[i])
        wl = wl.at[n_dec:n_dec + n_iclr, d_r:2 * d_r].set(rwkv_a2[i])
        wl = wl.at[n_dec + n_iclr:, 2 * d_r:].set(rwkv_g2[i]).astype(BF16)
        v0 = rwkv_v0[i - 1] if i > 0 else jnp.zeros((d_r,), F32)
        vec = jnp.stack([rwkv_w0[i], rwkv_a0[i], rwkv_k_k[i], rwkv_k_a[i], rwkv_r_k[i],
                         rwkv_ln_w[i], rwkv_ln_b[i], v0])
        if i > 0:
            n_vr = rwkv_v1.shape[2]
            v1 = jnp.zeros((d_r, V7X_LANES), F32).at[:, :n_vr].set(rwkv_v1[i - 1]).astype(BF16)
            v2 = jnp.zeros((V7X_LANES, d_r), F32).at[:n_vr, :].set(rwkv_v2[i - 1]).astype(BF16)
        else:
            v1 = v2 = None
        table = _attn_table(att_rel_bias[i], qb)
        wor = w_out[i, :d_r].astype(BF16)
        woa = w_out[i, d_r:].astype(BF16)
        n_rt = N_GROUPS + N_EXPERTS
        wrt = jnp.zeros((d, V7X_LANES), F32)
        wrt = wrt.at[:, :N_GROUPS].set(router_group_w[i]).at[:, N_GROUPS:n_rt].set(router_expert_w[i])
        wrt_hi = wrt.astype(BF16)
        wrt = jnp.concatenate([wrt_hi, (wrt - wrt_hi.astype(F32)).astype(BF16)], axis=1)
        brt = jnp.zeros((1, V7X_LANES), F32)
        brt = brt.at[0, :N_GROUPS].set(router_group_b[i]).at[0, N_GROUPS:n_rt].set(router_expert_b[i])

        z_r, qkv = _norm_proj(h, norm_mix_w[i][None], wr, wa)
        if i == 0:
            y_r, v_first = _rwkv(z_r, None, rwkv_mu[i][None], vec, wl, None, None,
                                 batch=batch, seq=seq, n_heads=n_heads_r)
        else:
            y_r = _rwkv(z_r, v_first, rwkv_mu[i][None], vec, wl, v1, v2,
                        batch=batch, seq=seq, n_heads=n_heads_r)
        y_a = _attn(qkv, table, batch=batch, seq=seq, n_heads=n_heads_a)

        h1, hn, route, route_t, cnt, tstat = _outproj_route(y_r, y_a, h, wor, woa, norm_ffn_w[i][None],
                                                            wrt, brt)
        ri = route_t.astype(I32)
        counts = cnt[0, :N_EXPERTS].astype(I32)
        padded = ((counts + TM_EXPERT - 1) // TM_EXPERT) * TM_EXPERT
        p_end = jnp.cumsum(padded)
        p_start = p_end - padded
        pos1 = _lookup(p_start, ri[_R_E1]) + ri[_R_RANK1]
        pos2 = _lookup(p_start, ri[_R_E2]) + ri[_R_RANK2]
        tile_start = jnp.arange(n_tiles, dtype=I32) * TM_EXPERT
        tile_expert = _bucket(p_end, tile_start)
        tile_valid = (tile_start < p_end[-1]).astype(I32)

        tile_first = jnp.concatenate([jnp.ones((1,), I32),
                                      (tile_expert[1:] != tile_expert[:-1]).astype(I32)])

        tm_d = min(TM_DISPATCH, t)
        nb_d = t // tm_d
        assert tm_d == min(TM_ROUTE, t)
        ts = tstat.reshape(nb_d, V7X_SUBLANES, -1)[:, :2, :N_EXPERTS].astype(I32)
        tile_cnt, tile_before = ts[:, 0], ts[:, 1]
        local_start = jnp.cumsum(tile_cnt, axis=1) - tile_cnt
        run_dst = p_start[None, :] + tile_before
        ls_tok = jnp.repeat(local_start.T, tm_d, axis=1)
        ids = jnp.arange(N_EXPERTS, dtype=I32)[:, None]
        lpos1 = jnp.sum(jnp.where(ri[_R_E1][None, :] == ids, ls_tok, 0), axis=0) + ri[_R_LRANK1]
        lpos2 = jnp.sum(jnp.where(ri[_R_E2][None, :] == ids, ls_tok, 0), axis=0) + ri[_R_LRANK2]
        idx_d = _index_records(lpos1, lpos2, tm_d,
                               jnp.concatenate([tile_cnt, run_dst, local_start], axis=1))
        n_free = n_rows - 2 * t
        zero_rows = TM_EXPERT // 2
        zinfo = jnp.concatenate([p_start + counts, padded - counts,
                                 jnp.stack([p_end[-1], (n_rows - p_end[-1]) // zero_rows])])
        zinfo = jnp.pad(zinfo, (0, -zinfo.shape[0] % _SMEM_RECORD_WORDS)).astype(I32)
        xs = _dispatch(hn, idx_d, zinfo, n_rows, tm_d, n_free)
        ys = _experts(xs, w1_all, w3_all, w2_all, tile_expert + i * N_EXPERTS, tile_valid, tile_first)

        tm_c = min(TM_COMBINE, t)
        h = _combine_ple(_index_records(pos1, pos2, tm_c), ys, h1, route, p.reshape(depth * t, -1),
                         norm_ple_w[i][None], ple_gate_w[i].astype(BF16), ple_gate_b[i][None],
                         ple_proj_w[i].astype(BF16), final_norm_w[None],
                         tm=tm_c, final=(i == depth - 1), layer=i)
    return h.reshape(batch, seq, d)
```

```python
import functools

import jax
import jax.numpy as jnp
from jax import lax
from jax.experimental import pallas as pl
from jax.experimental.pallas import tpu as pltpu

F32 = jnp.float32
BF16 = jnp.bfloat16
I32 = jnp.int32

CHUNK = 64
HEAD_DIM = 64
LEFT_CHUNKS = 8
MAX_REL = 256
N_GROUPS = 4
EXPERTS_PER_GROUP = 8
N_EXPERTS = N_GROUPS * EXPERTS_PER_GROUP
RMS_EPS = 1e-6
GN_EPS = 64e-5
NEG_INF = -1e30

V7X_LANES = 128
V7X_SUBLANES = 8
V7X_VMEM_LIMIT_BYTES = 48 * 1024 * 1024

TM_PROJ = 512
TB_RWKV = 512
RWKV_CHUNK_GROUP = 4
QB_ATTN = 256
TM_ROUTE = 512
TM_DISPATCH = 512
TM_EXPERT = 512
TM_COMBINE = 512
TM_COMBINE_PROJ = 256


def _params(*sem):
    return pltpu.CompilerParams(dimension_semantics=sem, vmem_limit_bytes=V7X_VMEM_LIMIT_BYTES)


def _rms(x, w):
    return x * lax.rsqrt(jnp.mean(x * x, axis=-1, keepdims=True) + RMS_EPS) * w


def _mm(a, b):
    return jnp.dot(a.astype(BF16), b.astype(BF16), preferred_element_type=F32)


def _mm_nt(a, b):
    return lax.dot_general(a.astype(BF16), b.astype(BF16), (((1,), (1,)), ((), ())),
                           preferred_element_type=F32)


def _mm_tn(a, b):
    return lax.dot_general(a.astype(BF16), b.astype(BF16), (((0,), (0,)), ((), ())),
                           preferred_element_type=F32)


def _to_token_tiles(ref, x):
    m, d = x.shape
    for s in range(d // V7X_LANES):
        ref[pl.ds(s, m, stride=V7X_SUBLANES), :] = x[:, s * V7X_LANES:(s + 1) * V7X_LANES]


def _from_token_tiles(ref, m):
    return jnp.concatenate([ref[pl.ds(s, m, stride=V7X_SUBLANES), :] for s in range(V7X_SUBLANES)],
                           axis=-1)


def _token_tile(ref, row):
    return ref.at[pl.ds(pl.multiple_of(row * V7X_SUBLANES, V7X_SUBLANES), V7X_SUBLANES), :]


def _split3(x):
    hi = x.astype(BF16)
    r1 = x - hi.astype(F32)
    mid = r1.astype(BF16)
    lo = (r1 - mid.astype(F32)).astype(BF16)
    return hi, mid, lo


def _mm_exact_lhs(a_bf16, x):
    hi, mid, lo = _split3(x)
    return (jnp.dot(a_bf16, hi, preferred_element_type=F32)
            + jnp.dot(a_bf16, mid, preferred_element_type=F32)
            + jnp.dot(a_bf16, lo, preferred_element_type=F32))


def _mm_split2_rhs(x, b_bf16):
    hi = x.astype(BF16)
    lo = (x - hi.astype(F32)).astype(BF16)
    return (jnp.dot(hi, b_bf16, preferred_element_type=F32)
            + jnp.dot(lo, b_bf16, preferred_element_type=F32))


def _norm_proj_kernel(h_ref, nw_ref, wr_ref, wa_ref, zr_ref, qkv_ref):
    hn = _rms(h_ref[...], nw_ref[...]).astype(BF16)
    zr_ref[...] = jnp.dot(hn, wr_ref[...], preferred_element_type=F32)
    qkv_ref[...] = jnp.dot(hn, wa_ref[...], preferred_element_type=F32).astype(BF16)


def _norm_proj(h, nw, wr, wa):
    t, d = h.shape
    tm = min(TM_PROJ, t)
    n_r, n_a = wr.shape[1], wa.shape[1]
    return pl.pallas_call(
        _norm_proj_kernel,
        out_shape=(jax.ShapeDtypeStruct((t, n_r), F32), jax.ShapeDtypeStruct((t, n_a), BF16)),
        grid=(t // tm,),
        in_specs=[pl.BlockSpec((tm, d), lambda i: (i, 0)),
                  pl.BlockSpec((1, d), lambda i: (0, 0)),
                  pl.BlockSpec((d, n_r), lambda i: (0, 0)),
                  pl.BlockSpec((d, n_a), lambda i: (0, 0))],
        out_specs=(pl.BlockSpec((tm, n_r), lambda i: (i, 0)),
                   pl.BlockSpec((tm, n_a), lambda i: (i, 0))),
        compiler_params=_params("parallel"),
        name="norm_proj",
    )(h, nw, wr, wa)


_V_W0, _V_A0, _V_KK, _V_KA, _V_RK, _V_LNW, _V_LNB, _V_V0 = range(8)


def _rwkv_kernel(*refs, has_vres, n_heads, d_r, group, nb):
    if has_vres:
        (z_ref, vf_ref, mu_ref, vec_ref, wl_ref, tril_ref, ones_ref, v1_ref, v2_ref, y_ref,
         s_ref, carry_ref, r_s, k_s, v_s, kk_s, a_s, lc_s, lw_s, bon_s, g_s) = refs
        vf_out_ref = None
    else:
        (z_ref, mu_ref, vec_ref, wl_ref, tril_ref, ones_ref, y_ref, vf_out_ref,
         s_ref, carry_ref, r_s, k_s, v_s, kk_s, a_s, lc_s, lw_s, bon_s, g_s) = refs
    per_token = (r_s, k_s, v_s, kk_s, a_s, lc_s, lw_s, bon_s, g_s)
    tb = z_ref.shape[0]
    gr = group * CHUNK
    k_step = pl.program_id(0)
    cur = lax.rem(k_step, 2)
    prv = 1 - cur

    @pl.when(k_step == 0)
    def _():
        s_ref[...] = jnp.zeros_like(s_ref)
        for ref in per_token:
            ref[...] = jnp.zeros_like(ref)

    @pl.when(lax.rem(k_step, nb) == 0)
    def _():
        carry_ref[...] = jnp.zeros_like(carry_ref)

    vec = vec_ref[...]

    def vrow(i):
        return vec[i:i + 1, :]

    ln_w = vrow(_V_LNW)
    ln_b = vrow(_V_LNB)
    mu = mu_ref[...]
    head_ones = ones_ref[...]
    tril = tril_ref[...]
    ci = lax.broadcasted_iota(I32, (CHUNK, CHUNK), 0)
    cj = lax.broadcasted_iota(I32, (CHUNK, CHUNK), 1)
    strict = cj < ci
    lower = cj <= ci
    eye = ci == cj
    eye_f = jnp.where(eye, 1.0, 0.0)
    hs = [slice(h * HEAD_DIM, (h + 1) * HEAD_DIM) for h in range(n_heads)]
    prev_first = lax.rem(k_step + nb - 1, nb) == 0

    def token_work(rows):
        z = z_ref[rows, :]
        row = lax.broadcasted_iota(I32, z.shape, 0)
        z_prev = jnp.where(row == 0, carry_ref[0:1, :], pltpu.roll(z, 1, axis=0))
        carry_ref[0:1, :] = z[gr - 1:gr, :]
        zs = z + (z_prev - z) * mu
        r = zs[:, 0:d_r]
        k = zs[:, d_r:2 * d_r]
        v = zs[:, 2 * d_r:3 * d_r]
        lo = zs[:, 3 * d_r:]
        n_lo = lo.shape[1]
        lane = lax.broadcasted_iota(I32, lo.shape, 1)
        lo_act = jnp.where(lane < n_lo // 4, jnp.tanh(lo),
                           jnp.where(lane < n_lo // 2, lo, jax.nn.sigmoid(lo)))
        lo_out = _mm(lo_act, wl_ref[...])
        if has_vres:
            vv = _mm(_mm(v, v1_ref[...]), v2_ref[...])
            v = v + (vf_ref[rows, :] - v) * jax.nn.sigmoid(vrow(_V_V0) + vv)
        w_log = -jax.nn.softplus(-(vrow(_V_W0) + lo_out[:, 0:d_r])) - 0.5
        lw = -jnp.exp(w_log)
        a = jax.nn.sigmoid(vrow(_V_A0) + lo_out[:, d_r:2 * d_r])
        kk = k * vrow(_V_KK)
        kk = kk * lax.rsqrt(jnp.maximum(_mm_split2_rhs(kk * kk, head_ones), 1e-24))
        k2 = k * (1.0 + (a - 1.0) * vrow(_V_KA))
        r_s[cur, rows, :] = r
        k_s[cur, rows, :] = k2
        v_s[cur, rows, :] = v
        kk_s[cur, rows, :] = kk
        a_s[cur, rows, :] = a
        lc_s[cur, rows, :] = _mm_exact_lhs(tril, lw)
        lw_s[cur, rows, :] = lw
        bon_s[cur, rows, :] = _mm_split2_rhs(r * k2 * vrow(_V_RK), head_ones) * v
        g_s[cur, rows, :] = lo_out[:, 2 * d_r:3 * d_r]

    def chunk_operands(r0):
        rs = pl.ds(r0, CHUNK)
        lc_c = lc_s[prv, rs, :]
        lw_c = lw_s[prv, rs, :]
        l_end = lc_s[prv, pl.ds(r0 + CHUNK - 1, 1), :]
        p_in = jnp.exp(lc_c)
        p_prev = jnp.exp(lc_c - lw_c)
        p_inv = jnp.exp(-lc_c)
        p_end = jnp.exp(l_end - lc_c)
        p_last = jnp.exp(l_end)
        kk_c = kk_s[prv, rs, :]
        b_c = kk_c * a_s[prv, rs, :]
        k_c = k_s[prv, rs, :]
        at = (-kk_c * p_prev).astype(BF16)
        bt = (b_c * p_inv).astype(BF16)
        bh = (b_c * p_end).astype(BF16)
        kt = (k_c * p_inv).astype(BF16)
        kh = (k_c * p_end).astype(BF16)
        rt = (r_s[prv, rs, :] * p_in).astype(BF16)
        vc = v_s[prv, rs, :].astype(BF16)
        per_head = [[x[:, sl] for sl in hs] for x in (at, bt, bh, kt, kh, rt, vc)]
        per_head.append([p_last[:, sl] for sl in hs])
        return per_head

    def group_body(gi, carry):
        g0 = pl.multiple_of(gi * gr, gr)
        rows = pl.ds(g0, gr)
        if vf_out_ref is not None:
            vf_out_ref[rows, :] = v_s[prv, rows, :]
        ops = [chunk_operands(g0 + c * CHUNK) for c in range(group)]
        at_h, bt_h, bh_h, kt_h, kh_h, rt_h, v_h, pl_h = ([x for c in range(group) for x in ops[c][q]]
                                                         for q in range(8))
        heads = range(group * n_heads)
        ar_h = [jnp.concatenate([at_h[h], rt_h[h]], axis=0) for h in heads]
        m_b = [_mm_nt(ar_h[h], bt_h[h]) for h in heads]
        m_k = [_mm_nt(ar_h[h], kt_h[h]) for h in heads]
        n_ab = [jnp.where(strict, m_b[h][:CHUNK], 0.0) for h in heads]
        a_ak = [jnp.where(strict, m_k[h][:CHUNK], 0.0) for h in heads]
        a_rb = [jnp.where(lower, m_b[h][CHUNK:], 0.0) for h in heads]
        a_rk = [jnp.where(lower, m_k[h][CHUNK:], 0.0) for h in heads]
        x_inv = [eye_f + n_ab[h] for h in heads]
        pw = [_mm(n_ab[h], n_ab[h]) for h in heads]
        akv = [_mm(a_ak[h], v_h[h]) for h in heads]
        n_sq = CHUNK.bit_length() - 2
        for it in range(n_sq):
            if it < n_sq - 1:
                st = [_mm(jnp.concatenate([x_inv[h], pw[h]], axis=0), pw[h]) for h in heads]
                x_inv = [x_inv[h] + st[h][:CHUNK] for h in heads]
                pw = [st[h][CHUNK:] for h in heads]
            else:
                st = [_mm(x_inv[h], pw[h]) for h in heads]
                x_inv = [x_inv[h] + st[h] for h in heads]
        w_h = [_mm(x_inv[h], at_h[h]) for h in heads]
        u0 = [_mm(x_inv[h], akv[h]) for h in heads]
        y0 = [_mm(a_rk[h], v_h[h]) + _mm(a_rb[h], u0[h]) for h in heads]
        r_p = [rt_h[h].astype(F32) + _mm(a_rb[h], w_h[h]) for h in heads]
        g_h = [jnp.where(eye, pl_h[h], 0.0) + _mm_tn(w_h[h], bh_h[h]) for h in heads]
        d_h = [_mm_tn(u0[h], bh_h[h]) + _mm_tn(v_h[h], kh_h[h]) for h in heads]
        fresh = prev_first & (gi == 0)
        s_h = [jnp.where(fresh, 0.0, s_ref[h]) for h in range(n_heads)]
        for c in range(group):
            rs = pl.ds(g0 + c * CHUNK, CHUNK)
            idx = [c * n_heads + h for h in range(n_heads)]
            y_h = [y0[i] + _mm_nt(r_p[i], s_h[h]) for h, i in enumerate(idx)]
            s_h = [_mm(s_h[h], g_h[i]) + d_h[i] for h, i in enumerate(idx)]
            y_heads = []
            for h in range(n_heads):
                mean = jnp.mean(y_h[h], axis=-1, keepdims=True)
                yc = y_h[h] - mean
                var = jnp.mean(yc * yc, axis=-1, keepdims=True)
                y_heads.append(yc * lax.rsqrt(var + GN_EPS))
            y_n = jnp.concatenate(y_heads, axis=-1)
            out = (y_n * ln_w + ln_b + bon_s[prv, rs, :]) * g_s[prv, rs, :]
            y_ref[rs, :] = out.astype(y_ref.dtype)
        for h in range(n_heads):
            s_ref[h] = s_h[h]
        token_work(rows)
        return carry

    lax.fori_loop(0, tb // gr, group_body, 0)


def _rwkv(z, v_first, mu, vec, wl, v1, v2, *, batch, seq, n_heads):
    t, n_z = z.shape
    d_r = n_heads * HEAD_DIM
    tb = min(TB_RWKV, seq)
    nb = seq // tb
    n_blocks = batch * nb
    has_vres = v_first is not None
    tok_in = lambda k: (jnp.minimum(k, n_blocks - 1), 0)
    tok_out = lambda k: (jnp.maximum(k - 1, 0), 0)
    const = lambda k: (0, 0)
    in_specs = [pl.BlockSpec((tb, n_z), tok_in)]
    args = [z]
    if has_vres:
        in_specs.append(pl.BlockSpec((tb, d_r), tok_in))
        args.append(v_first)
    n_chunks = tb // CHUNK
    group = RWKV_CHUNK_GROUP if n_chunks % RWKV_CHUNK_GROUP == 0 else 1
    ti = jnp.arange(group * CHUNK)
    tril = ((ti[:, None] // CHUNK == ti[None, :] // CHUNK) & (ti[None, :] <= ti[:, None])).astype(BF16)
    hi = jnp.arange(d_r) // HEAD_DIM
    head_ones = (hi[:, None] == hi[None, :]).astype(BF16)
    in_specs += [pl.BlockSpec(mu.shape, const), pl.BlockSpec(vec.shape, const),
                 pl.BlockSpec(wl.shape, const), pl.BlockSpec(tril.shape, const),
                 pl.BlockSpec(head_ones.shape, const)]
    args += [mu, vec, wl, tril, head_ones]
    if has_vres:
        in_specs += [pl.BlockSpec(v1.shape, const), pl.BlockSpec(v2.shape, const)]
        args += [v1, v2]
        out_shape = jax.ShapeDtypeStruct((t, d_r), BF16)
        out_specs = pl.BlockSpec((tb, d_r), tok_out)
    else:
        out_shape = (jax.ShapeDtypeStruct((t, d_r), BF16), jax.ShapeDtypeStruct((t, d_r), F32))
        out_specs = (pl.BlockSpec((tb, d_r), tok_out), pl.BlockSpec((tb, d_r), tok_out))
    scratch = [pltpu.VMEM((n_heads, HEAD_DIM, HEAD_DIM), F32),
               pltpu.VMEM((V7X_SUBLANES, n_z), F32)]
    scratch += [pltpu.VMEM((2, tb, d_r), F32) for _ in range(9)]
    return pl.pallas_call(
        functools.partial(_rwkv_kernel, has_vres=has_vres, n_heads=n_heads, d_r=d_r, group=group, nb=nb),
        out_shape=out_shape,
        grid=(n_blocks + 1,),
        in_specs=in_specs,
        out_specs=out_specs,
        scratch_shapes=scratch,
        compiler_params=_params("arbitrary"),
        name="rwkv_vres" if has_vres else "rwkv",
    )(*args)


def _attn_kernel(*refs, n_heads, n_parts):
    q_ref = refs[0]
    k_refs = refs[1:1 + n_parts]
    v_refs = refs[1 + n_parts:1 + 2 * n_parts]
    tab_ref = refs[1 + 2 * n_parts]
    o_ref = refs[2 + 2 * n_parts]
    qb = q_ref.shape[0]
    j = pl.program_id(1)
    scale = HEAD_DIM ** -0.5
    q = q_ref[...] * jnp.asarray(scale, q_ref.dtype)
    ks = [r[...] for r in k_refs]
    vs = [r[...] for r in v_refs]
    pw = 2 * HEAD_DIM
    lane = lax.broadcasted_iota(I32, (qb, pw), 1)
    sum_even = jnp.where(lane < HEAD_DIM, 1.0, 0.0).astype(q.dtype)
    sum_odd = jnp.where(lane < HEAD_DIM, 0.0, 1.0).astype(q.dtype)
    zero = jnp.zeros((), q.dtype)
    is_even = sum_even > zero
    outs = []
    for hp in range(n_heads // 2):
        sl = slice(hp * pw, (hp + 1) * pw)
        qq = q[:, sl]
        q_pair = (jnp.where(is_even, qq, zero), jnp.where(is_even, zero, qq))
        s_parts = [[], []]
        for p in range(n_parts):
            kk = ks[p][:, sl]
            back = n_parts - 1 - p
            for u in range(2):
                s = _mm_nt(q_pair[u], kk) + tab_ref[2 * hp + u, :, p * qb:(p + 1) * qb]
                if back > 0:
                    s = jnp.where(j >= back, s, NEG_INF)
                s_parts[u].append(s)
        m = []
        for u in range(2):
            mm = s_parts[u][0]
            for s in s_parts[u][1:]:
                mm = jnp.maximum(mm, s)
            m.append(mm.max(axis=-1, keepdims=True))
        acc = jnp.zeros((qb, 2 * pw), F32)
        for p in range(n_parts):
            vv = vs[p][:, sl]
            rhs = jnp.concatenate(
                [jnp.concatenate([jnp.where(is_even, vv, zero), sum_even], axis=1),
                 jnp.concatenate([jnp.where(is_even, zero, vv), sum_odd], axis=1)], axis=0)
            e = jnp.concatenate([jnp.exp(s_parts[u][p] - m[u]).astype(BF16) for u in range(2)], axis=1)
            acc = acc + jnp.dot(e, rhs, preferred_element_type=F32)
        outs.append(acc[:, :pw] / acc[:, pw:])
    o_ref[...] = jnp.concatenate(outs, axis=-1).astype(o_ref.dtype)


def _attn(qkv, table, *, batch, seq, n_heads):
    t = qkv.shape[0]
    d_a = n_heads * HEAD_DIM
    qb = min(QB_ATTN, seq)
    left = LEFT_CHUNKS * CHUNK
    assert left % qb == 0 and seq % qb == 0
    n_parts = left // qb + 1
    nb = seq // qb
    in_specs = [pl.BlockSpec((qb, d_a), lambda b, j: (b * nb + j, 0))]
    for p in range(n_parts):
        back = n_parts - 1 - p
        in_specs.append(pl.BlockSpec((qb, d_a), lambda b, j, back=back: (b * nb + jnp.maximum(j - back, 0), 1)))
    for p in range(n_parts):
        back = n_parts - 1 - p
        in_specs.append(pl.BlockSpec((qb, d_a), lambda b, j, back=back: (b * nb + jnp.maximum(j - back, 0), 2)))
    in_specs.append(pl.BlockSpec(table.shape, lambda b, j: (0, 0, 0)))
    return pl.pallas_call(
        functools.partial(_attn_kernel, n_heads=n_heads, n_parts=n_parts),
        out_shape=jax.ShapeDtypeStruct((t, d_a), BF16),
        grid=(batch, nb),
        in_specs=in_specs,
        out_specs=pl.BlockSpec((qb, d_a), lambda b, j: (b * nb + j, 0)),
        compiler_params=_params("parallel", "arbitrary"),
        name="attn",
    )(*([qkv] * (1 + 2 * n_parts)), table)


def _attn_table(rel_bias, qb):
    left = LEFT_CHUNKS * CHUNK
    n_keys = left + qb
    period = qb + n_keys - 1
    n_heads = rel_bias.shape[0]
    m = jnp.arange(period)
    rel = left - jnp.where(m < n_keys, m, m - period)
    g = rel_bias[:, jnp.clip(rel, -(CHUNK - 1), MAX_REL) + (CHUNK - 1)].astype(F32)
    flat = jnp.tile(g, (1, qb))[:, :qb * (period - 1)]
    bias = flat.reshape(n_heads, qb, period - 1)[:, :, :n_keys]
    cq = jnp.arange(qb)[:, None] // CHUNK
    ck = jnp.arange(n_keys)[None, :] // CHUNK
    valid = (ck >= cq) & (ck <= cq + LEFT_CHUNKS)
    return jnp.where(valid[None], bias, NEG_INF)


_R_E1, _R_E2, _R_C1, _R_C2, _R_RANK1, _R_RANK2, _R_LRANK1, _R_LRANK2 = range(8)


def _outproj_route_kernel(yr_ref, ya_ref, h_ref, wor_ref, woa_ref, nw_ref, wrt_ref, brt_ref,
                          h1_ref, hn_ref, route_ref, route_t_ref, cnt_ref, tstat_ref, carry_ref):
    i = pl.program_id(0)

    @pl.when(i == 0)
    def _():
        carry_ref[...] = jnp.zeros_like(carry_ref)

    h1 = (h_ref[...] + jnp.dot(yr_ref[...], wor_ref[...], preferred_element_type=F32)
          + jnp.dot(ya_ref[...], woa_ref[...], preferred_element_type=F32))
    h1_ref[...] = h1
    hn = _rms(h1, nw_ref[...])
    _to_token_tiles(hn_ref, hn)
    nl = brt_ref.shape[1]
    hn_hi = hn.astype(BF16)
    hn_lo = (hn - hn_hi.astype(F32)).astype(BF16)
    part = jnp.dot(hn_hi, wrt_ref[...], preferred_element_type=F32)
    logits = (part[:, :nl] + part[:, nl:]
              + jnp.dot(hn_lo, wrt_ref[:, :nl], preferred_element_type=F32) + brt_ref[...])
    tm = logits.shape[0]
    lane = lax.broadcasted_iota(I32, (tm, nl), 1)
    lane_f = lane.astype(F32)
    ninf = -jnp.inf
    big = float(nl)
    is_g = lane < N_GROUPS
    gl = jnp.where(is_g, logits, ninf)
    g_max = gl.max(axis=-1, keepdims=True)
    g_sel = jnp.where(gl == g_max, lane_f, big).min(axis=-1, keepdims=True)
    p_g = 1.0 / jnp.where(is_g, jnp.exp(logits - g_max), 0.0).sum(axis=-1, keepdims=True)
    e_lo = N_GROUPS + EXPERTS_PER_GROUP * g_sel
    in_grp = (lane_f >= e_lo) & (lane_f < e_lo + EXPERTS_PER_GROUP)
    el = jnp.where(in_grp, logits, ninf)
    m1 = el.max(axis=-1, keepdims=True)
    i1 = jnp.where(el == m1, lane_f, big).min(axis=-1, keepdims=True)
    el2 = jnp.where(lane_f == i1, ninf, el)
    m2 = el2.max(axis=-1, keepdims=True)
    i2 = jnp.where(el2 == m2, lane_f, big).min(axis=-1, keepdims=True)
    t2 = jnp.exp(m2 - m1)
    c1 = p_g / (1.0 + t2)
    c2 = p_g * t2 / (1.0 + t2)
    e1 = i1 - N_GROUPS
    e2 = i2 - N_GROUPS
    oh1 = lane_f == e1
    oh2 = lane_f == e2
    ohs = jnp.where(oh1 | oh2, 1.0, 0.0)
    ri = lax.broadcasted_iota(I32, (tm, tm), 0)
    rj = lax.broadcasted_iota(I32, (tm, tm), 1)
    before = jnp.where(rj < ri, 1.0, 0.0).astype(BF16)
    old_carry = carry_ref[0:1, :]
    cnt_tile = jnp.dot(before, ohs.astype(BF16), preferred_element_type=F32)
    cnt = cnt_tile + old_carry
    rank1 = jnp.where(oh1, cnt, 0.0).sum(axis=-1, keepdims=True)
    rank2 = jnp.where(oh2, cnt, 0.0).sum(axis=-1, keepdims=True)
    lrank1 = jnp.where(oh1, cnt_tile, 0.0).sum(axis=-1, keepdims=True)
    lrank2 = jnp.where(oh2, cnt_tile, 0.0).sum(axis=-1, keepdims=True)
    tile_cnt = ohs.sum(axis=0, keepdims=True)
    new_carry = old_carry + tile_cnt
    carry_ref[0:1, :] = new_carry
    cnt_ref[...] = jnp.broadcast_to(new_carry, cnt_ref.shape)
    srow = lax.broadcasted_iota(I32, tstat_ref.shape, 0)
    tstat_ref[...] = jnp.where(srow == 0, tile_cnt, jnp.where(srow == 1, old_carry, 0.0))
    route = jnp.zeros((tm, nl), F32)
    for idx, val in ((_R_E1, e1), (_R_E2, e2), (_R_C1, c1), (_R_C2, c2),
                     (_R_RANK1, rank1), (_R_RANK2, rank2), (_R_LRANK1, lrank1), (_R_LRANK2, lrank2)):
        route = jnp.where(lane == idx, val, route)
    route_ref[...] = route
    route_t_ref[...] = route.T[:route_t_ref.shape[0], :]


def _outproj_route(yr, ya, h, wor, woa, nw, wrt, brt):
    t, d = h.shape
    tm = min(TM_ROUTE, t)
    d_r, d_a = yr.shape[1], ya.shape[1]
    nl = brt.shape[1]
    tok = lambda i: (i, 0)
    const = lambda i: (0, 0)
    return pl.pallas_call(
        _outproj_route_kernel,
        out_shape=(jax.ShapeDtypeStruct((t, d), F32),
                   jax.ShapeDtypeStruct((t * V7X_SUBLANES, V7X_LANES), F32),
                   jax.ShapeDtypeStruct((t, nl), F32), jax.ShapeDtypeStruct((V7X_SUBLANES, t), F32),
                   jax.ShapeDtypeStruct((V7X_SUBLANES, nl), F32),
                   jax.ShapeDtypeStruct((t // tm * V7X_SUBLANES, nl), F32)),
        grid=(t // tm,),
        in_specs=[pl.BlockSpec((tm, d_r), tok), pl.BlockSpec((tm, d_a), tok), pl.BlockSpec((tm, d), tok),
                  pl.BlockSpec((d_r, d), const), pl.BlockSpec((d_a, d), const),
                  pl.BlockSpec((1, d), const), pl.BlockSpec(wrt.shape, const), pl.BlockSpec((1, nl), const)],
        out_specs=(pl.BlockSpec((tm, d), tok), pl.BlockSpec((tm * V7X_SUBLANES, V7X_LANES), tok),
                   pl.BlockSpec((tm, nl), tok), pl.BlockSpec((V7X_SUBLANES, tm), lambda i: (0, i)),
                   pl.BlockSpec((V7X_SUBLANES, nl), const), pl.BlockSpec((V7X_SUBLANES, nl), tok)),
        scratch_shapes=[pltpu.VMEM((V7X_SUBLANES, nl), F32)],
        compiler_params=_params("arbitrary"),
        name="outproj_route",
    )(yr, ya, h, wor, woa, nw, wrt, brt)


def _load_indices(idx_hbm, i, idx_smem, sem):
    n = idx_smem.shape[0]
    cp = pltpu.make_async_copy(idx_hbm.at[pl.ds(pl.multiple_of(i * n, n), n)], idx_smem, sem)
    cp.start()
    cp.wait()


def _tiles(ref, row, n):
    start = row * V7X_SUBLANES
    if not isinstance(row, int):
        start = pl.multiple_of(start, V7X_SUBLANES)
    return ref.at[pl.ds(start, n * V7X_SUBLANES), :]


def _start_run_copies(length, n_bits, copy_of):
    for b in range(n_bits):
        @pl.when(((length >> b) & 1) == 1)
        def _(b=b):
            done = (length >> (b + 1)) << (b + 1)
            copy_of(done, 1 << b, b).start(priority=b % 2)


def _dispatch_kernel(idx_hbm, zinfo_hbm, x_ref, xs_hbm, idx_smem, zinfo_smem, xloc, zero_vmem,
                     idx_sem, run_sem, zero_sem, *, tm, n_free):
    i = pl.program_id(0)
    nb = pl.num_programs(0)
    buf = lax.rem(i, 2)
    zero_rows = zero_vmem.shape[0] // V7X_SUBLANES

    @pl.when(i == 0)
    def _():
        cp = pltpu.make_async_copy(zinfo_hbm, zinfo_smem, idx_sem)
        cp.start()
        cp.wait()
        zero_vmem[...] = jnp.zeros_like(zero_vmem)

        def pad_runs(e, carry):
            start = zinfo_smem[e]
            _start_run_copies(
                zinfo_smem[N_EXPERTS + e], zero_rows.bit_length(),
                lambda done, n, b: pltpu.make_async_copy(_tiles(zero_vmem, 0, n),
                                                         _tiles(xs_hbm, start + done, n), zero_sem))
            return carry

        lax.fori_loop(0, N_EXPERTS, pad_runs, 0)
        tail_start = zinfo_smem[2 * N_EXPERTS]

        def tail_block(n, carry):
            pltpu.make_async_copy(zero_vmem, _tiles(xs_hbm, tail_start + n * zero_rows, zero_rows),
                                  zero_sem).start()
            return carry

        lax.fori_loop(0, zinfo_smem[2 * N_EXPERTS + 1], tail_block, 0)

    _load_indices(idx_hbm, i, idx_smem, idx_sem)

    def wait_runs(b):
        pltpu.make_async_copy(xloc.at[b], xs_hbm.at[pl.ds(0, xloc.shape[1]), :], run_sem.at[b]).wait()

    @pl.when(i >= 2)
    def _():
        wait_runs(buf)

    def place(tt, carry):
        row = x_ref[pl.ds(pl.multiple_of(tt * V7X_SUBLANES, V7X_SUBLANES), V7X_SUBLANES), :]
        for s in range(2):
            lp = idx_smem[s * tm + tt]
            xloc[buf, pl.ds(pl.multiple_of(lp * V7X_SUBLANES, V7X_SUBLANES), V7X_SUBLANES), :] = row
        return carry

    lax.fori_loop(0, tm, place, 0, unroll=8)

    def expert_run(e, carry):
        dst = idx_smem[2 * tm + N_EXPERTS + e]
        off = idx_smem[2 * tm + 2 * N_EXPERTS + e]
        _start_run_copies(
            idx_smem[2 * tm + e], tm.bit_length(),
            lambda done, n, b: pltpu.make_async_copy(_tiles(xloc.at[buf], off + done, n),
                                                     _tiles(xs_hbm, dst + done, n), run_sem.at[buf]))
        return carry

    lax.fori_loop(0, N_EXPERTS, expert_run, 0)

    @pl.when(i == nb - 1)
    def _():
        wait_runs(buf)

        @pl.when(nb >= 2)
        def _():
            wait_runs(1 - buf)

        pltpu.make_async_copy(xs_hbm.at[pl.ds(0, n_free * V7X_SUBLANES), :],
                              xs_hbm.at[pl.ds(0, n_free * V7X_SUBLANES), :], zero_sem).wait()


def _dispatch(x, idx, zinfo, n_rows, tm, n_free):
    t = x.shape[0] // V7X_SUBLANES
    nb = t // tm
    rec = idx.shape[0] // nb
    return pl.pallas_call(
        functools.partial(_dispatch_kernel, tm=tm, n_free=n_free),
        out_shape=jax.ShapeDtypeStruct((n_rows * V7X_SUBLANES, V7X_LANES), x.dtype),
        grid=(nb,),
        in_specs=[pl.BlockSpec(memory_space=pl.ANY), pl.BlockSpec(memory_space=pl.ANY),
                  pl.BlockSpec((tm * V7X_SUBLANES, V7X_LANES), lambda i: (i, 0))],
        out_specs=pl.BlockSpec(memory_space=pl.ANY),
        scratch_shapes=[pltpu.SMEM((rec,), I32), pltpu.SMEM(zinfo.shape, I32),
                        pltpu.VMEM((2, 2 * tm * V7X_SUBLANES, V7X_LANES), x.dtype),
                        pltpu.VMEM((TM_EXPERT // 2 * V7X_SUBLANES, V7X_LANES), x.dtype),
                        pltpu.SemaphoreType.DMA, pltpu.SemaphoreType.DMA((2,)), pltpu.SemaphoreType.DMA],
        compiler_params=_params("arbitrary"),
        name="dispatch",
    )(idx, zinfo, x)


def _experts_kernel(te_ref, tv_ref, tf_ref, x_ref, w1_ref, w3_ref, w2_ref, y_ref,
                    w1_b, w3_b, w2_b, *, tm):
    i = pl.program_id(0)

    @pl.when(tf_ref[i] > 0)
    def _():
        w1_b[...] = w1_ref[...].astype(BF16)
        w3_b[...] = w3_ref[...].astype(BF16)
        w2_b[...] = w2_ref[...].astype(BF16)

    @pl.when(tv_ref[i] > 0)
    def _():
        x = _from_token_tiles(x_ref, tm).astype(BF16)
        h_gate = jnp.dot(x, w1_b[...], preferred_element_type=F32)
        h_up = jnp.dot(x, w3_b[...], preferred_element_type=F32)
        hid = (h_gate * jax.nn.sigmoid(h_gate) * h_up).astype(BF16)
        _to_token_tiles(y_ref, jnp.dot(hid, w2_b[...], preferred_element_type=F32))

    @pl.when(tv_ref[i] == 0)
    def _():
        y_ref[...] = jnp.zeros_like(y_ref)


def _experts(xs, w1, w3, w2, tile_expert, tile_valid, tile_first):
    n_rows = xs.shape[0] // V7X_SUBLANES
    tm = TM_EXPERT
    nt = n_rows // tm
    d, f = w1.shape[1:]
    tile_spec = pl.BlockSpec((tm * V7X_SUBLANES, V7X_LANES), lambda i, te, tv, tf: (i, 0))
    grid_spec = pltpu.PrefetchScalarGridSpec(
        num_scalar_prefetch=3,
        grid=(nt,),
        in_specs=[tile_spec,
                  pl.BlockSpec((None, d, f), lambda i, te, tv, tf: (te[i], 0, 0)),
                  pl.BlockSpec((None, d, f), lambda i, te, tv, tf: (te[i], 0, 0)),
                  pl.BlockSpec((None, f, d), lambda i, te, tv, tf: (te[i], 0, 0))],
        out_specs=tile_spec,
        scratch_shapes=[pltpu.VMEM((d, f), BF16), pltpu.VMEM((d, f), BF16), pltpu.VMEM((f, d), BF16)],
    )
    return pl.pallas_call(
        functools.partial(_experts_kernel, tm=tm),
        out_shape=jax.ShapeDtypeStruct(xs.shape, F32),
        grid_spec=grid_spec,
        compiler_params=_params("arbitrary"),
        name="experts",
    )(tile_expert, tile_valid, tile_first, xs, w1, w3, w2)


def _combine_ple_kernel(pos_hbm, ys_hbm, h_ref, route_ref, p_ref, nw_ref, wg_ref, bg_ref, wp_ref,
                        fw_ref, *rest, tm, final, with_proj):
    if with_proj:
        nwn_ref, wr_ref, wa_ref, o_ref, zr_ref, qkv_ref, idx_smem, ybuf, idx_sem, row_sem = rest
    else:
        o_ref, idx_smem, ybuf, idx_sem, row_sem = rest
    i = pl.program_id(0)
    nb = pl.num_programs(0)
    cur = lax.rem(i, 2)
    nxt = 1 - cur

    def gather(step, buf):
        _load_indices(pos_hbm, step, idx_smem, idx_sem)

        def issue(tt, carry):
            for s in range(2):
                pltpu.make_async_copy(_token_tile(ys_hbm, idx_smem[s * tm + tt]),
                                      _token_tile(ybuf.at[buf, s], tt), row_sem.at[buf]).start(priority=s)
            return carry

        lax.fori_loop(0, tm, issue, 0, unroll=8)

    def wait_gather(buf):
        for s in range(2):
            pltpu.make_async_copy(ys_hbm.at[pl.ds(0, ybuf.shape[2]), :], ybuf.at[buf, s],
                                  row_sem.at[buf]).wait()

    @pl.when(i == 0)
    def _():
        gather(0, 0)

    @pl.when(i + 1 < nb)
    def _():
        gather(i + 1, nxt)

    wait_gather(cur)

    route = route_ref[...]
    c1 = route[:, _R_C1:_R_C1 + 1]
    c2 = route[:, _R_C2:_R_C2 + 1]
    h2 = (h_ref[...] + c1 * _from_token_tiles(ybuf.at[cur, 0], tm)
          + c2 * _from_token_tiles(ybuf.at[cur, 1], tm))
    hn = _rms(h2, nw_ref[...]).astype(BF16)
    gate = jax.nn.sigmoid(jnp.dot(hn, wg_ref[...], preferred_element_type=F32) + bg_ref[...])
    h3 = h2 + gate * jnp.dot(p_ref[...].astype(BF16), wp_ref[...], preferred_element_type=F32)
    if final:
        h3 = _rms(h3, fw_ref[...])
    o_ref[...] = h3
    if with_proj:
        hn_next = _rms(h3, nwn_ref[...]).astype(BF16)
        zr_ref[...] = jnp.dot(hn_next, wr_ref[...], preferred_element_type=F32)
        qkv_ref[...] = jnp.dot(hn_next, wa_ref[...], preferred_element_type=F32).astype(BF16)


def _combine_ple(pos_tiles, ys, h, route, p, nw, wg, bg, wp, fw, proj, *, tm, final, layer):
    t, d = h.shape
    nl = route.shape[1]
    dp = p.shape[1]
    tok = lambda i: (i, 0)
    const = lambda i: (0, 0)
    in_specs = [pl.BlockSpec(memory_space=pl.ANY), pl.BlockSpec(memory_space=pl.ANY),
                pl.BlockSpec((tm, d), tok), pl.BlockSpec((tm, nl), tok),
                pl.BlockSpec((tm, dp), lambda i: (i + layer * (t // tm), 0)),
                pl.BlockSpec((1, d), const), pl.BlockSpec((d, d), const), pl.BlockSpec((1, d), const),
                pl.BlockSpec((dp, d), const), pl.BlockSpec((1, d), const)]
    args = [pos_tiles, ys, h, route, p, nw, wg, bg, wp, fw]
    out_shape = jax.ShapeDtypeStruct((t, d), F32)
    out_specs = pl.BlockSpec((tm, d), tok)
    if proj is not None:
        nwn, wr, wa = proj
        in_specs += [pl.BlockSpec((1, d), const), pl.BlockSpec(wr.shape, const), pl.BlockSpec(wa.shape, const)]
        args += [nwn, wr, wa]
        out_shape = (out_shape, jax.ShapeDtypeStruct((t, wr.shape[1]), F32),
                     jax.ShapeDtypeStruct((t, wa.shape[1]), BF16))
        out_specs = (out_specs, pl.BlockSpec((tm, wr.shape[1]), tok), pl.BlockSpec((tm, wa.shape[1]), tok))
    return pl.pallas_call(
        functools.partial(_combine_ple_kernel, tm=tm, final=final, with_proj=proj is not None),
        out_shape=out_shape,
        grid=(t // tm,),
        in_specs=in_specs,
        out_specs=out_specs,
        scratch_shapes=[pltpu.SMEM((pos_tiles.shape[0] // (t // tm),), I32),
                        pltpu.VMEM((2, 2, tm * V7X_SUBLANES, V7X_LANES), F32),
                        pltpu.SemaphoreType.DMA, pltpu.SemaphoreType.DMA((2,))],
        compiler_params=_params("arbitrary"),
        name="combine_ple_final" if final else "combine_ple",
    )(*args)


_SMEM_RECORD_WORDS = 1024


def _index_records(pos1, pos2, tm, extra=None):
    nb = pos1.shape[0] // tm
    parts = [pos1.reshape(nb, tm), pos2.reshape(nb, tm)]
    if extra is not None:
        parts.append(extra.reshape(nb, -1))
    rec = jnp.concatenate(parts, axis=1)
    pad = -rec.shape[1] % _SMEM_RECORD_WORDS
    return jnp.pad(rec, ((0, 0), (0, pad))).reshape(-1)


def _lookup(table, idx):
    ids = jnp.arange(table.shape[0], dtype=I32)
    return jnp.sum(jnp.where(idx[None, :] == ids[:, None], table[:, None], 0), axis=0)


def _bucket(ends, x):
    return jnp.minimum(jnp.sum((x[None, :] >= ends[:, None]).astype(I32), axis=0), ends.shape[0] - 1)


def kernel(x, p, norm_mix_w, w_in, rwkv_mu, rwkv_w0, rwkv_w2, rwkv_a0, rwkv_a2, rwkv_g2, rwkv_k_k, rwkv_k_a, rwkv_r_k, rwkv_ln_w, rwkv_ln_b, rwkv_v0, rwkv_v1, rwkv_v2, att_rel_bias, w_out, norm_ffn_w, router_group_w, router_group_b, router_expert_w, router_expert_b, expert_w1, expert_w3, expert_w2, norm_ple_w, ple_gate_w, ple_gate_b, ple_proj_w, final_norm_w):
    batch, seq, d = x.shape
    depth = w_in.shape[0]
    t = batch * seq
    d_r = rwkv_w0.shape[1]
    n_heads_r = d_r // HEAD_DIM
    n_rwkv_in = rwkv_mu.shape[1]
    d_a = (w_in.shape[2] - n_rwkv_in) // 3
    n_heads_a = d_a // HEAD_DIM
    n_dec, n_iclr, n_gate = rwkv_w2.shape[1], rwkv_a2.shape[1], rwkv_g2.shape[1]
    assert n_dec == n_iclr and n_gate == n_dec + n_iclr
    n_lo = n_dec + n_iclr + n_gate
    f_exp = expert_w1.shape[-1]
    assert d == V7X_SUBLANES * V7X_LANES
    n_rows = 2 * t + N_EXPERTS * TM_EXPERT
    n_tiles = n_rows // TM_EXPERT
    qb = min(QB_ATTN, seq)

    w1_all = expert_w1.reshape(depth * N_EXPERTS, d, f_exp)
    w3_all = expert_w3.reshape(depth * N_EXPERTS, d, f_exp)
    w2_all = expert_w2.reshape(depth * N_EXPERTS, f_exp, d)

    h = x.reshape(t, d)
    v_first = None
    for i in range(depth):
        wr = w_in[i, :, :n_rwkv_in].astype(BF16)
        wa = w_in[i, :, n_rwkv_in:].astype(BF16)
        wl = jnp.zeros((n_lo, 3 * d_r), F32)
        wl = wl.at[:n_dec, :d_r].set(rwkv_w2[i])
        wl = wl.at[n_dec:n_dec + n_iclr, d_r:2 * d_r].set(rwkv_a2[i])
        wl = wl.at[n_dec + n_iclr:, 2 * d_r:].set(rwkv_g2[i]).astype(BF16)
        v0 = rwkv_v0[i - 1] if i > 0 else jnp.zeros((d_r,), F32)
        vec = jnp.stack([rwkv_w0[i], rwkv_a0[i], rwkv_k_k[i], rwkv_k_a[i], rwkv_r_k[i],
                         rwkv_ln_w[i], rwkv_ln_b[i], v0])
        if i > 0:
            n_vr = rwkv_v1.shape[2]
            v1 = jnp.zeros((d_r, V7X_LANES), F32).at[:, :n_vr].set(rwkv_v1[i - 1]).astype(BF16)
            v2 = jnp.zeros((V7X_LANES, d_r), F32).at[:n_vr, :].set(rwkv_v2[i - 1]).astype(BF16)
        else:
            v1 = v2 = None
        table = _attn_table(att_rel_bias[i], qb)
        wor = w_out[i, :d_r].astype(BF16)
        woa = w_out[i, d_r:].astype(BF16)
        n_rt = N_GROUPS + N_EXPERTS
        wrt = jnp.zeros((d, V7X_LANES), F32)
        wrt = wrt.at[:, :N_GROUPS].set(router_group_w[i]).at[:, N_GROUPS:n_rt].set(router_expert_w[i])
        wrt_hi = wrt.astype(BF16)
        wrt = jnp.concatenate([wrt_hi, (wrt - wrt_hi.astype(F32)).astype(BF16)], axis=1)
        brt = jnp.zeros((1, V7X_LANES), F32)
        brt = brt.at[0, :N_GROUPS].set(router_group_b[i]).at[0, N_GROUPS:n_rt].set(router_expert_b[i])

        if i == 0:
            z_r, qkv = _norm_proj(h, norm_mix_w[i][None], wr, wa)
        if i == 0:
            y_r, v_first = _rwkv(z_r, None, rwkv_mu[i][None], vec, wl, None, None,
                                 batch=batch, seq=seq, n_heads=n_heads_r)
        else:
            y_r = _rwkv(z_r, v_first, rwkv_mu[i][None], vec, wl, v1, v2,
                        batch=batch, seq=seq, n_heads=n_heads_r)
        y_a = _attn(qkv, table, batch=batch, seq=seq, n_heads=n_heads_a)

        h1, hn, route, route_t, cnt, tstat = _outproj_route(y_r, y_a, h, wor, woa, norm_ffn_w[i][None],
                                                            wrt, brt)
        ri = route_t.astype(I32)
        counts = cnt[0, :N_EXPERTS].astype(I32)
        padded = ((counts + TM_EXPERT - 1) // TM_EXPERT) * TM_EXPERT
        p_end = jnp.cumsum(padded)
        p_start = p_end - padded
        pos1 = _lookup(p_start, ri[_R_E1]) + ri[_R_RANK1]
        pos2 = _lookup(p_start, ri[_R_E2]) + ri[_R_RANK2]
        tile_start = jnp.arange(n_tiles, dtype=I32) * TM_EXPERT
        tile_expert = _bucket(p_end, tile_start)
        tile_valid = (tile_start < p_end[-1]).astype(I32)

        tile_first = jnp.concatenate([jnp.ones((1,), I32),
                                      (tile_expert[1:] != tile_expert[:-1]).astype(I32)])

        tm_d = min(TM_DISPATCH, t)
        nb_d = t // tm_d
        assert tm_d == min(TM_ROUTE, t)
        ts = tstat.reshape(nb_d, V7X_SUBLANES, -1)[:, :2, :N_EXPERTS].astype(I32)
        tile_cnt, tile_before = ts[:, 0], ts[:, 1]
        local_start = jnp.cumsum(tile_cnt, axis=1) - tile_cnt
        run_dst = p_start[None, :] + tile_before
        ls_tok = jnp.repeat(local_start.T, tm_d, axis=1)
        ids = jnp.arange(N_EXPERTS, dtype=I32)[:, None]
        lpos1 = jnp.sum(jnp.where(ri[_R_E1][None, :] == ids, ls_tok, 0), axis=0) + ri[_R_LRANK1]
        lpos2 = jnp.sum(jnp.where(ri[_R_E2][None, :] == ids, ls_tok, 0), axis=0) + ri[_R_LRANK2]
        idx_d = _index_records(lpos1, lpos2, tm_d,
                               jnp.concatenate([tile_cnt, run_dst, local_start], axis=1))
        n_free = n_rows - 2 * t
        zero_rows = TM_EXPERT // 2
        zinfo = jnp.concatenate([p_start + counts, padded - counts,
                                 jnp.stack([p_end[-1], (n_rows - p_end[-1]) // zero_rows])])
        zinfo = jnp.pad(zinfo, (0, -zinfo.shape[0] % _SMEM_RECORD_WORDS)).astype(I32)
        xs = _dispatch(hn, idx_d, zinfo, n_rows, tm_d, n_free)
        ys = _experts(xs, w1_all, w3_all, w2_all, tile_expert + i * N_EXPERTS, tile_valid, tile_first)

        last = i == depth - 1
        tm_c = min(TM_COMBINE if last else TM_COMBINE_PROJ, t)
        proj = None
        if not last:
            proj = (norm_mix_w[i + 1][None], w_in[i + 1, :, :n_rwkv_in].astype(BF16),
                    w_in[i + 1, :, n_rwkv_in:].astype(BF16))
        res = _combine_ple(_index_records(pos1, pos2, tm_c), ys, h1, route, p.reshape(depth * t, -1),
                           norm_ple_w[i][None], ple_gate_w[i].astype(BF16), ple_gate_b[i][None],
                           ple_proj_w[i].astype(BF16), final_norm_w[None], proj,
                           tm=tm_c, final=last, layer=i)
        if last:
            h = res
        else:
            h, z_r, qkv = res
    return h.reshape(batch, seq, d)
```

```python
import functools

import jax
import jax.numpy as jnp
from jax import lax
from jax.experimental import pallas as pl
from jax.experimental.pallas import tpu as pltpu

F32 = jnp.float32
BF16 = jnp.bfloat16
I32 = jnp.int32

CHUNK = 64
HEAD_DIM = 64
LEFT_CHUNKS = 8
MAX_REL = 256
N_GROUPS = 4
EXPERTS_PER_GROUP = 8
N_EXPERTS = N_GROUPS * EXPERTS_PER_GROUP
RMS_EPS = 1e-6
GN_EPS = 64e-5
NEG_INF = -1e30

V7X_LANES = 128
V7X_SUBLANES = 8
V7X_VMEM_LIMIT_BYTES = 56 * 1024 * 1024

TM_PROJ = 1024
TB_RWKV = 512
RWKV_CHUNK_GROUP = 4
QB_ATTN = 256
TM_ROUTE = 512
TM_DISPATCH = 512
TM_EXPERT = 512
TM_COMBINE = 1024


def _params(*sem):
    return pltpu.CompilerParams(dimension_semantics=sem, vmem_limit_bytes=V7X_VMEM_LIMIT_BYTES)


def _rms(x, w):
    return x * lax.rsqrt(jnp.mean(x * x, axis=-1, keepdims=True) + RMS_EPS) * w


def _mm(a, b):
    return jnp.dot(a.astype(BF16), b.astype(BF16), preferred_element_type=F32)


def _mm_nt(a, b):
    return lax.dot_general(a.astype(BF16), b.astype(BF16), (((1,), (1,)), ((), ())),
                           preferred_element_type=F32)


def _mm_tn(a, b):
    return lax.dot_general(a.astype(BF16), b.astype(BF16), (((0,), (0,)), ((), ())),
                           preferred_element_type=F32)


def _to_token_tiles(ref, x):
    m, d = x.shape
    for s in range(d // V7X_LANES):
        ref[pl.ds(s, m, stride=V7X_SUBLANES), :] = x[:, s * V7X_LANES:(s + 1) * V7X_LANES]


def _from_token_tiles(ref, m):
    return jnp.concatenate([ref[pl.ds(s, m, stride=V7X_SUBLANES), :] for s in range(V7X_SUBLANES)],
                           axis=-1)


def _token_tile(ref, row):
    return ref.at[pl.ds(pl.multiple_of(row * V7X_SUBLANES, V7X_SUBLANES), V7X_SUBLANES), :]


def _split3(x):
    hi = x.astype(BF16)
    r1 = x - hi.astype(F32)
    mid = r1.astype(BF16)
    lo = (r1 - mid.astype(F32)).astype(BF16)
    return hi, mid, lo


def _mm_exact_lhs(a_bf16, x):
    hi, mid, lo = _split3(x)
    return (jnp.dot(a_bf16, hi, preferred_element_type=F32)
            + jnp.dot(a_bf16, mid, preferred_element_type=F32)
            + jnp.dot(a_bf16, lo, preferred_element_type=F32))


def _mm_split2_rhs(x, b_bf16):
    hi = x.astype(BF16)
    lo = (x - hi.astype(F32)).astype(BF16)
    return (jnp.dot(hi, b_bf16, preferred_element_type=F32)
            + jnp.dot(lo, b_bf16, preferred_element_type=F32))


def _norm_proj_kernel(h_ref, nw_ref, wr_ref, wa_ref, zr_ref, qkv_ref):
    hn = _rms(h_ref[...], nw_ref[...]).astype(BF16)
    zr_ref[...] = jnp.dot(hn, wr_ref[...], preferred_element_type=F32)
    qkv_ref[...] = jnp.dot(hn, wa_ref[...], preferred_element_type=F32).astype(BF16)


def _norm_proj(h, nw, wr, wa):
    t, d = h.shape
    tm = min(TM_PROJ, t)
    n_r, n_a = wr.shape[1], wa.shape[1]
    return pl.pallas_call(
        _norm_proj_kernel,
        out_shape=(jax.ShapeDtypeStruct((t, n_r), F32), jax.ShapeDtypeStruct((t, n_a), BF16)),
        grid=(t // tm,),
        in_specs=[pl.BlockSpec((tm, d), lambda i: (i, 0)),
                  pl.BlockSpec((1, d), lambda i: (0, 0)),
                  pl.BlockSpec((d, n_r), lambda i: (0, 0)),
                  pl.BlockSpec((d, n_a), lambda i: (0, 0))],
        out_specs=(pl.BlockSpec((tm, n_r), lambda i: (i, 0)),
                   pl.BlockSpec((tm, n_a), lambda i: (i, 0))),
        compiler_params=_params("parallel"),
        name="norm_proj",
    )(h, nw, wr, wa)


_V_W0, _V_A0, _V_KK, _V_KA, _V_RK, _V_LNW, _V_LNB, _V_V0 = range(8)


def _rwkv_kernel(*refs, has_vres, n_heads, d_r, group, nb):
    if has_vres:
        (z_ref, vf_ref, mu_ref, vec_ref, wl_ref, tril_ref, ones_ref, v1_ref, v2_ref, y_ref,
         s_ref, carry_ref, r_s, k_s, v_s, kk_s, a_s, lc_s, lw_s, bon_s, g_s) = refs
        vf_out_ref = None
    else:
        (z_ref, mu_ref, vec_ref, wl_ref, tril_ref, ones_ref, y_ref, vf_out_ref,
         s_ref, carry_ref, r_s, k_s, v_s, kk_s, a_s, lc_s, lw_s, bon_s, g_s) = refs
    per_token = (r_s, k_s, v_s, kk_s, a_s, lc_s, lw_s, bon_s, g_s)
    tb = z_ref.shape[0]
    gr = group * CHUNK
    k_step = pl.program_id(0)
    cur = lax.rem(k_step, 2)
    prv = 1 - cur

    @pl.when(k_step == 0)
    def _():
        s_ref[...] = jnp.zeros_like(s_ref)
        for ref in per_token:
            ref[...] = jnp.zeros_like(ref)

    @pl.when(lax.rem(k_step, nb) == 0)
    def _():
        carry_ref[...] = jnp.zeros_like(carry_ref)

    vec = vec_ref[...]

    def vrow(i):
        return vec[i:i + 1, :]

    ln_w = vrow(_V_LNW)
    ln_b = vrow(_V_LNB)
    mu = mu_ref[...]
    head_ones = ones_ref[...]
    tril = tril_ref[...]
    ci = lax.broadcasted_iota(I32, (CHUNK, CHUNK), 0)
    cj = lax.broadcasted_iota(I32, (CHUNK, CHUNK), 1)
    strict = cj < ci
    lower = cj <= ci
    eye = ci == cj
    eye_f = jnp.where(eye, 1.0, 0.0)
    hs = [slice(h * HEAD_DIM, (h + 1) * HEAD_DIM) for h in range(n_heads)]
    prev_first = lax.rem(k_step + nb - 1, nb) == 0

    def token_work(rows):
        z = z_ref[rows, :]
        row = lax.broadcasted_iota(I32, z.shape, 0)
        z_prev = jnp.where(row == 0, carry_ref[0:1, :], pltpu.roll(z, 1, axis=0))
        carry_ref[0:1, :] = z[gr - 1:gr, :]
        zs = z + (z_prev - z) * mu
        r = zs[:, 0:d_r]
        k = zs[:, d_r:2 * d_r]
        v = zs[:, 2 * d_r:3 * d_r]
        lo = zs[:, 3 * d_r:]
        n_lo = lo.shape[1]
        lane = lax.broadcasted_iota(I32, lo.shape, 1)
        lo_act = jnp.where(lane < n_lo // 4, jnp.tanh(lo),
                           jnp.where(lane < n_lo // 2, lo, jax.nn.sigmoid(lo)))
        lo_out = _mm(lo_act, wl_ref[...])
        if has_vres:
            vv = _mm(_mm(v, v1_ref[...]), v2_ref[...])
            v = v + (vf_ref[rows, :] - v) * jax.nn.sigmoid(vrow(_V_V0) + vv)
        w_log = -jax.nn.softplus(-(vrow(_V_W0) + lo_out[:, 0:d_r])) - 0.5
        lw = -jnp.exp(w_log)
        a = jax.nn.sigmoid(vrow(_V_A0) + lo_out[:, d_r:2 * d_r])
        kk = k * vrow(_V_KK)
        kk = kk * lax.rsqrt(jnp.maximum(_mm_split2_rhs(kk * kk, head_ones), 1e-24))
        k2 = k * (1.0 + (a - 1.0) * vrow(_V_KA))
        r_s[cur, rows, :] = r
        k_s[cur, rows, :] = k2
        v_s[cur, rows, :] = v
        kk_s[cur, rows, :] = kk
        a_s[cur, rows, :] = a
        lc_s[cur, rows, :] = _mm_exact_lhs(tril, lw)
        lw_s[cur, rows, :] = lw
        bon_s[cur, rows, :] = _mm_split2_rhs(r * k2 * vrow(_V_RK), head_ones) * v
        g_s[cur, rows, :] = lo_out[:, 2 * d_r:3 * d_r]

    def chunk_operands(r0):
        rs = pl.ds(r0, CHUNK)
        lc_c = lc_s[prv, rs, :]
        lw_c = lw_s[prv, rs, :]
        l_end = lc_s[prv, pl.ds(r0 + CHUNK - 1, 1), :]
        p_in = jnp.exp(lc_c)
        p_prev = jnp.exp(lc_c - lw_c)
        p_inv = jnp.exp(-lc_c)
        p_end = jnp.exp(l_end - lc_c)
        p_last = jnp.exp(l_end)
        kk_c = kk_s[prv, rs, :]
        b_c = kk_c * a_s[prv, rs, :]
        k_c = k_s[prv, rs, :]
        at = (-kk_c * p_prev).astype(BF16)
        bt = (b_c * p_inv).astype(BF16)
        bh = (b_c * p_end).astype(BF16)
        kt = (k_c * p_inv).astype(BF16)
        kh = (k_c * p_end).astype(BF16)
        rt = (r_s[prv, rs, :] * p_in).astype(BF16)
        vc = v_s[prv, rs, :].astype(BF16)
        per_head = [[x[:, sl] for sl in hs] for x in (at, bt, bh, kt, kh, rt, vc)]
        per_head.append([p_last[:, sl] for sl in hs])
        return per_head

    def group_body(gi, carry):
        g0 = pl.multiple_of(gi * gr, gr)
        rows = pl.ds(g0, gr)
        if vf_out_ref is not None:
            vf_out_ref[rows, :] = v_s[prv, rows, :]
        ops = [chunk_operands(g0 + c * CHUNK) for c in range(group)]
        at_h, bt_h, bh_h, kt_h, kh_h, rt_h, v_h, pl_h = ([x for c in range(group) for x in ops[c][q]]
                                                         for q in range(8))
        heads = range(group * n_heads)
        ar_h = [jnp.concatenate([at_h[h], rt_h[h]], axis=0) for h in heads]
        m_b = [_mm_nt(ar_h[h], bt_h[h]) for h in heads]
        m_k = [_mm_nt(ar_h[h], kt_h[h]) for h in heads]
        n_ab = [jnp.where(strict, m_b[h][:CHUNK], 0.0) for h in heads]
        a_ak = [jnp.where(strict, m_k[h][:CHUNK], 0.0) for h in heads]
        a_rb = [jnp.where(lower, m_b[h][CHUNK:], 0.0) for h in heads]
        a_rk = [jnp.where(lower, m_k[h][CHUNK:], 0.0) for h in heads]
        x_inv = [eye_f + n_ab[h] for h in heads]
        pw = [_mm(n_ab[h], n_ab[h]) for h in heads]
        akv = [_mm(a_ak[h], v_h[h]) for h in heads]
        n_sq = CHUNK.bit_length() - 2
        for it in range(n_sq):
            if it < n_sq - 1:
                st = [_mm(jnp.concatenate([x_inv[h], pw[h]], axis=0), pw[h]) for h in heads]
                x_inv = [x_inv[h] + st[h][:CHUNK] for h in heads]
                pw = [st[h][CHUNK:] for h in heads]
            else:
                st = [_mm(x_inv[h], pw[h]) for h in heads]
                x_inv = [x_inv[h] + st[h] for h in heads]
        w_h = [_mm(x_inv[h], at_h[h]) for h in heads]
        u0 = [_mm(x_inv[h], akv[h]) for h in heads]
        y0 = [_mm(a_rk[h], v_h[h]) + _mm(a_rb[h], u0[h]) for h in heads]
        r_p = [rt_h[h].astype(F32) + _mm(a_rb[h], w_h[h]) for h in heads]
        g_h = [jnp.where(eye, pl_h[h], 0.0) + _mm_tn(w_h[h], bh_h[h]) for h in heads]
        d_h = [_mm_tn(u0[h], bh_h[h]) + _mm_tn(v_h[h], kh_h[h]) for h in heads]
        fresh = prev_first & (gi == 0)
        s_h = [jnp.where(fresh, 0.0, s_ref[h]) for h in range(n_heads)]
        for c in range(group):
            rs = pl.ds(g0 + c * CHUNK, CHUNK)
            idx = [c * n_heads + h for h in range(n_heads)]
            y_h = [y0[i] + _mm_nt(r_p[i], s_h[h]) for h, i in enumerate(idx)]
            s_h = [_mm(s_h[h], g_h[i]) + d_h[i] for h, i in enumerate(idx)]
            y_heads = []
            for h in range(n_heads):
                mean = jnp.mean(y_h[h], axis=-1, keepdims=True)
                yc = y_h[h] - mean
                var = jnp.mean(yc * yc, axis=-1, keepdims=True)
                y_heads.append(yc * lax.rsqrt(var + GN_EPS))
            y_n = jnp.concatenate(y_heads, axis=-1)
            out = (y_n * ln_w + ln_b + bon_s[prv, rs, :]) * g_s[prv, rs, :]
            y_ref[rs, :] = out.astype(y_ref.dtype)
        for h in range(n_heads):
            s_ref[h] = s_h[h]
        token_work(rows)
        return carry

    lax.fori_loop(0, tb // gr, group_body, 0)


def _rwkv(z, v_first, mu, vec, wl, v1, v2, *, batch, seq, n_heads):
    t, n_z = z.shape
    d_r = n_heads * HEAD_DIM
    tb = min(TB_RWKV, seq)
    nb = seq // tb
    n_blocks = batch * nb
    has_vres = v_first is not None
    tok_in = lambda k: (jnp.minimum(k, n_blocks - 1), 0)
    tok_out = lambda k: (jnp.maximum(k - 1, 0), 0)
    const = lambda k: (0, 0)
    in_specs = [pl.BlockSpec((tb, n_z), tok_in)]
    args = [z]
    if has_vres:
        in_specs.append(pl.BlockSpec((tb, d_r), tok_in))
        args.append(v_first)
    n_chunks = tb // CHUNK
    group = RWKV_CHUNK_GROUP if n_chunks % RWKV_CHUNK_GROUP == 0 else 1
    ti = jnp.arange(group * CHUNK)
    tril = ((ti[:, None] // CHUNK == ti[None, :] // CHUNK) & (ti[None, :] <= ti[:, None])).astype(BF16)
    hi = jnp.arange(d_r) // HEAD_DIM
    head_ones = (hi[:, None] == hi[None, :]).astype(BF16)
    in_specs += [pl.BlockSpec(mu.shape, const), pl.BlockSpec(vec.shape, const),
                 pl.BlockSpec(wl.shape, const), pl.BlockSpec(tril.shape, const),
                 pl.BlockSpec(head_ones.shape, const)]
    args += [mu, vec, wl, tril, head_ones]
    if has_vres:
        in_specs += [pl.BlockSpec(v1.shape, const), pl.BlockSpec(v2.shape, const)]
        args += [v1, v2]
        out_shape = jax.ShapeDtypeStruct((t, d_r), BF16)
        out_specs = pl.BlockSpec((tb, d_r), tok_out)
    else:
        out_shape = (jax.ShapeDtypeStruct((t, d_r), BF16), jax.ShapeDtypeStruct((t, d_r), F32))
        out_specs = (pl.BlockSpec((tb, d_r), tok_out), pl.BlockSpec((tb, d_r), tok_out))
    scratch = [pltpu.VMEM((n_heads, HEAD_DIM, HEAD_DIM), F32),
               pltpu.VMEM((V7X_SUBLANES, n_z), F32)]
    scratch += [pltpu.VMEM((2, tb, d_r), F32) for _ in range(9)]
    return pl.pallas_call(
        functools.partial(_rwkv_kernel, has_vres=has_vres, n_heads=n_heads, d_r=d_r, group=group, nb=nb),
        out_shape=out_shape,
        grid=(n_blocks + 1,),
        in_specs=in_specs,
        out_specs=out_specs,
        scratch_shapes=scratch,
        compiler_params=_params("arbitrary"),
        name="rwkv_vres" if has_vres else "rwkv",
    )(*args)


def _attn_kernel(*refs, n_heads, n_parts):
    q_ref = refs[0]
    k_refs = refs[1:1 + n_parts]
    v_refs = refs[1 + n_parts:1 + 2 * n_parts]
    tab_ref = refs[1 + 2 * n_parts]
    o_ref = refs[2 + 2 * n_parts]
    qb = q_ref.shape[0]
    j = pl.program_id(1)
    scale = HEAD_DIM ** -0.5
    q = q_ref[...] * jnp.asarray(scale, q_ref.dtype)
    ks = [r[...] for r in k_refs]
    vs = [r[...] for r in v_refs]
    pw = 2 * HEAD_DIM
    lane = lax.broadcasted_iota(I32, (qb, pw), 1)
    sum_even = jnp.where(lane < HEAD_DIM, 1.0, 0.0).astype(q.dtype)
    sum_odd = jnp.where(lane < HEAD_DIM, 0.0, 1.0).astype(q.dtype)
    zero = jnp.zeros((), q.dtype)
    is_even = sum_even > zero
    outs = []
    for hp in range(n_heads // 2):
        sl = slice(hp * pw, (hp + 1) * pw)
        qq = q[:, sl]
        q_pair = (jnp.where(is_even, qq, zero), jnp.where(is_even, zero, qq))
        s_parts = [[], []]
        for p in range(n_parts):
            kk = ks[p][:, sl]
            back = n_parts - 1 - p
            for u in range(2):
                s = _mm_nt(q_pair[u], kk) + tab_ref[2 * hp + u, :, p * qb:(p + 1) * qb]
                if back > 0:
                    s = jnp.where(j >= back, s, NEG_INF)
                s_parts[u].append(s)
        m = []
        for u in range(2):
            mm = s_parts[u][0]
            for s in s_parts[u][1:]:
                mm = jnp.maximum(mm, s)
            m.append(mm.max(axis=-1, keepdims=True))
        acc = jnp.zeros((qb, 2 * pw), F32)
        for p in range(n_parts):
            vv = vs[p][:, sl]
            rhs = jnp.concatenate(
                [jnp.concatenate([jnp.where(is_even, vv, zero), sum_even], axis=1),
                 jnp.concatenate([jnp.where(is_even, zero, vv), sum_odd], axis=1)], axis=0)
            e = jnp.concatenate([jnp.exp(s_parts[u][p] - m[u]).astype(BF16) for u in range(2)], axis=1)
            acc = acc + jnp.dot(e, rhs, preferred_element_type=F32)
        outs.append(acc[:, :pw] / acc[:, pw:])
    o_ref[...] = jnp.concatenate(outs, axis=-1).astype(o_ref.dtype)


def _attn(qkv, table, *, batch, seq, n_heads):
    t = qkv.shape[0]
    d_a = n_heads * HEAD_DIM
    qb = min(QB_ATTN, seq)
    left = LEFT_CHUNKS * CHUNK
    assert left % qb == 0 and seq % qb == 0
    n_parts = left // qb + 1
    nb = seq // qb
    in_specs = [pl.BlockSpec((qb, d_a), lambda b, j: (b * nb + j, 0))]
    for p in range(n_parts):
        back = n_parts - 1 - p
        in_specs.append(pl.BlockSpec((qb, d_a), lambda b, j, back=back: (b * nb + jnp.maximum(j - back, 0), 1)))
    for p in range(n_parts):
        back = n_parts - 1 - p
        in_specs.append(pl.BlockSpec((qb, d_a), lambda b, j, back=back: (b * nb + jnp.maximum(j - back, 0), 2)))
    in_specs.append(pl.BlockSpec(table.shape, lambda b, j: (0, 0, 0)))
    return pl.pallas_call(
        functools.partial(_attn_kernel, n_heads=n_heads, n_parts=n_parts),
        out_shape=jax.ShapeDtypeStruct((t, d_a), BF16),
        grid=(batch, nb),
        in_specs=in_specs,
        out_specs=pl.BlockSpec((qb, d_a), lambda b, j: (b * nb + j, 0)),
        compiler_params=_params("parallel", "arbitrary"),
        name="attn",
    )(*([qkv] * (1 + 2 * n_parts)), table)


def _attn_table(rel_bias, qb):
    left = LEFT_CHUNKS * CHUNK
    n_keys = left + qb
    period = qb + n_keys - 1
    n_heads = rel_bias.shape[0]
    m = jnp.arange(period)
    rel = left - jnp.where(m < n_keys, m, m - period)
    g = rel_bias[:, jnp.clip(rel, -(CHUNK - 1), MAX_REL) + (CHUNK - 1)].astype(F32)
    flat = jnp.tile(g, (1, qb))[:, :qb * (period - 1)]
    bias = flat.reshape(n_heads, qb, period - 1)[:, :, :n_keys]
    cq = jnp.arange(qb)[:, None] // CHUNK
    ck = jnp.arange(n_keys)[None, :] // CHUNK
    valid = (ck >= cq) & (ck <= cq + LEFT_CHUNKS)
    return jnp.where(valid[None], bias, NEG_INF)


_R_E1, _R_E2, _R_C1, _R_C2, _R_RANK1, _R_RANK2, _R_LRANK1, _R_LRANK2 = range(8)


def _outproj_route_kernel(yr_ref, ya_ref, h_ref, wor_ref, woa_ref, nw_ref, wrt_ref, brt_ref,
                          h1_ref, hn_ref, route_ref, route_t_ref, cnt_ref, tstat_ref, carry_ref):
    i = pl.program_id(0)

    @pl.when(i == 0)
    def _():
        carry_ref[...] = jnp.zeros_like(carry_ref)

    h1 = (h_ref[...] + jnp.dot(yr_ref[...], wor_ref[...], preferred_element_type=F32)
          + jnp.dot(ya_ref[...], woa_ref[...], preferred_element_type=F32))
    h1_ref[...] = h1
    hn = _rms(h1, nw_ref[...])
    _to_token_tiles(hn_ref, hn)
    nl = brt_ref.shape[1]
    hn_hi = hn.astype(BF16)
    hn_lo = (hn - hn_hi.astype(F32)).astype(BF16)
    part = jnp.dot(hn_hi, wrt_ref[...], preferred_element_type=F32)
    logits = (part[:, :nl] + part[:, nl:]
              + jnp.dot(hn_lo, wrt_ref[:, :nl], preferred_element_type=F32) + brt_ref[...])
    tm = logits.shape[0]
    lane = lax.broadcasted_iota(I32, (tm, nl), 1)
    lane_f = lane.astype(F32)
    ninf = -jnp.inf
    big = float(nl)
    is_g = lane < N_GROUPS
    gl = jnp.where(is_g, logits, ninf)
    g_max = gl.max(axis=-1, keepdims=True)
    g_sel = jnp.where(gl == g_max, lane_f, big).min(axis=-1, keepdims=True)
    p_g = 1.0 / jnp.where(is_g, jnp.exp(logits - g_max), 0.0).sum(axis=-1, keepdims=True)
    e_lo = N_GROUPS + EXPERTS_PER_GROUP * g_sel
    in_grp = (lane_f >= e_lo) & (lane_f < e_lo + EXPERTS_PER_GROUP)
    el = jnp.where(in_grp, logits, ninf)
    m1 = el.max(axis=-1, keepdims=True)
    i1 = jnp.where(el == m1, lane_f, big).min(axis=-1, keepdims=True)
    el2 = jnp.where(lane_f == i1, ninf, el)
    m2 = el2.max(axis=-1, keepdims=True)
    i2 = jnp.where(el2 == m2, lane_f, big).min(axis=-1, keepdims=True)
    t2 = jnp.exp(m2 - m1)
    c1 = p_g / (1.0 + t2)
    c2 = p_g * t2 / (1.0 + t2)
    e1 = i1 - N_GROUPS
    e2 = i2 - N_GROUPS
    oh1 = lane_f == e1
    oh2 = lane_f == e2
    ohs = jnp.where(oh1 | oh2, 1.0, 0.0)
    ri = lax.broadcasted_iota(I32, (tm, tm), 0)
    rj = lax.broadcasted_iota(I32, (tm, tm), 1)
    before = jnp.where(rj < ri, 1.0, 0.0).astype(BF16)
    old_carry = carry_ref[0:1, :]
    cnt_tile = jnp.dot(before, ohs.astype(BF16), preferred_element_type=F32)
    cnt = cnt_tile + old_carry
    rank1 = jnp.where(oh1, cnt, 0.0).sum(axis=-1, keepdims=True)
    rank2 = jnp.where(oh2, cnt, 0.0).sum(axis=-1, keepdims=True)
    lrank1 = jnp.where(oh1, cnt_tile, 0.0).sum(axis=-1, keepdims=True)
    lrank2 = jnp.where(oh2, cnt_tile, 0.0).sum(axis=-1, keepdims=True)
    tile_cnt = ohs.sum(axis=0, keepdims=True)
    new_carry = old_carry + tile_cnt
    carry_ref[0:1, :] = new_carry
    cnt_ref[...] = jnp.broadcast_to(new_carry, cnt_ref.shape)
    srow = lax.broadcasted_iota(I32, tstat_ref.shape, 0)
    tstat_ref[...] = jnp.where(srow == 0, tile_cnt, jnp.where(srow == 1, old_carry, 0.0))
    route = jnp.zeros((tm, nl), F32)
    for idx, val in ((_R_E1, e1), (_R_E2, e2), (_R_C1, c1), (_R_C2, c2),
                     (_R_RANK1, rank1), (_R_RANK2, rank2), (_R_LRANK1, lrank1), (_R_LRANK2, lrank2)):
        route = jnp.where(lane == idx, val, route)
    route_ref[...] = route
    route_t_ref[...] = route.T[:route_t_ref.shape[0], :]


def _outproj_route(yr, ya, h, wor, woa, nw, wrt, brt):
    t, d = h.shape
    tm = min(TM_ROUTE, t)
    d_r, d_a = yr.shape[1], ya.shape[1]
    nl = brt.shape[1]
    tok = lambda i: (i, 0)
    const = lambda i: (0, 0)
    return pl.pallas_call(
        _outproj_route_kernel,
        out_shape=(jax.ShapeDtypeStruct((t, d), F32),
                   jax.ShapeDtypeStruct((t * V7X_SUBLANES, V7X_LANES), F32),
                   jax.ShapeDtypeStruct((t, nl), F32), jax.ShapeDtypeStruct((V7X_SUBLANES, t), F32),
                   jax.ShapeDtypeStruct((V7X_SUBLANES, nl), F32),
                   jax.ShapeDtypeStruct((t // tm * V7X_SUBLANES, nl), F32)),
        grid=(t // tm,),
        in_specs=[pl.BlockSpec((tm, d_r), tok), pl.BlockSpec((tm, d_a), tok), pl.BlockSpec((tm, d), tok),
                  pl.BlockSpec((d_r, d), const), pl.BlockSpec((d_a, d), const),
                  pl.BlockSpec((1, d), const), pl.BlockSpec(wrt.shape, const), pl.BlockSpec((1, nl), const)],
        out_specs=(pl.BlockSpec((tm, d), tok), pl.BlockSpec((tm * V7X_SUBLANES, V7X_LANES), tok),
                   pl.BlockSpec((tm, nl), tok), pl.BlockSpec((V7X_SUBLANES, tm), lambda i: (0, i)),
                   pl.BlockSpec((V7X_SUBLANES, nl), const), pl.BlockSpec((V7X_SUBLANES, nl), tok)),
        scratch_shapes=[pltpu.VMEM((V7X_SUBLANES, nl), F32)],
        compiler_params=_params("arbitrary"),
        name="outproj_route",
    )(yr, ya, h, wor, woa, nw, wrt, brt)


def _load_indices(idx_hbm, i, idx_smem, sem):
    n = idx_smem.shape[0]
    cp = pltpu.make_async_copy(idx_hbm.at[pl.ds(pl.multiple_of(i * n, n), n)], idx_smem, sem)
    cp.start()
    cp.wait()


def _tiles(ref, row, n):
    start = row * V7X_SUBLANES
    if not isinstance(row, int):
        start = pl.multiple_of(start, V7X_SUBLANES)
    return ref.at[pl.ds(start, n * V7X_SUBLANES), :]


def _start_run_copies(length, n_bits, copy_of):
    for b in range(n_bits):
        @pl.when(((length >> b) & 1) == 1)
        def _(b=b):
            done = (length >> (b + 1)) << (b + 1)
            copy_of(done, 1 << b, b).start(priority=b % 2)


def _dispatch_kernel(idx_hbm, zinfo_hbm, x_ref, xs_hbm, idx_smem, zinfo_smem, xloc, zero_vmem,
                     idx_sem, run_sem, zero_sem, *, tm, n_free):
    i = pl.program_id(0)
    nb = pl.num_programs(0)
    buf = lax.rem(i, 2)
    zero_rows = zero_vmem.shape[0] // V7X_SUBLANES

    @pl.when(i == 0)
    def _():
        cp = pltpu.make_async_copy(zinfo_hbm, zinfo_smem, idx_sem)
        cp.start()
        cp.wait()
        zero_vmem[...] = jnp.zeros_like(zero_vmem)

        def pad_runs(e, carry):
            start = zinfo_smem[e]
            _start_run_copies(
                zinfo_smem[N_EXPERTS + e], zero_rows.bit_length(),
                lambda done, n, b: pltpu.make_async_copy(_tiles(zero_vmem, 0, n),
                                                         _tiles(xs_hbm, start + done, n), zero_sem))
            return carry

        lax.fori_loop(0, N_EXPERTS, pad_runs, 0)
        tail_start = zinfo_smem[2 * N_EXPERTS]

        def tail_block(n, carry):
            pltpu.make_async_copy(zero_vmem, _tiles(xs_hbm, tail_start + n * zero_rows, zero_rows),
                                  zero_sem).start()
            return carry

        lax.fori_loop(0, zinfo_smem[2 * N_EXPERTS + 1], tail_block, 0)

    _load_indices(idx_hbm, i, idx_smem, idx_sem)

    def wait_runs(b):
        pltpu.make_async_copy(xloc.at[b], xs_hbm.at[pl.ds(0, xloc.shape[1]), :], run_sem.at[b]).wait()

    @pl.when(i >= 2)
    def _():
        wait_runs(buf)

    def place(tt, carry):
        row = x_ref[pl.ds(pl.multiple_of(tt * V7X_SUBLANES, V7X_SUBLANES), V7X_SUBLANES), :]
        for s in range(2):
            lp = idx_smem[s * tm + tt]
            xloc[buf, pl.ds(pl.multiple_of(lp * V7X_SUBLANES, V7X_SUBLANES), V7X_SUBLANES), :] = row
        return carry

    lax.fori_loop(0, tm, place, 0, unroll=8)

    def expert_run(e, carry):
        dst = idx_smem[2 * tm + N_EXPERTS + e]
        off = idx_smem[2 * tm + 2 * N_EXPERTS + e]
        _start_run_copies(
            idx_smem[2 * tm + e], tm.bit_length(),
            lambda done, n, b: pltpu.make_async_copy(_tiles(xloc.at[buf], off + done, n),
                                                     _tiles(xs_hbm, dst + done, n), run_sem.at[buf]))
        return carry

    lax.fori_loop(0, N_EXPERTS, expert_run, 0)

    @pl.when(i == nb - 1)
    def _():
        wait_runs(buf)

        @pl.when(nb >= 2)
        def _():
            wait_runs(1 - buf)

        pltpu.make_async_copy(xs_hbm.at[pl.ds(0, n_free * V7X_SUBLANES), :],
                              xs_hbm.at[pl.ds(0, n_free * V7X_SUBLANES), :], zero_sem).wait()


def _dispatch(x, idx, zinfo, n_rows, tm, n_free):
    t = x.shape[0] // V7X_SUBLANES
    nb = t // tm
    rec = idx.shape[0] // nb
    return pl.pallas_call(
        functools.partial(_dispatch_kernel, tm=tm, n_free=n_free),
        out_shape=jax.ShapeDtypeStruct((n_rows * V7X_SUBLANES, V7X_LANES), x.dtype),
        grid=(nb,),
        in_specs=[pl.BlockSpec(memory_space=pl.ANY), pl.BlockSpec(memory_space=pl.ANY),
                  pl.BlockSpec((tm * V7X_SUBLANES, V7X_LANES), lambda i: (i, 0))],
        out_specs=pl.BlockSpec(memory_space=pl.ANY),
        scratch_shapes=[pltpu.SMEM((rec,), I32), pltpu.SMEM(zinfo.shape, I32),
                        pltpu.VMEM((2, 2 * tm * V7X_SUBLANES, V7X_LANES), x.dtype),
                        pltpu.VMEM((TM_EXPERT // 2 * V7X_SUBLANES, V7X_LANES), x.dtype),
                        pltpu.SemaphoreType.DMA, pltpu.SemaphoreType.DMA((2,)), pltpu.SemaphoreType.DMA],
        compiler_params=_params("arbitrary"),
        name="dispatch",
    )(idx, zinfo, x)


def _experts_kernel(te_ref, tv_ref, tf_ref, x_ref, w1_ref, w3_ref, w2_ref, y_ref,
                    w1_b, w3_b, w2_b, *, tm):
    i = pl.program_id(0)

    @pl.when(tf_ref[i] > 0)
    def _():
        w1_b[...] = w1_ref[...].astype(BF16)
        w3_b[...] = w3_ref[...].astype(BF16)
        w2_b[...] = w2_ref[...].astype(BF16)

    @pl.when(tv_ref[i] > 0)
    def _():
        x = _from_token_tiles(x_ref, tm).astype(BF16)
        h_gate = jnp.dot(x, w1_b[...], preferred_element_type=F32)
        h_up = jnp.dot(x, w3_b[...], preferred_element_type=F32)
        hid = (h_gate * jax.nn.sigmoid(h_gate) * h_up).astype(BF16)
        _to_token_tiles(y_ref, jnp.dot(hid, w2_b[...], preferred_element_type=F32))

    @pl.when(tv_ref[i] == 0)
    def _():
        y_ref[...] = jnp.zeros_like(y_ref)


def _experts(xs, w1, w3, w2, tile_expert, tile_valid, tile_first):
    n_rows = xs.shape[0] // V7X_SUBLANES
    tm = TM_EXPERT
    nt = n_rows // tm
    d, f = w1.shape[1:]
    tile_spec = pl.BlockSpec((tm * V7X_SUBLANES, V7X_LANES), lambda i, te, tv, tf: (i, 0))
    grid_spec = pltpu.PrefetchScalarGridSpec(
        num_scalar_prefetch=3,
        grid=(nt,),
        in_specs=[tile_spec,
                  pl.BlockSpec((None, d, f), lambda i, te, tv, tf: (te[i], 0, 0)),
                  pl.BlockSpec((None, d, f), lambda i, te, tv, tf: (te[i], 0, 0)),
                  pl.BlockSpec((None, f, d), lambda i, te, tv, tf: (te[i], 0, 0))],
        out_specs=tile_spec,
        scratch_shapes=[pltpu.VMEM((d, f), BF16), pltpu.VMEM((d, f), BF16), pltpu.VMEM((f, d), BF16)],
    )
    return pl.pallas_call(
        functools.partial(_experts_kernel, tm=tm),
        out_shape=jax.ShapeDtypeStruct(xs.shape, F32),
        grid_spec=grid_spec,
        compiler_params=_params("arbitrary"),
        name="experts",
    )(tile_expert, tile_valid, tile_first, xs, w1, w3, w2)


def _combine_ple_kernel(pos_hbm, ys_hbm, h_ref, route_ref, p_ref, nw_ref, wg_ref, bg_ref, wp_ref,
                        fw_ref, o_ref, idx_smem, ybuf, idx_sem, row_sem, *, tm, final):
    i = pl.program_id(0)
    nb = pl.num_programs(0)
    cur = lax.rem(i, 2)
    nxt = 1 - cur

    def gather(step, buf):
        _load_indices(pos_hbm, step, idx_smem, idx_sem)

        def issue(tt, carry):
            for s in range(2):
                pltpu.make_async_copy(_token_tile(ys_hbm, idx_smem[s * tm + tt]),
                                      _token_tile(ybuf.at[buf, s], tt), row_sem.at[buf]).start(priority=s)
            return carry

        lax.fori_loop(0, tm, issue, 0, unroll=8)

    def wait_gather(buf):
        for s in range(2):
            pltpu.make_async_copy(ys_hbm.at[pl.ds(0, ybuf.shape[2]), :], ybuf.at[buf, s],
                                  row_sem.at[buf]).wait()

    @pl.when(i == 0)
    def _():
        gather(0, 0)

    @pl.when(i + 1 < nb)
    def _():
        gather(i + 1, nxt)

    wait_gather(cur)

    route = route_ref[...]
    c1 = route[:, _R_C1:_R_C1 + 1]
    c2 = route[:, _R_C2:_R_C2 + 1]
    h2 = (h_ref[...] + c1 * _from_token_tiles(ybuf.at[cur, 0], tm)
          + c2 * _from_token_tiles(ybuf.at[cur, 1], tm))
    hn = _rms(h2, nw_ref[...]).astype(BF16)
    gate = jax.nn.sigmoid(jnp.dot(hn, wg_ref[...], preferred_element_type=F32) + bg_ref[...])
    h3 = h2 + gate * jnp.dot(p_ref[...].astype(BF16), wp_ref[...], preferred_element_type=F32)
    if final:
        h3 = _rms(h3, fw_ref[...])
    o_ref[...] = h3


def _combine_ple(pos_tiles, ys, h, route, p, nw, wg, bg, wp, fw, *, tm, final, layer):
    t, d = h.shape
    nl = route.shape[1]
    dp = p.shape[1]
    tok = lambda i: (i, 0)
    const = lambda i: (0, 0)
    return pl.pallas_call(
        functools.partial(_combine_ple_kernel, tm=tm, final=final),
        out_shape=jax.ShapeDtypeStruct((t, d), F32),
        grid=(t // tm,),
        in_specs=[pl.BlockSpec(memory_space=pl.ANY), pl.BlockSpec(memory_space=pl.ANY),
                  pl.BlockSpec((tm, d), tok), pl.BlockSpec((tm, nl), tok),
                  pl.BlockSpec((tm, dp), lambda i: (i + layer * (t // tm), 0)),
                  pl.BlockSpec((1, d), const), pl.BlockSpec((d, d), const), pl.BlockSpec((1, d), const),
                  pl.BlockSpec((dp, d), const), pl.BlockSpec((1, d), const)],
        out_specs=pl.BlockSpec((tm, d), tok),
        scratch_shapes=[pltpu.SMEM((pos_tiles.shape[0] // (t // tm),), I32),
                        pltpu.VMEM((2, 2, tm * V7X_SUBLANES, V7X_LANES), F32),
                        pltpu.SemaphoreType.DMA, pltpu.SemaphoreType.DMA((2,))],
        compiler_params=_params("arbitrary"),
        name="combine_ple_final" if final else "combine_ple",
    )(pos_tiles, ys, h, route, p, nw, wg, bg, wp, fw)


_SMEM_RECORD_WORDS = 1024


def _index_records(pos1, pos2, tm, extra=None):
    nb = pos1.shape[0] // tm
    parts = [pos1.reshape(nb, tm), pos2.reshape(nb, tm)]
    if extra is not None:
        parts.append(extra.reshape(nb, -1))
    rec = jnp.concatenate(parts, axis=1)
    pad = -rec.shape[1] % _SMEM_RECORD_WORDS
    return jnp.pad(rec, ((0, 0), (0, pad))).reshape(-1)


def _lookup(table, idx):
    ids = jnp.arange(table.shape[0], dtype=I32)
    return jnp.sum(jnp.where(idx[None, :] == ids[:, None], table[:, None], 0), axis=0)


def _bucket(ends, x):
    return jnp.minimum(jnp.sum((x[None, :] >= ends[:, None]).astype(I32), axis=0), ends.shape[0] - 1)


def kernel(x, p, norm_mix_w, w_in, rwkv_mu, rwkv_w0, rwkv_w2, rwkv_a0, rwkv_a2, rwkv_g2, rwkv_k_k, rwkv_k_a, rwkv_r_k, rwkv_ln_w, rwkv_ln_b, rwkv_v0, rwkv_v1, rwkv_v2, att_rel_bias, w_out, norm_ffn_w, router_group_w, router_group_b, router_expert_w, router_expert_b, expert_w1, expert_w3, expert_w2, norm_ple_w, ple_gate_w, ple_gate_b, ple_proj_w, final_norm_w):
    batch, seq, d = x.shape
    depth = w_in.shape[0]
    t = batch * seq
    d_r = rwkv_w0.shape[1]
    n_heads_r = d_r // HEAD_DIM
    n_rwkv_in = rwkv_mu.shape[1]
    d_a = (w_in.shape[2] - n_rwkv_in) // 3
    n_heads_a = d_a // HEAD_DIM
    n_dec, n_iclr, n_gate = rwkv_w2.shape[1], rwkv_a2.shape[1], rwkv_g2.shape[1]
    assert n_dec == n_iclr and n_gate == n_dec + n_iclr
    n_lo = n_dec + n_iclr + n_gate
    f_exp = expert_w1.shape[-1]
    assert d == V7X_SUBLANES * V7X_LANES
    n_rows = 2 * t + N_EXPERTS * TM_EXPERT
    n_tiles = n_rows // TM_EXPERT
    qb = min(QB_ATTN, seq)

    w1_all = expert_w1.reshape(depth * N_EXPERTS, d, f_exp)
    w3_all = expert_w3.reshape(depth * N_EXPERTS, d, f_exp)
    w2_all = expert_w2.reshape(depth * N_EXPERTS, f_exp, d)

    h = x.reshape(t, d)
    v_first = None
    for i in range(depth):
        wr = w_in[i, :, :n_rwkv_in].astype(BF16)
        wa = w_in[i, :, n_rwkv_in:].astype(BF16)
        wl = jnp.zeros((n_lo, 3 * d_r), F32)
        wl = wl.at[:n_dec, :d_r].set(rwkv_w2[i])
        wl = wl.at[n_dec:n_dec + n_iclr, d_r:2 * d_r].set(rwkv_a2[i])
        wl = wl.at[n_dec + n_iclr:, 2 * d_r:].set(rwkv_g2[i]).astype(BF16)
        v0 = rwkv_v0[i - 1] if i > 0 else jnp.zeros((d_r,), F32)
        vec = jnp.stack([rwkv_w0[i], rwkv_a0[i], rwkv_k_k[i], rwkv_k_a[i], rwkv_r_k[i],
                         rwkv_ln_w[i], rwkv_ln_b[i], v0])
        if i > 0:
            n_vr = rwkv_v1.shape[2]
            v1 = jnp.zeros((d_r, V7X_LANES), F32).at[:, :n_vr].set(rwkv_v1[i - 1]).astype(BF16)
            v2 = jnp.zeros((V7X_LANES, d_r), F32).at[:n_vr, :].set(rwkv_v2[i - 1]).astype(BF16)
        else:
            v1 = v2 = None
        table = _attn_table(att_rel_bias[i], qb)
        wor = w_out[i, :d_r].astype(BF16)
        woa = w_out[i, d_r:].astype(BF16)
        n_rt = N_GROUPS + N_EXPERTS
        wrt = jnp.zeros((d, V7X_LANES), F32)
        wrt = wrt.at[:, :N_GROUPS].set(router_group_w[i]).at[:, N_GROUPS:n_rt].set(router_expert_w[i])
        wrt_hi = wrt.astype(BF16)
        wrt = jnp.concatenate([wrt_hi, (wrt - wrt_hi.astype(F32)).astype(BF16)], axis=1)
        brt = jnp.zeros((1, V7X_LANES), F32)
        brt = brt.at[0, :N_GROUPS].set(router_group_b[i]).at[0, N_GROUPS:n_rt].set(router_expert_b[i])

        z_r, qkv = _norm_proj(h, norm_mix_w[i][None], wr, wa)
        if i == 0:
            y_r, v_first = _rwkv(z_r, None, rwkv_mu[i][None], vec, wl, None, None,
                                 batch=batch, seq=seq, n_heads=n_heads_r)
        else:
            y_r = _rwkv(z_r, v_first, rwkv_mu[i][None], vec, wl, v1, v2,
                        batch=batch, seq=seq, n_heads=n_heads_r)
        y_a = _attn(qkv, table, batch=batch, seq=seq, n_heads=n_heads_a)

        h1, hn, route, route_t, cnt, tstat = _outproj_route(y_r, y_a, h, wor, woa, norm_ffn_w[i][None],
                                                            wrt, brt)
        ri = route_t.astype(I32)
        counts = cnt[0, :N_EXPERTS].astype(I32)
        padded = ((counts + TM_EXPERT - 1) // TM_EXPERT) * TM_EXPERT
        p_end = jnp.cumsum(padded)
        p_start = p_end - padded
        pos1 = _lookup(p_start, ri[_R_E1]) + ri[_R_RANK1]
        pos2 = _lookup(p_start, ri[_R_E2]) + ri[_R_RANK2]
        tile_start = jnp.arange(n_tiles, dtype=I32) * TM_EXPERT
        tile_expert = _bucket(p_end, tile_start)
        tile_valid = (tile_start < p_end[-1]).astype(I32)

        tile_first = jnp.concatenate([jnp.ones((1,), I32),
                                      (tile_expert[1:] != tile_expert[:-1]).astype(I32)])

        tm_d = min(TM_DISPATCH, t)
        nb_d = t // tm_d
        assert tm_d == min(TM_ROUTE, t)
        ts = tstat.reshape(nb_d, V7X_SUBLANES, -1)[:, :2, :N_EXPERTS].astype(I32)
        tile_cnt, tile_before = ts[:, 0], ts[:, 1]
        local_start = jnp.cumsum(tile_cnt, axis=1) - tile_cnt
        run_dst = p_start[None, :] + tile_before
        ls_tok = jnp.repeat(local_start.T, tm_d, axis=1)
        ids = jnp.arange(N_EXPERTS, dtype=I32)[:, None]
        lpos1 = jnp.sum(jnp.where(ri[_R_E1][None, :] == ids, ls_tok, 0), axis=0) + ri[_R_LRANK1]
        lpos2 = jnp.sum(jnp.where(ri[_R_E2][None, :] == ids, ls_tok, 0), axis=0) + ri[_R_LRANK2]
        idx_d = _index_records(lpos1, lpos2, tm_d,
                               jnp.concatenate([tile_cnt, run_dst, local_start], axis=1))
        n_free = n_rows - 2 * t
        zero_rows = TM_EXPERT // 2
        zinfo = jnp.concatenate([p_start + counts, padded - counts,
                                 jnp.stack([p_end[-1], (n_rows - p_end[-1]) // zero_rows])])
        zinfo = jnp.pad(zinfo, (0, -zinfo.shape[0] % _SMEM_RECORD_WORDS)).astype(I32)
        xs = _dispatch(hn, idx_d, zinfo, n_rows, tm_d, n_free)
        ys = _experts(xs, w1_all, w3_all, w2_all, tile_expert + i * N_EXPERTS, tile_valid, tile_first)

        tm_c = min(TM_COMBINE, t)
        h = _combine_ple(_index_records(pos1, pos2, tm_c), ys, h1, route, p.reshape(depth * t, -1),
                         norm_ple_w[i][None], ple_gate_w[i].astype(BF16), ple_gate_b[i][None],
                         ple_proj_w[i].astype(BF16), final_norm_w[None],
                         tm=tm_c, final=(i == depth - 1), layer=i)
    return h.reshape(batch, seq, d)
```

```python
import functools

import jax
import jax.numpy as jnp
from jax import lax
from jax.experimental import pallas as pl
from jax.experimental.pallas import tpu as pltpu

F32 = jnp.float32
BF16 = jnp.bfloat16
I32 = jnp.int32

CHUNK = 64
HEAD_DIM = 64
LEFT_CHUNKS = 8
MAX_REL = 256
N_GROUPS = 4
EXPERTS_PER_GROUP = 8
N_EXPERTS = N_GROUPS * EXPERTS_PER_GROUP
RMS_EPS = 1e-6
GN_EPS = 64e-5
NEG_INF = -1e30

V7X_LANES = 128
V7X_SUBLANES = 8
V7X_VMEM_LIMIT_BYTES = 40 * 1024 * 1024

TM_PROJ = 512
TB_RWKV = 512
RWKV_CHUNK_GROUP = 4
QB_ATTN = 256
TM_ROUTE = 512
TM_DISPATCH = 512
TM_EXPERT = 512
TM_COMBINE = 512


def _params(*sem):
    return pltpu.CompilerParams(dimension_semantics=sem, vmem_limit_bytes=V7X_VMEM_LIMIT_BYTES)


def _rms(x, w):
    return x * lax.rsqrt(jnp.mean(x * x, axis=-1, keepdims=True) + RMS_EPS) * w


def _mm(a, b):
    return jnp.dot(a.astype(BF16), b.astype(BF16), preferred_element_type=F32)


def _mm_nt(a, b):
    return lax.dot_general(a.astype(BF16), b.astype(BF16), (((1,), (1,)), ((), ())),
                           preferred_element_type=F32)


def _mm_tn(a, b):
    return lax.dot_general(a.astype(BF16), b.astype(BF16), (((0,), (0,)), ((), ())),
                           preferred_element_type=F32)


def _to_token_tiles(ref, x):
    m, d = x.shape
    for s in range(d // V7X_LANES):
        ref[pl.ds(s, m, stride=V7X_SUBLANES), :] = x[:, s * V7X_LANES:(s + 1) * V7X_LANES]


def _from_token_tiles(ref, m):
    return jnp.concatenate([ref[pl.ds(s, m, stride=V7X_SUBLANES), :] for s in range(V7X_SUBLANES)],
                           axis=-1)


def _token_tile(ref, row):
    return ref.at[pl.ds(pl.multiple_of(row * V7X_SUBLANES, V7X_SUBLANES), V7X_SUBLANES), :]


def _split3(x):
    hi = x.astype(BF16)
    r1 = x - hi.astype(F32)
    mid = r1.astype(BF16)
    lo = (r1 - mid.astype(F32)).astype(BF16)
    return hi, mid, lo


def _mm_exact_lhs(a_bf16, x):
    hi, mid, lo = _split3(x)
    return (jnp.dot(a_bf16, hi, preferred_element_type=F32)
            + jnp.dot(a_bf16, mid, preferred_element_type=F32)
            + jnp.dot(a_bf16, lo, preferred_element_type=F32))


def _mm_split2_rhs(x, b_bf16):
    hi = x.astype(BF16)
    lo = (x - hi.astype(F32)).astype(BF16)
    return (jnp.dot(hi, b_bf16, preferred_element_type=F32)
            + jnp.dot(lo, b_bf16, preferred_element_type=F32))


def _norm_proj_kernel(h_ref, nw_ref, wr_ref, wa_ref, zr_ref, qkv_ref):
    hn = _rms(h_ref[...], nw_ref[...]).astype(BF16)
    zr_ref[...] = jnp.dot(hn, wr_ref[...], preferred_element_type=F32)
    qkv_ref[...] = jnp.dot(hn, wa_ref[...], preferred_element_type=F32).astype(BF16)


def _norm_proj(h, nw, wr, wa):
    t, d = h.shape
    tm = min(TM_PROJ, t)
    n_r, n_a = wr.shape[1], wa.shape[1]
    return pl.pallas_call(
        _norm_proj_kernel,
        out_shape=(jax.ShapeDtypeStruct((t, n_r), F32), jax.ShapeDtypeStruct((t, n_a), BF16)),
        grid=(t // tm,),
        in_specs=[pl.BlockSpec((tm, d), lambda i: (i, 0)),
                  pl.BlockSpec((1, d), lambda i: (0, 0)),
                  pl.BlockSpec((d, n_r), lambda i: (0, 0)),
                  pl.BlockSpec((d, n_a), lambda i: (0, 0))],
        out_specs=(pl.BlockSpec((tm, n_r), lambda i: (i, 0)),
                   pl.BlockSpec((tm, n_a), lambda i: (i, 0))),
        compiler_params=_params("parallel"),
        name="norm_proj",
    )(h, nw, wr, wa)


_V_W0, _V_A0, _V_KK, _V_KA, _V_RK, _V_LNW, _V_LNB, _V_V0 = range(8)


def _rwkv_kernel(*refs, has_vres, n_heads, d_r, group, nb):
    if has_vres:
        (z_ref, vf_ref, mu_ref, vec_ref, wl_ref, tril_ref, ones_ref, v1_ref, v2_ref, y_ref,
         s_ref, carry_ref, r_s, k_s, v_s, kk_s, a_s, lc_s, lw_s, bon_s, g_s) = refs
        vf_out_ref = None
    else:
        (z_ref, mu_ref, vec_ref, wl_ref, tril_ref, ones_ref, y_ref, vf_out_ref,
         s_ref, carry_ref, r_s, k_s, v_s, kk_s, a_s, lc_s, lw_s, bon_s, g_s) = refs
    per_token = (r_s, k_s, v_s, kk_s, a_s, lc_s, lw_s, bon_s, g_s)
    tb = z_ref.shape[0]
    gr = group * CHUNK
    k_step = pl.program_id(0)
    cur = lax.rem(k_step, 2)
    prv = 1 - cur

    @pl.when(k_step == 0)
    def _():
        s_ref[...] = jnp.zeros_like(s_ref)
        for ref in per_token:
            ref[...] = jnp.zeros_like(ref)

    @pl.when(lax.rem(k_step, nb) == 0)
    def _():
        carry_ref[...] = jnp.zeros_like(carry_ref)

    vec = vec_ref[...]

    def vrow(i):
        return vec[i:i + 1, :]

    ln_w = vrow(_V_LNW)
    ln_b = vrow(_V_LNB)
    mu = mu_ref[...]
    head_ones = ones_ref[...]
    tril = tril_ref[...]
    ci = lax.broadcasted_iota(I32, (CHUNK, CHUNK), 0)
    cj = lax.broadcasted_iota(I32, (CHUNK, CHUNK), 1)
    strict = cj < ci
    lower = cj <= ci
    eye = ci == cj
    eye_f = jnp.where(eye, 1.0, 0.0)
    hs = [slice(h * HEAD_DIM, (h + 1) * HEAD_DIM) for h in range(n_heads)]
    prev_first = lax.rem(k_step + nb - 1, nb) == 0

    def token_work(rows):
        z = z_ref[rows, :]
        row = lax.broadcasted_iota(I32, z.shape, 0)
        z_prev = jnp.where(row == 0, carry_ref[0:1, :], pltpu.roll(z, 1, axis=0))
        carry_ref[0:1, :] = z[gr - 1:gr, :]
        zs = z + (z_prev - z) * mu
        r = zs[:, 0:d_r]
        k = zs[:, d_r:2 * d_r]
        v = zs[:, 2 * d_r:3 * d_r]
        lo = zs[:, 3 * d_r:]
        n_lo = lo.shape[1]
        lane = lax.broadcasted_iota(I32, lo.shape, 1)
        lo_act = jnp.where(lane < n_lo // 4, jnp.tanh(lo),
                           jnp.where(lane < n_lo // 2, lo, jax.nn.sigmoid(lo)))
        lo_out = _mm(lo_act, wl_ref[...])
        if has_vres:
            vv = _mm(_mm(v, v1_ref[...]), v2_ref[...])
            v = v + (vf_ref[rows, :] - v) * jax.nn.sigmoid(vrow(_V_V0) + vv)
        w_log = -jax.nn.softplus(-(vrow(_V_W0) + lo_out[:, 0:d_r])) - 0.5
        lw = -jnp.exp(w_log)
        a = jax.nn.sigmoid(vrow(_V_A0) + lo_out[:, d_r:2 * d_r])
        kk = k * vrow(_V_KK)
        kk = kk * lax.rsqrt(jnp.maximum(_mm_split2_rhs(kk * kk, head_ones), 1e-24))
        k2 = k * (1.0 + (a - 1.0) * vrow(_V_KA))
        r_s[cur, rows, :] = r
        k_s[cur, rows, :] = k2
        v_s[cur, rows, :] = v
        kk_s[cur, rows, :] = kk
        a_s[cur, rows, :] = a
        lc_s[cur, rows, :] = _mm_exact_lhs(tril, lw)
        lw_s[cur, rows, :] = lw
        bon_s[cur, rows, :] = _mm_split2_rhs(r * k2 * vrow(_V_RK), head_ones) * v
        g_s[cur, rows, :] = lo_out[:, 2 * d_r:3 * d_r]

    def chunk_operands(r0):
        rs = pl.ds(r0, CHUNK)
        lc_c = lc_s[prv, rs, :]
        lw_c = lw_s[prv, rs, :]
        l_end = lc_s[prv, pl.ds(r0 + CHUNK - 1, 1), :]
        p_in = jnp.exp(lc_c)
        p_prev = jnp.exp(lc_c - lw_c)
        p_inv = jnp.exp(-lc_c)
        p_end = jnp.exp(l_end - lc_c)
        p_last = jnp.exp(l_end)
        kk_c = kk_s[prv, rs, :]
        b_c = kk_c * a_s[prv, rs, :]
        k_c = k_s[prv, rs, :]
        at = (-kk_c * p_prev).astype(BF16)
        bt = (b_c * p_inv).astype(BF16)
        bh = (b_c * p_end).astype(BF16)
        kt = (k_c * p_inv).astype(BF16)
        kh = (k_c * p_end).astype(BF16)
        rt = (r_s[prv, rs, :] * p_in).astype(BF16)
        vc = v_s[prv, rs, :].astype(BF16)
        per_head = [[x[:, sl] for sl in hs] for x in (at, bt, bh, kt, kh, rt, vc)]
        per_head.append([p_last[:, sl] for sl in hs])
        return per_head

    def group_body(gi, carry):
        g0 = pl.multiple_of(gi * gr, gr)
        rows = pl.ds(g0, gr)
        if vf_out_ref is not None:
            vf_out_ref[rows, :] = v_s[prv, rows, :]
        ops = [chunk_operands(g0 + c * CHUNK) for c in range(group)]
        at_h, bt_h, bh_h, kt_h, kh_h, rt_h, v_h, pl_h = ([x for c in range(group) for x in ops[c][q]]
                                                         for q in range(8))
        heads = range(group * n_heads)
        ar_h = [jnp.concatenate([at_h[h], rt_h[h]], axis=0) for h in heads]
        m_b = [_mm_nt(ar_h[h], bt_h[h]) for h in heads]
        m_k = [_mm_nt(ar_h[h], kt_h[h]) for h in heads]
        n_ab = [jnp.where(strict, m_b[h][:CHUNK], 0.0) for h in heads]
        a_ak = [jnp.where(strict, m_k[h][:CHUNK], 0.0) for h in heads]
        a_rb = [jnp.where(lower, m_b[h][CHUNK:], 0.0) for h in heads]
        a_rk = [jnp.where(lower, m_k[h][CHUNK:], 0.0) for h in heads]
        x_inv = [eye_f + n_ab[h] for h in heads]
        pw = [_mm(n_ab[h], n_ab[h]) for h in heads]
        akv = [_mm(a_ak[h], v_h[h]) for h in heads]
        n_sq = CHUNK.bit_length() - 2
        for it in range(n_sq):
            if it < n_sq - 1:
                st = [_mm(jnp.concatenate([x_inv[h], pw[h]], axis=0), pw[h]) for h in heads]
                x_inv = [x_inv[h] + st[h][:CHUNK] for h in heads]
                pw = [st[h][CHUNK:] for h in heads]
            else:
                st = [_mm(x_inv[h], pw[h]) for h in heads]
                x_inv = [x_inv[h] + st[h] for h in heads]
        w_h = [_mm(x_inv[h], at_h[h]) for h in heads]
        u0 = [_mm(x_inv[h], akv[h]) for h in heads]
        y0 = [_mm(a_rk[h], v_h[h]) + _mm(a_rb[h], u0[h]) for h in heads]
        r_p = [rt_h[h].astype(F32) + _mm(a_rb[h], w_h[h]) for h in heads]
        g_h = [jnp.where(eye, pl_h[h], 0.0) + _mm_tn(w_h[h], bh_h[h]) for h in heads]
        d_h = [_mm_tn(u0[h], bh_h[h]) + _mm_tn(v_h[h], kh_h[h]) for h in heads]
        fresh = prev_first & (gi == 0)
        s_h = [jnp.where(fresh, 0.0, s_ref[h]) for h in range(n_heads)]
        for c in range(group):
            rs = pl.ds(g0 + c * CHUNK, CHUNK)
            idx = [c * n_heads + h for h in range(n_heads)]
            y_h = [y0[i] + _mm_nt(r_p[i], s_h[h]) for h, i in enumerate(idx)]
            s_h = [_mm(s_h[h], g_h[i]) + d_h[i] for h, i in enumerate(idx)]
            y_heads = []
            for h in range(n_heads):
                mean = jnp.mean(y_h[h], axis=-1, keepdims=True)
                yc = y_h[h] - mean
                var = jnp.mean(yc * yc, axis=-1, keepdims=True)
                y_heads.append(yc * lax.rsqrt(var + GN_EPS))
            y_n = jnp.concatenate(y_heads, axis=-1)
            out = (y_n * ln_w + ln_b + bon_s[prv, rs, :]) * g_s[prv, rs, :]
            y_ref[rs, :] = out.astype(y_ref.dtype)
        for h in range(n_heads):
            s_ref[h] = s_h[h]
        token_work(rows)
        return carry

    lax.fori_loop(0, tb // gr, group_body, 0)


def _rwkv(z, v_first, mu, vec, wl, v1, v2, *, batch, seq, n_heads):
    t, n_z = z.shape
    d_r = n_heads * HEAD_DIM
    tb = min(TB_RWKV, seq)
    nb = seq // tb
    n_blocks = batch * nb
    has_vres = v_first is not None
    tok_in = lambda k: (jnp.minimum(k, n_blocks - 1), 0)
    tok_out = lambda k: (jnp.maximum(k - 1, 0), 0)
    const = lambda k: (0, 0)
    in_specs = [pl.BlockSpec((tb, n_z), tok_in)]
    args = [z]
    if has_vres:
        in_specs.append(pl.BlockSpec((tb, d_r), tok_in))
        args.append(v_first)
    n_chunks = tb // CHUNK
    group = RWKV_CHUNK_GROUP if n_chunks % RWKV_CHUNK_GROUP == 0 else 1
    ti = jnp.arange(group * CHUNK)
    tril = ((ti[:, None] // CHUNK == ti[None, :] // CHUNK) & (ti[None, :] <= ti[:, None])).astype(BF16)
    hi = jnp.arange(d_r) // HEAD_DIM
    head_ones = (hi[:, None] == hi[None, :]).astype(BF16)
    in_specs += [pl.BlockSpec(mu.shape, const), pl.BlockSpec(vec.shape, const),
                 pl.BlockSpec(wl.shape, const), pl.BlockSpec(tril.shape, const),
                 pl.BlockSpec(head_ones.shape, const)]
    args += [mu, vec, wl, tril, head_ones]
    if has_vres:
        in_specs += [pl.BlockSpec(v1.shape, const), pl.BlockSpec(v2.shape, const)]
        args += [v1, v2]
        out_shape = jax.ShapeDtypeStruct((t, d_r), BF16)
        out_specs = pl.BlockSpec((tb, d_r), tok_out)
    else:
        out_shape = (jax.ShapeDtypeStruct((t, d_r), BF16), jax.ShapeDtypeStruct((t, d_r), F32))
        out_specs = (pl.BlockSpec((tb, d_r), tok_out), pl.BlockSpec((tb, d_r), tok_out))
    scratch = [pltpu.VMEM((n_heads, HEAD_DIM, HEAD_DIM), F32),
               pltpu.VMEM((V7X_SUBLANES, n_z), F32)]
    scratch += [pltpu.VMEM((2, tb, d_r), F32) for _ in range(9)]
    return pl.pallas_call(
        functools.partial(_rwkv_kernel, has_vres=has_vres, n_heads=n_heads, d_r=d_r, group=group, nb=nb),
        out_shape=out_shape,
        grid=(n_blocks + 1,),
        in_specs=in_specs,
        out_specs=out_specs,
        scratch_shapes=scratch,
        compiler_params=_params("arbitrary"),
        name="rwkv_vres" if has_vres else "rwkv",
    )(*args)


def _attn_kernel(*refs, n_heads, n_parts):
    q_ref = refs[0]
    k_refs = refs[1:1 + n_parts]
    v_refs = refs[1 + n_parts:1 + 2 * n_parts]
    tab_ref = refs[1 + 2 * n_parts]
    o_ref = refs[2 + 2 * n_parts]
    qb = q_ref.shape[0]
    j = pl.program_id(1)
    scale = HEAD_DIM ** -0.5
    q = q_ref[...] * jnp.asarray(scale, q_ref.dtype)
    ks = [r[...] for r in k_refs]
    vs = [r[...] for r in v_refs]
    pw = 2 * HEAD_DIM
    lane = lax.broadcasted_iota(I32, (qb, pw), 1)
    sum_even = jnp.where(lane < HEAD_DIM, 1.0, 0.0).astype(q.dtype)
    sum_odd = jnp.where(lane < HEAD_DIM, 0.0, 1.0).astype(q.dtype)
    zero = jnp.zeros((), q.dtype)
    is_even = sum_even > zero
    outs = []
    for hp in range(n_heads // 2):
        sl = slice(hp * pw, (hp + 1) * pw)
        qq = q[:, sl]
        q_pair = (jnp.where(is_even, qq, zero), jnp.where(is_even, zero, qq))
        s_parts = [[], []]
        for p in range(n_parts):
            kk = ks[p][:, sl]
            back = n_parts - 1 - p
            for u in range(2):
                s = _mm_nt(q_pair[u], kk) + tab_ref[2 * hp + u, :, p * qb:(p + 1) * qb]
                if back > 0:
                    s = jnp.where(j >= back, s, NEG_INF)
                s_parts[u].append(s)
        m = []
        for u in range(2):
            mm = s_parts[u][0]
            for s in s_parts[u][1:]:
                mm = jnp.maximum(mm, s)
            m.append(mm.max(axis=-1, keepdims=True))
        acc = jnp.zeros((qb, 2 * pw), F32)
        for p in range(n_parts):
            vv = vs[p][:, sl]
            rhs = jnp.concatenate(
                [jnp.concatenate([jnp.where(is_even, vv, zero), sum_even], axis=1),
                 jnp.concatenate([jnp.where(is_even, zero, vv), sum_odd], axis=1)], axis=0)
            e = jnp.concatenate([jnp.exp(s_parts[u][p] - m[u]).astype(BF16) for u in range(2)], axis=1)
            acc = acc + jnp.dot(e, rhs, preferred_element_type=F32)
        outs.append(acc[:, :pw] / acc[:, pw:])
    o_ref[...] = jnp.concatenate(outs, axis=-1).astype(o_ref.dtype)


def _attn(qkv, table, *, batch, seq, n_heads):
    t = qkv.shape[0]
    d_a = n_heads * HEAD_DIM
    qb = min(QB_ATTN, seq)
    left = LEFT_CHUNKS * CHUNK
    assert left % qb == 0 and seq % qb == 0
    n_parts = left // qb + 1
    nb = seq // qb
    in_specs = [pl.BlockSpec((qb, d_a), lambda b, j: (b * nb + j, 0))]
    for p in range(n_parts):
        back = n_parts - 1 - p
        in_specs.append(pl.BlockSpec((qb, d_a), lambda b, j, back=back: (b * nb + jnp.maximum(j - back, 0), 1)))
    for p in range(n_parts):
        back = n_parts - 1 - p
        in_specs.append(pl.BlockSpec((qb, d_a), lambda b, j, back=back: (b * nb + jnp.maximum(j - back, 0), 2)))
    in_specs.append(pl.BlockSpec(table.shape, lambda b, j: (0, 0, 0)))
    return pl.pallas_call(
        functools.partial(_attn_kernel, n_heads=n_heads, n_parts=n_parts),
        out_shape=jax.ShapeDtypeStruct((t, d_a), BF16),
        grid=(batch, nb),
        in_specs=in_specs,
        out_specs=pl.BlockSpec((qb, d_a), lambda b, j: (b * nb + j, 0)),
        compiler_params=_params("parallel", "arbitrary"),
        name="attn",
    )(*([qkv] * (1 + 2 * n_parts)), table)


def _attn_table(rel_bias, qb):
    left = LEFT_CHUNKS * CHUNK
    n_keys = left + qb
    period = qb + n_keys - 1
    n_heads = rel_bias.shape[0]
    m = jnp.arange(period)
    rel = left - jnp.where(m < n_keys, m, m - period)
    g = rel_bias[:, jnp.clip(rel, -(CHUNK - 1), MAX_REL) + (CHUNK - 1)].astype(F32)
    flat = jnp.tile(g, (1, qb))[:, :qb * (period - 1)]
    bias = flat.reshape(n_heads, qb, period - 1)[:, :, :n_keys]
    cq = jnp.arange(qb)[:, None] // CHUNK
    ck = jnp.arange(n_keys)[None, :] // CHUNK
    valid = (ck >= cq) & (ck <= cq + LEFT_CHUNKS)
    return jnp.where(valid[None], bias, NEG_INF)


_R_E1, _R_E2, _R_C1, _R_C2, _R_RANK1, _R_RANK2, _R_LRANK1, _R_LRANK2 = range(8)


def _outproj_route_kernel(yr_ref, ya_ref, h_ref, wor_ref, woa_ref, nw_ref, wrt_ref, brt_ref,
                          h1_ref, hn_ref, route_ref, route_t_ref, cnt_ref, tstat_ref, carry_ref):
    i = pl.program_id(0)

    @pl.when(i == 0)
    def _():
        carry_ref[...] = jnp.zeros_like(carry_ref)

    h1 = (h_ref[...] + jnp.dot(yr_ref[...], wor_ref[...], preferred_element_type=F32)
          + jnp.dot(ya_ref[...], woa_ref[...], preferred_element_type=F32))
    h1_ref[...] = h1
    hn = _rms(h1, nw_ref[...])
    _to_token_tiles(hn_ref, hn)
    nl = brt_ref.shape[1]
    hn_hi = hn.astype(BF16)
    hn_lo = (hn - hn_hi.astype(F32)).astype(BF16)
    part = jnp.dot(hn_hi, wrt_ref[...], preferred_element_type=F32)
    logits = (part[:, :nl] + part[:, nl:]
              + jnp.dot(hn_lo, wrt_ref[:, :nl], preferred_element_type=F32) + brt_ref[...])
    tm = logits.shape[0]
    lane = lax.broadcasted_iota(I32, (tm, nl), 1)
    lane_f = lane.astype(F32)
    ninf = -jnp.inf
    big = float(nl)
    is_g = lane < N_GROUPS
    gl = jnp.where(is_g, logits, ninf)
    g_max = gl.max(axis=-1, keepdims=True)
    g_sel = jnp.where(gl == g_max, lane_f, big).min(axis=-1, keepdims=True)
    p_g = 1.0 / jnp.where(is_g, jnp.exp(logits - g_max), 0.0).sum(axis=-1, keepdims=True)
    e_lo = N_GROUPS + EXPERTS_PER_GROUP * g_sel
    in_grp = (lane_f >= e_lo) & (lane_f < e_lo + EXPERTS_PER_GROUP)
    el = jnp.where(in_grp, logits, ninf)
    m1 = el.max(axis=-1, keepdims=True)
    i1 = jnp.where(el == m1, lane_f, big).min(axis=-1, keepdims=True)
    el2 = jnp.where(lane_f == i1, ninf, el)
    m2 = el2.max(axis=-1, keepdims=True)
    i2 = jnp.where(el2 == m2, lane_f, big).min(axis=-1, keepdims=True)
    t2 = jnp.exp(m2 - m1)
    c1 = p_g / (1.0 + t2)
    c2 = p_g * t2 / (1.0 + t2)
    e1 = i1 - N_GROUPS
    e2 = i2 - N_GROUPS
    oh1 = lane_f == e1
    oh2 = lane_f == e2
    ohs = jnp.where(oh1 | oh2, 1.0, 0.0)
    ri = lax.broadcasted_iota(I32, (tm, tm), 0)
    rj = lax.broadcasted_iota(I32, (tm, tm), 1)
    before = jnp.where(rj < ri, 1.0, 0.0).astype(BF16)
    old_carry = carry_ref[0:1, :]
    cnt_tile = jnp.dot(before, ohs.astype(BF16), preferred_element_type=F32)
    cnt = cnt_tile + old_carry
    rank1 = jnp.where(oh1, cnt, 0.0).sum(axis=-1, keepdims=True)
    rank2 = jnp.where(oh2, cnt, 0.0).sum(axis=-1, keepdims=True)
    lrank1 = jnp.where(oh1, cnt_tile, 0.0).sum(axis=-1, keepdims=True)
    lrank2 = jnp.where(oh2, cnt_tile, 0.0).sum(axis=-1, keepdims=True)
    tile_cnt = ohs.sum(axis=0, keepdims=True)
    new_carry = old_carry + tile_cnt
    carry_ref[0:1, :] = new_carry
    cnt_ref[...] = jnp.broadcast_to(new_carry, cnt_ref.shape)
    srow = lax.broadcasted_iota(I32, tstat_ref.shape, 0)
    tstat_ref[...] = jnp.where(srow == 0, tile_cnt, jnp.where(srow == 1, old_carry, 0.0))
    route = jnp.zeros((tm, nl), F32)
    for idx, val in ((_R_E1, e1), (_R_E2, e2), (_R_C1, c1), (_R_C2, c2),
                     (_R_RANK1, rank1), (_R_RANK2, rank2), (_R_LRANK1, lrank1), (_R_LRANK2, lrank2)):
        route = jnp.where(lane == idx, val, route)
    route_ref[...] = route
    route_t_ref[...] = route.T[:route_t_ref.shape[0], :]


def _outproj_route(yr, ya, h, wor, woa, nw, wrt, brt):
    t, d = h.shape
    tm = min(TM_ROUTE, t)
    d_r, d_a = yr.shape[1], ya.shape[1]
    nl = brt.shape[1]
    tok = lambda i: (i, 0)
    const = lambda i: (0, 0)
    return pl.pallas_call(
        _outproj_route_kernel,
        out_shape=(jax.ShapeDtypeStruct((t, d), F32),
                   jax.ShapeDtypeStruct((t * V7X_SUBLANES, V7X_LANES), F32),
                   jax.ShapeDtypeStruct((t, nl), F32), jax.ShapeDtypeStruct((V7X_SUBLANES, t), F32),
                   jax.ShapeDtypeStruct((V7X_SUBLANES, nl), F32),
                   jax.ShapeDtypeStruct((t // tm * V7X_SUBLANES, nl), F32)),
        grid=(t // tm,),
        in_specs=[pl.BlockSpec((tm, d_r), tok), pl.BlockSpec((tm, d_a), tok), pl.BlockSpec((tm, d), tok),
                  pl.BlockSpec((d_r, d), const), pl.BlockSpec((d_a, d), const),
                  pl.BlockSpec((1, d), const), pl.BlockSpec(wrt.shape, const), pl.BlockSpec((1, nl), const)],
        out_specs=(pl.BlockSpec((tm, d), tok), pl.BlockSpec((tm * V7X_SUBLANES, V7X_LANES), tok),
                   pl.BlockSpec((tm, nl), tok), pl.BlockSpec((V7X_SUBLANES, tm), lambda i: (0, i)),
                   pl.BlockSpec((V7X_SUBLANES, nl), const), pl.BlockSpec((V7X_SUBLANES, nl), tok)),
        scratch_shapes=[pltpu.VMEM((V7X_SUBLANES, nl), F32)],
        compiler_params=_params("arbitrary"),
        name="outproj_route",
    )(yr, ya, h, wor, woa, nw, wrt, brt)


def _load_indices(idx_hbm, i, idx_smem, sem):
    n = idx_smem.shape[0]
    cp = pltpu.make_async_copy(idx_hbm.at[pl.ds(pl.multiple_of(i * n, n), n)], idx_smem, sem)
    cp.start()
    cp.wait()


def _tiles(ref, row, n):
    start = row * V7X_SUBLANES
    if not isinstance(row, int):
        start = pl.multiple_of(start, V7X_SUBLANES)
    return ref.at[pl.ds(start, n * V7X_SUBLANES), :]


def _start_run_copies(length, n_bits, copy_of):
    for b in range(n_bits):
        @pl.when(((length >> b) & 1) == 1)
        def _(b=b):
            done = (length >> (b + 1)) << (b + 1)
            copy_of(done, 1 << b, b).start(priority=b % 2)


def _dispatch_kernel(idx_hbm, zinfo_hbm, x_ref, xs_hbm, idx_smem, zinfo_smem, xloc, zero_vmem,
                     idx_sem, run_sem, zero_sem, *, tm, n_free):
    i = pl.program_id(0)
    nb = pl.num_programs(0)
    buf = lax.rem(i, 2)
    zero_rows = zero_vmem.shape[0] // V7X_SUBLANES

    @pl.when(i == 0)
    def _():
        cp = pltpu.make_async_copy(zinfo_hbm, zinfo_smem, idx_sem)
        cp.start()
        cp.wait()
        zero_vmem[...] = jnp.zeros_like(zero_vmem)

        def pad_runs(e, carry):
            start = zinfo_smem[e]
            _start_run_copies(
                zinfo_smem[N_EXPERTS + e], zero_rows.bit_length(),
                lambda done, n, b: pltpu.make_async_copy(_tiles(zero_vmem, 0, n),
                                                         _tiles(xs_hbm, start + done, n), zero_sem))
            return carry

        lax.fori_loop(0, N_EXPERTS, pad_runs, 0)
        tail_start = zinfo_smem[2 * N_EXPERTS]

        def tail_block(n, carry):
            pltpu.make_async_copy(zero_vmem, _tiles(xs_hbm, tail_start + n * zero_rows, zero_rows),
                                  zero_sem).start()
            return carry

        lax.fori_loop(0, zinfo_smem[2 * N_EXPERTS + 1], tail_block, 0)

    _load_indices(idx_hbm, i, idx_smem, idx_sem)

    def wait_runs(b):
        pltpu.make_async_copy(xloc.at[b], xs_hbm.at[pl.ds(0, xloc.shape[1]), :], run_sem.at[b]).wait()

    @pl.when(i >= 2)
    def _():
        wait_runs(buf)

    def place(tt, carry):
        row = x_ref[pl.ds(pl.multiple_of(tt * V7X_SUBLANES, V7X_SUBLANES), V7X_SUBLANES), :]
        for s in range(2):
            lp = idx_smem[s * tm + tt]
            xloc[buf, pl.ds(pl.multiple_of(lp * V7X_SUBLANES, V7X_SUBLANES), V7X_SUBLANES), :] = row
        return carry

    lax.fori_loop(0, tm, place, 0, unroll=8)

    def expert_run(e, carry):
        dst = idx_smem[2 * tm + N_EXPERTS + e]
        off = idx_smem[2 * tm + 2 * N_EXPERTS + e]
        _start_run_copies(
            idx_smem[2 * tm + e], tm.bit_length(),
            lambda done, n, b: pltpu.make_async_copy(_tiles(xloc.at[buf], off + done, n),
                                                     _tiles(xs_hbm, dst + done, n), run_sem.at[buf]))
        return carry

    lax.fori_loop(0, N_EXPERTS, expert_run, 0)

    @pl.when(i == nb - 1)
    def _():
        wait_runs(buf)

        @pl.when(nb >= 2)
        def _():
            wait_runs(1 - buf)

        pltpu.make_async_copy(xs_hbm.at[pl.ds(0, n_free * V7X_SUBLANES), :],
                              xs_hbm.at[pl.ds(0, n_free * V7X_SUBLANES), :], zero_sem).wait()


def _dispatch(x, idx, zinfo, n_rows, tm, n_free):
    t = x.shape[0] // V7X_SUBLANES
    nb = t // tm
    rec = idx.shape[0] // nb
    return pl.pallas_call(
        functools.partial(_dispatch_kernel, tm=tm, n_free=n_free),
        out_shape=jax.ShapeDtypeStruct((n_rows * V7X_SUBLANES, V7X_LANES), x.dtype),
        grid=(nb,),
        in_specs=[pl.BlockSpec(memory_space=pl.ANY), pl.BlockSpec(memory_space=pl.ANY),
                  pl.BlockSpec((tm * V7X_SUBLANES, V7X_LANES), lambda i: (i, 0))],
        out_specs=pl.BlockSpec(memory_space=pl.ANY),
        scratch_shapes=[pltpu.SMEM((rec,), I32), pltpu.SMEM(zinfo.shape, I32),
                        pltpu.VMEM((2, 2 * tm * V7X_SUBLANES, V7X_LANES), x.dtype),
                        pltpu.VMEM((TM_EXPERT // 2 * V7X_SUBLANES, V7X_LANES), x.dtype),
                        pltpu.SemaphoreType.DMA, pltpu.SemaphoreType.DMA((2,)), pltpu.SemaphoreType.DMA],
        compiler_params=_params("arbitrary"),
        name="dispatch",
    )(idx, zinfo, x)


def _experts_kernel(te_ref, tv_ref, tf_ref, x_ref, w1_ref, w3_ref, w2_ref, y_ref,
                    w1_b, w3_b, w2_b, *, tm):
    i = pl.program_id(0)

    @pl.when(tf_ref[i] > 0)
    def _():
        w1_b[...] = w1_ref[...].astype(BF16)
        w3_b[...] = w3_ref[...].astype(BF16)
        w2_b[...] = w2_ref[...].astype(BF16)

    @pl.when(tv_ref[i] > 0)
    def _():
        x = _from_token_tiles(x_ref, tm).astype(BF16)
        h_gate = jnp.dot(x, w1_b[...], preferred_element_type=F32)
        h_up = jnp.dot(x, w3_b[...], preferred_element_type=F32)
        hid = (h_gate * jax.nn.sigmoid(h_gate) * h_up).astype(BF16)
        _to_token_tiles(y_ref, jnp.dot(hid, w2_b[...], preferred_element_type=F32))

    @pl.when(tv_ref[i] == 0)
    def _():
        y_ref[...] = jnp.zeros_like(y_ref)


def _experts(xs, w1, w3, w2, tile_expert, tile_valid, tile_first):
    n_rows = xs.shape[0] // V7X_SUBLANES
    tm = TM_EXPERT
    nt = n_rows // tm
    d, f = w1.shape[1:]
    tile_spec = pl.BlockSpec((tm * V7X_SUBLANES, V7X_LANES), lambda i, te, tv, tf: (i, 0))
    grid_spec = pltpu.PrefetchScalarGridSpec(
        num_scalar_prefetch=3,
        grid=(nt,),
        in_specs=[tile_spec,
                  pl.BlockSpec((None, d, f), lambda i, te, tv, tf: (te[i], 0, 0)),
                  pl.BlockSpec((None, d, f), lambda i, te, tv, tf: (te[i], 0, 0)),
                  pl.BlockSpec((None, f, d), lambda i, te, tv, tf: (te[i], 0, 0))],
        out_specs=tile_spec,
        scratch_shapes=[pltpu.VMEM((d, f), BF16), pltpu.VMEM((d, f), BF16), pltpu.VMEM((f, d), BF16)],
    )
    return pl.pallas_call(
        functools.partial(_experts_kernel, tm=tm),
        out_shape=jax.ShapeDtypeStruct(xs.shape, F32),
        grid_spec=grid_spec,
        compiler_params=_params("arbitrary"),
        name="experts",
    )(tile_expert, tile_valid, tile_first, xs, w1, w3, w2)


def _combine_ple_kernel(pos_hbm, ys_hbm, h_ref, route_ref, p_ref, nw_ref, wg_ref, bg_ref, wp_ref,
                        fw_ref, o_ref, idx_smem, ybuf, idx_sem, row_sem, *, tm, final):
    i = pl.program_id(0)
    nb = pl.num_programs(0)
    cur = lax.rem(i, 2)
    nxt = 1 - cur

    def gather(step, buf):
        _load_indices(pos_hbm, step, idx_smem, idx_sem)

        def issue(tt, carry):
            for s in range(2):
                pltpu.make_async_copy(_token_tile(ys_hbm, idx_smem[s * tm + tt]),
                                      _token_tile(ybuf.at[buf, s], tt), row_sem.at[buf]).start(priority=s)
            return carry

        lax.fori_loop(0, tm, issue, 0, unroll=8)

    def wait_gather(buf):
        for s in range(2):
            pltpu.make_async_copy(ys_hbm.at[pl.ds(0, ybuf.shape[2]), :], ybuf.at[buf, s],
                                  row_sem.at[buf]).wait()

    @pl.when(i == 0)
    def _():
        gather(0, 0)

    @pl.when(i + 1 < nb)
    def _():
        gather(i + 1, nxt)

    wait_gather(cur)

    route = route_ref[...]
    c1 = route[:, _R_C1:_R_C1 + 1]
    c2 = route[:, _R_C2:_R_C2 + 1]
    h2 = (h_ref[...] + c1 * _from_token_tiles(ybuf.at[cur, 0], tm)
          + c2 * _from_token_tiles(ybuf.at[cur, 1], tm))
    hn = _rms(h2, nw_ref[...]).astype(BF16)
    gate = jax.nn.sigmoid(jnp.dot(hn, wg_ref[...], preferred_element_type=F32) + bg_ref[...])
    h3 = h2 + gate * jnp.dot(p_ref[...].astype(BF16), wp_ref[...], preferred_element_type=F32)
    if final:
        h3 = _rms(h3, fw_ref[...])
    o_ref[...] = h3


def _combine_ple(pos_tiles, ys, h, route, p, nw, wg, bg, wp, fw, *, tm, final, layer):
    t, d = h.shape
    nl = route.shape[1]
    dp = p.shape[1]
    tok = lambda i: (i, 0)
    const = lambda i: (0, 0)
    return pl.pallas_call(
        functools.partial(_combine_ple_kernel, tm=tm, final=final),
        out_shape=jax.ShapeDtypeStruct((t, d), F32),
        grid=(t // tm,),
        in_specs=[pl.BlockSpec(memory_space=pl.ANY), pl.BlockSpec(memory_space=pl.ANY),
                  pl.BlockSpec((tm, d), tok), pl.BlockSpec((tm, nl), tok),
                  pl.BlockSpec((tm, dp), lambda i: (i + layer * (t // tm), 0)),
                  pl.BlockSpec((1, d), const), pl.BlockSpec((d, d), const), pl.BlockSpec((1, d), const),
                  pl.BlockSpec((dp, d), const), pl.BlockSpec((1, d), const)],
        out_specs=pl.BlockSpec((tm, d), tok),
        scratch_shapes=[pltpu.SMEM((pos_tiles.shape[0] // (t // tm),), I32),
                        pltpu.VMEM((2, 2, tm * V7X_SUBLANES, V7X_LANES), F32),
                        pltpu.SemaphoreType.DMA, pltpu.SemaphoreType.DMA((2,))],
        compiler_params=_params("arbitrary"),
        name="combine_ple_final" if final else "combine_ple",
    )(pos_tiles, ys, h, route, p, nw, wg, bg, wp, fw)


_SMEM_RECORD_WORDS = 1024


def _index_records(pos1, pos2, tm, extra=None):
    nb = pos1.shape[0] // tm
    parts = [pos1.reshape(nb, tm), pos2.reshape(nb, tm)]
    if extra is not None:
        parts.append(extra.reshape(nb, -1))
    rec = jnp.concatenate(parts, axis=1)
    pad = -rec.shape[1] % _SMEM_RECORD_WORDS
    return jnp.pad(rec, ((0, 0), (0, pad))).reshape(-1)


def _lookup(table, idx):
    ids = jnp.arange(table.shape[0], dtype=I32)
    return jnp.sum(jnp.where(idx[None, :] == ids[:, None], table[:, None], 0), axis=0)


def _bucket(ends, x):
    return jnp.minimum(jnp.sum((x[None, :] >= ends[:, None]).astype(I32), axis=0), ends.shape[0] - 1)


def kernel(x, p, norm_mix_w, w_in, rwkv_mu, rwkv_w0, rwkv_w2, rwkv_a0, rwkv_a2, rwkv_g2, rwkv_k_k, rwkv_k_a, rwkv_r_k, rwkv_ln_w, rwkv_ln_b, rwkv_v0, rwkv_v1, rwkv_v2, att_rel_bias, w_out, norm_ffn_w, router_group_w, router_group_b, router_expert_w, router_expert_b, expert_w1, expert_w3, expert_w2, norm_ple_w, ple_gate_w, ple_gate_b, ple_proj_w, final_norm_w):
    batch, seq, d = x.shape
    depth = w_in.shape[0]
    t = batch * seq
    d_r = rwkv_w0.shape[1]
    n_heads_r = d_r // HEAD_DIM
    n_rwkv_in = rwkv_mu.shape[1]
    d_a = (w_in.shape[2] - n_rwkv_in) // 3
    n_heads_a = d_a // HEAD_DIM
    n_dec, n_iclr, n_gate = rwkv_w2.shape[1], rwkv_a2.shape[1], rwkv_g2.shape[1]
    assert n_dec == n_iclr and n_gate == n_dec + n_iclr
    n_lo = n_dec + n_iclr + n_gate
    f_exp = expert_w1.shape[-1]
    assert d == V7X_SUBLANES * V7X_LANES
    n_rows = 2 * t + N_EXPERTS * TM_EXPERT
    n_tiles = n_rows // TM_EXPERT
    qb = min(QB_ATTN, seq)

    w1_all = expert_w1.reshape(depth * N_EXPERTS, d, f_exp)
    w3_all = expert_w3.reshape(depth * N_EXPERTS, d, f_exp)
    w2_all = expert_w2.reshape(depth * N_EXPERTS, f_exp, d)

    h = x.reshape(t, d)
    v_first = None
    for i in range(depth):
        wr = w_in[i, :, :n_rwkv_in].astype(BF16)
        wa = w_in[i, :, n_rwkv_in:].astype(BF16)
        wl = jnp.zeros((n_lo, 3 * d_r), F32)
        wl = wl.at[:n_dec, :d_r].set(rwkv_w2[i])
        wl = wl.at[n_dec:n_dec + n_iclr, d_r:2 * d_r].set(rwkv_a2[i])
        wl = wl.at[n_dec + n_iclr:, 2 * d_r:].set(rwkv_g2[i]).astype(BF16)
        v0 = rwkv_v0[i - 1] if i > 0 else jnp.zeros((d_r,), F32)
        vec = jnp.stack([rwkv_w0[i], rwkv_a0[i], rwkv_k_k[i], rwkv_k_a[i], rwkv_r_k[i],
                         rwkv_ln_w[i], rwkv_ln_b[i], v0])
        if i > 0:
            n_vr = rwkv_v1.shape[2]
            v1 = jnp.zeros((d_r, V7X_LANES), F32).at[:, :n_vr].set(rwkv_v1[i - 1]).astype(BF16)
            v2 = jnp.zeros((V7X_LANES, d_r), F32).at[:n_vr, :].set(rwkv_v2[i - 1]).astype(BF16)
        else:
            v1 = v2 = None
        table = _attn_table(att_rel_bias[i], qb)
        wor = w_out[i, :d_r].astype(BF16)
        woa = w_out[i, d_r:].astype(BF16)
        n_rt = N_GROUPS + N_EXPERTS
        wrt = jnp.zeros((d, V7X_LANES), F32)
        wrt = wrt.at[:, :N_GROUPS].set(router_group_w[i]).at[:, N_GROUPS:n_rt].set(router_expert_w[i])
        wrt_hi = wrt.astype(BF16)
        wrt = jnp.concatenate([wrt_hi, (wrt - wrt_hi.astype(F32)).astype(BF16)], axis=1)
        brt = jnp.zeros((1, V7X_LANES), F32)
        brt = brt.at[0, :N_GROUPS].set(router_group_b[i]).at[0, N_GROUPS:n_rt].set(router_expert_b[i])

        z_r, qkv = _norm_proj(h, norm_mix_w[i][None], wr, wa)
        if i == 0:
            y_r, v_first = _rwkv(z_r, None, rwkv_mu[i][None], vec, wl, None, None,
                                 batch=batch, seq=seq, n_heads=n_heads_r)
        else:
            y_r = _rwkv(z_r, v_first, rwkv_mu[i][None], vec, wl, v1, v2,
                        batch=batch, seq=seq, n_heads=n_heads_r)
        y_a = _attn(qkv, table, batch=batch, seq=seq, n_heads=n_heads_a)

        h1, hn, route, route_t, cnt, tstat = _outproj_route(y_r, y_a, h, wor, woa, norm_ffn_w[i][None],
                                                            wrt, brt)
        ri = route_t.astype(I32)
        counts = cnt[0, :N_EXPERTS].astype(I32)
        padded = ((counts + TM_EXPERT - 1) // TM_EXPERT) * TM_EXPERT
        p_end = jnp.cumsum(padded)
        p_start = p_end - padded
        pos1 = _lookup(p_start, ri[_R_E1]) + ri[_R_RANK1]
        pos2 = _lookup(p_start, ri[_R_E2]) + ri[_R_RANK2]
        tile_start = jnp.arange(n_tiles, dtype=I32) * TM_EXPERT
        tile_expert = _bucket(p_end, tile_start)
        tile_valid = (tile_start < p_end[-1]).astype(I32)

        tile_first = jnp.concatenate([jnp.ones((1,), I32),
                                      (tile_expert[1:] != tile_expert[:-1]).astype(I32)])

        tm_d = min(TM_DISPATCH, t)
        nb_d = t // tm_d
        assert tm_d == min(TM_ROUTE, t)
        ts = tstat.reshape(nb_d, V7X_SUBLANES, -1)[:, :2, :N_EXPERTS].astype(I32)
        tile_cnt, tile_before = ts[:, 0], ts[:, 1]
        local_start = jnp.cumsum(tile_cnt, axis=1) - tile_cnt
        run_dst = p_start[None, :] + tile_before
        ls_tok = jnp.repeat(local_start.T, tm_d, axis=1)
        ids = jnp.arange(N_EXPERTS, dtype=I32)[:, None]
        lpos1 = jnp.sum(jnp.where(ri[_R_E1][None, :] == ids, ls_tok, 0), axis=0) + ri[_R_LRANK1]
        lpos2 = jnp.sum(jnp.where(ri[_R_E2][None, :] == ids, ls_tok, 0), axis=0) + ri[_R_LRANK2]
        idx_d = _index_records(lpos1, lpos2, tm_d,
                               jnp.concatenate([tile_cnt, run_dst, local_start], axis=1))
        n_free = n_rows - 2 * t
        zero_rows = TM_EXPERT // 2
        zinfo = jnp.concatenate([p_start + counts, padded - counts,
                                 jnp.stack([p_end[-1], (n_rows - p_end[-1]) // zero_rows])])
        zinfo = jnp.pad(zinfo, (0, -zinfo.shape[0] % _SMEM_RECORD_WORDS)).astype(I32)
        xs = _dispatch(hn, idx_d, zinfo, n_rows, tm_d, n_free)
        ys = _experts(xs, w1_all, w3_all, w2_all, tile_expert + i * N_EXPERTS, tile_valid, tile_first)

        tm_c = min(TM_COMBINE, t)
        h = _combine_ple(_index_records(pos1, pos2, tm_c), ys, h1, route, p.reshape(depth * t, -1),
                         norm_ple_w[i][None], ple_gate_w[i].astype(BF16), ple_gate_b[i][None],
                         ple_proj_w[i].astype(BF16), final_norm_w[None],
                         tm=tm_c, final=(i == depth - 1), layer=i)
    return h.reshape(batch, seq, d)
```

```python
import functools

import jax
import jax.numpy as jnp
from jax import lax
from jax.experimental import pallas as pl
from jax.experimental.pallas import tpu as pltpu

F32 = jnp.float32
BF16 = jnp.bfloat16
I32 = jnp.int32

CHUNK = 64
HEAD_DIM = 64
LEFT_CHUNKS = 8
MAX_REL = 256
N_GROUPS = 4
EXPERTS_PER_GROUP = 8
N_EXPERTS = N_GROUPS * EXPERTS_PER_GROUP
RMS_EPS = 1e-6
GN_EPS = 64e-5
NEG_INF = -1e30

V7X_LANES = 128
V7X_SUBLANES = 8
V7X_VMEM_LIMIT_BYTES = 40 * 1024 * 1024

TM_PROJ = 512
TB_RWKV = 512
RWKV_CHUNK_GROUP = 4
QB_ATTN = 256
TM_ROUTE = 512
TM_DISPATCH = 512
TM_EXPERT = 512
TM_COMBINE = 512


def _params(*sem):
    return pltpu.CompilerParams(dimension_semantics=sem, vmem_limit_bytes=V7X_VMEM_LIMIT_BYTES)


def _rms(x, w):
    return x * lax.rsqrt(jnp.mean(x * x, axis=-1, keepdims=True) + RMS_EPS) * w


def _mm(a, b):
    return jnp.dot(a.astype(BF16), b.astype(BF16), preferred_element_type=F32)


def _mm_nt(a, b):
    return lax.dot_general(a.astype(BF16), b.astype(BF16), (((1,), (1,)), ((), ())),
                           preferred_element_type=F32)


def _mm_tn(a, b):
    return lax.dot_general(a.astype(BF16), b.astype(BF16), (((0,), (0,)), ((), ())),
                           preferred_element_type=F32)


def _to_token_tiles(ref, x):
    m, d = x.shape
    for s in range(d // V7X_LANES):
        ref[pl.ds(s, m, stride=V7X_SUBLANES), :] = x[:, s * V7X_LANES:(s + 1) * V7X_LANES]


def _from_token_tiles(ref, m):
    return jnp.concatenate([ref[pl.ds(s, m, stride=V7X_SUBLANES), :] for s in range(V7X_SUBLANES)],
                           axis=-1)


def _token_tile(ref, row):
    return ref.at[pl.ds(pl.multiple_of(row * V7X_SUBLANES, V7X_SUBLANES), V7X_SUBLANES), :]


def _split3(x):
    hi = x.astype(BF16)
    r1 = x - hi.astype(F32)
    mid = r1.astype(BF16)
    lo = (r1 - mid.astype(F32)).astype(BF16)
    return hi, mid, lo


def _mm_exact_lhs(a_bf16, x):
    hi, mid, lo = _split3(x)
    return (jnp.dot(a_bf16, hi, preferred_element_type=F32)
            + jnp.dot(a_bf16, mid, preferred_element_type=F32)
            + jnp.dot(a_bf16, lo, preferred_element_type=F32))


def _mm_split2_rhs(x, b_bf16):
    hi = x.astype(BF16)
    lo = (x - hi.astype(F32)).astype(BF16)
    return (jnp.dot(hi, b_bf16, preferred_element_type=F32)
            + jnp.dot(lo, b_bf16, preferred_element_type=F32))


def _norm_proj_kernel(h_ref, nw_ref, wr_ref, wa_ref, zr_ref, qkv_ref):
    hn = _rms(h_ref[...], nw_ref[...]).astype(BF16)
    zr_ref[...] = jnp.dot(hn, wr_ref[...], preferred_element_type=F32)
    qkv_ref[...] = jnp.dot(hn, wa_ref[...], preferred_element_type=F32).astype(BF16)


def _norm_proj(h, nw, wr, wa):
    t, d = h.shape
    tm = min(TM_PROJ, t)
    n_r, n_a = wr.shape[1], wa.shape[1]
    return pl.pallas_call(
        _norm_proj_kernel,
        out_shape=(jax.ShapeDtypeStruct((t, n_r), F32), jax.ShapeDtypeStruct((t, n_a), BF16)),
        grid=(t // tm,),
        in_specs=[pl.BlockSpec((tm, d), lambda i: (i, 0)),
                  pl.BlockSpec((1, d), lambda i: (0, 0)),
                  pl.BlockSpec((d, n_r), lambda i: (0, 0)),
                  pl.BlockSpec((d, n_a), lambda i: (0, 0))],
        out_specs=(pl.BlockSpec((tm, n_r), lambda i: (i, 0)),
                   pl.BlockSpec((tm, n_a), lambda i: (i, 0))),
        compiler_params=_params("parallel"),
        name="norm_proj",
    )(h, nw, wr, wa)


_V_W0, _V_A0, _V_KK, _V_KA, _V_RK, _V_LNW, _V_LNB, _V_V0 = range(8)


def _rwkv_kernel(*refs, has_vres, n_heads, d_r, group, nb):
    if has_vres:
        (z_ref, vf_ref, mu_ref, vec_ref, wl_ref, tril_ref, ones_ref, v1_ref, v2_ref, y_ref,
         s_ref, carry_ref, r_s, k_s, v_s, kk_s, a_s, lc_s, lw_s, bon_s, g_s) = refs
        vf_out_ref = None
    else:
        (z_ref, mu_ref, vec_ref, wl_ref, tril_ref, ones_ref, y_ref, vf_out_ref,
         s_ref, carry_ref, r_s, k_s, v_s, kk_s, a_s, lc_s, lw_s, bon_s, g_s) = refs
    per_token = (r_s, k_s, v_s, kk_s, a_s, lc_s, lw_s, bon_s, g_s)
    tb = z_ref.shape[0]
    gr = group * CHUNK
    k_step = pl.program_id(0)
    cur = lax.rem(k_step, 2)
    prv = 1 - cur

    @pl.when(k_step == 0)
    def _():
        s_ref[...] = jnp.zeros_like(s_ref)
        for ref in per_token:
            ref[...] = jnp.zeros_like(ref)

    @pl.when(lax.rem(k_step, nb) == 0)
    def _():
        carry_ref[...] = jnp.zeros_like(carry_ref)

    vec = vec_ref[...]

    def vrow(i):
        return vec[i:i + 1, :]

    ln_w = vrow(_V_LNW)
    ln_b = vrow(_V_LNB)
    mu = mu_ref[...]
    head_ones = ones_ref[...]
    tril = tril_ref[...]
    ci = lax.broadcasted_iota(I32, (CHUNK, CHUNK), 0)
    cj = lax.broadcasted_iota(I32, (CHUNK, CHUNK), 1)
    strict = cj < ci
    lower = cj <= ci
    eye = ci == cj
    eye_f = jnp.where(eye, 1.0, 0.0)
    hs = [slice(h * HEAD_DIM, (h + 1) * HEAD_DIM) for h in range(n_heads)]
    prev_first = lax.rem(k_step + nb - 1, nb) == 0

    def token_work(rows):
        z = z_ref[rows, :]
        row = lax.broadcasted_iota(I32, z.shape, 0)
        z_prev = jnp.where(row == 0, carry_ref[0:1, :], pltpu.roll(z, 1, axis=0))
        carry_ref[0:1, :] = z[gr - 1:gr, :]
        zs = z + (z_prev - z) * mu
        r = zs[:, 0:d_r]
        k = zs[:, d_r:2 * d_r]
        v = zs[:, 2 * d_r:3 * d_r]
        lo = zs[:, 3 * d_r:]
        n_lo = lo.shape[1]
        lane = lax.broadcasted_iota(I32, lo.shape, 1)
        lo_act = jnp.where(lane < n_lo // 4, jnp.tanh(lo),
                           jnp.where(lane < n_lo // 2, lo, jax.nn.sigmoid(lo)))
        lo_out = _mm(lo_act, wl_ref[...])
        if has_vres:
            vv = _mm(_mm(v, v1_ref[...]), v2_ref[...])
            v = v + (vf_ref[rows, :] - v) * jax.nn.sigmoid(vrow(_V_V0) + vv)
        w_log = -jax.nn.softplus(-(vrow(_V_W0) + lo_out[:, 0:d_r])) - 0.5
        lw = -jnp.exp(w_log)
        a = jax.nn.sigmoid(vrow(_V_A0) + lo_out[:, d_r:2 * d_r])
        kk = k * vrow(_V_KK)
        kk = kk * lax.rsqrt(jnp.maximum(_mm_split2_rhs(kk * kk, head_ones), 1e-24))
        k2 = k * (1.0 + (a - 1.0) * vrow(_V_KA))
        r_s[cur, rows, :] = r
        k_s[cur, rows, :] = k2
        v_s[cur, rows, :] = v
        kk_s[cur, rows, :] = kk
        a_s[cur, rows, :] = a
        lc_s[cur, rows, :] = _mm_exact_lhs(tril, lw)
        lw_s[cur, rows, :] = lw
        bon_s[cur, rows, :] = _mm_split2_rhs(r * k2 * vrow(_V_RK), head_ones) * v
        g_s[cur, rows, :] = lo_out[:, 2 * d_r:3 * d_r]

    def chunk_operands(r0):
        rs = pl.ds(r0, CHUNK)
        lc_c = lc_s[prv, rs, :]
        lw_c = lw_s[prv, rs, :]
        l_end = lc_s[prv, pl.ds(r0 + CHUNK - 1, 1), :]
        p_in = jnp.exp(lc_c)
        p_prev = jnp.exp(lc_c - lw_c)
        p_inv = jnp.exp(-lc_c)
        p_end = jnp.exp(l_end - lc_c)
        p_last = jnp.exp(l_end)
        kk_c = kk_s[prv, rs, :]
        b_c = kk_c * a_s[prv, rs, :]
        k_c = k_s[prv, rs, :]
        at = (-kk_c * p_prev).astype(BF16)
        bt = (b_c * p_inv).astype(BF16)
        bh = (b_c * p_end).astype(BF16)
        kt = (k_c * p_inv).astype(BF16)
        kh = (k_c * p_end).astype(BF16)
        rt = (r_s[prv, rs, :] * p_in).astype(BF16)
        vc = v_s[prv, rs, :].astype(BF16)
        per_head = [[x[:, sl] for sl in hs] for x in (at, bt, bh, kt, kh, rt, vc)]
        per_head.append([p_last[:, sl] for sl in hs])
        return per_head

    def group_body(gi, carry):
        g0 = pl.multiple_of(gi * gr, gr)
        rows = pl.ds(g0, gr)
        if vf_out_ref is not None:
            vf_out_ref[rows, :] = v_s[prv, rows, :]
        ops = [chunk_operands(g0 + c * CHUNK) for c in range(group)]
        at_h, bt_h, bh_h, kt_h, kh_h, rt_h, v_h, pl_h = ([x for c in range(group) for x in ops[c][q]]
                                                         for q in range(8))
        heads = range(group * n_heads)
        ar_h = [jnp.concatenate([at_h[h], rt_h[h]], axis=0) for h in heads]
        m_b = [_mm_nt(ar_h[h], bt_h[h]) for h in heads]
        m_k = [_mm_nt(ar_h[h], kt_h[h]) for h in heads]
        n_ab = [jnp.where(strict, m_b[h][:CHUNK], 0.0) for h in heads]
        a_ak = [jnp.where(strict, m_k[h][:CHUNK], 0.0) for h in heads]
        a_rb = [jnp.where(lower, m_b[h][CHUNK:], 0.0) for h in heads]
        a_rk = [jnp.where(lower, m_k[h][CHUNK:], 0.0) for h in heads]
        x_inv = [eye_f + n_ab[h] for h in heads]
        pw = [_mm(n_ab[h], n_ab[h]) for h in heads]
        akv = [_mm(a_ak[h], v_h[h]) for h in heads]
        n_sq = CHUNK.bit_length() - 2
        for it in range(n_sq):
            if it < n_sq - 1:
                st = [_mm(jnp.concatenate([x_inv[h], pw[h]], axis=0), pw[h]) for h in heads]
                x_inv = [x_inv[h] + st[h][:CHUNK] for h in heads]
                pw = [st[h][CHUNK:] for h in heads]
            else:
                st = [_mm(x_inv[h], pw[h]) for h in heads]
                x_inv = [x_inv[h] + st[h] for h in heads]
        w_h = [_mm(x_inv[h], at_h[h]) for h in heads]
        u0 = [_mm(x_inv[h], akv[h]) for h in heads]
        y0 = [_mm(a_rk[h], v_h[h]) + _mm(a_rb[h], u0[h]) for h in heads]
        r_p = [rt_h[h].astype(F32) + _mm(a_rb[h], w_h[h]) for h in heads]
        g_h = [jnp.where(eye, pl_h[h], 0.0) + _mm_tn(w_h[h], bh_h[h]) for h in heads]
        d_h = [_mm_tn(u0[h], bh_h[h]) + _mm_tn(v_h[h], kh_h[h]) for h in heads]
        fresh = prev_first & (gi == 0)
        s_h = [jnp.where(fresh, 0.0, s_ref[h]) for h in range(n_heads)]
        for c in range(group):
            rs = pl.ds(g0 + c * CHUNK, CHUNK)
            idx = [c * n_heads + h for h in range(n_heads)]
            y_h = [y0[i] + _mm_nt(r_p[i], s_h[h]) for h, i in enumerate(idx)]
            s_h = [_mm(s_h[h], g_h[i]) + d_h[i] for h, i in enumerate(idx)]
            y_heads = []
            for h in range(n_heads):
                mean = jnp.mean(y_h[h], axis=-1, keepdims=True)
                yc = y_h[h] - mean
                var = jnp.mean(yc * yc, axis=-1, keepdims=True)
                y_heads.append(yc * lax.rsqrt(var + GN_EPS))
            y_n = jnp.concatenate(y_heads, axis=-1)
            out = (y_n * ln_w + ln_b + bon_s[prv, rs, :]) * g_s[prv, rs, :]
            y_ref[rs, :] = out.astype(y_ref.dtype)
        for h in range(n_heads):
            s_ref[h] = s_h[h]
        token_work(rows)
        return carry

    lax.fori_loop(0, tb // gr, group_body, 0)


def _rwkv(z, v_first, mu, vec, wl, v1, v2, *, batch, seq, n_heads):
    t, n_z = z.shape
    d_r = n_heads * HEAD_DIM
    tb = min(TB_RWKV, seq)
    nb = seq // tb
    n_blocks = batch * nb
    has_vres = v_first is not None
    tok_in = lambda k: (jnp.minimum(k, n_blocks - 1), 0)
    tok_out = lambda k: (jnp.maximum(k - 1, 0), 0)
    const = lambda k: (0, 0)
    in_specs = [pl.BlockSpec((tb, n_z), tok_in)]
    args = [z]
    if has_vres:
        in_specs.append(pl.BlockSpec((tb, d_r), tok_in))
        args.append(v_first)
    n_chunks = tb // CHUNK
    group = RWKV_CHUNK_GROUP if n_chunks % RWKV_CHUNK_GROUP == 0 else 1
    ti = jnp.arange(group * CHUNK)
    tril = ((ti[:, None] // CHUNK == ti[None, :] // CHUNK) & (ti[None, :] <= ti[:, None])).astype(BF16)
    hi = jnp.arange(d_r) // HEAD_DIM
    head_ones = (hi[:, None] == hi[None, :]).astype(BF16)
    in_specs += [pl.BlockSpec(mu.shape, const), pl.BlockSpec(vec.shape, const),
                 pl.BlockSpec(wl.shape, const), pl.BlockSpec(tril.shape, const),
                 pl.BlockSpec(head_ones.shape, const)]
    args += [mu, vec, wl, tril, head_ones]
    if has_vres:
        in_specs += [pl.BlockSpec(v1.shape, const), pl.BlockSpec(v2.shape, const)]
        args += [v1, v2]
        out_shape = jax.ShapeDtypeStruct((t, d_r), BF16)
        out_specs = pl.BlockSpec((tb, d_r), tok_out)
    else:
        out_shape = (jax.ShapeDtypeStruct((t, d_r), BF16), jax.ShapeDtypeStruct((t, d_r), F32))
        out_specs = (pl.BlockSpec((tb, d_r), tok_out), pl.BlockSpec((tb, d_r), tok_out))
    scratch = [pltpu.VMEM((n_heads, HEAD_DIM, HEAD_DIM), F32),
               pltpu.VMEM((V7X_SUBLANES, n_z), F32)]
    scratch += [pltpu.VMEM((2, tb, d_r), F32) for _ in range(9)]
    return pl.pallas_call(
        functools.partial(_rwkv_kernel, has_vres=has_vres, n_heads=n_heads, d_r=d_r, group=group, nb=nb),
        out_shape=out_shape,
        grid=(n_blocks + 1,),
        in_specs=in_specs,
        out_specs=out_specs,
        scratch_shapes=scratch,
        compiler_params=_params("arbitrary"),
        name="rwkv_vres" if has_vres else "rwkv",
    )(*args)


def _attn_kernel(*refs, n_heads, n_parts):
    q_ref = refs[0]
    k_refs = refs[1:1 + n_parts]
    v_refs = refs[1 + n_parts:1 + 2 * n_parts]
    tab_ref = refs[1 + 2 * n_parts]
    o_ref = refs[2 + 2 * n_parts]
    qb = q_ref.shape[0]
    j = pl.program_id(1)
    scale = HEAD_DIM ** -0.5
    q = q_ref[...] * jnp.asarray(scale, q_ref.dtype)
    ks = [r[...] for r in k_refs]
    vs = [r[...] for r in v_refs]
    pw = 2 * HEAD_DIM
    lane = lax.broadcasted_iota(I32, (qb, pw), 1)
    sum_even = jnp.where(lane < HEAD_DIM, 1.0, 0.0).astype(q.dtype)
    sum_odd = jnp.where(lane < HEAD_DIM, 0.0, 1.0).astype(q.dtype)
    zero = jnp.zeros((), q.dtype)
    is_even = sum_even > zero
    outs = []
    for hp in range(n_heads // 2):
        sl = slice(hp * pw, (hp + 1) * pw)
        qq = q[:, sl]
        q_pair = (jnp.where(is_even, qq, zero), jnp.where(is_even, zero, qq))
        s_parts = [[], []]
        for p in range(n_parts):
            kk = ks[p][:, sl]
            back = n_parts - 1 - p
            for u in range(2):
                s = _mm_nt(q_pair[u], kk) + tab_ref[2 * hp + u, :, p * qb:(p + 1) * qb]
                if back > 0:
                    s = jnp.where(j >= back, s, NEG_INF)
                s_parts[u].append(s)
        m = []
        for u in range(2):
            mm = s_parts[u][0]
            for s in s_parts[u][1:]:
                mm = jnp.maximum(mm, s)
            m.append(mm.max(axis=-1, keepdims=True))
        acc = jnp.zeros((qb, 2 * pw), F32)
        for p in range(n_parts):
            vv = vs[p][:, sl]
            rhs = jnp.concatenate(
                [jnp.concatenate([jnp.where(is_even, vv, zero), sum_even], axis=1),
                 jnp.concatenate([jnp.where(is_even, zero, vv), sum_odd], axis=1)], axis=0)
            e = jnp.concatenate([jnp.exp(s_parts[u][p] - m[u]).astype(BF16) for u in range(2)], axis=1)
            acc = acc + jnp.dot(e, rhs, preferred_element_type=F32)
        outs.append(acc[:, :pw] / acc[:, pw:])
    o_ref[...] = jnp.concatenate(outs, axis=-1).astype(o_ref.dtype)


def _attn(qkv, table, *, batch, seq, n_heads):
    t = qkv.shape[0]
    d_a = n_heads * HEAD_DIM
    qb = min(QB_ATTN, seq)
    left = LEFT_CHUNKS * CHUNK
    assert left % qb == 0 and seq % qb == 0
    n_parts = left // qb + 1
    nb = seq // qb
    in_specs = [pl.BlockSpec((qb, d_a), lambda b, j: (b * nb + j, 0))]
    for p in range(n_parts):
        back = n_parts - 1 - p
        in_specs.append(pl.BlockSpec((qb, d_a), lambda b, j, back=back: (b * nb + jnp.maximum(j - back, 0), 1)))
    for p in range(n_parts):
        back = n_parts - 1 - p
        in_specs.append(pl.BlockSpec((qb, d_a), lambda b, j, back=back: (b * nb + jnp.maximum(j - back, 0), 2)))
    in_specs.append(pl.BlockSpec(table.shape, lambda b, j: (0, 0, 0)))
    return pl.pallas_call(
        functools.partial(_attn_kernel, n_heads=n_heads, n_parts=n_parts),
        out_shape=jax.ShapeDtypeStruct((t, d_a), BF16),
        grid=(batch, nb),
        in_specs=in_specs,
        out_specs=pl.BlockSpec((qb, d_a), lambda b, j: (b * nb + j, 0)),
        compiler_params=_params("parallel", "arbitrary"),
        name="attn",
    )(*([qkv] * (1 + 2 * n_parts)), table)


def _attn_table(rel_bias, qb):
    left = LEFT_CHUNKS * CHUNK
    n_keys = left + qb
    period = qb + n_keys - 1
    n_heads = rel_bias.shape[0]
    m = jnp.arange(period)
    rel = left - jnp.where(m < n_keys, m, m - period)
    g = rel_bias[:, jnp.clip(rel, -(CHUNK - 1), MAX_REL) + (CHUNK - 1)].astype(F32)
    flat = jnp.tile(g, (1, qb))[:, :qb * (period - 1)]
    bias = flat.reshape(n_heads, qb, period - 1)[:, :, :n_keys]
    cq = jnp.arange(qb)[:, None] // CHUNK
    ck = jnp.arange(n_keys)[None, :] // CHUNK
    valid = (ck >= cq) & (ck <= cq + LEFT_CHUNKS)
    return jnp.where(valid[None], bias, NEG_INF)


_R_E1, _R_E2, _R_C1, _R_C2, _R_RANK1, _R_RANK2, _R_LRANK1, _R_LRANK2 = range(8)


def _outproj_route_kernel(yr_ref, ya_ref, h_ref, wor_ref, woa_ref, nw_ref, wrt_ref, brt_ref,
                          h1_ref, hn_ref, route_ref, route_t_ref, cnt_ref, tstat_ref, carry_ref):
    i = pl.program_id(0)

    @pl.when(i == 0)
    def _():
        carry_ref[...] = jnp.zeros_like(carry_ref)

    h1 = (h_ref[...] + jnp.dot(yr_ref[...], wor_ref[...], preferred_element_type=F32)
          + jnp.dot(ya_ref[...], woa_ref[...], preferred_element_type=F32))
    h1_ref[...] = h1
    hn = _rms(h1, nw_ref[...])
    _to_token_tiles(hn_ref, hn)
    nl = brt_ref.shape[1]
    hn_hi = hn.astype(BF16)
    hn_lo = (hn - hn_hi.astype(F32)).astype(BF16)
    part = jnp.dot(hn_hi, wrt_ref[...], preferred_element_type=F32)
    logits = (part[:, :nl] + part[:, nl:]
              + jnp.dot(hn_lo, wrt_ref[:, :nl], preferred_element_type=F32) + brt_ref[...])
    tm = logits.shape[0]
    lane = lax.broadcasted_iota(I32, (tm, nl), 1)
    lane_f = lane.astype(F32)
    ninf = -jnp.inf
    big = float(nl)
    is_g = lane < N_GROUPS
    gl = jnp.where(is_g, logits, ninf)
    g_max = gl.max(axis=-1, keepdims=True)
    g_sel = jnp.where(gl == g_max, lane_f, big).min(axis=-1, keepdims=True)
    p_g = 1.0 / jnp.where(is_g, jnp.exp(logits - g_max), 0.0).sum(axis=-1, keepdims=True)
    e_lo = N_GROUPS + EXPERTS_PER_GROUP * g_sel
    in_grp = (lane_f >= e_lo) & (lane_f < e_lo + EXPERTS_PER_GROUP)
    el = jnp.where(in_grp, logits, ninf)
    m1 = el.max(axis=-1, keepdims=True)
    i1 = jnp.where(el == m1, lane_f, big).min(axis=-1, keepdims=True)
    el2 = jnp.where(lane_f == i1, ninf, el)
    m2 = el2.max(axis=-1, keepdims=True)
    i2 = jnp.where(el2 == m2, lane_f, big).min(axis=-1, keepdims=True)
    t2 = jnp.exp(m2 - m1)
    c1 = p_g / (1.0 + t2)
    c2 = p_g * t2 / (1.0 + t2)
    e1 = i1 - N_GROUPS
    e2 = i2 - N_GROUPS
    oh1 = lane_f == e1
    oh2 = lane_f == e2
    ohs = jnp.where(oh1 | oh2, 1.0, 0.0)
    ri = lax.broadcasted_iota(I32, (tm, tm), 0)
    rj = lax.broadcasted_iota(I32, (tm, tm), 1)
    before = jnp.where(rj < ri, 1.0, 0.0).astype(BF16)
    old_carry = carry_ref[0:1, :]
    cnt_tile = jnp.dot(before, ohs.astype(BF16), preferred_element_type=F32)
    cnt = cnt_tile + old_carry
    rank1 = jnp.where(oh1, cnt, 0.0).sum(axis=-1, keepdims=True)
    rank2 = jnp.where(oh2, cnt, 0.0).sum(axis=-1, keepdims=True)
    lrank1 = jnp.where(oh1, cnt_tile, 0.0).sum(axis=-1, keepdims=True)
    lrank2 = jnp.where(oh2, cnt_tile, 0.0).sum(axis=-1, keepdims=True)
    tile_cnt = ohs.sum(axis=0, keepdims=True)
    new_carry = old_carry + tile_cnt
    carry_ref[0:1, :] = new_carry
    cnt_ref[...] = jnp.broadcast_to(new_carry, cnt_ref.shape)
    srow = lax.broadcasted_iota(I32, tstat_ref.shape, 0)
    tstat_ref[...] = jnp.where(srow == 0, tile_cnt, jnp.where(srow == 1, old_carry, 0.0))
    route = jnp.zeros((tm, nl), F32)
    for idx, val in ((_R_E1, e1), (_R_E2, e2), (_R_C1, c1), (_R_C2, c2),
                     (_R_RANK1, rank1), (_R_RANK2, rank2), (_R_LRANK1, lrank1), (_R_LRANK2, lrank2)):
        route = jnp.where(lane == idx, val, route)
    route_ref[...] = route
    route_t_ref[...] = route.T[:route_t_ref.shape[0], :]


def _outproj_route(yr, ya, h, wor, woa, nw, wrt, brt):
    t, d = h.shape
    tm = min(TM_ROUTE, t)
    d_r, d_a = yr.shape[1], ya.shape[1]
    nl = brt.shape[1]
    tok = lambda i: (i, 0)
    const = lambda i: (0, 0)
    return pl.pallas_call(
        _outproj_route_kernel,
        out_shape=(jax.ShapeDtypeStruct((t, d), F32),
                   jax.ShapeDtypeStruct((t * V7X_SUBLANES, V7X_LANES), F32),
                   jax.ShapeDtypeStruct((t, nl), F32), jax.ShapeDtypeStruct((V7X_SUBLANES, t), F32),
                   jax.ShapeDtypeStruct((V7X_SUBLANES, nl), F32),
                   jax.ShapeDtypeStruct((t // tm * V7X_SUBLANES, nl), F32)),
        grid=(t // tm,),
        in_specs=[pl.BlockSpec((tm, d_r), tok), pl.BlockSpec((tm, d_a), tok), pl.BlockSpec((tm, d), tok),
                  pl.BlockSpec((d_r, d), const), pl.BlockSpec((d_a, d), const),
                  pl.BlockSpec((1, d), const), pl.BlockSpec(wrt.shape, const), pl.BlockSpec((1, nl), const)],
        out_specs=(pl.BlockSpec((tm, d), tok), pl.BlockSpec((tm * V7X_SUBLANES, V7X_LANES), tok),
                   pl.BlockSpec((tm, nl), tok), pl.BlockSpec((V7X_SUBLANES, tm), lambda i: (0, i)),
                   pl.BlockSpec((V7X_SUBLANES, nl), const), pl.BlockSpec((V7X_SUBLANES, nl), tok)),
        scratch_shapes=[pltpu.VMEM((V7X_SUBLANES, nl), F32)],
        compiler_params=_params("arbitrary"),
        name="outproj_route",
    )(yr, ya, h, wor, woa, nw, wrt, brt)


def _load_indices(idx_hbm, i, idx_smem, sem):
    n = idx_smem.shape[0]
    cp = pltpu.make_async_copy(idx_hbm.at[pl.ds(pl.multiple_of(i * n, n), n)], idx_smem, sem)
    cp.start()
    cp.wait()


def _tiles(ref, row, n):
    start = row * V7X_SUBLANES
    if not isinstance(row, int):
        start = pl.multiple_of(start, V7X_SUBLANES)
    return ref.at[pl.ds(start, n * V7X_SUBLANES), :]


def _start_run_copies(length, n_bits, copy_of):
    for b in range(n_bits):
        @pl.when(((length >> b) & 1) == 1)
        def _(b=b):
            done = (length >> (b + 1)) << (b + 1)
            copy_of(done, 1 << b, b).start(priority=b % 2)


def _dispatch_kernel(idx_hbm, zinfo_hbm, x_ref, xs_hbm, idx_smem, zinfo_smem, xloc, zero_vmem,
                     idx_sem, run_sem, zero_sem, *, tm, n_free):
    i = pl.program_id(0)
    nb = pl.num_programs(0)
    buf = lax.rem(i, 2)
    zero_rows = zero_vmem.shape[0] // V7X_SUBLANES

    @pl.when(i == 0)
    def _():
        cp = pltpu.make_async_copy(zinfo_hbm, zinfo_smem, idx_sem)
        cp.start()
        cp.wait()
        zero_vmem[...] = jnp.zeros_like(zero_vmem)

        def pad_runs(e, carry):
            start = zinfo_smem[e]
            _start_run_copies(
                zinfo_smem[N_EXPERTS + e], zero_rows.bit_length(),
                lambda done, n, b: pltpu.make_async_copy(_tiles(zero_vmem, 0, n),
                                                         _tiles(xs_hbm, start + done, n), zero_sem))
            return carry

        lax.fori_loop(0, N_EXPERTS, pad_runs, 0)
        tail_start = zinfo_smem[2 * N_EXPERTS]

        def tail_block(n, carry):
            pltpu.make_async_copy(zero_vmem, _tiles(xs_hbm, tail_start + n * zero_rows, zero_rows),
                                  zero_sem).start()
            return carry

        lax.fori_loop(0, zinfo_smem[2 * N_EXPERTS + 1], tail_block, 0)

    _load_indices(idx_hbm, i, idx_smem, idx_sem)

    def wait_runs(b):
        pltpu.make_async_copy(xloc.at[b], xs_hbm.at[pl.ds(0, xloc.shape[1]), :], run_sem.at[b]).wait()

    @pl.when(i >= 2)
    def _():
        wait_runs(buf)

    def place(tt, carry):
        row = x_ref[pl.ds(pl.multiple_of(tt * V7X_SUBLANES, V7X_SUBLANES), V7X_SUBLANES), :]
        for s in range(2):
            lp = idx_smem[s * tm + tt]
            xloc[buf, pl.ds(pl.multiple_of(lp * V7X_SUBLANES, V7X_SUBLANES), V7X_SUBLANES), :] = row
        return carry

    lax.fori_loop(0, tm, place, 0, unroll=8)

    def expert_run(e, carry):
        dst = idx_smem[2 * tm + N_EXPERTS + e]
        off = idx_smem[2 * tm + 2 * N_EXPERTS + e]
        _start_run_copies(
            idx_smem[2 * tm + e], tm.bit_length(),
            lambda done, n, b: pltpu.make_async_copy(_tiles(xloc.at[buf], off + done, n),
                                                     _tiles(xs_hbm, dst + done, n), run_sem.at[buf]))
        return carry

    lax.fori_loop(0, N_EXPERTS, expert_run, 0)

    @pl.when(i == nb - 1)
    def _():
        wait_runs(buf)

        @pl.when(nb >= 2)
        def _():
            wait_runs(1 - buf)

        pltpu.make_async_copy(xs_hbm.at[pl.ds(0, n_free * V7X_SUBLANES), :],
                              xs_hbm.at[pl.ds(0, n_free * V7X_SUBLANES), :], zero_sem).wait()


def _dispatch(x, idx, zinfo, n_rows, tm, n_free):
    t = x.shape[0] // V7X_SUBLANES
    nb = t // tm
    rec = idx.shape[0] // nb
    return pl.pallas_call(
        functools.partial(_dispatch_kernel, tm=tm, n_free=n_free),
        out_shape=jax.ShapeDtypeStruct((n_rows * V7X_SUBLANES, V7X_LANES), x.dtype),
        grid=(nb,),
        in_specs=[pl.BlockSpec(memory_space=pl.ANY), pl.BlockSpec(memory_space=pl.ANY),
                  pl.BlockSpec((tm * V7X_SUBLANES, V7X_LANES), lambda i: (i, 0))],
        out_specs=pl.BlockSpec(memory_space=pl.ANY),
        scratch_shapes=[pltpu.SMEM((rec,), I32), pltpu.SMEM(zinfo.shape, I32),
                        pltpu.VMEM((2, 2 * tm * V7X_SUBLANES, V7X_LANES), x.dtype),
                        pltpu.VMEM((TM_EXPERT // 2 * V7X_SUBLANES, V7X_LANES), x.dtype),
                        pltpu.SemaphoreType.DMA, pltpu.SemaphoreType.DMA((2,)), pltpu.SemaphoreType.DMA],
        compiler_params=_params("arbitrary"),
        name="dispatch",
    )(idx, zinfo, x)


def _experts_kernel(te_ref, tv_ref, tf_ref, x_ref, w1_ref, w3_ref, w2_ref, y_ref,
                    w1_b, w3_b, w2_b, *, tm):
    i = pl.program_id(0)

    @pl.when(tf_ref[i] > 0)
    def _():
        w1_b[...] = w1_ref[...].astype(BF16)
        w3_b[...] = w3_ref[...].astype(BF16)
        w2_b[...] = w2_ref[...].astype(BF16)

    @pl.when(tv_ref[i] > 0)
    def _():
        x = _from_token_tiles(x_ref, tm).astype(BF16)
        h_gate = jnp.dot(x, w1_b[...], preferred_element_type=F32)
        h_up = jnp.dot(x, w3_b[...], preferred_element_type=F32)
        hid = (h_gate * jax.nn.sigmoid(h_gate) * h_up).astype(BF16)
        _to_token_tiles(y_ref, jnp.dot(hid, w2_b[...], preferred_element_type=F32))

    @pl.when(tv_ref[i] == 0)
    def _():
        y_ref[...] = jnp.zeros_like(y_ref)


def _experts(xs, w1, w3, w2, tile_expert, tile_valid, tile_first):
    n_rows = xs.shape[0] // V7X_SUBLANES
    tm = TM_EXPERT
    nt = n_rows // tm
    d, f = w1.shape[1:]
    tile_spec = pl.BlockSpec((tm * V7X_SUBLANES, V7X_LANES), lambda i, te, tv, tf: (i, 0))
    grid_spec = pltpu.PrefetchScalarGridSpec(
        num_scalar_prefetch=3,
        grid=(nt,),
        in_specs=[tile_spec,
                  pl.BlockSpec((None, d, f), lambda i, te, tv, tf: (te[i], 0, 0)),
                  pl.BlockSpec((None, d, f), lambda i, te, tv, tf: (te[i], 0, 0)),
                  pl.BlockSpec((None, f, d), lambda i, te, tv, tf: (te[i], 0, 0))],
        out_specs=tile_spec,
        scratch_shapes=[pltpu.VMEM((d, f), BF16), pltpu.VMEM((d, f), BF16), pltpu.VMEM((f, d), BF16)],
    )
    return pl.pallas_call(
        functools.partial(_experts_kernel, tm=tm),
        out_shape=jax.ShapeDtypeStruct(xs.shape, F32),
        grid_spec=grid_spec,
        compiler_params=_params("arbitrary"),
        name="experts",
    )(tile_expert, tile_valid, tile_first, xs, w1, w3, w2)


def _combine_ple_kernel(pos_hbm, ys_hbm, h_ref, route_ref, p_ref, nw_ref, wg_ref, bg_ref, wp_ref,
                        fw_ref, o_ref, idx_smem, ybuf, idx_sem, row_sem, *, tm, final):
    i = pl.program_id(0)
    nb = pl.num_programs(0)
    n_buf = ybuf.shape[0]
    cur = lax.rem(i, n_buf)

    def gather(step, buf):
        _load_indices(pos_hbm, step, idx_smem, idx_sem)

        def issue(tt, carry):
            for s in range(2):
                pltpu.make_async_copy(_token_tile(ys_hbm, idx_smem[s * tm + tt]),
                                      _token_tile(ybuf.at[buf, s], tt), row_sem.at[buf]).start(priority=s)
            return carry

        lax.fori_loop(0, tm, issue, 0, unroll=8)

    def wait_gather(buf):
        for s in range(2):
            pltpu.make_async_copy(ys_hbm.at[pl.ds(0, ybuf.shape[2]), :], ybuf.at[buf, s],
                                  row_sem.at[buf]).wait()

    @pl.when(i == 0)
    def _():
        gather(0, 0)

        @pl.when(nb > 1)
        def _():
            gather(1, 1)

    wait_gather(cur)

    route = route_ref[...]
    c1 = route[:, _R_C1:_R_C1 + 1]
    c2 = route[:, _R_C2:_R_C2 + 1]
    h2 = (h_ref[...] + c1 * _from_token_tiles(ybuf.at[cur, 0], tm)
          + c2 * _from_token_tiles(ybuf.at[cur, 1], tm))
    hn = _rms(h2, nw_ref[...]).astype(BF16)
    gate = jax.nn.sigmoid(jnp.dot(hn, wg_ref[...], preferred_element_type=F32) + bg_ref[...])
    h3 = h2 + gate * jnp.dot(p_ref[...].astype(BF16), wp_ref[...], preferred_element_type=F32)
    if final:
        h3 = _rms(h3, fw_ref[...])
    o_ref[...] = h3

    @pl.when(i + 2 < nb)
    def _():
        gather(i + 2, lax.rem(i + 2, n_buf))


def _combine_ple(pos_tiles, ys, h, route, p, nw, wg, bg, wp, fw, *, tm, final, layer):
    t, d = h.shape
    nl = route.shape[1]
    dp = p.shape[1]
    tok = lambda i: (i, 0)
    const = lambda i: (0, 0)
    return pl.pallas_call(
        functools.partial(_combine_ple_kernel, tm=tm, final=final),
        out_shape=jax.ShapeDtypeStruct((t, d), F32),
        grid=(t // tm,),
        in_specs=[pl.BlockSpec(memory_space=pl.ANY), pl.BlockSpec(memory_space=pl.ANY),
                  pl.BlockSpec((tm, d), tok), pl.BlockSpec((tm, nl), tok),
                  pl.BlockSpec((tm, dp), lambda i: (i + layer * (t // tm), 0)),
                  pl.BlockSpec((1, d), const), pl.BlockSpec((d, d), const), pl.BlockSpec((1, d), const),
                  pl.BlockSpec((dp, d), const), pl.BlockSpec((1, d), const)],
        out_specs=pl.BlockSpec((tm, d), tok),
        scratch_shapes=[pltpu.SMEM((pos_tiles.shape[0] // (t // tm),), I32),
                        pltpu.VMEM((3, 2, tm * V7X_SUBLANES, V7X_LANES), F32),
                        pltpu.SemaphoreType.DMA, pltpu.SemaphoreType.DMA((3,))],
        compiler_params=_params("arbitrary"),
        name="combine_ple_final" if final else "combine_ple",
    )(pos_tiles, ys, h, route, p, nw, wg, bg, wp, fw)


_SMEM_RECORD_WORDS = 1024


def _index_records(pos1, pos2, tm, extra=None):
    nb = pos1.shape[0] // tm
    parts = [pos1.reshape(nb, tm), pos2.reshape(nb, tm)]
    if extra is not None:
        parts.append(extra.reshape(nb, -1))
    rec = jnp.concatenate(parts, axis=1)
    pad = -rec.shape[1] % _SMEM_RECORD_WORDS
    return jnp.pad(rec, ((0, 0), (0, pad))).reshape(-1)


def _lookup(table, idx):
    ids = jnp.arange(table.shape[0], dtype=I32)
    return jnp.sum(jnp.where(idx[None, :] == ids[:, None], table[:, None], 0), axis=0)


def _bucket(ends, x):
    return jnp.minimum(jnp.sum((x[None, :] >= ends[:, None]).astype(I32), axis=0), ends.shape[0] - 1)


def kernel(x, p, norm_mix_w, w_in, rwkv_mu, rwkv_w0, rwkv_w2, rwkv_a0, rwkv_a2, rwkv_g2, rwkv_k_k, rwkv_k_a, rwkv_r_k, rwkv_ln_w, rwkv_ln_b, rwkv_v0, rwkv_v1, rwkv_v2, att_rel_bias, w_out, norm_ffn_w, router_group_w, router_group_b, router_expert_w, router_expert_b, expert_w1, expert_w3, expert_w2, norm_ple_w, ple_gate_w, ple_gate_b, ple_proj_w, final_norm_w):
    batch, seq, d = x.shape
    depth = w_in.shape[0]
    t = batch * seq
    d_r = rwkv_w0.shape[1]
    n_heads_r = d_r // HEAD_DIM
    n_rwkv_in = rwkv_mu.shape[1]
    d_a = (w_in.shape[2] - n_rwkv_in) // 3
    n_heads_a = d_a // HEAD_DIM
    n_dec, n_iclr, n_gate = rwkv_w2.shape[1], rwkv_a2.shape[1], rwkv_g2.shape[1]
    assert n_dec == n_iclr and n_gate == n_dec + n_iclr
    n_lo = n_dec + n_iclr + n_gate
    f_exp = expert_w1.shape[-1]
    assert d == V7X_SUBLANES * V7X_LANES
    n_rows = 2 * t + N_EXPERTS * TM_EXPERT
    n_tiles = n_rows // TM_EXPERT
    qb = min(QB_ATTN, seq)

    w1_all = expert_w1.reshape(depth * N_EXPERTS, d, f_exp)
    w3_all = expert_w3.reshape(depth * N_EXPERTS, d, f_exp)
    w2_all = expert_w2.reshape(depth * N_EXPERTS, f_exp, d)

    h = x.reshape(t, d)
    v_first = None
    for i in range(depth):
        wr = w_in[i, :, :n_rwkv_in].astype(BF16)
        wa = w_in[i, :, n_rwkv_in:].astype(BF16)
        wl = jnp.zeros((n_lo, 3 * d_r), F32)
        wl = wl.at[:n_dec, :d_r].set(rwkv_w2[i])
        wl = wl.at[n_dec:n_dec + n_iclr, d_r:2 * d_r].set(rwkv_a2[i])
        wl = wl.at[n_dec + n_iclr:, 2 * d_r:].set(rwkv_g2[i]).astype(BF16)
        v0 = rwkv_v0[i - 1] if i > 0 else jnp.zeros((d_r,), F32)
        vec = jnp.stack([rwkv_w0[i], rwkv_a0[i], rwkv_k_k[i], rwkv_k_a[i], rwkv_r_k[i],
                         rwkv_ln_w[i], rwkv_ln_b[i], v0])
        if i > 0:
            n_vr = rwkv_v1.shape[2]
            v1 = jnp.zeros((d_r, V7X_LANES), F32).at[:, :n_vr].set(rwkv_v1[i - 1]).astype(BF16)
            v2 = jnp.zeros((V7X_LANES, d_r), F32).at[:n_vr, :].set(rwkv_v2[i - 1]).astype(BF16)
        else:
            v1 = v2 = None
        table = _attn_table(att_rel_bias[i], qb)
        wor = w_out[i, :d_r].astype(BF16)
        woa = w_out[i, d_r:].astype(BF16)
        n_rt = N_GROUPS + N_EXPERTS
        wrt = jnp.zeros((d, V7X_LANES), F32)
        wrt = wrt.at[:, :N_GROUPS].set(router_group_w[i]).at[:, N_GROUPS:n_rt].set(router_expert_w[i])
        wrt_hi = wrt.astype(BF16)
        wrt = jnp.concatenate([wrt_hi, (wrt - wrt_hi.astype(F32)).astype(BF16)], axis=1)
        brt = jnp.zeros((1, V7X_LANES), F32)
        brt = brt.at[0, :N_GROUPS].set(router_group_b[i]).at[0, N_GROUPS:n_rt].set(router_expert_b[i])

        z_r, qkv = _norm_proj(h, norm_mix_w[i][None], wr, wa)
        if i == 0:
            y_r, v_first = _rwkv(z_r, None, rwkv_mu[i][None], vec, wl, None, None,
                                 batch=batch, seq=seq, n_heads=n_heads_r)
        else:
            y_r = _rwkv(z_r, v_first, rwkv_mu[i][None], vec, wl, v1, v2,
                        batch=batch, seq=seq, n_heads=n_heads_r)
        y_a = _attn(qkv, table, batch=batch, seq=seq, n_heads=n_heads_a)

        h1, hn, route, route_t, cnt, tstat = _outproj_route(y_r, y_a, h, wor, woa, norm_ffn_w[i][None],
                                                            wrt, brt)
        ri = route_t.astype(I32)
        counts = cnt[0, :N_EXPERTS].astype(I32)
        padded = ((counts + TM_EXPERT - 1) // TM_EXPERT) * TM_EXPERT
        p_end = jnp.cumsum(padded)
        p_start = p_end - padded
        pos1 = _lookup(p_start, ri[_R_E1]) + ri[_R_RANK1]
        pos2 = _lookup(p_start, ri[_R_E2]) + ri[_R_RANK2]
        tile_start = jnp.arange(n_tiles, dtype=I32) * TM_EXPERT
        tile_expert = _bucket(p_end, tile_start)
        tile_valid = (tile_start < p_end[-1]).astype(I32)

        tile_first = jnp.concatenate([jnp.ones((1,), I32),
                                      (tile_expert[1:] != tile_expert[:-1]).astype(I32)])

        tm_d = min(TM_DISPATCH, t)
        nb_d = t // tm_d
        assert tm_d == min(TM_ROUTE, t)
        ts = tstat.reshape(nb_d, V7X_SUBLANES, -1)[:, :2, :N_EXPERTS].astype(I32)
        tile_cnt, tile_before = ts[:, 0], ts[:, 1]
        local_start = jnp.cumsum(tile_cnt, axis=1) - tile_cnt
        run_dst = p_start[None, :] + tile_before
        ls_tok = jnp.repeat(local_start.T, tm_d, axis=1)
        ids = jnp.arange(N_EXPERTS, dtype=I32)[:, None]
        lpos1 = jnp.sum(jnp.where(ri[_R_E1][None, :] == ids, ls_tok, 0), axis=0) + ri[_R_LRANK1]
        lpos2 = jnp.sum(jnp.where(ri[_R_E2][None, :] == ids, ls_tok, 0), axis=0) + ri[_R_LRANK2]
        idx_d = _index_records(lpos1, lpos2, tm_d,
                               jnp.concatenate([tile_cnt, run_dst, local_start], axis=1))
        n_free = n_rows - 2 * t
        zero_rows = TM_EXPERT // 2
        zinfo = jnp.concatenate([p_start + counts, padded - counts,
                                 jnp.stack([p_end[-1], (n_rows - p_end[-1]) // zero_rows])])
        zinfo = jnp.pad(zinfo, (0, -zinfo.shape[0] % _SMEM_RECORD_WORDS)).astype(I32)
        xs = _dispatch(hn, idx_d, zinfo, n_rows, tm_d, n_free)
        ys = _experts(xs, w1_all, w3_all, w2_all, tile_expert + i * N_EXPERTS, tile_valid, tile_first)

        tm_c = min(TM_COMBINE, t)
        h = _combine_ple(_index_records(pos1, pos2, tm_c), ys, h1, route, p.reshape(depth * t, -1),
                         norm_ple_w[i][None], ple_gate_w[i].astype(BF16), ple_gate_b[i][None],
                         ple_proj_w[i].astype(BF16), final_norm_w[None],
                         tm=tm_c, final=(i == depth - 1), layer=i)
    return h.reshape(batch, seq, d)
```
